```python
import jax
import jax.numpy as jnp
from jax import lax
import numpy as np

D_MODEL = 2048
BATCH = 32
SEQ = 256
DEPTH = 2
DEC_BATCH = 8
DEC_SEQ = 4096
PAST_LEN = 512

GRID_W = 64
N_BRANCH = 4
BRANCH_W = 512
ATT_HEADS = 8
ATT_KV = 2
HEAD_DIM = 64
ROPE_THETA = 10000.0
Q_BLOCK = 128
SSD_HEADS = 8
SSD_P = 64
SSD_N = 64
SSD_GROUPS = 2
SSD_CONV = 3
SSD_CONV_CH = SSD_HEADS * SSD_P + 2 * SSD_GROUPS * SSD_N
RWKV_HEADS = 8
RWKV_HD = 64
RWKV_W_RANK = 64
RWKV_A_RANK = 64
RWKV_G_RANK = 128
RWKV_DECAY_SCALE = 0.6065306597126334
RWKV_LN_EPS = 64e-5
RWKV_BLOCK = 3 * BRANCH_W + RWKV_W_RANK + RWKV_A_RANK + RWKV_G_RANK
GLA_HEADS = 4
GLA_DK = 64
GLA_DV = 128
GLA_GATE_RANK = 16
GLA_GATE_NORM = 16.0
CHUNK = 64
D_FF = ((8 * D_MODEL // 3 + 255) // 256) * 256
IN_SPLITS = (ATT_HEADS * HEAD_DIM, ATT_KV * HEAD_DIM, ATT_KV * HEAD_DIM,
             SSD_HEADS * SSD_P, SSD_HEADS * SSD_P, SSD_GROUPS * SSD_N, SSD_GROUPS * SSD_N, 2 * SSD_HEADS,
             RWKV_BLOCK,
             GLA_HEADS * GLA_DK, GLA_HEADS * GLA_DK, GLA_HEADS * GLA_DV, GLA_GATE_RANK, GLA_HEADS * GLA_DV)
D_IN = sum(IN_SPLITS)
F32 = jnp.float32

kernel_name = 'hybrid_diffusion_prefix_step'


def rms_norm(x, g, eps=1e-6):
    xf = x.astype(F32)
    y = xf * lax.rsqrt(jnp.mean(xf * xf, axis=-1, keepdims=True) + eps)
    return (y * g.astype(F32)).astype(x.dtype)


def split_cols(a, sizes):
    return jnp.split(a, np.cumsum(sizes)[:-1].tolist(), axis=-1)


def flip(a):
    return jnp.flip(a, axis=1)


def rope_2d(x, rows, cols):
    half = HEAD_DIM // 2
    nf = half // 2
    freqs = ROPE_THETA ** (-jnp.arange(nf, dtype=F32) / nf)

    def rot(xh, pos):
        ang = pos.astype(F32)[:, None] * freqs[None, :]
        cos = jnp.cos(ang)[None, :, None, :]
        sin = jnp.sin(ang)[None, :, None, :]
        x1 = xh[..., :nf].astype(F32)
        x2 = xh[..., nf:].astype(F32)
        return jnp.concatenate([x1 * cos - x2 * sin, x2 * cos + x1 * sin], axis=-1)

    return jnp.concatenate([rot(x[..., :half], rows), rot(x[..., half:], cols)], axis=-1).astype(x.dtype)


def blocked_attention(q, k, v):
    b, t = q.shape[:2]
    grp = ATT_HEADS // ATT_KV
    qb = q.astype(F32).reshape(b, t // Q_BLOCK, Q_BLOCK, ATT_KV, grp, HEAD_DIM).swapaxes(0, 1)
    kf = k.astype(F32)
    vf = v.astype(F32)
    scale = HEAD_DIM ** -0.5

    def one_block(qblk):
        s = jnp.einsum('bqkgd,bskd->bkgqs', qblk, kf) * scale
        w = jax.nn.softmax(s, axis=-1)
        return jnp.einsum('bkgqs,bskd->bqkgd', w, vf)

    o = lax.map(one_block, qb)
    return o.swapaxes(0, 1).reshape(b, t, ATT_HEADS * HEAD_DIM).astype(q.dtype)


def chunk_scan(q, k, v, logdec, s0):
    b, t, h, _ = q.shape
    nc = t // CHUNK
    scalar = logdec.shape[-1] == 1
    mask = jnp.tril(jnp.ones((CHUNK, CHUNK), dtype=bool))[None, :, :, None, None]

    def to_chunks(a):
        return a.astype(F32).reshape(b, nc, CHUNK, *a.shape[2:]).swapaxes(0, 1)

    def step(S, inp):
        qc, kc, vc, gc = inp
        G = jnp.cumsum(gc, axis=1)
        o = jnp.einsum('bchk,bhkv->bchv', qc * jnp.exp(G), S)
        diff = G[:, :, None] - G[:, None, :]
        dec = jnp.exp(jnp.where(mask, diff, -jnp.inf))
        if scalar:
            att = jnp.einsum('bihk,bjhk->bhij', qc, kc) * dec[..., 0].transpose(0, 3, 1, 2)
        else:
            att = jnp.einsum('bihk,bjhk,bijhk->bhij', qc, kc, dec)
        o = o + jnp.einsum('bhij,bjhv->bihv', att, vc)
        g_last = G[:, -1]
        kd = kc * jnp.exp(g_last[:, None] - G)
        S = jnp.exp(g_last)[..., None] * S + jnp.einsum('bchk,bchv->bhkv', kd, vc)
        return S, o

    S, o = lax.scan(step, s0.astype(F32), (to_chunks(q), to_chunks(k), to_chunks(v), to_chunks(logdec)))
    o = o.swapaxes(0, 1).reshape(b, t, h, v.shape[-1])
    return o.astype(v.dtype), S


def rwkv_scan(r, w, k, v, kk, a, s0):
    def step(S, inp):
        rt, wt, kt, vt, kkt, at = inp
        sa = jnp.einsum('bhvk,bhk->bhv', S, -kkt)
        S = S * wt[:, :, None, :] + sa[..., None] * (kkt * at)[:, :, None, :] + vt[..., None] * kt[:, :, None, :]
        return S, jnp.einsum('bhvk,bhk->bhv', S, rt)

    xs = tuple(z.astype(F32).swapaxes(0, 1) for z in (r, w, k, v, kk, a))
    S, o = lax.scan(step, s0.astype(F32), xs)
    return o.swapaxes(0, 1), S


def centred_dwconv(a, w, bias):
    pad = w.shape[-1] // 2
    t = a.shape[1]
    ap = jnp.pad(a, ((0, 0), (pad, pad), (0, 0)))
    out = bias + ap[:, 0:t] * w[:, 0]
    for j in range(1, w.shape[-1]):
        out = out + ap[:, j:j + t] * w[:, j]
    return out


def token_shift(a):
    ap = jnp.pad(a, ((0, 0), (1, 1), (0, 0)))
    return 0.5 * (ap[:, :-2] + ap[:, 2:])


def attention_branch(aq, ak, av, q_norm, k_norm, pos, ctx_k, ctx_v):
    b, t = aq.shape[:2]
    q = rms_norm(aq.reshape(b, t, ATT_HEADS, HEAD_DIM), q_norm)
    k = rms_norm(ak.reshape(b, t, ATT_KV, HEAD_DIM), k_norm)
    v = av.reshape(b, t, ATT_KV, HEAD_DIM)
    if pos is None:
        return blocked_attention(q, k, v), k, v
    q = rope_2d(q, *pos)
    k = rope_2d(k, *pos)
    k_all = jnp.concatenate([k, ctx_k.astype(k.dtype)], axis=1)
    v_all = jnp.concatenate([v, ctx_v.astype(v.dtype)], axis=1)
    return blocked_attention(q, k_all, v_all), k, v


def ssd_branch(sz, sx, sb, sc, sdt, p, s0):
    b, t = sx.shape[:2]
    xbc = jax.nn.silu(centred_dwconv(jnp.concatenate([sx, sb, sc], axis=-1), p['ssd_conv_w'], p['ssd_conv_b']))
    xs, bm, cm = split_cols(xbc, (SSD_HEADS * SSD_P, SSD_GROUPS * SSD_N, SSD_GROUPS * SSD_N))
    rep = SSD_HEADS // SSD_GROUPS
    xs = xs.reshape(b, t, SSD_HEADS, SSD_P)
    bm = jnp.repeat(bm.reshape(b, t, SSD_GROUPS, SSD_N), rep, axis=2)
    cm = jnp.repeat(cm.reshape(b, t, SSD_GROUPS, SSD_N), rep, axis=2)
    dt = jax.nn.softplus(sdt.reshape(b, t, 2, SSD_HEADS).astype(F32) + p['ssd_dt_bias'].astype(F32))
    logdec = dt * -jnp.exp(p['ssd_a_log'].astype(F32))
    kf = bm[:, :, None].astype(F32) * dt[..., None]
    y_f, s_f = chunk_scan(cm, kf[:, :, 0], xs, logdec[:, :, 0, :, None], s0[:, 0])
    y_b, s_b = chunk_scan(flip(cm), flip(kf[:, :, 1]), flip(xs), flip(logdec[:, :, 1, :, None]), s0[:, 1])
    y = y_f + flip(y_b) + p['ssd_d'][:, None] * xs
    y = rms_norm(y.reshape(b, t, SSD_HEADS * SSD_P) * jax.nn.silu(sz), p['ssd_norm'])
    return y, jnp.stack([s_f, s_b], axis=1)


def rwkv_branch(blk, p, s0):
    b, t = blk.shape[:2]
    blk = blk + (token_shift(blk) - blk) * p['rwkv_mu']
    r, k, v, wl, al, gl = split_cols(blk, (BRANCH_W, BRANCH_W, BRANCH_W, RWKV_W_RANK, RWKV_A_RANK, RWKV_G_RANK))

    def heads(z):
        return z.astype(F32).reshape(b, t, RWKV_HEADS, RWKV_HD)

    w_logit = p['rwkv_w0'] + jnp.einsum('btr,zrc->btzc', jnp.tanh(wl), p['rwkv_w2'])
    decay = jnp.exp(-RWKV_DECAY_SCALE * jax.nn.sigmoid(w_logit.astype(F32))).reshape(b, t, 2, RWKV_HEADS, RWKV_HD)
    a = heads(jax.nn.sigmoid(p['rwkv_a0'] + al @ p['rwkv_a2']))
    g = jax.nn.sigmoid(gl) @ p['rwkv_g2']
    r_h, k_h, v_h = heads(r), heads(k), heads(v)
    kk = k_h * p['rwkv_kk'].astype(F32).reshape(RWKV_HEADS, RWKV_HD)
    kk = kk * lax.rsqrt(jnp.sum(kk * kk, axis=-1, keepdims=True) + 1e-12)
    k_h = k_h * (1.0 + (a - 1.0) * p['rwkv_ka'].astype(F32).reshape(RWKV_HEADS, RWKV_HD))
    o_f, s_f = rwkv_scan(r_h, decay[:, :, 0], k_h, v_h, kk, a, s0[:, 0])
    o_b, s_b = rwkv_scan(flip(r_h), flip(decay[:, :, 1]), flip(k_h), flip(v_h), flip(kk), flip(a), s0[:, 1])
    o = o_f + flip(o_b)
    mu = jnp.mean(o, axis=-1, keepdims=True)
    var = jnp.mean(jnp.square(o - mu), axis=-1, keepdims=True)
    o = ((o - mu) * lax.rsqrt(var + RWKV_LN_EPS)).reshape(b, t, BRANCH_W) * p['rwkv_ln_g'] + p['rwkv_ln_b']
    bonus = jnp.sum(r_h * k_h * p['rwkv_rk'].astype(F32).reshape(RWKV_HEADS, RWKV_HD), axis=-1, keepdims=True) * v_h
    o = (o + bonus.reshape(b, t, BRANCH_W)) * g
    return o.astype(blk.dtype), jnp.stack([s_f, s_b], axis=1)


def gla_branch(gq, gk, gv, ggl, gog, p, s0):
    b, t = gq.shape[:2]
    q = gq.reshape(b, t, GLA_HEADS, GLA_DK) * (GLA_DK ** -0.5)
    k = gk.reshape(b, t, GLA_HEADS, GLA_DK)
    v = gv.reshape(b, t, GLA_HEADS, GLA_DV)
    logit = jnp.einsum('btr,zrk->btzk', ggl, p['gla_g2']) + p['gla_gb']
    log_a = (jax.nn.log_sigmoid(logit.astype(F32)) / GLA_GATE_NORM).reshape(b, t, 2, GLA_HEADS, GLA_DK)
    o_f, s_f = chunk_scan(q, k, v, log_a[:, :, 0], s0[:, 0])
    o_b, s_b = chunk_scan(flip(q), flip(k), flip(v), flip(log_a[:, :, 1]), s0[:, 1])
    o = rms_norm(o_f + flip(o_b), p['gla_norm']).reshape(b, t, GLA_HEADS * GLA_DV) * jax.nn.silu(gog)
    return o, jnp.stack([s_f, s_b], axis=1)


def token_mix(h, p, pos, ctx):
    b = h.shape[0]
    (aq, ak, av, sz, sx, sb, sc, sdt, rblk, gq, gk, gv, ggl, gog) = split_cols(h @ p['w_in'], IN_SPLITS)
    if ctx is None:
        ctx_k = ctx_v = None
        s_ssd = jnp.zeros((b, 2, SSD_HEADS, SSD_N, SSD_P), F32)
        s_rwkv = jnp.zeros((b, 2, RWKV_HEADS, RWKV_HD, RWKV_HD), F32)
        s_gla = jnp.zeros((b, 2, GLA_HEADS, GLA_DK, GLA_DV), F32)
    else:
        ctx_k, ctx_v, s_ssd, s_rwkv, s_gla = ctx
    o_att, k_own, v_own = attention_branch(aq, ak, av, p['q_norm'], p['k_norm'], pos, ctx_k, ctx_v)
    o_ssd, s_ssd = ssd_branch(sz, sx, sb, sc, sdt, p, s_ssd)
    o_rwkv, s_rwkv = rwkv_branch(rblk, p, s_rwkv)
    o_gla, s_gla = gla_branch(gq, gk, gv, ggl, gog, p, s_gla)
    merged = None
    for i, o in enumerate((o_att, o_ssd, o_rwkv, o_gla)):
        term = jax.nn.sigmoid(h @ p['w_gate'][i]) * (o @ p['w_branch'][i])
        merged = term if merged is None else merged + term
    return merged @ p['w_o'], (k_own, v_own, s_ssd, s_rwkv, s_gla)


def block(x, cond, p, pos, ctx):
    mod = (jax.nn.silu(cond) @ p['w_mod'] + p['b_mod']).reshape(-1, 1, 6 * D_MODEL)
    sh1, sc1, g1, sh2, sc2, g2 = jnp.split(mod, 6, axis=-1)
    h = rms_norm(x, p['norm1']) * (1 + sc1) + sh1
    mix, ctx_out = token_mix(h, p, pos, ctx)
    x = x + g1 * mix
    h = rms_norm(x, p['norm2']) * (1 + sc2) + sh2
    ff = (jax.nn.silu(h @ p['ffn_w1']) * (h @ p['ffn_w3'])) @ p['ffn_w2']
    return x + g2 * ff, ctx_out


def setup_inputs(seed: int = 0) -> dict:
    key = jax.random.key(seed)
    keys = iter(jax.random.split(key, 64))

    def nrm(shape, scale):
        return jax.random.normal(next(keys), shape, F32) * scale

    def gain(shape):
        return 1.0 + nrm(shape, 0.02)

    L, D = DEPTH, D_MODEL
    x_prompt = nrm((BATCH, SEQ, D), 1.0)
    x_sample = nrm((DEC_BATCH, DEC_SEQ, D), 1.0)
    cache_attn_k = nrm((DEC_BATCH, L, PAST_LEN, ATT_KV, HEAD_DIM), 1.0)
    cache_attn_v = nrm((DEC_BATCH, L, PAST_LEN, ATT_KV, HEAD_DIM), 1.0)
    state_ssd = nrm((DEC_BATCH, L, 2, SSD_HEADS, SSD_N, SSD_P), 0.3)
    state_rwkv = nrm((DEC_BATCH, L, 2, RWKV_HEADS, RWKV_HD, RWKV_HD), 0.3)
    state_gla = nrm((DEC_BATCH, L, 2, GLA_HEADS, GLA_DK, GLA_DV), 0.3)
    c = nrm((DEC_BATCH, D), 1.0)
    c_ctx = nrm((D,), 1.0)
    w_mod = nrm((L, D, 6 * D), 0.5 * D ** -0.5)
    b_mod = nrm((L, 6 * D), 0.02)
    norm1 = gain((L, D))
    norm2 = gain((L, D))
    w_in = nrm((L, D, D_IN), D ** -0.5)
    q_norm = gain((L, HEAD_DIM))
    k_norm = gain((L, HEAD_DIM))
    ssd_conv_w = nrm((L, SSD_CONV_CH, SSD_CONV), 0.5)
    ssd_conv_b = nrm((L, SSD_CONV_CH), 0.02)
    dt0 = jnp.exp(jax.random.uniform(next(keys), (L, 2, SSD_HEADS), F32, -6.9, -2.3))
    ssd_dt_bias = dt0 + jnp.log(-jnp.expm1(-dt0))
    ssd_a_log = jnp.log(jax.random.uniform(next(keys), (L, 2, SSD_HEADS), F32, 1.0, 16.0))
    ssd_d = gain((L, SSD_HEADS))
    ssd_norm = gain((L, SSD_HEADS * SSD_P))
    rwkv_mu = jax.random.uniform(next(keys), (L, RWKV_BLOCK), F32)
    rwkv_w0 = jax.random.uniform(next(keys), (L, 2, BRANCH_W), F32, -3.0, 1.0)
    rwkv_w2 = nrm((L, 2, RWKV_W_RANK, BRANCH_W), 0.1)
    rwkv_a0 = nrm((L, BRANCH_W), 0.5)
    rwkv_a2 = nrm((L, RWKV_A_RANK, BRANCH_W), 0.5 * RWKV_A_RANK ** -0.5)
    rwkv_g2 = nrm((L, RWKV_G_RANK, BRANCH_W), RWKV_G_RANK ** -0.5)
    rwkv_kk = 0.85 + nrm((L, BRANCH_W), 0.02)
    rwkv_ka = gain((L, BRANCH_W))
    rwkv_rk = nrm((L, BRANCH_W), 0.1)
    rwkv_ln_g = gain((L, BRANCH_W))
    rwkv_ln_b = nrm((L, BRANCH_W), 0.02)
    gla_g2 = nrm((L, 2, GLA_GATE_RANK, GLA_HEADS * GLA_DK), 0.5 * GLA_GATE_RANK ** -0.5)
    gla_gb = 1.0 + nrm((L, 2, GLA_HEADS * GLA_DK), 0.5)
    gla_norm = gain((L, GLA_DV))
    w_gate = nrm((L, N_BRANCH, D, D), D ** -0.5)
    w_branch = nrm((L, N_BRANCH, BRANCH_W, D), BRANCH_W ** -0.5)
    w_o = nrm((L, D, D), D ** -0.5)
    ffn_w1 = nrm((L, D, D_FF), D ** -0.5)
    ffn_w3 = nrm((L, D, D_FF), D ** -0.5)
    ffn_w2 = nrm((L, D_FF, D), D_FF ** -0.5)
    final_norm = gain((D,))
    return {'x_prompt': x_prompt, 'x_sample': x_sample,
            'cache_attn_k': cache_attn_k, 'cache_attn_v': cache_attn_v,
            'state_ssd': state_ssd, 'state_rwkv': state_rwkv, 'state_gla': state_gla,
            'c': c, 'c_ctx': c_ctx,
            'w_mod': w_mod, 'b_mod': b_mod, 'norm1': norm1, 'norm2': norm2, 'w_in': w_in,
            'q_norm': q_norm, 'k_norm': k_norm,
            'ssd_conv_w': ssd_conv_w, 'ssd_conv_b': ssd_conv_b, 'ssd_dt_bias': ssd_dt_bias,
            'ssd_a_log': ssd_a_log, 'ssd_d': ssd_d, 'ssd_norm': ssd_norm,
            'rwkv_mu': rwkv_mu, 'rwkv_w0': rwkv_w0, 'rwkv_w2': rwkv_w2, 'rwkv_a0': rwkv_a0,
            'rwkv_a2': rwkv_a2, 'rwkv_g2': rwkv_g2, 'rwkv_kk': rwkv_kk, 'rwkv_ka': rwkv_ka,
            'rwkv_rk': rwkv_rk, 'rwkv_ln_g': rwkv_ln_g, 'rwkv_ln_b': rwkv_ln_b,
            'gla_g2': gla_g2, 'gla_gb': gla_gb, 'gla_norm': gla_norm,
            'w_gate': w_gate, 'w_branch': w_branch, 'w_o': w_o,
            'ffn_w1': ffn_w1, 'ffn_w3': ffn_w3, 'ffn_w2': ffn_w2, 'final_norm': final_norm}


def reference(x_prompt, x_sample, cache_attn_k, cache_attn_v, state_ssd, state_rwkv, state_gla, c, c_ctx,
              w_mod, b_mod, norm1, norm2, w_in, q_norm, k_norm,
              ssd_conv_w, ssd_conv_b, ssd_dt_bias, ssd_a_log, ssd_d, ssd_norm,
              rwkv_mu, rwkv_w0, rwkv_w2, rwkv_a0, rwkv_a2, rwkv_g2, rwkv_kk, rwkv_ka, rwkv_rk,
              rwkv_ln_g, rwkv_ln_b, gla_g2, gla_gb, gla_norm,
              w_gate, w_branch, w_o, ffn_w1, ffn_w3, ffn_w2, final_norm):
    def params_at(l):
        return dict(w_mod=w_mod[l], b_mod=b_mod[l], norm1=norm1[l], norm2=norm2[l], w_in=w_in[l],
                    q_norm=q_norm[l], k_norm=k_norm[l],
                    ssd_conv_w=ssd_conv_w[l], ssd_conv_b=ssd_conv_b[l], ssd_dt_bias=ssd_dt_bias[l],
                    ssd_a_log=ssd_a_log[l], ssd_d=ssd_d[l], ssd_norm=ssd_norm[l],
                    rwkv_mu=rwkv_mu[l], rwkv_w0=rwkv_w0[l], rwkv_w2=rwkv_w2[l], rwkv_a0=rwkv_a0[l],
                    rwkv_a2=rwkv_a2[l], rwkv_g2=rwkv_g2[l], rwkv_kk=rwkv_kk[l], rwkv_ka=rwkv_ka[l],
                    rwkv_rk=rwkv_rk[l], rwkv_ln_g=rwkv_ln_g[l], rwkv_ln_b=rwkv_ln_b[l],
                    gla_g2=gla_g2[l], gla_gb=gla_gb[l], gla_norm=gla_norm[l],
                    w_gate=w_gate[l], w_branch=w_branch[l], w_o=w_o[l],
                    ffn_w1=ffn_w1[l], ffn_w3=ffn_w3[l], ffn_w2=ffn_w2[l])

    xp = x_prompt
    new_k, new_v, new_ssd, new_rwkv, new_gla = [], [], [], [], []
    for l in range(DEPTH):
        xp, (k_l, v_l, ssd_l, rwkv_l, gla_l) = block(xp, c_ctx, params_at(l), None, None)
        new_k.append(k_l)
        new_v.append(v_l)
        new_ssd.append(ssd_l)
        new_rwkv.append(rwkv_l)
        new_gla.append(gla_l)

    rows = x_sample.shape[1] // GRID_W
    t_idx = jnp.arange(rows * GRID_W)
    pos = (t_idx // GRID_W, t_idx % GRID_W)
    xs = x_sample
    for l in range(DEPTH):
        ctx = (cache_attn_k[:, l], cache_attn_v[:, l], state_ssd[:, l], state_rwkv[:, l], state_gla[:, l])
        xs, _ = block(xs, c, params_at(l), pos, ctx)

    y_prompt = rms_norm(xp, final_norm)
    y_sample = rms_norm(xs, final_norm)
    return (y_prompt, y_sample, jnp.stack(new_k, axis=1), jnp.stack(new_v, axis=1),
            jnp.stack(new_ssd, axis=1), jnp.stack(new_rwkv, axis=1), jnp.stack(new_gla, axis=1))
```

```python
import functools

import jax
import jax.numpy as jnp
import numpy as np
from jax import lax
from jax.experimental import pallas as pl
from jax.experimental.pallas import tpu as pltpu

F32 = jnp.float32
BF16 = jnp.bfloat16
HI = lax.Precision.HIGHEST

D_MODEL = 2048
GRID_W = 64
ATT_HEADS = 8
ATT_KV = 2
ATT_GROUP = ATT_HEADS // ATT_KV
HEAD_DIM = 64
ROPE_THETA = 10000.0
SSD_HEADS = 8
SSD_P = 64
SSD_N = 64
SSD_GROUPS = 2
RWKV_HEADS = 8
RWKV_HD = 64
RWKV_DECAY_SCALE = 0.6065306597126334
RWKV_LN_EPS = 64e-5
GLA_HEADS = 4
GLA_DK = 64
GLA_DV = 128
GLA_GATE_NORM = 16.0
CHUNK = 64
BRANCH_W = 512

ATT_W = 768
SSD_W = 1408
RWKV_W = 1792
GLA_W = 1664
LANE = 128
SUBLANE = 8
MOD_ROWS = 16
VMEM_LIMIT = 56 * 1024 * 1024
ROW_TILE = 256
MM_TILE = 512


def _cp(*sem):
    return pltpu.CompilerParams(dimension_semantics=sem, vmem_limit_bytes=VMEM_LIMIT)


def _dot(a, b, prec=None):
    return jnp.dot(a, b, preferred_element_type=F32, precision=prec)


def _dot_nt(a, b, prec=None):
    return lax.dot_general(a, b, (((1,), (1,)), ((), ())), preferred_element_type=F32, precision=prec)


def _dot_tn(a, b, prec=None):
    return lax.dot_general(a, b, (((0,), (0,)), ((), ())), preferred_element_type=F32, precision=prec)


def _silu(x):
    return x * jax.nn.sigmoid(x)


def _softplus(x):
    return jnp.maximum(x, 0.0) + jnp.log1p(jnp.exp(-jnp.abs(x)))


def _bf(x):
    return x.astype(BF16)


def _lanes(pieces):
    return jnp.concatenate(pieces, axis=1)


def _rows(pieces):
    return jnp.concatenate(pieces, axis=0)


def _mod_kernel(c_ref, w_ref, b_ref, o_ref):
    c = c_ref[...]
    o_ref[0] = _dot(_bf(_silu(c)), _bf(w_ref[0])) + b_ref[0]


def _modulation(cond, w_mod, b_mod):
    nl, d, n6 = w_mod.shape
    tn = 1024
    out = pl.pallas_call(
        _mod_kernel,
        grid=(nl, n6 // tn),
        in_specs=[pl.BlockSpec((MOD_ROWS, d), lambda l, j: (0, 0)),
                  pl.BlockSpec((1, d, tn), lambda l, j: (l, 0, j)),
                  pl.BlockSpec((1, 1, tn), lambda l, j: (l, 0, j))],
        out_specs=pl.BlockSpec((1, MOD_ROWS, tn), lambda l, j: (l, 0, j)),
        out_shape=jax.ShapeDtypeStruct((nl, MOD_ROWS, n6), F32),
        compiler_params=_cp("parallel", "parallel"),
        name="modulation",
    )(cond, w_mod, b_mod.reshape(nl, 1, n6))
    return out.reshape(nl, MOD_ROWS, 6, d)


class _Stream:
    def __init__(self, n, t, group0, per_seq):
        self.n, self.t, self.group0, self.per_seq = n, t, group0, per_seq
        self.rows = n * t

    def group_map(self, tile):
        g0, per_seq, t = self.group0, self.per_seq, self.t
        if per_seq:
            return lambda i: (g0 + (i * tile) // t, 0, 0)
        return lambda i: (g0, 0, 0)


def _mm_tile(st):
    return min(MM_TILE, st.t)


def _row_tile(st):
    return min(ROW_TILE, st.t)


def _normmod_kernel(x_ref, g_ref, mod_ref, o_ref, *, sc_idx, sh_idx):
    x = x_ref[...]
    y = x * lax.rsqrt(jnp.mean(x * x, axis=-1, keepdims=True) + 1e-6) * g_ref[...]
    o_ref[...] = _bf(y * (1.0 + mod_ref[0, sc_idx:sc_idx + 1, :]) + mod_ref[0, sh_idx:sh_idx + 1, :])


def _normmod(st, x, gain, mod_l, sc_idx, sh_idx):
    m, d = x.shape
    tm = _mm_tile(st)
    return pl.pallas_call(
        functools.partial(_normmod_kernel, sc_idx=sc_idx, sh_idx=sh_idx),
        grid=(m // tm,),
        in_specs=[pl.BlockSpec((tm, d), lambda i: (i, 0)),
                  pl.BlockSpec((1, d), lambda i: (0, 0)),
                  pl.BlockSpec((1, 6, d), st.group_map(tm))],
        out_specs=pl.BlockSpec((tm, d), lambda i: (i, 0)),
        out_shape=jax.ShapeDtypeStruct((m, d), BF16),
        compiler_params=_cp("parallel"),
        name="normmod",
    )(x, gain.reshape(1, d), mod_l)


def _rms_kernel(x_ref, g_ref, o_ref):
    x = x_ref[...]
    o_ref[...] = x * lax.rsqrt(jnp.mean(x * x, axis=-1, keepdims=True) + 1e-6) * g_ref[...]


def _final_norm(st, x, gain):
    m, d = x.shape
    tm = _mm_tile(st)
    return pl.pallas_call(
        _rms_kernel,
        grid=(m // tm,),
        in_specs=[pl.BlockSpec((tm, d), lambda i: (i, 0)), pl.BlockSpec((1, d), lambda i: (0, 0))],
        out_specs=pl.BlockSpec((tm, d), lambda i: (i, 0)),
        out_shape=jax.ShapeDtypeStruct((m, d), F32),
        compiler_params=_cp("parallel"),
        name="final_norm",
    )(x, gain.reshape(1, d))


def _mm_kernel(x_ref, w_ref, o_ref):
    o_ref[...] = _dot(x_ref[...], w_ref[...]).astype(o_ref.dtype)


def _matmul(st, x, w, out_dtype=F32):
    m, k = x.shape
    n = w.shape[1]
    tm = _mm_tile(st)
    return pl.pallas_call(
        _mm_kernel,
        grid=(m // tm,),
        in_specs=[pl.BlockSpec((tm, k), lambda i: (i, 0)), pl.BlockSpec((k, n), lambda i: (0, 0))],
        out_specs=pl.BlockSpec((tm, n), lambda i: (i, 0)),
        out_shape=jax.ShapeDtypeStruct((m, n), out_dtype),
        compiler_params=_cp("parallel"),
        name="in_proj",
    )(x, w)


def _att_prep_kernel(*refs, rope):
    if rope:
        p_ref, qn_ref, kn_ref, bdq_ref, bdk_ref, cos_ref, sa_ref, sb_ref, q_out, k_out = refs
    else:
        p_ref, qn_ref, kn_ref, bdq_ref, bdk_ref, q_out, k_out = refs
    p = p_ref[...]
    aq = p[:, :512]
    ak = p[:, 512:640]
    q = aq * lax.rsqrt(_dot(aq * aq, bdq_ref[...], HI) + 1e-6) * qn_ref[...]
    k = ak * lax.rsqrt(_dot(ak * ak, bdk_ref[...], HI) + 1e-6) * kn_ref[...]
    if rope:
        c, sa, sb = cos_ref[...], sa_ref[...], sb_ref[...]
        k = k * c + pltpu.roll(k, LANE - 16, 1) * sa + pltpu.roll(k, 16, 1) * sb
        c4, sa4, sb4 = _lanes([c] * 4), _lanes([sa] * 4), _lanes([sb] * 4)
        q = q * c4 + pltpu.roll(q, 512 - 16, 1) * sa4 + pltpu.roll(q, 16, 1) * sb4
    q_out[...] = _bf(q * (HEAD_DIM ** -0.5))
    k_out[...] = k.astype(k_out.dtype)


def _block_diag(width, block, value):
    idx = np.arange(width) // block
    return jnp.asarray((idx[:, None] == idx[None, :]).astype(np.float32) * value)


def _att_prep(st, proj, q_norm, k_norm, rope_tables):
    m = proj.shape[0]
    tr = _row_tile(st)
    rope = rope_tables is not None
    full = lambda shape: pl.BlockSpec(shape, lambda i: (0, 0))
    in_specs = [pl.BlockSpec((tr, ATT_W), lambda i: (i, 0)), full((1, 512)), full((1, 128)),
                full((512, 512)), full((128, 128))]
    args = [proj, jnp.tile(q_norm, ATT_HEADS).reshape(1, 512), jnp.tile(k_norm, ATT_KV).reshape(1, 128),
            _block_diag(512, HEAD_DIM, 1.0 / HEAD_DIM), _block_diag(128, HEAD_DIM, 1.0 / HEAD_DIM)]
    if rope:
        tps = st.t // tr
        in_specs += [pl.BlockSpec((tr, LANE), lambda i: (i % tps, 0))] * 3
        args += list(rope_tables)
    return pl.pallas_call(
        functools.partial(_att_prep_kernel, rope=rope),
        grid=(m // tr,),
        in_specs=in_specs,
        out_specs=[pl.BlockSpec((tr, 512), lambda i: (i, 0)), pl.BlockSpec((tr, 128), lambda i: (i, 0))],
        out_shape=[jax.ShapeDtypeStruct((m, 512), BF16),
                   jax.ShapeDtypeStruct((m, 128), BF16 if rope else F32)],
        compiler_params=_cp("parallel"),
        name="att_prep",
    )(*args)


def _rope_tables(t):
    half = HEAD_DIM // 2
    nf = half // 2
    freqs = ROPE_THETA ** (-jnp.arange(nf, dtype=F32) / nf)
    tt = jnp.arange(t)
    ang_r = (tt // GRID_W).astype(F32)[:, None] * freqs[None, :]
    ang_c = (tt % GRID_W).astype(F32)[:, None] * freqs[None, :]
    zero = jnp.zeros_like(ang_r)
    cos = jnp.concatenate([jnp.cos(ang_r)] * 2 + [jnp.cos(ang_c)] * 2, axis=1)
    sa = jnp.concatenate([-jnp.sin(ang_r), zero, -jnp.sin(ang_c), zero], axis=1)
    sb = jnp.concatenate([zero, jnp.sin(ang_r), zero, jnp.sin(ang_c)], axis=1)
    return tuple(jnp.tile(a, (1, LANE // HEAD_DIM)) for a in (cos, sa, sb))


def _attn_kernel(*refs, n_src, tq):
    q_ref, o_ref = refs[0], refs[-1]
    q = q_ref[...]
    outs = []
    for kv in range(ATT_KV):
        lo = kv * HEAD_DIM
        qs = _rows([q[:, (kv * ATT_GROUP + g) * HEAD_DIM:(kv * ATT_GROUP + g + 1) * HEAD_DIM]
                    for g in range(ATT_GROUP)])
        ks = [_bf(refs[1 + 2 * s][:, lo:lo + HEAD_DIM]) for s in range(n_src)]
        vs = [_bf(refs[2 + 2 * s][:, lo:lo + HEAD_DIM]) for s in range(n_src)]
        ss = [_dot_nt(qs, k) for k in ks]
        mx = functools.reduce(jnp.maximum, [jnp.max(s, axis=-1, keepdims=True) for s in ss])
        ps = [jnp.exp(s - mx) for s in ss]
        den = functools.reduce(jnp.add, [jnp.sum(p, axis=-1, keepdims=True) for p in ps])
        o = functools.reduce(jnp.add, [_dot(_bf(p), v) for p, v in zip(ps, vs)]) / den
        outs += [o[g * tq:(g + 1) * tq] for g in range(ATT_GROUP)]
    o_ref[...] = _bf(_lanes(outs))


def _attention(st, q, k, v, cache_k=None, cache_v=None):
    tq = min(128, st.t)
    nq = st.t // tq
    t = st.t
    in_specs = [pl.BlockSpec((tq, 512), lambda b, i: (b * nq + i, 0)),
                pl.BlockSpec((t, 128), lambda b, i: (b, 0)),
                pl.BlockSpec((t, 128), lambda b, i: (b, 0))]
    args = [q, k, v]
    n_src = 1
    if cache_k is not None:
        p = cache_k.shape[1]
        in_specs += [pl.BlockSpec((None, p, 128), lambda b, i: (b, 0, 0))] * 2
        args += [cache_k, cache_v]
        n_src = 2
    return pl.pallas_call(
        functools.partial(_attn_kernel, n_src=n_src, tq=tq),
        grid=(st.n, nq),
        in_specs=in_specs,
        out_specs=pl.BlockSpec((tq, 512), lambda b, i: (b * nq + i, 0)),
        out_shape=jax.ShapeDtypeStruct((st.rows, 512), BF16),
        compiler_params=_cp("parallel", "parallel"),
        name="attention",
    )(*args)


def _halo_specs(st, width):
    tr = _row_tile(st)
    per8 = tr // SUBLANE
    last8 = st.rows // SUBLANE - 1
    return [pl.BlockSpec((tr, width), lambda i: (i, 0)),
            pl.BlockSpec((SUBLANE, width), lambda i: (jnp.maximum(i * per8 - 1, 0), 0)),
            pl.BlockSpec((SUBLANE, width), lambda i: (jnp.minimum((i + 1) * per8, last8), 0))]


def _neighbours(cur, prev8, next8, tiles_per_seq):
    tr = cur.shape[0]
    j = pl.program_id(0) % tiles_per_seq
    pr = jnp.where(j != 0, prev8[SUBLANE - 1:SUBLANE], 0.0)
    nx = jnp.where(j != tiles_per_seq - 1, next8[0:1], 0.0)
    row = lax.broadcasted_iota(jnp.int32, cur.shape, 0)
    x_prev = jnp.where(row == 0, pr, pltpu.roll(cur, 1, 0))
    x_next = jnp.where(row == tr - 1, nx, pltpu.roll(cur, tr - 1, 0))
    return x_prev, x_next


def _chunk_masks(fwd):
    ri = lax.broadcasted_iota(jnp.int32, (CHUNK, CHUNK), 0)
    ci = lax.broadcasted_iota(jnp.int32, (CHUNK, CHUNK), 1)
    d = jnp.where(fwd, ri - ci, ci - ri)
    return d >= 0, d > 0


def _scan_specs(st, width):
    nc = st.t // CHUNK

    def row_map(z, b, c):
        return (b * nc + c + z * (nc - 1 - 2 * c), 0)

    def out_map(z, b, c):
        return (z, b * nc + c + z * (nc - 1 - 2 * c), 0)

    return nc, row_map, out_map


def _ssd_prep_kernel(cur_ref, prev_ref, next_ref, cw_ref, cb_ref, dtb_ref, alog_ref,
                     xbc_out, dt_out, ld_out, *, tiles_per_seq):
    cur = cur_ref[...]
    xc = cur[:, 512:1280]
    x_prev, x_next = _neighbours(xc, prev_ref[:, 512:1280], next_ref[:, 512:1280], tiles_per_seq)
    cw = cw_ref[...]
    conv = cb_ref[...] + x_prev * cw[0:1] + xc * cw[1:2] + x_next * cw[2:3]
    xbc_out[...] = _silu(conv)
    dt = _softplus(cur[:, 1280:1408] + dtb_ref[...])
    dt_out[...] = dt
    ld_out[...] = dt * -jnp.exp(alog_ref[...])


def _pad_lanes(v, width=LANE):
    v = v.reshape(1, -1)
    return jnp.pad(v, ((0, 0), (0, width - v.shape[1])))


def _ssd_prep(st, proj, conv_w, conv_b, dt_bias, a_log):
    m = proj.shape[0]
    tr = _row_tile(st)
    full = lambda shape: pl.BlockSpec(shape, lambda i: (0, 0))
    return pl.pallas_call(
        functools.partial(_ssd_prep_kernel, tiles_per_seq=st.t // tr),
        grid=(m // tr,),
        in_specs=_halo_specs(st, SSD_W) + [full((3, 768)), full((1, 768)), full((1, LANE)), full((1, LANE))],
        out_specs=[pl.BlockSpec((tr, 768), lambda i: (i, 0)), pl.BlockSpec((tr, LANE), lambda i: (i, 0)),
                   pl.BlockSpec((tr, LANE), lambda i: (i, 0))],
        out_shape=[jax.ShapeDtypeStruct((m, 768), F32), jax.ShapeDtypeStruct((m, LANE), F32),
                   jax.ShapeDtypeStruct((m, LANE), F32)],
        compiler_params=_cp("parallel"),
        name="ssd_prep",
    )(proj, proj, proj, conv_w.T, conv_b.reshape(1, 768), _pad_lanes(dt_bias), _pad_lanes(a_log))


def _ssd_scan_kernel(xbc_ref, dt_ref, ld_ref, s0_ref, y_ref, s_ref):
    z = pl.program_id(0)

    @pl.when(pl.program_id(2) == 0)
    def _():
        s_ref[...] = s0_ref[...]

    fwd = z == 0
    incl, _ = _chunk_masks(fwd)
    tri = incl.astype(F32)
    dt = dt_ref[...]
    cum = _dot(tri, ld_ref[...], HI)
    pick = (lax.broadcasted_iota(jnp.int32, (2 * SSD_HEADS, LANE), 0)
            == lax.broadcasted_iota(jnp.int32, (2 * SSD_HEADS, LANE), 1)).astype(F32)
    cum_t = _dot_nt(pick, cum, HI)
    dt_t = _dot_nt(pick, dt, HI)
    xbc = xbc_ref[...]
    outs = []
    for g in range(SSD_GROUPS):
        bg = xbc[:, 512 + g * SSD_N:512 + (g + 1) * SSD_N]
        cg = xbc[:, 640 + g * SSD_N:640 + (g + 1) * SSD_N]
        cb = _dot_nt(_bf(cg), _bf(bg))
        for hh in range(SSD_HEADS // SSD_GROUPS):
            h = g * (SSD_HEADS // SSD_GROUPS) + hh
            hb = SSD_HEADS + h
            gcol = jnp.where(fwd, cum[:, h:h + 1], cum[:, hb:hb + 1])
            grow = jnp.where(fwd, cum_t[h:h + 1], cum_t[hb:hb + 1])
            dtcol = jnp.where(fwd, dt[:, h:h + 1], dt[:, hb:hb + 1])
            dtrow = jnp.where(fwd, dt_t[h:h + 1], dt_t[hb:hb + 1])
            dec = jnp.exp(jnp.where(incl, gcol - grow, -jnp.inf))
            att = cb * dec * dtrow
            xs = _bf(xbc[:, h * SSD_P:(h + 1) * SSD_P])
            s = s_ref[h]
            outs.append(_dot(_bf(att), xs) + jnp.exp(gcol) * _dot(_bf(cg), _bf(s)))
            glast = jnp.where(fwd, gcol[CHUNK - 1:CHUNK], gcol[0:1])
            kd = bg * (dtcol * jnp.exp(glast - gcol))
            s_ref[h] = jnp.exp(glast) * s + _dot_tn(_bf(kd), xs)
    y_ref[...] = _lanes(outs)


def _ssd_scan(st, xbc, dt, ld, s0):
    nc, row_map, out_map = _scan_specs(st, 768)
    state_spec = pl.BlockSpec((None, None, SSD_HEADS, SSD_N, SSD_P), lambda z, b, c: (b, z, 0, 0, 0))
    return pl.pallas_call(
        _ssd_scan_kernel,
        grid=(2, st.n, nc),
        in_specs=[pl.BlockSpec((CHUNK, 768), row_map), pl.BlockSpec((CHUNK, LANE), row_map),
                  pl.BlockSpec((CHUNK, LANE), row_map), state_spec],
        out_specs=[pl.BlockSpec((None, CHUNK, 512), out_map), state_spec],
        out_shape=[jax.ShapeDtypeStruct((2, st.rows, 512), F32),
                   jax.ShapeDtypeStruct((st.n, 2, SSD_HEADS, SSD_N, SSD_P), F32)],
        compiler_params=_cp("parallel", "parallel", "arbitrary"),
        name="ssd_scan",
    )(xbc, dt, ld, s0)


def _ssd_post_kernel(y_ref, xbc_ref, p_ref, d_ref, g_ref, o_ref):
    y = y_ref[0] + y_ref[1] + d_ref[...] * xbc_ref[:, :512]
    y = y * _silu(p_ref[:, :512])
    o_ref[...] = _bf(y * lax.rsqrt(jnp.mean(y * y, axis=-1, keepdims=True) + 1e-6) * g_ref[...])


def _ssd_post(st, y2, xbc, proj, ssd_d, ssd_norm):
    m = proj.shape[0]
    tr = _row_tile(st)
    full = lambda shape: pl.BlockSpec(shape, lambda i: (0, 0))
    return pl.pallas_call(
        _ssd_post_kernel,
        grid=(m // tr,),
        in_specs=[pl.BlockSpec((2, tr, 512), lambda i: (0, i, 0)), pl.BlockSpec((tr, 768), lambda i: (i, 0)),
                  pl.BlockSpec((tr, SSD_W), lambda i: (i, 0)), full((1, 512)), full((1, 512))],
        out_specs=pl.BlockSpec((tr, 512), lambda i: (i, 0)),
        out_shape=jax.ShapeDtypeStruct((m, 512), BF16),
        compiler_params=_cp("parallel"),
        name="ssd_post",
    )(y2, xbc, proj, jnp.repeat(ssd_d, SSD_P).reshape(1, 512), ssd_norm.reshape(1, 512))


def _gla_scan_kernel(p_ref, g2_ref, gb_ref, s0_ref, y_ref, s_ref):
    z = pl.program_id(0)

    @pl.when(pl.program_id(2) == 0)
    def _():
        s_ref[...] = s0_ref[...]

    fwd = z == 0
    incl, _ = _chunk_masks(fwd)
    p = p_ref[...]
    logit = _dot(p[:, 1024:1152], g2_ref[...], HI) + gb_ref[...]
    log_a = -_softplus(-logit) * (1.0 / GLA_GATE_NORM)
    cum = _dot(incl.astype(F32), log_a, HI)
    mid = cum[CHUNK // 2:CHUNK // 2 + 1]
    glast = jnp.where(fwd, cum[CHUNK - 1:CHUNK], cum[0:1])
    q = p[:, 0:256] * (GLA_DK ** -0.5)
    k = p[:, 256:512]
    q_mid = _bf(q * jnp.exp(cum - mid))
    k_mid = _bf(k * jnp.exp(mid - cum))
    q_in = _bf(q * jnp.exp(cum))
    k_out = _bf(k * jnp.exp(glast - cum))
    e_last = jnp.exp(glast)
    outs = []
    for h in range(GLA_HEADS):
        sl = slice(h * GLA_DK, (h + 1) * GLA_DK)
        v = _bf(p[:, 512 + h * GLA_DV:512 + (h + 1) * GLA_DV])
        att = jnp.where(incl, _dot_nt(q_mid[:, sl], k_mid[:, sl]), 0.0)
        s_t = s_ref[h]
        outs.append(_dot(_bf(att), v) + _dot_nt(q_in[:, sl], _bf(s_t)))
        s_ref[h] = s_t * e_last[:, sl] + _dot_tn(v, k_out[:, sl])
    y_ref[...] = _lanes(outs)


def _gla_scan(st, proj, g2p, gb, s0_t):
    nc, row_map, out_map = _scan_specs(st, GLA_W)
    state_spec = pl.BlockSpec((None, None, GLA_HEADS, GLA_DV, GLA_DK), lambda z, b, c: (b, z, 0, 0, 0))
    return pl.pallas_call(
        _gla_scan_kernel,
        grid=(2, st.n, nc),
        in_specs=[pl.BlockSpec((CHUNK, GLA_W), row_map),
                  pl.BlockSpec((None, LANE, 256), lambda z, b, c: (z, 0, 0)),
                  pl.BlockSpec((None, 1, 256), lambda z, b, c: (z, 0, 0)), state_spec],
        out_specs=[pl.BlockSpec((None, CHUNK, 512), out_map), state_spec],
        out_shape=[jax.ShapeDtypeStruct((2, st.rows, 512), F32),
                   jax.ShapeDtypeStruct((st.n, 2, GLA_HEADS, GLA_DV, GLA_DK), F32)],
        compiler_params=_cp("parallel", "parallel", "arbitrary"),
        name="gla_scan",
    )(proj, g2p, gb, s0_t)


def _gla_post_kernel(y_ref, p_ref, g_ref, o_ref):
    o = y_ref[0] + y_ref[1]
    gate = _silu(p_ref[:, 1152:1664])
    outs = []
    for h in range(GLA_HEADS):
        oh = o[:, h * GLA_DV:(h + 1) * GLA_DV]
        outs.append(oh * lax.rsqrt(jnp.mean(oh * oh, axis=-1, keepdims=True) + 1e-6) * g_ref[...])
    o_ref[...] = _bf(_lanes(outs) * gate)


def _gla_post(st, y2, proj, gla_norm):
    m = proj.shape[0]
    tr = _row_tile(st)
    return pl.pallas_call(
        _gla_post_kernel,
        grid=(m // tr,),
        in_specs=[pl.BlockSpec((2, tr, 512), lambda i: (0, i, 0)), pl.BlockSpec((tr, GLA_W), lambda i: (i, 0)),
                  pl.BlockSpec((1, GLA_DV), lambda i: (0, 0))],
        out_specs=pl.BlockSpec((tr, 512), lambda i: (i, 0)),
        out_shape=jax.ShapeDtypeStruct((m, 512), BF16),
        compiler_params=_cp("parallel"),
        name="gla_post",
    )(y2, proj, gla_norm.reshape(1, GLA_DV))


def _rwkv_prep_kernel(cur_ref, prev_ref, next_ref, mu_ref, w2_ref, w0_ref, a2_ref, a0_ref, g2_ref,
                      kkw_ref, ka_ref, rk_ref, bd_ref, in_out, lw_out, post_out, *, tiles_per_seq):
    cur = cur_ref[...]
    x_prev, x_next = _neighbours(cur, prev_ref[...], next_ref[...], tiles_per_seq)
    blk = cur + (0.5 * (x_prev + x_next) - cur) * mu_ref[...]
    r, k, v = blk[:, 0:512], blk[:, 512:1024], blk[:, 1024:1536]
    w_logit = w0_ref[...] + _dot(jnp.tanh(blk[:, 1536:1600]), w2_ref[...], HI)
    lw_out[...] = -RWKV_DECAY_SCALE * jax.nn.sigmoid(w_logit)
    a = jax.nn.sigmoid(a0_ref[...] + _dot(blk[:, 1600:1664], a2_ref[...], HI))
    g = _dot(jax.nn.sigmoid(blk[:, 1664:1792]), g2_ref[...], HI)
    bd = bd_ref[...]
    kk = k * kkw_ref[...]
    kk = kk * lax.rsqrt(_dot(kk * kk, bd, HI) + 1e-12)
    k2 = k * (1.0 + (a - 1.0) * ka_ref[...])
    in_out[:, 0:512] = r
    in_out[:, 512:1024] = k2
    in_out[:, 1024:1536] = v
    in_out[:, 1536:2048] = -kk
    in_out[:, 2048:2560] = kk * a
    post_out[:, 0:512] = g
    post_out[:, 512:1024] = _dot(r * k2 * rk_ref[...], bd, HI) * v


def _rwkv_prep(st, proj, mu, w2, w0, a2, a0, g2, kkw, ka, rk):
    m = proj.shape[0]
    tr = _row_tile(st)
    full = lambda shape: pl.BlockSpec(shape, lambda i: (0, 0))
    row = lambda w: pl.BlockSpec((tr, w), lambda i: (i, 0))
    vec = lambda a: a.reshape(1, -1)
    return pl.pallas_call(
        functools.partial(_rwkv_prep_kernel, tiles_per_seq=st.t // tr),
        grid=(m // tr,),
        in_specs=_halo_specs(st, RWKV_W) + [full((1, RWKV_W)), full((64, 1024)), full((1, 1024)),
                                            full((64, 512)), full((1, 512)), full((128, 512)),
                                            full((1, 512)), full((1, 512)), full((1, 512)), full((512, 512))],
        out_specs=[row(2560), row(1024), row(1024)],
        out_shape=[jax.ShapeDtypeStruct((m, 2560), F32), jax.ShapeDtypeStruct((m, 1024), F32),
                   jax.ShapeDtypeStruct((m, 1024), F32)],
        compiler_params=_cp("parallel"),
        name="rwkv_prep",
    )(proj, proj, proj, vec(mu), jnp.concatenate([w2[0], w2[1]], axis=1), vec(w0), a2, vec(a0), g2,
      vec(kkw), vec(ka), vec(rk), _block_diag(512, RWKV_HD, 1.0))


def _rwkv_scan_kernel(x_ref, lw_ref, s0_ref, y_ref, s_ref):
    z = pl.program_id(0)

    @pl.when(pl.program_id(2) == 0)
    def _():
        s_ref[...] = s0_ref[...]

    fwd = z == 0
    incl, strict = _chunk_masks(fwd)
    lw = lw_ref[...]
    cum = _dot(incl.astype(F32), lw, HI)
    cum_x = cum - lw
    mid = cum[CHUNK // 2:CHUNK // 2 + 1]
    glast = jnp.where(fwd, cum[CHUNK - 1:CHUNK], cum[0:1])
    x = x_ref[...]
    r, k, v, al, be = x[:, 0:512], x[:, 512:1024], x[:, 1024:1536], x[:, 1536:2048], x[:, 2048:2560]
    e_mid, e_nmid = jnp.exp(cum - mid), jnp.exp(mid - cum)
    r_mid, a_mid = _bf(r * e_mid), _bf(al * jnp.exp(cum_x - mid))
    b_mid, k_mid = _bf(be * e_nmid), _bf(k * e_nmid)
    r_in, a_in = _bf(r * jnp.exp(cum)), _bf(al * jnp.exp(cum_x))
    e_out = jnp.exp(glast - cum)
    b_out, k_out = _bf(be * e_out), _bf(k * e_out)
    e_last = jnp.exp(glast)
    eye = (lax.broadcasted_iota(jnp.int32, (CHUNK, CHUNK), 0)
           == lax.broadcasted_iota(jnp.int32, (CHUNK, CHUNK), 1)).astype(F32)
    outs = []
    for h in range(RWKV_HEADS):
        sl = slice(h * RWKV_HD, (h + 1) * RWKV_HD)
        pair = _dot_nt(_rows([a_mid[:, sl], r_mid[:, sl]]), _rows([b_mid[:, sl], k_mid[:, sl]]))
        a_ab = jnp.where(strict, pair[:CHUNK, :CHUNK], 0.0)
        a_ak = jnp.where(strict, pair[:CHUNK, CHUNK:], 0.0)
        a_rb = jnp.where(incl, pair[CHUNK:, :CHUNK], 0.0)
        a_rk = jnp.where(incl, pair[CHUNK:, CHUNK:], 0.0)
        inv = eye + a_ab
        powr = a_ab
        for _ in range(5):
            powr = _dot(_bf(powr), _bf(powr))
            inv = inv + _dot(_bf(inv), _bf(powr))
        vh = v[:, sl]
        vb = _bf(vh)
        s = s_ref[h]
        sb = _bf(s)
        u = _dot(_bf(inv), _bf(_dot(_bf(a_ak), vb) + _dot_nt(a_in[:, sl], sb)))
        outs.append(_dot_nt(r_in[:, sl], sb) + _dot(_bf(a_rb), _bf(u)) + _dot(_bf(a_rk), vb))
        s_ref[h] = s * e_last[:, sl] + _dot_tn(_bf(_rows([u, vh])), _rows([b_out[:, sl], k_out[:, sl]]))
    y_ref[...] = _lanes(outs)


def _rwkv_scan(st, xin, lw, s0):
    nc, row_map, out_map = _scan_specs(st, 2560)

    def lw_map(z, b, c):
        return (row_map(z, b, c)[0], z)

    state_spec = pl.BlockSpec((None, None, RWKV_HEADS, RWKV_HD, RWKV_HD), lambda z, b, c: (b, z, 0, 0, 0))
    return pl.pallas_call(
        _rwkv_scan_kernel,
        grid=(2, st.n, nc),
        in_specs=[pl.BlockSpec((CHUNK, 2560), row_map), pl.BlockSpec((CHUNK, 512), lw_map), state_spec],
        out_specs=[pl.BlockSpec((None, CHUNK, 512), out_map), state_spec],
        out_shape=[jax.ShapeDtypeStruct((2, st.rows, 512), F32),
                   jax.ShapeDtypeStruct((st.n, 2, RWKV_HEADS, RWKV_HD, RWKV_HD), F32)],
        compiler_params=_cp("parallel", "parallel", "arbitrary"),
        name="rwkv_scan",
    )(xin, lw, s0)


def _rwkv_post_kernel(y_ref, post_ref, lng_ref, lnb_ref, bd_ref, o_ref):
    o = y_ref[0] + y_ref[1]
    bd = bd_ref[...]
    d = o - _dot(o, bd, HI)
    o = d * lax.rsqrt(_dot(d * d, bd, HI) + RWKV_LN_EPS) * lng_ref[...] + lnb_ref[...]
    o_ref[...] = _bf((o + post_ref[:, 512:1024]) * post_ref[:, 0:512])


def _rwkv_post(st, y2, post, ln_g, ln_b):
    m = post.shape[0]
    tr = _row_tile(st)
    full = lambda shape: pl.BlockSpec(shape, lambda i: (0, 0))
    return pl.pallas_call(
        _rwkv_post_kernel,
        grid=(m // tr,),
        in_specs=[pl.BlockSpec((2, tr, 512), lambda i: (0, i, 0)), pl.BlockSpec((tr, 1024), lambda i: (i, 0)),
                  full((1, 512)), full((1, 512)), full((512, 512))],
        out_specs=pl.BlockSpec((tr, 512), lambda i: (i, 0)),
        out_shape=jax.ShapeDtypeStruct((m, 512), BF16),
        compiler_params=_cp("parallel"),
        name="rwkv_post",
    )(y2, post, ln_g.reshape(1, 512), ln_b.reshape(1, 512), _block_diag(512, RWKV_HD, 1.0 / RWKV_HD))


def _merge_kernel(h_ref, o0_ref, o1_ref, o2_ref, o3_ref, wg_ref, wb_ref, out_ref):
    h = h_ref[...]
    acc = None
    for i, o_ref in enumerate((o0_ref, o1_ref, o2_ref, o3_ref)):
        term = jax.nn.sigmoid(_dot(h, wg_ref[i])) * _dot(o_ref[...], wb_ref[i])
        acc = term if acc is None else acc + term
    out_ref[...] = _bf(acc)


def _merge(st, h, branch_outs, wg, wb):
    m, d = h.shape
    tm = _mm_tile(st)
    tn = 512
    row = lambda w: pl.BlockSpec((tm, w), lambda j, i: (i, 0))
    return pl.pallas_call(
        _merge_kernel,
        grid=(d // tn, m // tm),
        in_specs=[row(d)] + [row(BRANCH_W)] * 4 + [pl.BlockSpec((4, d, tn), lambda j, i: (0, 0, j)),
                                                    pl.BlockSpec((4, BRANCH_W, tn), lambda j, i: (0, 0, j))],
        out_specs=pl.BlockSpec((tm, tn), lambda j, i: (i, j)),
        out_shape=jax.ShapeDtypeStruct((m, d), BF16),
        compiler_params=_cp("parallel", "parallel"),
        name="merge",
    )(h, *branch_outs, wg, wb)


def _wo_kernel(m_ref, w_ref, x_ref, mod_ref, o_ref, *, gate_idx):
    o_ref[...] = x_ref[...] + mod_ref[0, gate_idx:gate_idx + 1, :] * _dot(m_ref[...], w_ref[...])


def _out_proj(st, merged, w_o, x, mod_l, gate_idx):
    m, d = x.shape
    tm = _mm_tile(st)
    return pl.pallas_call(
        functools.partial(_wo_kernel, gate_idx=gate_idx),
        grid=(m // tm,),
        in_specs=[pl.BlockSpec((tm, d), lambda i: (i, 0)), pl.BlockSpec((d, d), lambda i: (0, 0)),
                  pl.BlockSpec((tm, d), lambda i: (i, 0)), pl.BlockSpec((1, 6, d), st.group_map(tm))],
        out_specs=pl.BlockSpec((tm, d), lambda i: (i, 0)),
        out_shape=jax.ShapeDtypeStruct((m, d), F32),
        compiler_params=_cp("parallel"),
        name="out_proj",
    )(merged, w_o, x, mod_l)


def _ffn_kernel(h_ref, w1_ref, w3_ref, w2_ref, x_ref, mod_ref, o_ref, acc_ref, *, gate_idx):
    f = pl.program_id(1)

    @pl.when(f == 0)
    def _():
        acc_ref[...] = jnp.zeros_like(acc_ref)

    h = h_ref[...]
    u = _silu(_dot(h, w1_ref[...])) * _dot(h, w3_ref[...])
    acc_ref[...] += _dot(_bf(u), w2_ref[...])

    @pl.when(f == pl.num_programs(1) - 1)
    def _():
        o_ref[...] = x_ref[...] + mod_ref[0, gate_idx:gate_idx + 1, :] * acc_ref[...]


def _ffn(st, h, w1, w3, w2, x, mod_l, gate_idx):
    m, d = x.shape
    dff = w1.shape[1]
    tm = _mm_tile(st)
    tf = 512
    return pl.pallas_call(
        functools.partial(_ffn_kernel, gate_idx=gate_idx),
        grid=(m // tm, dff // tf),
        in_specs=[pl.BlockSpec((tm, d), lambda i, f: (i, 0)),
                  pl.BlockSpec((d, tf), lambda i, f: (0, f)),
                  pl.BlockSpec((d, tf), lambda i, f: (0, f)),
                  pl.BlockSpec((tf, d), lambda i, f: (f, 0)),
                  pl.BlockSpec((tm, d), lambda i, f: (i, 0)),
                  pl.BlockSpec((1, 6, d), lambda i, f: st.group_map(tm)(i))],
        out_specs=pl.BlockSpec((tm, d), lambda i, f: (i, 0)),
        out_shape=jax.ShapeDtypeStruct((m, d), F32),
        scratch_shapes=[pltpu.VMEM((tm, d), F32)],
        compiler_params=_cp("parallel", "arbitrary"),
        name="ffn",
    )(h, w1, w3, w2, x, mod_l)


def _split_w_in(w_in):
    z = lambda w: jnp.zeros(w_in.shape[:2] + (w,), w_in.dtype)
    att = w_in[..., 0:768]
    ssd = jnp.concatenate([w_in[..., 768:2064], z(SSD_W - 1296)], axis=-1)
    rwkv = w_in[..., 2064:3856]
    gla = jnp.concatenate([w_in[..., 3856:4880], w_in[..., 4880:4896], z(LANE - 16), w_in[..., 4896:5408]], axis=-1)
    return tuple(_bf(w) for w in (att, ssd, rwkv, gla))


def _block(st, x, mod_l, p, rope_tables, ctx):
    h = _normmod(st, x, p['norm1'], mod_l, 1, 0)
    proj_att = _matmul(st, h, p['w_att'])
    proj_ssd = _matmul(st, h, p['w_ssd'])
    proj_rwkv = _matmul(st, h, p['w_rwkv'])
    proj_gla = _matmul(st, h, p['w_gla'])

    q, k = _att_prep(st, proj_att, p['q_norm'], p['k_norm'], rope_tables)
    v = proj_att[:, 640:768]
    if ctx is None:
        o_att = _attention(st, q, k, v)
        s_ssd = jnp.zeros((st.n, 2, SSD_HEADS, SSD_N, SSD_P), F32)
        s_rwkv = jnp.zeros((st.n, 2, RWKV_HEADS, RWKV_HD, RWKV_HD), F32)
        s_gla_t = jnp.zeros((st.n, 2, GLA_HEADS, GLA_DV, GLA_DK), F32)
    else:
        ctx_k, ctx_v, s_ssd, s_rwkv, s_gla = ctx
        p_len = ctx_k.shape[1]
        o_att = _attention(st, q, k, v, _bf(ctx_k.reshape(st.n, p_len, 128)), _bf(ctx_v.reshape(st.n, p_len, 128)))
        s_gla_t = jnp.swapaxes(s_gla, -1, -2)

    xbc, dt, ld = _ssd_prep(st, proj_ssd, p['ssd_conv_w'], p['ssd_conv_b'], p['ssd_dt_bias'], p['ssd_a_log'])
    y_ssd, new_ssd = _ssd_scan(st, xbc, dt, ld, s_ssd)
    o_ssd = _ssd_post(st, y_ssd, xbc, proj_ssd, p['ssd_d'], p['ssd_norm'])

    rin, lw, rpost = _rwkv_prep(st, proj_rwkv, p['rwkv_mu'], p['rwkv_w2'], p['rwkv_w0'], p['rwkv_a2'],
                                p['rwkv_a0'], p['rwkv_g2'], p['rwkv_kk'], p['rwkv_ka'], p['rwkv_rk'])
    y_rwkv, new_rwkv = _rwkv_scan(st, rin, lw, s_rwkv)
    o_rwkv = _rwkv_post(st, y_rwkv, rpost, p['rwkv_ln_g'], p['rwkv_ln_b'])

    g2p = jnp.pad(p['gla_g2'], ((0, 0), (0, LANE - p['gla_g2'].shape[1]), (0, 0)))
    y_gla, new_gla_t = _gla_scan(st, proj_gla, g2p, p['gla_gb'].reshape(2, 1, 256), s_gla_t)
    o_gla = _gla_post(st, y_gla, proj_gla, p['gla_norm'])

    merged = _merge(st, h, (o_att, o_ssd, o_rwkv, o_gla), p['w_gate'], p['w_branch'])
    x = _out_proj(st, merged, p['w_o'], x, mod_l, 2)
    h2 = _normmod(st, x, p['norm2'], mod_l, 4, 3)
    x = _ffn(st, h2, p['ffn_w1'], p['ffn_w3'], p['ffn_w2'], x, mod_l, 5)
    return x, (k, v, new_ssd, new_rwkv, jnp.swapaxes(new_gla_t, -1, -2))


def kernel(x_prompt, x_sample, cache_attn_k, cache_attn_v, state_ssd, state_rwkv, state_gla, c, c_ctx, w_mod, b_mod, norm1, norm2, w_in, q_norm, k_norm, ssd_conv_w, ssd_conv_b, ssd_dt_bias, ssd_a_log, ssd_d, ssd_norm, rwkv_mu, rwkv_w0, rwkv_w2, rwkv_a0, rwkv_a2, rwkv_g2, rwkv_kk, rwkv_ka, rwkv_rk, rwkv_ln_g, rwkv_ln_b, gla_g2, gla_gb, gla_norm, w_gate, w_branch, w_o, ffn_w1, ffn_w3, ffn_w2, final_norm):
    nb, seq, d = x_prompt.shape
    db, dseq, _ = x_sample.shape
    depth = w_in.shape[0]
    assert d == D_MODEL and seq % CHUNK == 0 and dseq % CHUNK == 0 and 1 + db <= MOD_ROWS
    ctx_st = _Stream(nb, seq, 0, False)
    lat_st = _Stream(db, dseq, 1, True)

    cond = jnp.concatenate([c_ctx[None], c, jnp.zeros((MOD_ROWS - 1 - db, d), F32)], axis=0)
    mod = _modulation(cond, w_mod, b_mod)

    w_att, w_ssd, w_rwkv, w_gla = _split_w_in(w_in)
    w_gate_b, w_branch_b, w_o_b = _bf(w_gate), _bf(w_branch), _bf(w_o)
    w1_b, w3_b, w2_b = _bf(ffn_w1), _bf(ffn_w3), _bf(ffn_w2)

    def params_at(l):
        return dict(norm1=norm1[l], norm2=norm2[l], w_att=w_att[l], w_ssd=w_ssd[l], w_rwkv=w_rwkv[l],
                    w_gla=w_gla[l], q_norm=q_norm[l], k_norm=k_norm[l],
                    ssd_conv_w=ssd_conv_w[l], ssd_conv_b=ssd_conv_b[l], ssd_dt_bias=ssd_dt_bias[l],
                    ssd_a_log=ssd_a_log[l], ssd_d=ssd_d[l], ssd_norm=ssd_norm[l],
                    rwkv_mu=rwkv_mu[l], rwkv_w0=rwkv_w0[l], rwkv_w2=rwkv_w2[l], rwkv_a0=rwkv_a0[l],
                    rwkv_a2=rwkv_a2[l], rwkv_g2=rwkv_g2[l], rwkv_kk=rwkv_kk[l], rwkv_ka=rwkv_ka[l],
                    rwkv_rk=rwkv_rk[l], rwkv_ln_g=rwkv_ln_g[l], rwkv_ln_b=rwkv_ln_b[l],
                    gla_g2=gla_g2[l], gla_gb=gla_gb[l], gla_norm=gla_norm[l],
                    w_gate=w_gate_b[l], w_branch=w_branch_b[l], w_o=w_o_b[l],
                    ffn_w1=w1_b[l], ffn_w3=w3_b[l], ffn_w2=w2_b[l])

    xp = x_prompt.reshape(nb * seq, d)
    new_k, new_v, new_ssd, new_rwkv, new_gla = [], [], [], [], []
    for l in range(depth):
        xp, (k_l, v_l, ssd_l, rwkv_l, gla_l) = _block(ctx_st, xp, mod[l], params_at(l), None, None)
        new_k.append(k_l.reshape(nb, seq, ATT_KV, HEAD_DIM))
        new_v.append(v_l.reshape(nb, seq, ATT_KV, HEAD_DIM))
        new_ssd.append(ssd_l)
        new_rwkv.append(rwkv_l)
        new_gla.append(gla_l)

    rope_tables = _rope_tables(dseq)
    xs = x_sample.reshape(db * dseq, d)
    for l in range(depth):
        ctx = (cache_attn_k[:, l], cache_attn_v[:, l], state_ssd[:, l], state_rwkv[:, l], state_gla[:, l])
        xs, _ = _block(lat_st, xs, mod[l], params_at(l), rope_tables, ctx)

    y_prompt = _final_norm(ctx_st, xp, final_norm).reshape(nb, seq, d)
    y_sample = _final_norm(lat_st, xs, final_norm).reshape(db, dseq, d)
    return (y_prompt, y_sample, jnp.stack(new_k, axis=1), jnp.stack(new_v, axis=1),
            jnp.stack(new_ssd, axis=1), jnp.stack(new_rwkv, axis=1), jnp.stack(new_gla, axis=1))
```

```python
import functools

import jax
import jax.numpy as jnp
import numpy as np
from jax import lax
from jax.experimental import pallas as pl
from jax.experimental.pallas import tpu as pltpu

F32 = jnp.float32
BF16 = jnp.bfloat16
HI = lax.Precision.HIGHEST

D_MODEL = 2048
GRID_W = 64
ATT_HEADS = 8
ATT_KV = 2
ATT_GROUP = ATT_HEADS // ATT_KV
HEAD_DIM = 64
ROPE_THETA = 10000.0
SSD_HEADS = 8
SSD_P = 64
SSD_N = 64
SSD_GROUPS = 2
RWKV_HEADS = 8
RWKV_HD = 64
RWKV_DECAY_SCALE = 0.6065306597126334
RWKV_LN_EPS = 64e-5
GLA_HEADS = 4
GLA_DK = 64
GLA_DV = 128
GLA_GATE_NORM = 16.0
CHUNK = 64
BRANCH_W = 512

ATT_W = 768
SSD_W = 1408
RWKV_W = 1792
GLA_W = 1664
LANE = 128
SUBLANE = 8
MOD_ROWS = 16
VMEM_LIMIT = 56 * 1024 * 1024
ROW_TILE = 256
MM_TILE = 512
SCAN_CHUNKS = 4


def _cp(*sem):
    return pltpu.CompilerParams(dimension_semantics=sem, vmem_limit_bytes=VMEM_LIMIT)


def _dot(a, b, prec=None):
    return jnp.dot(a, b, preferred_element_type=F32, precision=prec)


def _dot_nt(a, b, prec=None):
    return lax.dot_general(a, b, (((1,), (1,)), ((), ())), preferred_element_type=F32, precision=prec)


def _dot_tn(a, b, prec=None):
    return lax.dot_general(a, b, (((0,), (0,)), ((), ())), preferred_element_type=F32, precision=prec)


def _silu(x):
    return x * jax.nn.sigmoid(x)


def _softplus(x):
    return jnp.maximum(x, 0.0) + jnp.log1p(jnp.exp(-jnp.abs(x)))


def _bf(x):
    return x.astype(BF16)


def _lanes(pieces):
    return jnp.concatenate(pieces, axis=1)


def _rows(pieces):
    return jnp.concatenate(pieces, axis=0)


def _mod_kernel(c_ref, w_ref, b_ref, o_ref):
    c = c_ref[...]
    o_ref[0] = _dot(_bf(_silu(c)), _bf(w_ref[0])) + b_ref[0]


def _modulation(cond, w_mod, b_mod):
    nl, d, n6 = w_mod.shape
    tn = 1024
    out = pl.pallas_call(
        _mod_kernel,
        grid=(nl, n6 // tn),
        in_specs=[pl.BlockSpec((MOD_ROWS, d), lambda l, j: (0, 0)),
                  pl.BlockSpec((1, d, tn), lambda l, j: (l, 0, j)),
                  pl.BlockSpec((1, 1, tn), lambda l, j: (l, 0, j))],
        out_specs=pl.BlockSpec((1, MOD_ROWS, tn), lambda l, j: (l, 0, j)),
        out_shape=jax.ShapeDtypeStruct((nl, MOD_ROWS, n6), F32),
        compiler_params=_cp("parallel", "parallel"),
        name="modulation",
    )(cond, w_mod, b_mod.reshape(nl, 1, n6))
    return out.reshape(nl, MOD_ROWS, 6, d)


class _Stream:
    def __init__(self, n, t, group0, per_seq):
        self.n, self.t, self.group0, self.per_seq = n, t, group0, per_seq
        self.rows = n * t

    def group_map(self, tile):
        g0, per_seq, t = self.group0, self.per_seq, self.t
        if per_seq:
            return lambda i: (g0 + (i * tile) // t, 0, 0)
        return lambda i: (g0, 0, 0)


def _mm_tile(st):
    return min(MM_TILE, st.t)


def _row_tile(st):
    return min(ROW_TILE, st.t)


def _normmod_kernel(x_ref, g_ref, mod_ref, o_ref, *, sc_idx, sh_idx):
    x = x_ref[...]
    y = x * lax.rsqrt(jnp.mean(x * x, axis=-1, keepdims=True) + 1e-6) * g_ref[...]
    o_ref[...] = _bf(y * (1.0 + mod_ref[0, sc_idx:sc_idx + 1, :]) + mod_ref[0, sh_idx:sh_idx + 1, :])


def _normmod(st, x, gain, mod_l, sc_idx, sh_idx):
    m, d = x.shape
    tm = _mm_tile(st)
    return pl.pallas_call(
        functools.partial(_normmod_kernel, sc_idx=sc_idx, sh_idx=sh_idx),
        grid=(m // tm,),
        in_specs=[pl.BlockSpec((tm, d), lambda i: (i, 0)),
                  pl.BlockSpec((1, d), lambda i: (0, 0)),
                  pl.BlockSpec((1, 6, d), st.group_map(tm))],
        out_specs=pl.BlockSpec((tm, d), lambda i: (i, 0)),
        out_shape=jax.ShapeDtypeStruct((m, d), BF16),
        compiler_params=_cp("parallel"),
        name="normmod",
    )(x, gain.reshape(1, d), mod_l)


def _rms_kernel(x_ref, g_ref, o_ref):
    x = x_ref[...]
    o_ref[...] = x * lax.rsqrt(jnp.mean(x * x, axis=-1, keepdims=True) + 1e-6) * g_ref[...]


def _final_norm(st, x, gain):
    m, d = x.shape
    tm = _mm_tile(st)
    return pl.pallas_call(
        _rms_kernel,
        grid=(m // tm,),
        in_specs=[pl.BlockSpec((tm, d), lambda i: (i, 0)), pl.BlockSpec((1, d), lambda i: (0, 0))],
        out_specs=pl.BlockSpec((tm, d), lambda i: (i, 0)),
        out_shape=jax.ShapeDtypeStruct((m, d), F32),
        compiler_params=_cp("parallel"),
        name="final_norm",
    )(x, gain.reshape(1, d))


def _mm_kernel(x_ref, w_ref, o_ref):
    o_ref[...] = _dot(x_ref[...], w_ref[...]).astype(o_ref.dtype)


def _matmul(st, x, w, out_dtype=F32):
    m, k = x.shape
    n = w.shape[1]
    tm = _mm_tile(st)
    return pl.pallas_call(
        _mm_kernel,
        grid=(m // tm,),
        in_specs=[pl.BlockSpec((tm, k), lambda i: (i, 0)), pl.BlockSpec((k, n), lambda i: (0, 0))],
        out_specs=pl.BlockSpec((tm, n), lambda i: (i, 0)),
        out_shape=jax.ShapeDtypeStruct((m, n), out_dtype),
        compiler_params=_cp("parallel"),
        name="in_proj",
    )(x, w)


def _att_prep_kernel(*refs, rope):
    if rope:
        p_ref, qn_ref, kn_ref, bdq_ref, bdk_ref, cos_ref, sa_ref, sb_ref, q_out, k_out = refs
    else:
        p_ref, qn_ref, kn_ref, bdq_ref, bdk_ref, q_out, k_out = refs
    p = p_ref[...]
    aq = p[:, :512]
    ak = p[:, 512:640]
    q = aq * lax.rsqrt(_dot(aq * aq, bdq_ref[...], HI) + 1e-6) * qn_ref[...]
    k = ak * lax.rsqrt(_dot(ak * ak, bdk_ref[...], HI) + 1e-6) * kn_ref[...]
    if rope:
        c, sa, sb = cos_ref[...], sa_ref[...], sb_ref[...]
        k = k * c + pltpu.roll(k, LANE - 16, 1) * sa + pltpu.roll(k, 16, 1) * sb
        c4, sa4, sb4 = _lanes([c] * 4), _lanes([sa] * 4), _lanes([sb] * 4)
        q = q * c4 + pltpu.roll(q, 512 - 16, 1) * sa4 + pltpu.roll(q, 16, 1) * sb4
    q_out[...] = _bf(q * (HEAD_DIM ** -0.5))
    k_out[...] = k.astype(k_out.dtype)


def _block_diag(width, block, value):
    idx = np.arange(width) // block
    return jnp.asarray((idx[:, None] == idx[None, :]).astype(np.float32) * value)


def _att_prep(st, proj, q_norm, k_norm, rope_tables):
    m = proj.shape[0]
    tr = _row_tile(st)
    rope = rope_tables is not None
    full = lambda shape: pl.BlockSpec(shape, lambda i: (0, 0))
    in_specs = [pl.BlockSpec((tr, ATT_W), lambda i: (i, 0)), full((1, 512)), full((1, 128)),
                full((512, 512)), full((128, 128))]
    args = [proj, jnp.tile(q_norm, ATT_HEADS).reshape(1, 512), jnp.tile(k_norm, ATT_KV).reshape(1, 128),
            _block_diag(512, HEAD_DIM, 1.0 / HEAD_DIM), _block_diag(128, HEAD_DIM, 1.0 / HEAD_DIM)]
    if rope:
        tps = st.t // tr
        in_specs += [pl.BlockSpec((tr, LANE), lambda i: (i % tps, 0))] * 3
        args += list(rope_tables)
    return pl.pallas_call(
        functools.partial(_att_prep_kernel, rope=rope),
        grid=(m // tr,),
        in_specs=in_specs,
        out_specs=[pl.BlockSpec((tr, 512), lambda i: (i, 0)), pl.BlockSpec((tr, 128), lambda i: (i, 0))],
        out_shape=[jax.ShapeDtypeStruct((m, 512), BF16),
                   jax.ShapeDtypeStruct((m, 128), BF16 if rope else F32)],
        compiler_params=_cp("parallel"),
        name="att_prep",
    )(*args)


def _rope_tables(t):
    half = HEAD_DIM // 2
    nf = half // 2
    freqs = ROPE_THETA ** (-jnp.arange(nf, dtype=F32) / nf)
    tt = jnp.arange(t)
    ang_r = (tt // GRID_W).astype(F32)[:, None] * freqs[None, :]
    ang_c = (tt % GRID_W).astype(F32)[:, None] * freqs[None, :]
    zero = jnp.zeros_like(ang_r)
    cos = jnp.concatenate([jnp.cos(ang_r)] * 2 + [jnp.cos(ang_c)] * 2, axis=1)
    sa = jnp.concatenate([-jnp.sin(ang_r), zero, -jnp.sin(ang_c), zero], axis=1)
    sb = jnp.concatenate([zero, jnp.sin(ang_r), zero, jnp.sin(ang_c)], axis=1)
    return tuple(jnp.tile(a, (1, LANE // HEAD_DIM)) for a in (cos, sa, sb))


def _attn_kernel(*refs, n_src, tq):
    q_ref, o_ref = refs[0], refs[-1]
    q = q_ref[...]
    outs = []
    for kv in range(ATT_KV):
        lo = kv * HEAD_DIM
        qs = _rows([q[:, (kv * ATT_GROUP + g) * HEAD_DIM:(kv * ATT_GROUP + g + 1) * HEAD_DIM]
                    for g in range(ATT_GROUP)])
        ks = [_bf(refs[1 + 2 * s][:, lo:lo + HEAD_DIM]) for s in range(n_src)]
        vs = [_bf(refs[2 + 2 * s][:, lo:lo + HEAD_DIM]) for s in range(n_src)]
        ss = [_dot_nt(qs, k) for k in ks]
        mx = functools.reduce(jnp.maximum, [jnp.max(s, axis=-1, keepdims=True) for s in ss])
        ps = [jnp.exp(s - mx) for s in ss]
        den = functools.reduce(jnp.add, [jnp.sum(p, axis=-1, keepdims=True) for p in ps])
        o = functools.reduce(jnp.add, [_dot(_bf(p), v) for p, v in zip(ps, vs)]) / den
        outs += [o[g * tq:(g + 1) * tq] for g in range(ATT_GROUP)]
    o_ref[...] = _bf(_lanes(outs))


def _attention(st, q, k, v, cache_k=None, cache_v=None):
    tq = min(128, st.t)
    nq = st.t // tq
    t = st.t
    in_specs = [pl.BlockSpec((tq, 512), lambda b, i: (b * nq + i, 0)),
                pl.BlockSpec((t, 128), lambda b, i: (b, 0)),
                pl.BlockSpec((t, 128), lambda b, i: (b, 0))]
    args = [q, k, v]
    n_src = 1
    if cache_k is not None:
        p = cache_k.shape[1]
        in_specs += [pl.BlockSpec((None, p, 128), lambda b, i: (b, 0, 0))] * 2
        args += [cache_k, cache_v]
        n_src = 2
    return pl.pallas_call(
        functools.partial(_attn_kernel, n_src=n_src, tq=tq),
        grid=(st.n, nq),
        in_specs=in_specs,
        out_specs=pl.BlockSpec((tq, 512), lambda b, i: (b * nq + i, 0)),
        out_shape=jax.ShapeDtypeStruct((st.rows, 512), BF16),
        compiler_params=_cp("parallel", "parallel"),
        name="attention",
    )(*args)


def _halo_specs(st, width):
    tr = _row_tile(st)
    per8 = tr // SUBLANE
    last8 = st.rows // SUBLANE - 1
    return [pl.BlockSpec((tr, width), lambda i: (i, 0)),
            pl.BlockSpec((SUBLANE, width), lambda i: (jnp.maximum(i * per8 - 1, 0), 0)),
            pl.BlockSpec((SUBLANE, width), lambda i: (jnp.minimum((i + 1) * per8, last8), 0))]


def _neighbours(cur, prev8, next8, tiles_per_seq):
    tr = cur.shape[0]
    j = pl.program_id(0) % tiles_per_seq
    pr = jnp.where(j != 0, prev8[SUBLANE - 1:SUBLANE], 0.0)
    nx = jnp.where(j != tiles_per_seq - 1, next8[0:1], 0.0)
    row = lax.broadcasted_iota(jnp.int32, cur.shape, 0)
    x_prev = jnp.where(row == 0, pr, pltpu.roll(cur, 1, 0))
    x_next = jnp.where(row == tr - 1, nx, pltpu.roll(cur, tr - 1, 0))
    return x_prev, x_next


def _chunk_masks(fwd):
    ri = lax.broadcasted_iota(jnp.int32, (CHUNK, CHUNK), 0)
    ci = lax.broadcasted_iota(jnp.int32, (CHUNK, CHUNK), 1)
    d = jnp.where(fwd, ri - ci, ci - ri)
    return d >= 0, d > 0


def _scan_specs(st, width):
    nc = st.t // CHUNK

    def row_map(z, b, c):
        return (b * nc + c + z * (nc - 1 - 2 * c), 0)

    def out_map(z, b, c):
        return (z, b * nc + c + z * (nc - 1 - 2 * c), 0)

    return nc, row_map, out_map


def _ssd_prep_kernel(cur_ref, prev_ref, next_ref, cw_ref, cb_ref, dtb_ref, alog_ref,
                     xbc_out, dt_out, ld_out, *, tiles_per_seq):
    cur = cur_ref[...]
    xc = cur[:, 512:1280]
    x_prev, x_next = _neighbours(xc, prev_ref[:, 512:1280], next_ref[:, 512:1280], tiles_per_seq)
    cw = cw_ref[...]
    conv = cb_ref[...] + x_prev * cw[0:1] + xc * cw[1:2] + x_next * cw[2:3]
    xbc_out[...] = _silu(conv)
    dt = _softplus(cur[:, 1280:1408] + dtb_ref[...])
    dt_out[...] = dt
    ld_out[...] = dt * -jnp.exp(alog_ref[...])


def _pad_lanes(v, width=LANE):
    v = v.reshape(1, -1)
    return jnp.pad(v, ((0, 0), (0, width - v.shape[1])))


def _ssd_prep(st, proj, conv_w, conv_b, dt_bias, a_log):
    m = proj.shape[0]
    tr = _row_tile(st)
    full = lambda shape: pl.BlockSpec(shape, lambda i: (0, 0))
    return pl.pallas_call(
        functools.partial(_ssd_prep_kernel, tiles_per_seq=st.t // tr),
        grid=(m // tr,),
        in_specs=_halo_specs(st, SSD_W) + [full((3, 768)), full((1, 768)), full((1, LANE)), full((1, LANE))],
        out_specs=[pl.BlockSpec((tr, 768), lambda i: (i, 0)), pl.BlockSpec((tr, LANE), lambda i: (i, 0)),
                   pl.BlockSpec((tr, LANE), lambda i: (i, 0))],
        out_shape=[jax.ShapeDtypeStruct((m, 768), F32), jax.ShapeDtypeStruct((m, LANE), F32),
                   jax.ShapeDtypeStruct((m, LANE), F32)],
        compiler_params=_cp("parallel"),
        name="ssd_prep",
    )(proj, proj, proj, conv_w.T, conv_b.reshape(1, 768), _pad_lanes(dt_bias), _pad_lanes(a_log))


def _ssd_scan_kernel(xbc_ref, dt_ref, ld_ref, s0_ref, y_ref, s_ref):
    z = pl.program_id(0)

    @pl.when(pl.program_id(2) == 0)
    def _():
        s_ref[...] = s0_ref[...]

    fwd = z == 0
    incl, _ = _chunk_masks(fwd)
    tri = incl.astype(F32)
    dt = dt_ref[...]
    cum = _dot(tri, ld_ref[...], HI)
    pick = (lax.broadcasted_iota(jnp.int32, (2 * SSD_HEADS, LANE), 0)
            == lax.broadcasted_iota(jnp.int32, (2 * SSD_HEADS, LANE), 1)).astype(F32)
    cum_t = _dot_nt(pick, cum, HI)
    dt_t = _dot_nt(pick, dt, HI)
    xbc = xbc_ref[...]
    outs = []
    for g in range(SSD_GROUPS):
        bg = xbc[:, 512 + g * SSD_N:512 + (g + 1) * SSD_N]
        cg = xbc[:, 640 + g * SSD_N:640 + (g + 1) * SSD_N]
        cb = _dot_nt(_bf(cg), _bf(bg))
        for hh in range(SSD_HEADS // SSD_GROUPS):
            h = g * (SSD_HEADS // SSD_GROUPS) + hh
            hb = SSD_HEADS + h
            gcol = jnp.where(fwd, cum[:, h:h + 1], cum[:, hb:hb + 1])
            grow = jnp.where(fwd, cum_t[h:h + 1], cum_t[hb:hb + 1])
            dtcol = jnp.where(fwd, dt[:, h:h + 1], dt[:, hb:hb + 1])
            dtrow = jnp.where(fwd, dt_t[h:h + 1], dt_t[hb:hb + 1])
            dec = jnp.exp(jnp.where(incl, gcol - grow, -jnp.inf))
            att = cb * dec * dtrow
            xs = _bf(xbc[:, h * SSD_P:(h + 1) * SSD_P])
            s = s_ref[h]
            outs.append(_dot(_bf(att), xs) + jnp.exp(gcol) * _dot(_bf(cg), _bf(s)))
            glast = jnp.where(fwd, gcol[CHUNK - 1:CHUNK], gcol[0:1])
            kd = bg * (dtcol * jnp.exp(glast - gcol))
            s_ref[h] = jnp.exp(glast) * s + _dot_tn(_bf(kd), xs)
    y_ref[...] = _lanes(outs)


def _ssd_scan(st, xbc, dt, ld, s0):
    nc, row_map, out_map = _scan_specs(st, 768)
    state_spec = pl.BlockSpec((None, None, SSD_HEADS, SSD_N, SSD_P), lambda z, b, c: (b, z, 0, 0, 0))
    return pl.pallas_call(
        _ssd_scan_kernel,
        grid=(2, st.n, nc),
        in_specs=[pl.BlockSpec((CHUNK, 768), row_map), pl.BlockSpec((CHUNK, LANE), row_map),
                  pl.BlockSpec((CHUNK, LANE), row_map), state_spec],
        out_specs=[pl.BlockSpec((None, CHUNK, 512), out_map), state_spec],
        out_shape=[jax.ShapeDtypeStruct((2, st.rows, 512), F32),
                   jax.ShapeDtypeStruct((st.n, 2, SSD_HEADS, SSD_N, SSD_P), F32)],
        compiler_params=_cp("parallel", "parallel", "arbitrary"),
        name="ssd_scan",
    )(xbc, dt, ld, s0)


def _ssd_post_kernel(y_ref, xbc_ref, p_ref, d_ref, g_ref, o_ref):
    y = y_ref[0] + y_ref[1] + d_ref[...] * xbc_ref[:, :512]
    y = y * _silu(p_ref[:, :512])
    o_ref[...] = _bf(y * lax.rsqrt(jnp.mean(y * y, axis=-1, keepdims=True) + 1e-6) * g_ref[...])


def _ssd_post(st, y2, xbc, proj, ssd_d, ssd_norm):
    m = proj.shape[0]
    tr = _row_tile(st)
    full = lambda shape: pl.BlockSpec(shape, lambda i: (0, 0))
    return pl.pallas_call(
        _ssd_post_kernel,
        grid=(m // tr,),
        in_specs=[pl.BlockSpec((2, tr, 512), lambda i: (0, i, 0)), pl.BlockSpec((tr, 768), lambda i: (i, 0)),
                  pl.BlockSpec((tr, SSD_W), lambda i: (i, 0)), full((1, 512)), full((1, 512))],
        out_specs=pl.BlockSpec((tr, 512), lambda i: (i, 0)),
        out_shape=jax.ShapeDtypeStruct((m, 512), BF16),
        compiler_params=_cp("parallel"),
        name="ssd_post",
    )(y2, xbc, proj, jnp.repeat(ssd_d, SSD_P).reshape(1, 512), ssd_norm.reshape(1, 512))


def _gla_scan_kernel(p_ref, g2_ref, gb_ref, s0_ref, y_ref, s_ref):
    z = pl.program_id(0)

    @pl.when(pl.program_id(2) == 0)
    def _():
        s_ref[...] = s0_ref[...]

    fwd = z == 0
    incl, _ = _chunk_masks(fwd)
    p = p_ref[...]
    logit = _dot(p[:, 1024:1152], g2_ref[...], HI) + gb_ref[...]
    log_a = -_softplus(-logit) * (1.0 / GLA_GATE_NORM)
    cum = _dot(incl.astype(F32), log_a, HI)
    mid = cum[CHUNK // 2:CHUNK // 2 + 1]
    glast = jnp.where(fwd, cum[CHUNK - 1:CHUNK], cum[0:1])
    q = p[:, 0:256] * (GLA_DK ** -0.5)
    k = p[:, 256:512]
    q_mid = _bf(q * jnp.exp(cum - mid))
    k_mid = _bf(k * jnp.exp(mid - cum))
    q_in = _bf(q * jnp.exp(cum))
    k_out = _bf(k * jnp.exp(glast - cum))
    e_last = jnp.exp(glast)
    outs = []
    for h in range(GLA_HEADS):
        sl = slice(h * GLA_DK, (h + 1) * GLA_DK)
        v = _bf(p[:, 512 + h * GLA_DV:512 + (h + 1) * GLA_DV])
        att = jnp.where(incl, _dot_nt(q_mid[:, sl], k_mid[:, sl]), 0.0)
        s_t = s_ref[h]
        outs.append(_dot(_bf(att), v) + _dot_nt(q_in[:, sl], _bf(s_t)))
        s_ref[h] = s_t * e_last[:, sl] + _dot_tn(v, k_out[:, sl])
    y_ref[...] = _lanes(outs)


def _gla_scan(st, proj, g2p, gb, s0_t):
    nc, row_map, out_map = _scan_specs(st, GLA_W)
    state_spec = pl.BlockSpec((None, None, GLA_HEADS, GLA_DV, GLA_DK), lambda z, b, c: (b, z, 0, 0, 0))
    return pl.pallas_call(
        _gla_scan_kernel,
        grid=(2, st.n, nc),
        in_specs=[pl.BlockSpec((CHUNK, GLA_W), row_map),
                  pl.BlockSpec((None, LANE, 256), lambda z, b, c: (z, 0, 0)),
                  pl.BlockSpec((None, 1, 256), lambda z, b, c: (z, 0, 0)), state_spec],
        out_specs=[pl.BlockSpec((None, CHUNK, 512), out_map), state_spec],
        out_shape=[jax.ShapeDtypeStruct((2, st.rows, 512), F32),
                   jax.ShapeDtypeStruct((st.n, 2, GLA_HEADS, GLA_DV, GLA_DK), F32)],
        compiler_params=_cp("parallel", "parallel", "arbitrary"),
        name="gla_scan",
    )(proj, g2p, gb, s0_t)


def _gla_post_kernel(y_ref, p_ref, g_ref, o_ref):
    o = y_ref[0] + y_ref[1]
    gate = _silu(p_ref[:, 1152:1664])
    outs = []
    for h in range(GLA_HEADS):
        oh = o[:, h * GLA_DV:(h + 1) * GLA_DV]
        outs.append(oh * lax.rsqrt(jnp.mean(oh * oh, axis=-1, keepdims=True) + 1e-6) * g_ref[...])
    o_ref[...] = _bf(_lanes(outs) * gate)


def _gla_post(st, y2, proj, gla_norm):
    m = proj.shape[0]
    tr = _row_tile(st)
    return pl.pallas_call(
        _gla_post_kernel,
        grid=(m // tr,),
        in_specs=[pl.BlockSpec((2, tr, 512), lambda i: (0, i, 0)), pl.BlockSpec((tr, GLA_W), lambda i: (i, 0)),
                  pl.BlockSpec((1, GLA_DV), lambda i: (0, 0))],
        out_specs=pl.BlockSpec((tr, 512), lambda i: (i, 0)),
        out_shape=jax.ShapeDtypeStruct((m, 512), BF16),
        compiler_params=_cp("parallel"),
        name="gla_post",
    )(y2, proj, gla_norm.reshape(1, GLA_DV))


def _rwkv_prep_kernel(cur_ref, prev_ref, next_ref, mu_ref, w2_ref, w0_ref, a2_ref, a0_ref, g2_ref,
                      kkw_ref, ka_ref, rk_ref, bd_ref, in_out, lw_out, post_out, *, tiles_per_seq):
    cur = cur_ref[...]
    x_prev, x_next = _neighbours(cur, prev_ref[...], next_ref[...], tiles_per_seq)
    blk = cur + (0.5 * (x_prev + x_next) - cur) * mu_ref[...]
    r, k, v = blk[:, 0:512], blk[:, 512:1024], blk[:, 1024:1536]
    w_logit = w0_ref[...] + _dot(jnp.tanh(blk[:, 1536:1600]), w2_ref[...], HI)
    lw_out[...] = -RWKV_DECAY_SCALE * jax.nn.sigmoid(w_logit)
    a = jax.nn.sigmoid(a0_ref[...] + _dot(blk[:, 1600:1664], a2_ref[...], HI))
    g = _dot(jax.nn.sigmoid(blk[:, 1664:1792]), g2_ref[...], HI)
    bd = bd_ref[...]
    kk = k * kkw_ref[...]
    kk = kk * lax.rsqrt(_dot(kk * kk, bd, HI) + 1e-12)
    k2 = k * (1.0 + (a - 1.0) * ka_ref[...])
    in_out[:, 0:512] = r
    in_out[:, 512:1024] = k2
    in_out[:, 1024:1536] = v
    in_out[:, 1536:2048] = -kk
    in_out[:, 2048:2560] = kk * a
    post_out[:, 0:512] = g
    post_out[:, 512:1024] = _dot(r * k2 * rk_ref[...], bd, HI) * v


def _rwkv_prep(st, proj, mu, w2, w0, a2, a0, g2, kkw, ka, rk):
    m = proj.shape[0]
    tr = _row_tile(st)
    full = lambda shape: pl.BlockSpec(shape, lambda i: (0, 0))
    row = lambda w: pl.BlockSpec((tr, w), lambda i: (i, 0))
    vec = lambda a: a.reshape(1, -1)
    return pl.pallas_call(
        functools.partial(_rwkv_prep_kernel, tiles_per_seq=st.t // tr),
        grid=(m // tr,),
        in_specs=_halo_specs(st, RWKV_W) + [full((1, RWKV_W)), full((64, 1024)), full((1, 1024)),
                                            full((64, 512)), full((1, 512)), full((128, 512)),
                                            full((1, 512)), full((1, 512)), full((1, 512)), full((512, 512))],
        out_specs=[row(2560), row(1024), row(1024)],
        out_shape=[jax.ShapeDtypeStruct((m, 2560), F32), jax.ShapeDtypeStruct((m, 1024), F32),
                   jax.ShapeDtypeStruct((m, 1024), F32)],
        compiler_params=_cp("parallel"),
        name="rwkv_prep",
    )(proj, proj, proj, vec(mu), jnp.concatenate([w2[0], w2[1]], axis=1), vec(w0), a2, vec(a0), g2,
      vec(kkw), vec(ka), vec(rk), _block_diag(512, RWKV_HD, 1.0))


def _dir_masks(z):
    ri = lax.broadcasted_iota(jnp.int32, (CHUNK, CHUNK), 0)
    ci = lax.broadcasted_iota(jnp.int32, (CHUNK, CHUNK), 1)
    return ((ci <= ri), (ci < ri)) if z == 0 else ((ci >= ri), (ci > ri))


def _rwkv_chunk(x_ref, lw_ref, rows, z):
    incl, strict = _dir_masks(z)
    lw = lw_ref[rows, :]
    cum = _dot(incl.astype(F32), lw, HI)
    cum_x = cum - lw
    mid = cum[CHUNK // 2:CHUNK // 2 + 1]
    glast = cum[CHUNK - 1:CHUNK] if z == 0 else cum[0:1]
    x = x_ref[rows, :]
    r, k, v, al, be = x[:, 0:512], x[:, 512:1024], x[:, 1024:1536], x[:, 1536:2048], x[:, 2048:2560]
    e_mid, e_nmid = jnp.exp(cum - mid), jnp.exp(mid - cum)
    e_out = jnp.exp(glast - cum)
    return dict(incl=incl, strict=strict, v=v,
                r_mid=_bf(r * e_mid), a_mid=_bf(al * jnp.exp(cum_x - mid)),
                b_mid=_bf(be * e_nmid), k_mid=_bf(k * e_nmid),
                r_in=_bf(r * jnp.exp(cum)), a_in=al * jnp.exp(cum_x),
                b_out=_bf(be * e_out), k_out=_bf(k * e_out), e_last=jnp.exp(glast))


_HEAD_SLICES = [slice(h * RWKV_HD, (h + 1) * RWKV_HD) for h in range(RWKV_HEADS)]


def _rwkv_state_free(chunks, hooks):
    hooks = list(hooks)

    def run_hook():
        if hooks:
            hooks.pop(0)()

    chains = [(c, sl) for c in chunks for sl in _HEAD_SLICES]
    run_hook()
    pair = [_dot_nt(_rows([c['a_mid'][:, sl], c['r_mid'][:, sl]]), _rows([c['b_mid'][:, sl], c['k_mid'][:, sl]]))
            for c, sl in chains]
    half = CHUNK
    a_ab = [_bf(jnp.where(c['strict'], p[:half, :half], 0.0)) for (c, _), p in zip(chains, pair)]
    a_ak = [_bf(jnp.where(c['strict'], p[:half, half:], 0.0)) for (c, _), p in zip(chains, pair)]
    a_rb = [_bf(jnp.where(c['incl'], p[half:, :half], 0.0)) for (c, _), p in zip(chains, pair)]
    a_rk = [_bf(jnp.where(c['incl'], p[half:, half:], 0.0)) for (c, _), p in zip(chains, pair)]
    vb = [_bf(c['v'][:, sl]) for c, sl in chains]
    av = [_dot(a, v) for a, v in zip(a_ak, vb)]
    o0 = [_dot(a, v) for a, v in zip(a_rk, vb)]
    sol = [_lanes([c['a_in'][:, sl], x]) for (c, sl), x in zip(chains, av)]
    powr = a_ab
    for it in range(6):
        sol = [s + _dot(p, _bf(s)) for p, s in zip(powr, sol)]
        if it in (1, 3):
            run_hook()
        if it < 5:
            powr = [_bf(_dot(p, p)) for p in powr]
    while hooks:
        run_hook()
    return [dict(w=_bf(s[:, :RWKV_HD]), u0=s[:, RWKV_HD:], o0=o, a_rb=a, v=c['v'][:, sl], r_in=c['r_in'][:, sl],
                 b_out=c['b_out'][:, sl], k_out=c['k_out'][:, sl], e_last=c['e_last'][:, sl])
            for (c, sl), s, o, a in zip(chains, sol, o0, a_rb)]


def _rwkv_state_stages(res, state, write_out):
    box = {}

    def read_state():
        box['su'] = [_dot_nt(_rows([c['w'], c['r_in']]), _bf(s)) for c, s in zip(res, state)]

    def update_state():
        box['u'] = [c['u0'] + su[:CHUNK] for c, su in zip(res, box['su'])]
        state[:] = [s * c['e_last'] + _dot_tn(_bf(_rows([u, c['v']])), _rows([c['b_out'], c['k_out']]))
                    for c, s, u in zip(res, state, box['u'])]

    def emit():
        write_out([su[CHUNK:] + _dot(c['a_rb'], _bf(u)) + c['o0'] for c, su, u in zip(res, box['su'], box['u'])])

    return [read_state, update_state, emit]


def _rwkv_scan_kernel(xf_ref, xb_ref, lwf_ref, lwb_ref, s0_ref, yf_ref, yb_ref, s_ref, *, ncb):
    @pl.when(pl.program_id(1) == 0)
    def _():
        s_ref[...] = s0_ref[...]

    state = [s_ref[z, h] for z in range(2) for h in range(RWKV_HEADS)]

    def chunk_rows(step, z):
        cc = step if z == 0 else ncb - 1 - step
        return slice(cc * CHUNK, (cc + 1) * CHUNK)

    def pair_inputs(step):
        return [_rwkv_chunk(xf_ref, lwf_ref, chunk_rows(step, 0), 0),
                _rwkv_chunk(xb_ref, lwb_ref, chunk_rows(step, 1), 1)]

    def writer(step):
        def write_out(outs):
            yf_ref[chunk_rows(step, 0), :] = _lanes(outs[:RWKV_HEADS])
            yb_ref[chunk_rows(step, 1), :] = _lanes(outs[RWKV_HEADS:])
        return write_out

    res = _rwkv_state_free(pair_inputs(0), [])
    for step in range(ncb):
        hooks = _rwkv_state_stages(res, state, writer(step))
        if step + 1 < ncb:
            res = _rwkv_state_free(pair_inputs(step + 1), hooks)
        else:
            for hook in hooks:
                hook()
    for z in range(2):
        for h in range(RWKV_HEADS):
            s_ref[z, h] = state[z * RWKV_HEADS + h]


def _rwkv_scan(st, xin, lw, s0):
    ncb = min(SCAN_CHUNKS, st.t // CHUNK)
    rb = ncb * CHUNK
    nblk = st.t // rb
    fwd = lambda b, j: (b * nblk + j, 0)
    bwd = lambda b, j: (b * nblk + nblk - 1 - j, 0)
    state_spec = pl.BlockSpec((None, 2, RWKV_HEADS, RWKV_HD, RWKV_HD), lambda b, j: (b, 0, 0, 0, 0))
    return pl.pallas_call(
        functools.partial(_rwkv_scan_kernel, ncb=ncb),
        grid=(st.n, nblk),
        in_specs=[pl.BlockSpec((rb, 2560), fwd), pl.BlockSpec((rb, 2560), bwd),
                  pl.BlockSpec((rb, 512), fwd), pl.BlockSpec((rb, 512), lambda b, j: (bwd(b, j)[0], 1)),
                  state_spec],
        out_specs=[pl.BlockSpec((rb, 512), fwd), pl.BlockSpec((rb, 512), bwd), state_spec],
        out_shape=[jax.ShapeDtypeStruct((st.rows, 512), F32), jax.ShapeDtypeStruct((st.rows, 512), F32),
                   jax.ShapeDtypeStruct((st.n, 2, RWKV_HEADS, RWKV_HD, RWKV_HD), F32)],
        compiler_params=_cp("parallel", "arbitrary"),
        name="rwkv_scan",
    )(xin, xin, lw, lw, s0)


def _rwkv_post_kernel(yf_ref, yb_ref, post_ref, lng_ref, lnb_ref, bd_ref, o_ref):
    o = yf_ref[...] + yb_ref[...]
    bd = bd_ref[...]
    d = o - _dot(o, bd, HI)
    o = d * lax.rsqrt(_dot(d * d, bd, HI) + RWKV_LN_EPS) * lng_ref[...] + lnb_ref[...]
    o_ref[...] = _bf((o + post_ref[:, 512:1024]) * post_ref[:, 0:512])


def _rwkv_post(st, y_f, y_b, post, ln_g, ln_b):
    m = post.shape[0]
    tr = _row_tile(st)
    full = lambda shape: pl.BlockSpec(shape, lambda i: (0, 0))
    row = lambda w: pl.BlockSpec((tr, w), lambda i: (i, 0))
    return pl.pallas_call(
        _rwkv_post_kernel,
        grid=(m // tr,),
        in_specs=[row(512), row(512), row(1024), full((1, 512)), full((1, 512)), full((512, 512))],
        out_specs=row(512),
        out_shape=jax.ShapeDtypeStruct((m, 512), BF16),
        compiler_params=_cp("parallel"),
        name="rwkv_post",
    )(y_f, y_b, post, ln_g.reshape(1, 512), ln_b.reshape(1, 512), _block_diag(512, RWKV_HD, 1.0 / RWKV_HD))


def _merge_kernel(h_ref, o0_ref, o1_ref, o2_ref, o3_ref, wg_ref, wb_ref, out_ref):
    h = h_ref[...]
    acc = None
    for i, o_ref in enumerate((o0_ref, o1_ref, o2_ref, o3_ref)):
        term = jax.nn.sigmoid(_dot(h, wg_ref[i])) * _dot(o_ref[...], wb_ref[i])
        acc = term if acc is None else acc + term
    out_ref[...] = _bf(acc)


def _merge(st, h, branch_outs, wg, wb):
    m, d = h.shape
    tm = _mm_tile(st)
    tn = 512
    row = lambda w: pl.BlockSpec((tm, w), lambda j, i: (i, 0))
    return pl.pallas_call(
        _merge_kernel,
        grid=(d // tn, m // tm),
        in_specs=[row(d)] + [row(BRANCH_W)] * 4 + [pl.BlockSpec((4, d, tn), lambda j, i: (0, 0, j)),
                                                    pl.BlockSpec((4, BRANCH_W, tn), lambda j, i: (0, 0, j))],
        out_specs=pl.BlockSpec((tm, tn), lambda j, i: (i, j)),
        out_shape=jax.ShapeDtypeStruct((m, d), BF16),
        compiler_params=_cp("parallel", "parallel"),
        name="merge",
    )(h, *branch_outs, wg, wb)


def _wo_kernel(m_ref, w_ref, x_ref, mod_ref, o_ref, *, gate_idx):
    o_ref[...] = x_ref[...] + mod_ref[0, gate_idx:gate_idx + 1, :] * _dot(m_ref[...], w_ref[...])


def _out_proj(st, merged, w_o, x, mod_l, gate_idx):
    m, d = x.shape
    tm = _mm_tile(st)
    return pl.pallas_call(
        functools.partial(_wo_kernel, gate_idx=gate_idx),
        grid=(m // tm,),
        in_specs=[pl.BlockSpec((tm, d), lambda i: (i, 0)), pl.BlockSpec((d, d), lambda i: (0, 0)),
                  pl.BlockSpec((tm, d), lambda i: (i, 0)), pl.BlockSpec((1, 6, d), st.group_map(tm))],
        out_specs=pl.BlockSpec((tm, d), lambda i: (i, 0)),
        out_shape=jax.ShapeDtypeStruct((m, d), F32),
        compiler_params=_cp("parallel"),
        name="out_proj",
    )(merged, w_o, x, mod_l)


def _ffn_kernel(h_ref, w1_ref, w3_ref, w2_ref, x_ref, mod_ref, o_ref, acc_ref, *, gate_idx):
    f = pl.program_id(1)

    @pl.when(f == 0)
    def _():
        acc_ref[...] = jnp.zeros_like(acc_ref)

    h = h_ref[...]
    u = _silu(_dot(h, w1_ref[...])) * _dot(h, w3_ref[...])
    acc_ref[...] += _dot(_bf(u), w2_ref[...])

    @pl.when(f == pl.num_programs(1) - 1)
    def _():
        o_ref[...] = x_ref[...] + mod_ref[0, gate_idx:gate_idx + 1, :] * acc_ref[...]


def _ffn(st, h, w1, w3, w2, x, mod_l, gate_idx):
    m, d = x.shape
    dff = w1.shape[1]
    tm = _mm_tile(st)
    tf = 512
    return pl.pallas_call(
        functools.partial(_ffn_kernel, gate_idx=gate_idx),
        grid=(m // tm, dff // tf),
        in_specs=[pl.BlockSpec((tm, d), lambda i, f: (i, 0)),
                  pl.BlockSpec((d, tf), lambda i, f: (0, f)),
                  pl.BlockSpec((d, tf), lambda i, f: (0, f)),
                  pl.BlockSpec((tf, d), lambda i, f: (f, 0)),
                  pl.BlockSpec((tm, d), lambda i, f: (i, 0)),
                  pl.BlockSpec((1, 6, d), lambda i, f: st.group_map(tm)(i))],
        out_specs=pl.BlockSpec((tm, d), lambda i, f: (i, 0)),
        out_shape=jax.ShapeDtypeStruct((m, d), F32),
        scratch_shapes=[pltpu.VMEM((tm, d), F32)],
        compiler_params=_cp("parallel", "arbitrary"),
        name="ffn",
    )(h, w1, w3, w2, x, mod_l)


def _split_w_in(w_in):
    z = lambda w: jnp.zeros(w_in.shape[:2] + (w,), w_in.dtype)
    att = w_in[..., 0:768]
    ssd = jnp.concatenate([w_in[..., 768:2064], z(SSD_W - 1296)], axis=-1)
    rwkv = w_in[..., 2064:3856]
    gla = jnp.concatenate([w_in[..., 3856:4880], w_in[..., 4880:4896], z(LANE - 16), w_in[..., 4896:5408]], axis=-1)
    return tuple(_bf(w) for w in (att, ssd, rwkv, gla))


def _block(st, x, mod_l, p, rope_tables, ctx):
    h = _normmod(st, x, p['norm1'], mod_l, 1, 0)
    proj_att = _matmul(st, h, p['w_att'])
    proj_ssd = _matmul(st, h, p['w_ssd'])
    proj_rwkv = _matmul(st, h, p['w_rwkv'])
    proj_gla = _matmul(st, h, p['w_gla'])

    q, k = _att_prep(st, proj_att, p['q_norm'], p['k_norm'], rope_tables)
    v = proj_att[:, 640:768]
    if ctx is None:
        o_att = _attention(st, q, k, v)
        s_ssd = jnp.zeros((st.n, 2, SSD_HEADS, SSD_N, SSD_P), F32)
        s_rwkv = jnp.zeros((st.n, 2, RWKV_HEADS, RWKV_HD, RWKV_HD), F32)
        s_gla_t = jnp.zeros((st.n, 2, GLA_HEADS, GLA_DV, GLA_DK), F32)
    else:
        ctx_k, ctx_v, s_ssd, s_rwkv, s_gla = ctx
        p_len = ctx_k.shape[1]
        o_att = _attention(st, q, k, v, _bf(ctx_k.reshape(st.n, p_len, 128)), _bf(ctx_v.reshape(st.n, p_len, 128)))
        s_gla_t = jnp.swapaxes(s_gla, -1, -2)

    xbc, dt, ld = _ssd_prep(st, proj_ssd, p['ssd_conv_w'], p['ssd_conv_b'], p['ssd_dt_bias'], p['ssd_a_log'])
    y_ssd, new_ssd = _ssd_scan(st, xbc, dt, ld, s_ssd)
    o_ssd = _ssd_post(st, y_ssd, xbc, proj_ssd, p['ssd_d'], p['ssd_norm'])

    rin, lw, rpost = _rwkv_prep(st, proj_rwkv, p['rwkv_mu'], p['rwkv_w2'], p['rwkv_w0'], p['rwkv_a2'],
                                p['rwkv_a0'], p['rwkv_g2'], p['rwkv_kk'], p['rwkv_ka'], p['rwkv_rk'])
    y_rwkv_f, y_rwkv_b, new_rwkv = _rwkv_scan(st, rin, lw, s_rwkv)
    o_rwkv = _rwkv_post(st, y_rwkv_f, y_rwkv_b, rpost, p['rwkv_ln_g'], p['rwkv_ln_b'])

    g2p = jnp.pad(p['gla_g2'], ((0, 0), (0, LANE - p['gla_g2'].shape[1]), (0, 0)))
    y_gla, new_gla_t = _gla_scan(st, proj_gla, g2p, p['gla_gb'].reshape(2, 1, 256), s_gla_t)
    o_gla = _gla_post(st, y_gla, proj_gla, p['gla_norm'])

    merged = _merge(st, h, (o_att, o_ssd, o_rwkv, o_gla), p['w_gate'], p['w_branch'])
    x = _out_proj(st, merged, p['w_o'], x, mod_l, 2)
    h2 = _normmod(st, x, p['norm2'], mod_l, 4, 3)
    x = _ffn(st, h2, p['ffn_w1'], p['ffn_w3'], p['ffn_w2'], x, mod_l, 5)
    return x, (k, v, new_ssd, new_rwkv, jnp.swapaxes(new_gla_t, -1, -2))


def kernel(x_prompt, x_sample, cache_attn_k, cache_attn_v, state_ssd, state_rwkv, state_gla, c, c_ctx, w_mod, b_mod, norm1, norm2, w_in, q_norm, k_norm, ssd_conv_w, ssd_conv_b, ssd_dt_bias, ssd_a_log, ssd_d, ssd_norm, rwkv_mu, rwkv_w0, rwkv_w2, rwkv_a0, rwkv_a2, rwkv_g2, rwkv_kk, rwkv_ka, rwkv_rk, rwkv_ln_g, rwkv_ln_b, gla_g2, gla_gb, gla_norm, w_gate, w_branch, w_o, ffn_w1, ffn_w3, ffn_w2, final_norm):
    nb, seq, d = x_prompt.shape
    db, dseq, _ = x_sample.shape
    depth = w_in.shape[0]
    assert d == D_MODEL and seq % CHUNK == 0 and dseq % CHUNK == 0 and 1 + db <= MOD_ROWS
    ctx_st = _Stream(nb, seq, 0, False)
    lat_st = _Stream(db, dseq, 1, True)

    cond = jnp.concatenate([c_ctx[None], c, jnp.zeros((MOD_ROWS - 1 - db, d), F32)], axis=0)
    mod = _modulation(cond, w_mod, b_mod)

    w_att, w_ssd, w_rwkv, w_gla = _split_w_in(w_in)
    w_gate_b, w_branch_b, w_o_b = _bf(w_gate), _bf(w_branch), _bf(w_o)
    w1_b, w3_b, w2_b = _bf(ffn_w1), _bf(ffn_w3), _bf(ffn_w2)

    def params_at(l):
        return dict(norm1=norm1[l], norm2=norm2[l], w_att=w_att[l], w_ssd=w_ssd[l], w_rwkv=w_rwkv[l],
                    w_gla=w_gla[l], q_norm=q_norm[l], k_norm=k_norm[l],
                    ssd_conv_w=ssd_conv_w[l], ssd_conv_b=ssd_conv_b[l], ssd_dt_bias=ssd_dt_bias[l],
                    ssd_a_log=ssd_a_log[l], ssd_d=ssd_d[l], ssd_norm=ssd_norm[l],
                    rwkv_mu=rwkv_mu[l], rwkv_w0=rwkv_w0[l], rwkv_w2=rwkv_w2[l], rwkv_a0=rwkv_a0[l],
                    rwkv_a2=rwkv_a2[l], rwkv_g2=rwkv_g2[l], rwkv_kk=rwkv_kk[l], rwkv_ka=rwkv_ka[l],
                    rwkv_rk=rwkv_rk[l], rwkv_ln_g=rwkv_ln_g[l], rwkv_ln_b=rwkv_ln_b[l],
                    gla_g2=gla_g2[l], gla_gb=gla_gb[l], gla_norm=gla_norm[l],
                    w_gate=w_gate_b[l], w_branch=w_branch_b[l], w_o=w_o_b[l],
                    ffn_w1=w1_b[l], ffn_w3=w3_b[l], ffn_w2=w2_b[l])

    xp = x_prompt.reshape(nb * seq, d)
    new_k, new_v, new_ssd, new_rwkv, new_gla = [], [], [], [], []
    for l in range(depth):
        xp, (k_l, v_l, ssd_l, rwkv_l, gla_l) = _block(ctx_st, xp, mod[l], params_at(l), None, None)
        new_k.append(k_l.reshape(nb, seq, ATT_KV, HEAD_DIM))
        new_v.append(v_l.reshape(nb, seq, ATT_KV, HEAD_DIM))
        new_ssd.append(ssd_l)
        new_rwkv.append(rwkv_l)
        new_gla.append(gla_l)

    rope_tables = _rope_tables(dseq)
    xs = x_sample.reshape(db * dseq, d)
    for l in range(depth):
        ctx = (cache_attn_k[:, l], cache_attn_v[:, l], state_ssd[:, l], state_rwkv[:, l], state_gla[:, l])
        xs, _ = _block(lat_st, xs, mod[l], params_at(l), rope_tables, ctx)

    y_prompt = _final_norm(ctx_st, xp, final_norm).reshape(nb, seq, d)
    y_sample = _final_norm(lat_st, xs, final_norm).reshape(db, dseq, d)
    return (y_prompt, y_sample, jnp.stack(new_k, axis=1), jnp.stack(new_v, axis=1),
            jnp.stack(new_ssd, axis=1), jnp.stack(new_rwkv, axis=1), jnp.stack(new_gla, axis=1))
```

```python
import functools

import jax
import jax.numpy as jnp
import numpy as np
from jax import lax
from jax.experimental import pallas as pl
from jax.experimental.pallas import tpu as pltpu

F32 = jnp.float32
BF16 = jnp.bfloat16
HI = lax.Precision.HIGHEST

D_MODEL = 2048
GRID_W = 64
ATT_HEADS = 8
ATT_KV = 2
ATT_GROUP = ATT_HEADS // ATT_KV
HEAD_DIM = 64
ROPE_THETA = 10000.0
SSD_HEADS = 8
SSD_P = 64
SSD_N = 64
SSD_GROUPS = 2
RWKV_HEADS = 8
RWKV_HD = 64
RWKV_DECAY_SCALE = 0.6065306597126334
RWKV_LN_EPS = 64e-5
GLA_HEADS = 4
GLA_DK = 64
GLA_DV = 128
GLA_GATE_NORM = 16.0
CHUNK = 64
BRANCH_W = 512

ATT_W = 768
SSD_W = 1408
RWKV_W = 1792
GLA_W = 1664
LANE = 128
SUBLANE = 8
MOD_ROWS = 16
VMEM_LIMIT = 56 * 1024 * 1024
ROW_TILE = 256
MM_TILE = 512
SCAN_CHUNKS = 4


def _cp(*sem):
    return pltpu.CompilerParams(dimension_semantics=sem, vmem_limit_bytes=VMEM_LIMIT)


def _dot(a, b, prec=None):
    return jnp.dot(a, b, preferred_element_type=F32, precision=prec)


def _dot_nt(a, b, prec=None):
    return lax.dot_general(a, b, (((1,), (1,)), ((), ())), preferred_element_type=F32, precision=prec)


def _dot_tn(a, b, prec=None):
    return lax.dot_general(a, b, (((0,), (0,)), ((), ())), preferred_element_type=F32, precision=prec)


def _silu(x):
    return x * jax.nn.sigmoid(x)


def _softplus(x):
    return jnp.maximum(x, 0.0) + jnp.log1p(jnp.exp(-jnp.abs(x)))


def _bf(x):
    return x.astype(BF16)


def _lanes(pieces):
    return jnp.concatenate(pieces, axis=1)


def _rows(pieces):
    return jnp.concatenate(pieces, axis=0)


def _mod_kernel(c_ref, w_ref, b_ref, o_ref):
    c = c_ref[...]
    o_ref[0] = _dot(_bf(_silu(c)), _bf(w_ref[0])) + b_ref[0]


def _modulation(cond, w_mod, b_mod):
    nl, d, n6 = w_mod.shape
    tn = 1024
    out = pl.pallas_call(
        _mod_kernel,
        grid=(nl, n6 // tn),
        in_specs=[pl.BlockSpec((MOD_ROWS, d), lambda l, j: (0, 0)),
                  pl.BlockSpec((1, d, tn), lambda l, j: (l, 0, j)),
                  pl.BlockSpec((1, 1, tn), lambda l, j: (l, 0, j))],
        out_specs=pl.BlockSpec((1, MOD_ROWS, tn), lambda l, j: (l, 0, j)),
        out_shape=jax.ShapeDtypeStruct((nl, MOD_ROWS, n6), F32),
        compiler_params=_cp("parallel", "parallel"),
        name="modulation",
    )(cond, w_mod, b_mod.reshape(nl, 1, n6))
    return out.reshape(nl, MOD_ROWS, 6, d)


class _Stream:
    def __init__(self, n, t, group0, per_seq):
        self.n, self.t, self.group0, self.per_seq = n, t, group0, per_seq
        self.rows = n * t

    def group_map(self, tile):
        g0, per_seq, t = self.group0, self.per_seq, self.t
        if per_seq:
            return lambda i: (g0 + (i * tile) // t, 0, 0)
        return lambda i: (g0, 0, 0)


def _mm_tile(st):
    return min(MM_TILE, st.t)


def _row_tile(st):
    return min(ROW_TILE, st.t)


def _normmod_kernel(x_ref, g_ref, mod_ref, o_ref, *, sc_idx, sh_idx):
    x = x_ref[...]
    y = x * lax.rsqrt(jnp.mean(x * x, axis=-1, keepdims=True) + 1e-6) * g_ref[...]
    o_ref[...] = _bf(y * (1.0 + mod_ref[0, sc_idx:sc_idx + 1, :]) + mod_ref[0, sh_idx:sh_idx + 1, :])


def _normmod(st, x, gain, mod_l, sc_idx, sh_idx):
    m, d = x.shape
    tm = _mm_tile(st)
    return pl.pallas_call(
        functools.partial(_normmod_kernel, sc_idx=sc_idx, sh_idx=sh_idx),
        grid=(m // tm,),
        in_specs=[pl.BlockSpec((tm, d), lambda i: (i, 0)),
                  pl.BlockSpec((1, d), lambda i: (0, 0)),
                  pl.BlockSpec((1, 6, d), st.group_map(tm))],
        out_specs=pl.BlockSpec((tm, d), lambda i: (i, 0)),
        out_shape=jax.ShapeDtypeStruct((m, d), BF16),
        compiler_params=_cp("parallel"),
        name="normmod",
    )(x, gain.reshape(1, d), mod_l)


def _rms_kernel(x_ref, g_ref, o_ref):
    x = x_ref[...]
    o_ref[...] = x * lax.rsqrt(jnp.mean(x * x, axis=-1, keepdims=True) + 1e-6) * g_ref[...]


def _final_norm(st, x, gain):
    m, d = x.shape
    tm = _mm_tile(st)
    return pl.pallas_call(
        _rms_kernel,
        grid=(m // tm,),
        in_specs=[pl.BlockSpec((tm, d), lambda i: (i, 0)), pl.BlockSpec((1, d), lambda i: (0, 0))],
        out_specs=pl.BlockSpec((tm, d), lambda i: (i, 0)),
        out_shape=jax.ShapeDtypeStruct((m, d), F32),
        compiler_params=_cp("parallel"),
        name="final_norm",
    )(x, gain.reshape(1, d))


def _mm_kernel(x_ref, w_ref, o_ref):
    o_ref[...] = _dot(x_ref[...], w_ref[...]).astype(o_ref.dtype)


def _matmul(st, x, w, out_dtype=F32):
    m, k = x.shape
    n = w.shape[1]
    tm = _mm_tile(st)
    return pl.pallas_call(
        _mm_kernel,
        grid=(m // tm,),
        in_specs=[pl.BlockSpec((tm, k), lambda i: (i, 0)), pl.BlockSpec((k, n), lambda i: (0, 0))],
        out_specs=pl.BlockSpec((tm, n), lambda i: (i, 0)),
        out_shape=jax.ShapeDtypeStruct((m, n), out_dtype),
        compiler_params=_cp("parallel"),
        name="in_proj",
    )(x, w)


def _att_prep_kernel(*refs, rope):
    if rope:
        p_ref, qn_ref, kn_ref, bdq_ref, bdk_ref, cos_ref, sa_ref, sb_ref, q_out, k_out = refs
    else:
        p_ref, qn_ref, kn_ref, bdq_ref, bdk_ref, q_out, k_out = refs
    p = p_ref[...]
    aq = p[:, :512]
    ak = p[:, 512:640]
    q = aq * lax.rsqrt(_dot(aq * aq, bdq_ref[...], HI) + 1e-6) * qn_ref[...]
    k = ak * lax.rsqrt(_dot(ak * ak, bdk_ref[...], HI) + 1e-6) * kn_ref[...]
    if rope:
        c, sa, sb = cos_ref[...], sa_ref[...], sb_ref[...]
        k = k * c + pltpu.roll(k, LANE - 16, 1) * sa + pltpu.roll(k, 16, 1) * sb
        c4, sa4, sb4 = _lanes([c] * 4), _lanes([sa] * 4), _lanes([sb] * 4)
        q = q * c4 + pltpu.roll(q, 512 - 16, 1) * sa4 + pltpu.roll(q, 16, 1) * sb4
    q_out[...] = _bf(q * (HEAD_DIM ** -0.5))
    k_out[...] = k.astype(k_out.dtype)


def _block_diag(width, block, value):
    idx = np.arange(width) // block
    return jnp.asarray((idx[:, None] == idx[None, :]).astype(np.float32) * value)


def _att_prep(st, proj, q_norm, k_norm, rope_tables):
    m = proj.shape[0]
    tr = _row_tile(st)
    rope = rope_tables is not None
    full = lambda shape: pl.BlockSpec(shape, lambda i: (0, 0))
    in_specs = [pl.BlockSpec((tr, ATT_W), lambda i: (i, 0)), full((1, 512)), full((1, 128)),
                full((512, 512)), full((128, 128))]
    args = [proj, jnp.tile(q_norm, ATT_HEADS).reshape(1, 512), jnp.tile(k_norm, ATT_KV).reshape(1, 128),
            _block_diag(512, HEAD_DIM, 1.0 / HEAD_DIM), _block_diag(128, HEAD_DIM, 1.0 / HEAD_DIM)]
    if rope:
        tps = st.t // tr
        in_specs += [pl.BlockSpec((tr, LANE), lambda i: (i % tps, 0))] * 3
        args += list(rope_tables)
    return pl.pallas_call(
        functools.partial(_att_prep_kernel, rope=rope),
        grid=(m // tr,),
        in_specs=in_specs,
        out_specs=[pl.BlockSpec((tr, 512), lambda i: (i, 0)), pl.BlockSpec((tr, 128), lambda i: (i, 0))],
        out_shape=[jax.ShapeDtypeStruct((m, 512), BF16),
                   jax.ShapeDtypeStruct((m, 128), BF16 if rope else F32)],
        compiler_params=_cp("parallel"),
        name="att_prep",
    )(*args)


def _rope_tables(t):
    half = HEAD_DIM // 2
    nf = half // 2
    freqs = ROPE_THETA ** (-jnp.arange(nf, dtype=F32) / nf)
    tt = jnp.arange(t)
    ang_r = (tt // GRID_W).astype(F32)[:, None] * freqs[None, :]
    ang_c = (tt % GRID_W).astype(F32)[:, None] * freqs[None, :]
    zero = jnp.zeros_like(ang_r)
    cos = jnp.concatenate([jnp.cos(ang_r)] * 2 + [jnp.cos(ang_c)] * 2, axis=1)
    sa = jnp.concatenate([-jnp.sin(ang_r), zero, -jnp.sin(ang_c), zero], axis=1)
    sb = jnp.concatenate([zero, jnp.sin(ang_r), zero, jnp.sin(ang_c)], axis=1)
    return tuple(jnp.tile(a, (1, LANE // HEAD_DIM)) for a in (cos, sa, sb))


def _attn_kernel(*refs, n_src, tq):
    q_ref, o_ref = refs[0], refs[-1]
    q = q_ref[...]
    outs = []
    for kv in range(ATT_KV):
        lo = kv * HEAD_DIM
        qs = _rows([q[:, (kv * ATT_GROUP + g) * HEAD_DIM:(kv * ATT_GROUP + g + 1) * HEAD_DIM]
                    for g in range(ATT_GROUP)])
        ks = [_bf(refs[1 + 2 * s][:, lo:lo + HEAD_DIM]) for s in range(n_src)]
        vs = [_bf(refs[2 + 2 * s][:, lo:lo + HEAD_DIM]) for s in range(n_src)]
        ss = [_dot_nt(qs, k) for k in ks]
        mx = functools.reduce(jnp.maximum, [jnp.max(s, axis=-1, keepdims=True) for s in ss])
        ps = [jnp.exp(s - mx) for s in ss]
        den = functools.reduce(jnp.add, [jnp.sum(p, axis=-1, keepdims=True) for p in ps])
        o = functools.reduce(jnp.add, [_dot(_bf(p), v) for p, v in zip(ps, vs)]) / den
        outs += [o[g * tq:(g + 1) * tq] for g in range(ATT_GROUP)]
    o_ref[...] = _bf(_lanes(outs))


def _attention(st, q, k, v, cache_k=None, cache_v=None):
    tq = min(128, st.t)
    nq = st.t // tq
    t = st.t
    in_specs = [pl.BlockSpec((tq, 512), lambda b, i: (b * nq + i, 0)),
                pl.BlockSpec((t, 128), lambda b, i: (b, 0)),
                pl.BlockSpec((t, 128), lambda b, i: (b, 0))]
    args = [q, k, v]
    n_src = 1
    if cache_k is not None:
        p = cache_k.shape[1]
        in_specs += [pl.BlockSpec((None, p, 128), lambda b, i: (b, 0, 0))] * 2
        args += [cache_k, cache_v]
        n_src = 2
    return pl.pallas_call(
        functools.partial(_attn_kernel, n_src=n_src, tq=tq),
        grid=(st.n, nq),
        in_specs=in_specs,
        out_specs=pl.BlockSpec((tq, 512), lambda b, i: (b * nq + i, 0)),
        out_shape=jax.ShapeDtypeStruct((st.rows, 512), BF16),
        compiler_params=_cp("parallel", "parallel"),
        name="attention",
    )(*args)


def _halo_specs(st, width):
    tr = _row_tile(st)
    per8 = tr // SUBLANE
    last8 = st.rows // SUBLANE - 1
    return [pl.BlockSpec((tr, width), lambda i: (i, 0)),
            pl.BlockSpec((SUBLANE, width), lambda i: (jnp.maximum(i * per8 - 1, 0), 0)),
            pl.BlockSpec((SUBLANE, width), lambda i: (jnp.minimum((i + 1) * per8, last8), 0))]


def _neighbours(cur, prev8, next8, tiles_per_seq):
    tr = cur.shape[0]
    j = pl.program_id(0) % tiles_per_seq
    pr = jnp.where(j != 0, prev8[SUBLANE - 1:SUBLANE], 0.0)
    nx = jnp.where(j != tiles_per_seq - 1, next8[0:1], 0.0)
    row = lax.broadcasted_iota(jnp.int32, cur.shape, 0)
    x_prev = jnp.where(row == 0, pr, pltpu.roll(cur, 1, 0))
    x_next = jnp.where(row == tr - 1, nx, pltpu.roll(cur, tr - 1, 0))
    return x_prev, x_next


def _dir_masks(z):
    ri = lax.broadcasted_iota(jnp.int32, (CHUNK, CHUNK), 0)
    ci = lax.broadcasted_iota(jnp.int32, (CHUNK, CHUNK), 1)
    return ((ci <= ri), (ci < ri)) if z == 0 else ((ci >= ri), (ci > ri))


def _tri_blocks(tr):
    i = np.arange(tr)
    same = (i[:, None] // CHUNK) == (i[None, :] // CHUNK)
    lower = same & (i[None, :] <= i[:, None])
    upper = same & (i[None, :] >= i[:, None])
    return jnp.asarray(lower.astype(np.float32)), jnp.asarray(upper.astype(np.float32))


def _scan_blocks(st):
    ncb = min(SCAN_CHUNKS, st.t // CHUNK)
    rb = ncb * CHUNK
    nblk = st.t // rb
    fwd = lambda b, j: (b * nblk + j, 0)
    bwd = lambda b, j: (b * nblk + nblk - 1 - j, 0)
    return ncb, rb, nblk, fwd, bwd


def _chunk_rows(step, z, ncb):
    cc = step if z == 0 else ncb - 1 - step
    return slice(cc * CHUNK, (cc + 1) * CHUNK)


def _ssd_prep_kernel(cur_ref, prev_ref, next_ref, cw_ref, cb_ref, dtb_ref, alog_ref, lo_ref, up_ref,
                     xbc_out, dt_out, cum_out, dtt_out, cumt_out, *, tiles_per_seq):
    cur = cur_ref[...]
    xc = cur[:, 512:1280]
    x_prev, x_next = _neighbours(xc, prev_ref[:, 512:1280], next_ref[:, 512:1280], tiles_per_seq)
    cw = cw_ref[...]
    conv = cb_ref[...] + x_prev * cw[0:1] + xc * cw[1:2] + x_next * cw[2:3]
    xbc_out[...] = _silu(conv)
    dt = _softplus(cur[:, 1280:1408] + dtb_ref[...])
    dt_out[...] = dt
    ld = dt * -jnp.exp(alog_ref[...])
    lane = lax.broadcasted_iota(jnp.int32, ld.shape, 1)
    cum = jnp.where(lane < SSD_HEADS, _dot(lo_ref[...], ld, HI), _dot(up_ref[...], ld, HI))
    cum_out[...] = cum
    pick = (lax.broadcasted_iota(jnp.int32, (2 * SSD_HEADS, LANE), 0)
            == lax.broadcasted_iota(jnp.int32, (2 * SSD_HEADS, LANE), 1)).astype(F32)
    dtt_out[...] = _dot_nt(pick, dt, HI)
    cumt_out[...] = _dot_nt(pick, cum, HI)


def _pad_lanes(v, width=LANE):
    v = v.reshape(1, -1)
    return jnp.pad(v, ((0, 0), (0, width - v.shape[1])))


def _ssd_prep(st, proj, conv_w, conv_b, dt_bias, a_log):
    m = proj.shape[0]
    tr = _row_tile(st)
    full = lambda shape: pl.BlockSpec(shape, lambda i: (0, 0))
    row = lambda w: pl.BlockSpec((tr, w), lambda i: (i, 0))
    col = pl.BlockSpec((2 * SSD_HEADS, tr), lambda i: (0, i))
    lower, upper = _tri_blocks(tr)
    return pl.pallas_call(
        functools.partial(_ssd_prep_kernel, tiles_per_seq=st.t // tr),
        grid=(m // tr,),
        in_specs=_halo_specs(st, SSD_W) + [full((3, 768)), full((1, 768)), full((1, LANE)), full((1, LANE)),
                                           full((tr, tr)), full((tr, tr))],
        out_specs=[row(768), row(LANE), row(LANE), col, col],
        out_shape=[jax.ShapeDtypeStruct((m, 768), F32), jax.ShapeDtypeStruct((m, LANE), F32),
                   jax.ShapeDtypeStruct((m, LANE), F32), jax.ShapeDtypeStruct((2 * SSD_HEADS, m), F32),
                   jax.ShapeDtypeStruct((2 * SSD_HEADS, m), F32)],
        compiler_params=_cp("parallel"),
        name="ssd_prep",
    )(proj, proj, proj, conv_w.T, conv_b.reshape(1, 768), _pad_lanes(dt_bias), _pad_lanes(a_log), lower, upper)


def _ssd_scan_kernel(xf_ref, xb_ref, dtf_ref, dtb_ref, cf_ref, cb_ref, dttf_ref, dttb_ref, ctf_ref, ctb_ref,
                     s0_ref, yf_ref, yb_ref, s_ref, *, ncb):
    @pl.when(pl.program_id(1) == 0)
    def _():
        s_ref[...] = s0_ref[...]

    hpg = SSD_HEADS // SSD_GROUPS
    state = {(z, g): _lanes([s_ref[z, g * hpg + hh] for hh in range(hpg)])
             for z in range(2) for g in range(SSD_GROUPS)}
    refs = ((xf_ref, dtf_ref, cf_ref, dttf_ref, ctf_ref, yf_ref), (xb_ref, dtb_ref, cb_ref, dttb_ref, ctb_ref, yb_ref))
    groups = [(z, g) for z in range(2) for g in range(SSD_GROUPS)]
    heads = [(z, h) for z in range(2) for h in range(SSD_HEADS)]
    for step in range(ncb):
        xbc, dt, cum, dtt, cumt, incl, rows = {}, {}, {}, {}, {}, {}, {}
        for z in range(2):
            x_ref, dt_ref, c_ref, dtt_ref, ct_ref, _ = refs[z]
            rows[z] = _chunk_rows(step, z, ncb)
            xbc[z], dt[z], cum[z] = x_ref[rows[z], :], dt_ref[rows[z], :], c_ref[rows[z], :]
            dtt[z], cumt[z] = dtt_ref[:, rows[z]], ct_ref[:, rows[z]]
            incl[z] = _dir_masks(z)[0]
        bmat = {(z, g): xbc[z][:, 512 + g * SSD_N:512 + (g + 1) * SSD_N] for z, g in groups}
        cmat = {(z, g): _bf(xbc[z][:, 640 + g * SSD_N:640 + (g + 1) * SSD_N]) for z, g in groups}
        cb = {k: _dot_nt(cmat[k], _bf(bmat[k])) for k in groups}
        cs = {k: _dot(cmat[k], _bf(state[k])) for k in groups}
        gcol, glast, xs = {}, {}, {}
        for z, h in heads:
            ln = z * SSD_HEADS + h
            gcol[z, h] = cum[z][:, ln:ln + 1]
            glast[z, h] = gcol[z, h][CHUNK - 1:CHUNK] if z == 0 else gcol[z, h][0:1]
            xs[z, h] = xbc[z][:, h * SSD_P:(h + 1) * SSD_P]
        inc = {}
        for z, g in groups:
            xw = []
            for h in range(g * hpg, (g + 1) * hpg):
                ln = z * SSD_HEADS + h
                xw.append(xs[z, h] * (dt[z][:, ln:ln + 1] * jnp.exp(glast[z, h] - gcol[z, h])))
            inc[z, g] = _dot_tn(_bf(bmat[z, g]), _bf(_lanes(xw)))
        att = {}
        for z, h in heads:
            ln = z * SSD_HEADS + h
            dec = jnp.exp(jnp.where(incl[z], gcol[z, h] - cumt[z][ln:ln + 1], -jnp.inf))
            att[z, h] = _bf(cb[z, h // hpg] * dec * dtt[z][ln:ln + 1])
        intra = {k: _dot(att[k], _bf(xs[k])) for k in heads}
        for z in range(2):
            outs = []
            for g in range(SSD_GROUPS):
                hs = range(g * hpg, (g + 1) * hpg)
                e_in = _lanes([jnp.broadcast_to(jnp.exp(gcol[z, h]), (CHUNK, SSD_P)) for h in hs])
                outs.append(_lanes([intra[z, h] for h in hs]) + cs[z, g] * e_in)
                e_last = _lanes([jnp.broadcast_to(jnp.exp(glast[z, h]), (1, SSD_P)) for h in hs])
                state[z, g] = state[z, g] * e_last + inc[z, g]
            refs[z][5][rows[z], :] = _lanes(outs)
    for z in range(2):
        for h in range(SSD_HEADS):
            s_ref[z, h] = state[z, h // hpg][:, (h % hpg) * SSD_P:(h % hpg + 1) * SSD_P]


def _ssd_scan(st, xbc, dt, cum, dtt, cumt, s0):
    ncb, rb, nblk, fwd, bwd = _scan_blocks(st)
    fwd_t = lambda b, j: (0, fwd(b, j)[0])
    bwd_t = lambda b, j: (0, bwd(b, j)[0])
    state_spec = pl.BlockSpec((None, 2, SSD_HEADS, SSD_N, SSD_P), lambda b, j: (b, 0, 0, 0, 0))
    rows = lambda w, m: pl.BlockSpec((rb, w), m)
    cols = lambda m: pl.BlockSpec((2 * SSD_HEADS, rb), m)
    return pl.pallas_call(
        functools.partial(_ssd_scan_kernel, ncb=ncb),
        grid=(st.n, nblk),
        in_specs=[rows(768, fwd), rows(768, bwd), rows(LANE, fwd), rows(LANE, bwd), rows(LANE, fwd), rows(LANE, bwd),
                  cols(fwd_t), cols(bwd_t), cols(fwd_t), cols(bwd_t), state_spec],
        out_specs=[rows(512, fwd), rows(512, bwd), state_spec],
        out_shape=[jax.ShapeDtypeStruct((st.rows, 512), F32), jax.ShapeDtypeStruct((st.rows, 512), F32),
                   jax.ShapeDtypeStruct((st.n, 2, SSD_HEADS, SSD_N, SSD_P), F32)],
        compiler_params=_cp("parallel", "arbitrary"),
        name="ssd_scan",
    )(xbc, xbc, dt, dt, cum, cum, dtt, dtt, cumt, cumt, s0)


def _ssd_post_kernel(yf_ref, yb_ref, xbc_ref, p_ref, d_ref, g_ref, o_ref):
    y = yf_ref[...] + yb_ref[...] + d_ref[...] * xbc_ref[:, :512]
    y = y * _silu(p_ref[:, :512])
    o_ref[...] = _bf(y * lax.rsqrt(jnp.mean(y * y, axis=-1, keepdims=True) + 1e-6) * g_ref[...])


def _ssd_post(st, y_f, y_b, xbc, proj, ssd_d, ssd_norm):
    m = proj.shape[0]
    tr = _row_tile(st)
    full = lambda shape: pl.BlockSpec(shape, lambda i: (0, 0))
    row = lambda w: pl.BlockSpec((tr, w), lambda i: (i, 0))
    return pl.pallas_call(
        _ssd_post_kernel,
        grid=(m // tr,),
        in_specs=[row(512), row(512), row(768), row(SSD_W), full((1, 512)), full((1, 512))],
        out_specs=row(512),
        out_shape=jax.ShapeDtypeStruct((m, 512), BF16),
        compiler_params=_cp("parallel"),
        name="ssd_post",
    )(y_f, y_b, xbc, proj, jnp.repeat(ssd_d, SSD_P).reshape(1, 512), ssd_norm.reshape(1, 512))


GLA_QK = GLA_HEADS * GLA_DK
GLA_SAFE_RANGE = 60.0


def _gla_prep_kernel(p_ref, g2_ref, gb_ref, lo_ref, up_ref, cum_out):
    logit = _dot(p_ref[:, 1024:1152], g2_ref[...], HI) + gb_ref[...]
    log_a = -_softplus(-logit) * (1.0 / GLA_GATE_NORM)
    cum_out[:, :GLA_QK] = _dot(lo_ref[...], log_a[:, :GLA_QK], HI)
    cum_out[:, GLA_QK:] = _dot(up_ref[...], log_a[:, GLA_QK:], HI)


def _gla_prep(st, proj, g2p, gb):
    m = proj.shape[0]
    tr = _row_tile(st)
    full = lambda shape: pl.BlockSpec(shape, lambda i: (0, 0))
    lower, upper = _tri_blocks(tr)
    return pl.pallas_call(
        _gla_prep_kernel,
        grid=(m // tr,),
        in_specs=[pl.BlockSpec((tr, GLA_W), lambda i: (i, 0)), full((LANE, 2 * GLA_QK)), full((1, 2 * GLA_QK)),
                  full((tr, tr)), full((tr, tr))],
        out_specs=pl.BlockSpec((tr, 2 * GLA_QK), lambda i: (i, 0)),
        out_shape=jax.ShapeDtypeStruct((m, 2 * GLA_QK), F32),
        compiler_params=_cp("parallel"),
        name="gla_prep",
    )(proj, g2p, gb, lower, upper)


def _gla_intra_exact(p_ref, c_ref, rows, z, seg_ref):
    r0 = rows.start
    q = p_ref[rows, 0:GLA_QK] * (GLA_DK ** -0.5)
    cum = c_ref[rows, :]
    row = lax.broadcasted_iota(jnp.int32, (CHUNK, 1), 0)
    seg = seg_ref[...]

    def body(j, acc):
        kj = p_ref[pl.ds(r0 + j, 1), GLA_QK:2 * GLA_QK]
        vj = p_ref[pl.ds(r0 + j, 1), 2 * GLA_QK:2 * GLA_QK + GLA_HEADS * GLA_DV]
        seen = (row >= j) if z == 0 else (row <= j)
        w = jnp.where(seen, q * kj * jnp.exp(jnp.minimum(cum - c_ref[pl.ds(r0 + j, 1), :], 0.0)), 0.0)
        score = _dot(w, seg, HI)
        return tuple(a + score[:, h:h + 1] * vj[:, h * GLA_DV:(h + 1) * GLA_DV] for h, a in enumerate(acc))

    zero = jnp.zeros((CHUNK, GLA_DV), F32)
    return list(lax.fori_loop(0, CHUNK, body, (zero,) * GLA_HEADS))


def _gla_scan_kernel(pf_ref, pb_ref, cf_ref, cb_ref, seg_ref, s0_ref, yf_ref, yb_ref, s_ref, *, ncb):
    @pl.when(pl.program_id(1) == 0)
    def _():
        s_ref[...] = s0_ref[...]

    refs = ((pf_ref, cf_ref, yf_ref), (pb_ref, cb_ref, yb_ref))
    heads = [(z, h) for z in range(2) for h in range(GLA_HEADS)]
    hsl = [slice(h * GLA_DK, (h + 1) * GLA_DK) for h in range(GLA_HEADS)]
    state = {(z, h): s_ref[z, h] for z, h in heads}
    chunks = [(step, z) for step in range(ncb) for z in range(2)]
    rows = {(step, z): _chunk_rows(step, z, ncb) for step, z in chunks}
    cum = {k: refs[k[1]][1][rows[k], :] for k in chunks}
    mid = {k: cum[k][CHUNK // 2:CHUNK // 2 + 1] for k in chunks}
    span = functools.reduce(jnp.maximum, [jnp.max(jnp.abs(cum[k] - mid[k])) for k in chunks])

    def v_of(k, h):
        return _bf(refs[k[1]][0][rows[k], 2 * GLA_QK + h * GLA_DV:2 * GLA_QK + (h + 1) * GLA_DV])

    def intra_factored():
        att = {}
        for k in chunks:
            p_ref = refs[k[1]][0]
            q_mid = _bf(p_ref[rows[k], 0:GLA_QK] * (GLA_DK ** -0.5) * jnp.exp(cum[k] - mid[k]))
            k_mid = _bf(p_ref[rows[k], GLA_QK:2 * GLA_QK] * jnp.exp(mid[k] - cum[k]))
            incl = _dir_masks(k[1])[0]
            for h in range(GLA_HEADS):
                att[k, h] = _bf(jnp.where(incl, _dot_nt(q_mid[:, hsl[h]], k_mid[:, hsl[h]]), 0.0))
        return [_dot(att[k, h], v_of(k, h)) for k in chunks for h in range(GLA_HEADS)]

    def intra_exact():
        out = []
        for k in chunks:
            out += _gla_intra_exact(refs[k[1]][0], refs[k[1]][1], rows[k], k[1], seg_ref)
        return out

    intra = lax.cond(span <= GLA_SAFE_RANGE, intra_factored, intra_exact)
    intra = {(k, h): intra[i * GLA_HEADS + h] for i, k in enumerate(chunks) for h in range(GLA_HEADS)}

    for step in range(ncb):
        q_in, k_out, e_last = {}, {}, {}
        for z in range(2):
            k = (step, z)
            p_ref = refs[z][0]
            glast = cum[k][CHUNK - 1:CHUNK] if z == 0 else cum[k][0:1]
            q_in[z] = _bf(p_ref[rows[k], 0:GLA_QK] * (GLA_DK ** -0.5) * jnp.exp(cum[k]))
            k_out[z] = _bf(p_ref[rows[k], GLA_QK:2 * GLA_QK] * jnp.exp(glast - cum[k]))
            e_last[z] = jnp.exp(glast)
        inter = {(z, h): _dot_nt(q_in[z][:, hsl[h]], _bf(state[z, h])) for z, h in heads}
        inc = {(z, h): _dot_tn(v_of((step, z), h), k_out[z][:, hsl[h]]) for z, h in heads}
        for z in range(2):
            refs[z][2][rows[step, z], :] = _lanes([intra[(step, z), h] + inter[z, h] for h in range(GLA_HEADS)])
            for h in range(GLA_HEADS):
                state[z, h] = state[z, h] * e_last[z][:, hsl[h]] + inc[z, h]
    for z, h in heads:
        s_ref[z, h] = state[z, h]


def _gla_scan(st, proj, cum, s0_t):
    ncb, rb, nblk, fwd, bwd = _scan_blocks(st)
    seg = _block_diag(GLA_QK, GLA_DK, 1.0)[:, ::GLA_DK]
    seg = jnp.pad(seg, ((0, 0), (0, LANE - GLA_HEADS)))
    state_spec = pl.BlockSpec((None, 2, GLA_HEADS, GLA_DV, GLA_DK), lambda b, j: (b, 0, 0, 0, 0))
    qkv_w = 2 * GLA_QK + GLA_HEADS * GLA_DV
    return pl.pallas_call(
        functools.partial(_gla_scan_kernel, ncb=ncb),
        grid=(st.n, nblk),
        in_specs=[pl.BlockSpec((rb, qkv_w), fwd), pl.BlockSpec((rb, qkv_w), bwd),
                  pl.BlockSpec((rb, GLA_QK), fwd), pl.BlockSpec((rb, GLA_QK), lambda b, j: (bwd(b, j)[0], 1)),
                  pl.BlockSpec((GLA_QK, LANE), lambda b, j: (0, 0)), state_spec],
        out_specs=[pl.BlockSpec((rb, 512), fwd), pl.BlockSpec((rb, 512), bwd), state_spec],
        out_shape=[jax.ShapeDtypeStruct((st.rows, 512), F32), jax.ShapeDtypeStruct((st.rows, 512), F32),
                   jax.ShapeDtypeStruct((st.n, 2, GLA_HEADS, GLA_DV, GLA_DK), F32)],
        compiler_params=_cp("parallel", "arbitrary"),
        name="gla_scan",
    )(proj, proj, cum, cum, seg, s0_t)


def _gla_post_kernel(yf_ref, yb_ref, p_ref, g_ref, o_ref):
    o = yf_ref[...] + yb_ref[...]
    gate = _silu(p_ref[:, 1152:1664])
    outs = []
    for h in range(GLA_HEADS):
        oh = o[:, h * GLA_DV:(h + 1) * GLA_DV]
        outs.append(oh * lax.rsqrt(jnp.mean(oh * oh, axis=-1, keepdims=True) + 1e-6) * g_ref[...])
    o_ref[...] = _bf(_lanes(outs) * gate)


def _gla_post(st, y_f, y_b, proj, gla_norm):
    m = proj.shape[0]
    tr = _row_tile(st)
    return pl.pallas_call(
        _gla_post_kernel,
        grid=(m // tr,),
        in_specs=[pl.BlockSpec((tr, 512), lambda i: (i, 0)), pl.BlockSpec((tr, 512), lambda i: (i, 0)),
                  pl.BlockSpec((tr, GLA_W), lambda i: (i, 0)), pl.BlockSpec((1, GLA_DV), lambda i: (0, 0))],
        out_specs=pl.BlockSpec((tr, 512), lambda i: (i, 0)),
        out_shape=jax.ShapeDtypeStruct((m, 512), BF16),
        compiler_params=_cp("parallel"),
        name="gla_post",
    )(y_f, y_b, proj, gla_norm.reshape(1, GLA_DV))


def _rwkv_prep_kernel(cur_ref, prev_ref, next_ref, mu_ref, w2_ref, w0_ref, a2_ref, a0_ref, g2_ref,
                      kkw_ref, ka_ref, rk_ref, bd_ref, in_out, lw_out, post_out, *, tiles_per_seq):
    cur = cur_ref[...]
    x_prev, x_next = _neighbours(cur, prev_ref[...], next_ref[...], tiles_per_seq)
    blk = cur + (0.5 * (x_prev + x_next) - cur) * mu_ref[...]
    r, k, v = blk[:, 0:512], blk[:, 512:1024], blk[:, 1024:1536]
    w_logit = w0_ref[...] + _dot(jnp.tanh(blk[:, 1536:1600]), w2_ref[...], HI)
    lw_out[...] = -RWKV_DECAY_SCALE * jax.nn.sigmoid(w_logit)
    a = jax.nn.sigmoid(a0_ref[...] + _dot(blk[:, 1600:1664], a2_ref[...], HI))
    g = _dot(jax.nn.sigmoid(blk[:, 1664:1792]), g2_ref[...], HI)
    bd = bd_ref[...]
    kk = k * kkw_ref[...]
    kk = kk * lax.rsqrt(_dot(kk * kk, bd, HI) + 1e-12)
    k2 = k * (1.0 + (a - 1.0) * ka_ref[...])
    in_out[:, 0:512] = r
    in_out[:, 512:1024] = k2
    in_out[:, 1024:1536] = v
    in_out[:, 1536:2048] = -kk
    in_out[:, 2048:2560] = kk * a
    post_out[:, 0:512] = g
    post_out[:, 512:1024] = _dot(r * k2 * rk_ref[...], bd, HI) * v


def _rwkv_prep(st, proj, mu, w2, w0, a2, a0, g2, kkw, ka, rk):
    m = proj.shape[0]
    tr = _row_tile(st)
    full = lambda shape: pl.BlockSpec(shape, lambda i: (0, 0))
    row = lambda w: pl.BlockSpec((tr, w), lambda i: (i, 0))
    vec = lambda a: a.reshape(1, -1)
    return pl.pallas_call(
        functools.partial(_rwkv_prep_kernel, tiles_per_seq=st.t // tr),
        grid=(m // tr,),
        in_specs=_halo_specs(st, RWKV_W) + [full((1, RWKV_W)), full((64, 1024)), full((1, 1024)),
                                            full((64, 512)), full((1, 512)), full((128, 512)),
                                            full((1, 512)), full((1, 512)), full((1, 512)), full((512, 512))],
        out_specs=[row(2560), row(1024), row(1024)],
        out_shape=[jax.ShapeDtypeStruct((m, 2560), F32), jax.ShapeDtypeStruct((m, 1024), F32),
                   jax.ShapeDtypeStruct((m, 1024), F32)],
        compiler_params=_cp("parallel"),
        name="rwkv_prep",
    )(proj, proj, proj, vec(mu), jnp.concatenate([w2[0], w2[1]], axis=1), vec(w0), a2, vec(a0), g2,
      vec(kkw), vec(ka), vec(rk), _block_diag(512, RWKV_HD, 1.0))


def _rwkv_chunk(x_ref, lw_ref, rows, z):
    incl, strict = _dir_masks(z)
    lw = lw_ref[rows, :]
    cum = _dot(incl.astype(F32), lw, HI)
    cum_x = cum - lw
    mid = cum[CHUNK // 2:CHUNK // 2 + 1]
    glast = cum[CHUNK - 1:CHUNK] if z == 0 else cum[0:1]
    x = x_ref[rows, :]
    r, k, v, al, be = x[:, 0:512], x[:, 512:1024], x[:, 1024:1536], x[:, 1536:2048], x[:, 2048:2560]
    e_mid, e_nmid = jnp.exp(cum - mid), jnp.exp(mid - cum)
    e_out = jnp.exp(glast - cum)
    return dict(incl=incl, strict=strict, v=v,
                r_mid=_bf(r * e_mid), a_mid=_bf(al * jnp.exp(cum_x - mid)),
                b_mid=_bf(be * e_nmid), k_mid=_bf(k * e_nmid),
                r_in=_bf(r * jnp.exp(cum)), a_in=al * jnp.exp(cum_x),
                b_out=_bf(be * e_out), k_out=_bf(k * e_out), e_last=jnp.exp(glast))


_HEAD_SLICES = [slice(h * RWKV_HD, (h + 1) * RWKV_HD) for h in range(RWKV_HEADS)]


def _rwkv_state_free(chunks, hooks):
    hooks = list(hooks)

    def run_hook():
        if hooks:
            hooks.pop(0)()

    chains = [(c, sl) for c in chunks for sl in _HEAD_SLICES]
    run_hook()
    pair = [_dot_nt(_rows([c['a_mid'][:, sl], c['r_mid'][:, sl]]), _rows([c['b_mid'][:, sl], c['k_mid'][:, sl]]))
            for c, sl in chains]
    half = CHUNK
    a_ab = [_bf(jnp.where(c['strict'], p[:half, :half], 0.0)) for (c, _), p in zip(chains, pair)]
    a_ak = [_bf(jnp.where(c['strict'], p[:half, half:], 0.0)) for (c, _), p in zip(chains, pair)]
    a_rb = [_bf(jnp.where(c['incl'], p[half:, :half], 0.0)) for (c, _), p in zip(chains, pair)]
    a_rk = [_bf(jnp.where(c['incl'], p[half:, half:], 0.0)) for (c, _), p in zip(chains, pair)]
    vb = [_bf(c['v'][:, sl]) for c, sl in chains]
    av = [_dot(a, v) for a, v in zip(a_ak, vb)]
    o0 = [_dot(a, v) for a, v in zip(a_rk, vb)]
    sol = [_lanes([c['a_in'][:, sl], x]) for (c, sl), x in zip(chains, av)]
    powr = a_ab
    for it in range(6):
        sol = [s + _dot(p, _bf(s)) for p, s in zip(powr, sol)]
        if it in (1, 3):
            run_hook()
        if it < 5:
            powr = [_bf(_dot(p, p)) for p in powr]
    while hooks:
        run_hook()
    return [dict(w=_bf(s[:, :RWKV_HD]), u0=s[:, RWKV_HD:], o0=o, a_rb=a, v=c['v'][:, sl], r_in=c['r_in'][:, sl],
                 b_out=c['b_out'][:, sl], k_out=c['k_out'][:, sl], e_last=c['e_last'][:, sl])
            for (c, sl), s, o, a in zip(chains, sol, o0, a_rb)]


def _rwkv_state_stages(res, state, write_out):
    box = {}

    def read_state():
        box['su'] = [_dot_nt(_rows([c['w'], c['r_in']]), _bf(s)) for c, s in zip(res, state)]

    def update_state():
        box['u'] = [c['u0'] + su[:CHUNK] for c, su in zip(res, box['su'])]
        state[:] = [s * c['e_last'] + _dot_tn(_bf(_rows([u, c['v']])), _rows([c['b_out'], c['k_out']]))
                    for c, s, u in zip(res, state, box['u'])]

    def emit():
        write_out([su[CHUNK:] + _dot(c['a_rb'], _bf(u)) + c['o0'] for c, su, u in zip(res, box['su'], box['u'])])

    return [read_state, update_state, emit]


def _rwkv_scan_kernel(xf_ref, xb_ref, lwf_ref, lwb_ref, s0_ref, yf_ref, yb_ref, s_ref, *, ncb):
    @pl.when(pl.program_id(1) == 0)
    def _():
        s_ref[...] = s0_ref[...]

    state = [s_ref[z, h] for z in range(2) for h in range(RWKV_HEADS)]

    def chunk_rows(step, z):
        cc = step if z == 0 else ncb - 1 - step
        return slice(cc * CHUNK, (cc + 1) * CHUNK)

    def pair_inputs(step):
        return [_rwkv_chunk(xf_ref, lwf_ref, chunk_rows(step, 0), 0),
                _rwkv_chunk(xb_ref, lwb_ref, chunk_rows(step, 1), 1)]

    def writer(step):
        def write_out(outs):
            yf_ref[chunk_rows(step, 0), :] = _lanes(outs[:RWKV_HEADS])
            yb_ref[chunk_rows(step, 1), :] = _lanes(outs[RWKV_HEADS:])
        return write_out

    res = _rwkv_state_free(pair_inputs(0), [])
    for step in range(ncb):
        hooks = _rwkv_state_stages(res, state, writer(step))
        if step + 1 < ncb:
            res = _rwkv_state_free(pair_inputs(step + 1), hooks)
        else:
            for hook in hooks:
                hook()
    for z in range(2):
        for h in range(RWKV_HEADS):
            s_ref[z, h] = state[z * RWKV_HEADS + h]


def _rwkv_scan(st, xin, lw, s0):
    ncb, rb, nblk, fwd, bwd = _scan_blocks(st)
    state_spec = pl.BlockSpec((None, 2, RWKV_HEADS, RWKV_HD, RWKV_HD), lambda b, j: (b, 0, 0, 0, 0))
    return pl.pallas_call(
        functools.partial(_rwkv_scan_kernel, ncb=ncb),
        grid=(st.n, nblk),
        in_specs=[pl.BlockSpec((rb, 2560), fwd), pl.BlockSpec((rb, 2560), bwd),
                  pl.BlockSpec((rb, 512), fwd), pl.BlockSpec((rb, 512), lambda b, j: (bwd(b, j)[0], 1)),
                  state_spec],
        out_specs=[pl.BlockSpec((rb, 512), fwd), pl.BlockSpec((rb, 512), bwd), state_spec],
        out_shape=[jax.ShapeDtypeStruct((st.rows, 512), F32), jax.ShapeDtypeStruct((st.rows, 512), F32),
                   jax.ShapeDtypeStruct((st.n, 2, RWKV_HEADS, RWKV_HD, RWKV_HD), F32)],
        compiler_params=_cp("parallel", "arbitrary"),
        name="rwkv_scan",
    )(xin, xin, lw, lw, s0)


def _rwkv_post_kernel(yf_ref, yb_ref, post_ref, lng_ref, lnb_ref, bd_ref, o_ref):
    o = yf_ref[...] + yb_ref[...]
    bd = bd_ref[...]
    d = o - _dot(o, bd, HI)
    o = d * lax.rsqrt(_dot(d * d, bd, HI) + RWKV_LN_EPS) * lng_ref[...] + lnb_ref[...]
    o_ref[...] = _bf((o + post_ref[:, 512:1024]) * post_ref[:, 0:512])


def _rwkv_post(st, y_f, y_b, post, ln_g, ln_b):
    m = post.shape[0]
    tr = _row_tile(st)
    full = lambda shape: pl.BlockSpec(shape, lambda i: (0, 0))
    row = lambda w: pl.BlockSpec((tr, w), lambda i: (i, 0))
    return pl.pallas_call(
        _rwkv_post_kernel,
        grid=(m // tr,),
        in_specs=[row(512), row(512), row(1024), full((1, 512)), full((1, 512)), full((512, 512))],
        out_specs=row(512),
        out_shape=jax.ShapeDtypeStruct((m, 512), BF16),
        compiler_params=_cp("parallel"),
        name="rwkv_post",
    )(y_f, y_b, post, ln_g.reshape(1, 512), ln_b.reshape(1, 512), _block_diag(512, RWKV_HD, 1.0 / RWKV_HD))


def _merge_kernel(h_ref, o0_ref, o1_ref, o2_ref, o3_ref, wg_ref, wb_ref, out_ref):
    h = h_ref[...]
    acc = None
    for i, o_ref in enumerate((o0_ref, o1_ref, o2_ref, o3_ref)):
        term = jax.nn.sigmoid(_dot(h, wg_ref[i])) * _dot(o_ref[...], wb_ref[i])
        acc = term if acc is None else acc + term
    out_ref[...] = _bf(acc)


def _merge(st, h, branch_outs, wg, wb):
    m, d = h.shape
    tm = _mm_tile(st)
    tn = 512
    row = lambda w: pl.BlockSpec((tm, w), lambda j, i: (i, 0))
    return pl.pallas_call(
        _merge_kernel,
        grid=(d // tn, m // tm),
        in_specs=[row(d)] + [row(BRANCH_W)] * 4 + [pl.BlockSpec((4, d, tn), lambda j, i: (0, 0, j)),
                                                    pl.BlockSpec((4, BRANCH_W, tn), lambda j, i: (0, 0, j))],
        out_specs=pl.BlockSpec((tm, tn), lambda j, i: (i, j)),
        out_shape=jax.ShapeDtypeStruct((m, d), BF16),
        compiler_params=_cp("parallel", "parallel"),
        name="merge",
    )(h, *branch_outs, wg, wb)


def _wo_kernel(m_ref, w_ref, x_ref, mod_ref, o_ref, *, gate_idx):
    o_ref[...] = x_ref[...] + mod_ref[0, gate_idx:gate_idx + 1, :] * _dot(m_ref[...], w_ref[...])


def _out_proj(st, merged, w_o, x, mod_l, gate_idx):
    m, d = x.shape
    tm = _mm_tile(st)
    return pl.pallas_call(
        functools.partial(_wo_kernel, gate_idx=gate_idx),
        grid=(m // tm,),
        in_specs=[pl.BlockSpec((tm, d), lambda i: (i, 0)), pl.BlockSpec((d, d), lambda i: (0, 0)),
                  pl.BlockSpec((tm, d), lambda i: (i, 0)), pl.BlockSpec((1, 6, d), st.group_map(tm))],
        out_specs=pl.BlockSpec((tm, d), lambda i: (i, 0)),
        out_shape=jax.ShapeDtypeStruct((m, d), F32),
        compiler_params=_cp("parallel"),
        name="out_proj",
    )(merged, w_o, x, mod_l)


def _ffn_kernel(h_ref, w1_ref, w3_ref, w2_ref, x_ref, mod_ref, o_ref, acc_ref, *, gate_idx):
    f = pl.program_id(1)

    @pl.when(f == 0)
    def _():
        acc_ref[...] = jnp.zeros_like(acc_ref)

    h = h_ref[...]
    u = _silu(_dot(h, w1_ref[...])) * _dot(h, w3_ref[...])
    acc_ref[...] += _dot(_bf(u), w2_ref[...])

    @pl.when(f == pl.num_programs(1) - 1)
    def _():
        o_ref[...] = x_ref[...] + mod_ref[0, gate_idx:gate_idx + 1, :] * acc_ref[...]


def _ffn(st, h, w1, w3, w2, x, mod_l, gate_idx):
    m, d = x.shape
    dff = w1.shape[1]
    tm = _mm_tile(st)
    tf = 512
    return pl.pallas_call(
        functools.partial(_ffn_kernel, gate_idx=gate_idx),
        grid=(m // tm, dff // tf),
        in_specs=[pl.BlockSpec((tm, d), lambda i, f: (i, 0)),
                  pl.BlockSpec((d, tf), lambda i, f: (0, f)),
                  pl.BlockSpec((d, tf), lambda i, f: (0, f)),
                  pl.BlockSpec((tf, d), lambda i, f: (f, 0)),
                  pl.BlockSpec((tm, d), lambda i, f: (i, 0)),
                  pl.BlockSpec((1, 6, d), lambda i, f: st.group_map(tm)(i))],
        out_specs=pl.BlockSpec((tm, d), lambda i, f: (i, 0)),
        out_shape=jax.ShapeDtypeStruct((m, d), F32),
        scratch_shapes=[pltpu.VMEM((tm, d), F32)],
        compiler_params=_cp("parallel", "arbitrary"),
        name="ffn",
    )(h, w1, w3, w2, x, mod_l)


def _split_w_in(w_in):
    z = lambda w: jnp.zeros(w_in.shape[:2] + (w,), w_in.dtype)
    att = w_in[..., 0:768]
    ssd = jnp.concatenate([w_in[..., 768:2064], z(SSD_W - 1296)], axis=-1)
    rwkv = w_in[..., 2064:3856]
    gla = jnp.concatenate([w_in[..., 3856:4880], w_in[..., 4880:4896], z(LANE - 16), w_in[..., 4896:5408]], axis=-1)
    return tuple(_bf(w) for w in (att, ssd, rwkv, gla))


def _block(st, x, mod_l, p, rope_tables, ctx):
    h = _normmod(st, x, p['norm1'], mod_l, 1, 0)
    proj_att = _matmul(st, h, p['w_att'])
    proj_ssd = _matmul(st, h, p['w_ssd'])
    proj_rwkv = _matmul(st, h, p['w_rwkv'])
    proj_gla = _matmul(st, h, p['w_gla'])

    q, k = _att_prep(st, proj_att, p['q_norm'], p['k_norm'], rope_tables)
    v = proj_att[:, 640:768]
    if ctx is None:
        o_att = _attention(st, q, k, v)
        s_ssd = jnp.zeros((st.n, 2, SSD_HEADS, SSD_N, SSD_P), F32)
        s_rwkv = jnp.zeros((st.n, 2, RWKV_HEADS, RWKV_HD, RWKV_HD), F32)
        s_gla_t = jnp.zeros((st.n, 2, GLA_HEADS, GLA_DV, GLA_DK), F32)
    else:
        ctx_k, ctx_v, s_ssd, s_rwkv, s_gla = ctx
        p_len = ctx_k.shape[1]
        o_att = _attention(st, q, k, v, _bf(ctx_k.reshape(st.n, p_len, 128)), _bf(ctx_v.reshape(st.n, p_len, 128)))
        s_gla_t = jnp.swapaxes(s_gla, -1, -2)

    xbc, dt, cum, dtt, cumt = _ssd_prep(st, proj_ssd, p['ssd_conv_w'], p['ssd_conv_b'], p['ssd_dt_bias'],
                                        p['ssd_a_log'])
    y_ssd_f, y_ssd_b, new_ssd = _ssd_scan(st, xbc, dt, cum, dtt, cumt, s_ssd)
    o_ssd = _ssd_post(st, y_ssd_f, y_ssd_b, xbc, proj_ssd, p['ssd_d'], p['ssd_norm'])

    rin, lw, rpost = _rwkv_prep(st, proj_rwkv, p['rwkv_mu'], p['rwkv_w2'], p['rwkv_w0'], p['rwkv_a2'],
                                p['rwkv_a0'], p['rwkv_g2'], p['rwkv_kk'], p['rwkv_ka'], p['rwkv_rk'])
    y_rwkv_f, y_rwkv_b, new_rwkv = _rwkv_scan(st, rin, lw, s_rwkv)
    o_rwkv = _rwkv_post(st, y_rwkv_f, y_rwkv_b, rpost, p['rwkv_ln_g'], p['rwkv_ln_b'])

    g2 = jnp.concatenate([p['gla_g2'][0], p['gla_g2'][1]], axis=1)
    g2p = jnp.pad(g2, ((0, LANE - g2.shape[0]), (0, 0)))
    gla_cum = _gla_prep(st, proj_gla, g2p, p['gla_gb'].reshape(1, 2 * GLA_QK))
    y_gla_f, y_gla_b, new_gla_t = _gla_scan(st, proj_gla, gla_cum, s_gla_t)
    o_gla = _gla_post(st, y_gla_f, y_gla_b, proj_gla, p['gla_norm'])

    merged = _merge(st, h, (o_att, o_ssd, o_rwkv, o_gla), p['w_gate'], p['w_branch'])
    x = _out_proj(st, merged, p['w_o'], x, mod_l, 2)
    h2 = _normmod(st, x, p['norm2'], mod_l, 4, 3)
    x = _ffn(st, h2, p['ffn_w1'], p['ffn_w3'], p['ffn_w2'], x, mod_l, 5)
    return x, (k, v, new_ssd, new_rwkv, jnp.swapaxes(new_gla_t, -1, -2))


def kernel(x_prompt, x_sample, cache_attn_k, cache_attn_v, state_ssd, state_rwkv, state_gla, c, c_ctx, w_mod, b_mod, norm1, norm2, w_in, q_norm, k_norm, ssd_conv_w, ssd_conv_b, ssd_dt_bias, ssd_a_log, ssd_d, ssd_norm, rwkv_mu, rwkv_w0, rwkv_w2, rwkv_a0, rwkv_a2, rwkv_g2, rwkv_kk, rwkv_ka, rwkv_rk, rwkv_ln_g, rwkv_ln_b, gla_g2, gla_gb, gla_norm, w_gate, w_branch, w_o, ffn_w1, ffn_w3, ffn_w2, final_norm):
    nb, seq, d = x_prompt.shape
    db, dseq, _ = x_sample.shape
    depth = w_in.shape[0]
    assert d == D_MODEL and seq % CHUNK == 0 and dseq % CHUNK == 0 and 1 + db <= MOD_ROWS
    ctx_st = _Stream(nb, seq, 0, False)
    lat_st = _Stream(db, dseq, 1, True)

    cond = jnp.concatenate([c_ctx[None], c, jnp.zeros((MOD_ROWS - 1 - db, d), F32)], axis=0)
    mod = _modulation(cond, w_mod, b_mod)

    w_att, w_ssd, w_rwkv, w_gla = _split_w_in(w_in)
    w_gate_b, w_branch_b, w_o_b = _bf(w_gate), _bf(w_branch), _bf(w_o)
    w1_b, w3_b, w2_b = _bf(ffn_w1), _bf(ffn_w3), _bf(ffn_w2)

    def params_at(l):
        return dict(norm1=norm1[l], norm2=norm2[l], w_att=w_att[l], w_ssd=w_ssd[l], w_rwkv=w_rwkv[l],
                    w_gla=w_gla[l], q_norm=q_norm[l], k_norm=k_norm[l],
                    ssd_conv_w=ssd_conv_w[l], ssd_conv_b=ssd_conv_b[l], ssd_dt_bias=ssd_dt_bias[l],
                    ssd_a_log=ssd_a_log[l], ssd_d=ssd_d[l], ssd_norm=ssd_norm[l],
                    rwkv_mu=rwkv_mu[l], rwkv_w0=rwkv_w0[l], rwkv_w2=rwkv_w2[l], rwkv_a0=rwkv_a0[l],
                    rwkv_a2=rwkv_a2[l], rwkv_g2=rwkv_g2[l], rwkv_kk=rwkv_kk[l], rwkv_ka=rwkv_ka[l],
                    rwkv_rk=rwkv_rk[l], rwkv_ln_g=rwkv_ln_g[l], rwkv_ln_b=rwkv_ln_b[l],
                    gla_g2=gla_g2[l], gla_gb=gla_gb[l], gla_norm=gla_norm[l],
                    w_gate=w_gate_b[l], w_branch=w_branch_b[l], w_o=w_o_b[l],
                    ffn_w1=w1_b[l], ffn_w3=w3_b[l], ffn_w2=w2_b[l])

    xp = x_prompt.reshape(nb * seq, d)
    new_k, new_v, new_ssd, new_rwkv, new_gla = [], [], [], [], []
    for l in range(depth):
        xp, (k_l, v_l, ssd_l, rwkv_l, gla_l) = _block(ctx_st, xp, mod[l], params_at(l), None, None)
        new_k.append(k_l.reshape(nb, seq, ATT_KV, HEAD_DIM))
        new_v.append(v_l.reshape(nb, seq, ATT_KV, HEAD_DIM))
        new_ssd.append(ssd_l)
        new_rwkv.append(rwkv_l)
        new_gla.append(gla_l)

    rope_tables = _rope_tables(dseq)
    xs = x_sample.reshape(db * dseq, d)
    for l in range(depth):
        ctx = (cache_attn_k[:, l], cache_attn_v[:, l], state_ssd[:, l], state_rwkv[:, l], state_gla[:, l])
        xs, _ = _block(lat_st, xs, mod[l], params_at(l), rope_tables, ctx)

    y_prompt = _final_norm(ctx_st, xp, final_norm).reshape(nb, seq, d)
    y_sample = _final_norm(lat_st, xs, final_norm).reshape(db, dseq, d)
    return (y_prompt, y_sample, jnp.stack(new_k, axis=1), jnp.stack(new_v, axis=1),
            jnp.stack(new_ssd, axis=1), jnp.stack(new_rwkv, axis=1), jnp.stack(new_gla, axis=1))
```

```python
import functools

import jax
import jax.numpy as jnp
import numpy as np
from jax import lax
from jax.experimental import pallas as pl
from jax.experimental.pallas import tpu as pltpu

F32 = jnp.float32
BF16 = jnp.bfloat16
HI = lax.Precision.HIGHEST

D_MODEL = 2048
GRID_W = 64
ATT_HEADS = 8
ATT_KV = 2
ATT_GROUP = ATT_HEADS // ATT_KV
HEAD_DIM = 64
ROPE_THETA = 10000.0
SSD_HEADS = 8
SSD_P = 64
SSD_N = 64
SSD_GROUPS = 2
RWKV_HEADS = 8
RWKV_HD = 64
RWKV_DECAY_SCALE = 0.6065306597126334
RWKV_LN_EPS = 64e-5
GLA_HEADS = 4
GLA_DK = 64
GLA_DV = 128
GLA_GATE_NORM = 16.0
CHUNK = 64
BRANCH_W = 512

ATT_W = 768
SSD_W = 1408
RWKV_W = 1792
GLA_W = 1664
LANE = 128
SUBLANE = 8
MOD_ROWS = 16
VMEM_LIMIT = 56 * 1024 * 1024
ROW_TILE = 256
MM_TILE = 512
SCAN_CHUNKS = 4
KEY_TILE = 256
KEY_UNROLL = 6
LOG2E = 1.4426950408889634
ATT_MIN_ROW_SUM = 2.0 ** -90


def _cp(*sem):
    return pltpu.CompilerParams(dimension_semantics=sem, vmem_limit_bytes=VMEM_LIMIT)


def _dot(a, b, prec=None):
    return jnp.dot(a, b, preferred_element_type=F32, precision=prec)


def _dot_nt(a, b, prec=None):
    return lax.dot_general(a, b, (((1,), (1,)), ((), ())), preferred_element_type=F32, precision=prec)


def _dot_tn(a, b, prec=None):
    return lax.dot_general(a, b, (((0,), (0,)), ((), ())), preferred_element_type=F32, precision=prec)


def _silu(x):
    return x * jax.nn.sigmoid(x)


def _softplus(x):
    return jnp.maximum(x, 0.0) + jnp.log1p(jnp.exp(-jnp.abs(x)))


def _bf(x):
    return x.astype(BF16)


def _lanes(pieces):
    return jnp.concatenate(pieces, axis=1)


def _rows(pieces):
    return jnp.concatenate(pieces, axis=0)


def _mod_kernel(c_ref, w_ref, b_ref, o_ref):
    c = c_ref[...]
    o_ref[0] = _dot(_bf(_silu(c)), _bf(w_ref[0])) + b_ref[0]


def _modulation(cond, w_mod, b_mod):
    nl, d, n6 = w_mod.shape
    tn = 1024
    out = pl.pallas_call(
        _mod_kernel,
        grid=(nl, n6 // tn),
        in_specs=[pl.BlockSpec((MOD_ROWS, d), lambda l, j: (0, 0)),
                  pl.BlockSpec((1, d, tn), lambda l, j: (l, 0, j)),
                  pl.BlockSpec((1, 1, tn), lambda l, j: (l, 0, j))],
        out_specs=pl.BlockSpec((1, MOD_ROWS, tn), lambda l, j: (l, 0, j)),
        out_shape=jax.ShapeDtypeStruct((nl, MOD_ROWS, n6), F32),
        compiler_params=_cp("parallel", "parallel"),
        name="modulation",
    )(cond, w_mod, b_mod.reshape(nl, 1, n6))
    return out.reshape(nl, MOD_ROWS, 6, d)


class _Stream:
    def __init__(self, n, t, group0, per_seq):
        self.n, self.t, self.group0, self.per_seq = n, t, group0, per_seq
        self.rows = n * t

    def group_map(self, tile):
        g0, per_seq, t = self.group0, self.per_seq, self.t
        if per_seq:
            return lambda i: (g0 + (i * tile) // t, 0, 0)
        return lambda i: (g0, 0, 0)


def _mm_tile(st):
    return min(MM_TILE, st.t)


def _row_tile(st):
    return min(ROW_TILE, st.t)


def _normmod_kernel(x_ref, g_ref, mod_ref, o_ref, *, sc_idx, sh_idx):
    x = x_ref[...]
    y = x * lax.rsqrt(jnp.mean(x * x, axis=-1, keepdims=True) + 1e-6) * g_ref[...]
    o_ref[...] = _bf(y * (1.0 + mod_ref[0, sc_idx:sc_idx + 1, :]) + mod_ref[0, sh_idx:sh_idx + 1, :])


def _normmod(st, x, gain, mod_l, sc_idx, sh_idx):
    m, d = x.shape
    tm = _mm_tile(st)
    return pl.pallas_call(
        functools.partial(_normmod_kernel, sc_idx=sc_idx, sh_idx=sh_idx),
        grid=(m // tm,),
        in_specs=[pl.BlockSpec((tm, d), lambda i: (i, 0)),
                  pl.BlockSpec((1, d), lambda i: (0, 0)),
                  pl.BlockSpec((1, 6, d), st.group_map(tm))],
        out_specs=pl.BlockSpec((tm, d), lambda i: (i, 0)),
        out_shape=jax.ShapeDtypeStruct((m, d), BF16),
        compiler_params=_cp("parallel"),
        name="normmod",
    )(x, gain.reshape(1, d), mod_l)


def _rms_kernel(x_ref, g_ref, o_ref):
    x = x_ref[...]
    o_ref[...] = x * lax.rsqrt(jnp.mean(x * x, axis=-1, keepdims=True) + 1e-6) * g_ref[...]


def _final_norm(st, x, gain):
    m, d = x.shape
    tm = _mm_tile(st)
    return pl.pallas_call(
        _rms_kernel,
        grid=(m // tm,),
        in_specs=[pl.BlockSpec((tm, d), lambda i: (i, 0)), pl.BlockSpec((1, d), lambda i: (0, 0))],
        out_specs=pl.BlockSpec((tm, d), lambda i: (i, 0)),
        out_shape=jax.ShapeDtypeStruct((m, d), F32),
        compiler_params=_cp("parallel"),
        name="final_norm",
    )(x, gain.reshape(1, d))


def _mm_kernel(x_ref, w_ref, o_ref):
    o_ref[...] = _dot(x_ref[...], w_ref[...]).astype(o_ref.dtype)


def _matmul(st, x, w, out_dtype=F32):
    m, k = x.shape
    n = w.shape[1]
    tm = _mm_tile(st)
    return pl.pallas_call(
        _mm_kernel,
        grid=(m // tm,),
        in_specs=[pl.BlockSpec((tm, k), lambda i: (i, 0)), pl.BlockSpec((k, n), lambda i: (0, 0))],
        out_specs=pl.BlockSpec((tm, n), lambda i: (i, 0)),
        out_shape=jax.ShapeDtypeStruct((m, n), out_dtype),
        compiler_params=_cp("parallel"),
        name="in_proj",
    )(x, w)


def _att_prep_kernel(*refs, rope):
    if rope:
        p_ref, qn_ref, kn_ref, bdq_ref, bdk_ref, cos_ref, sa_ref, sb_ref, q_out, k_out, v_out = refs
    else:
        p_ref, qn_ref, kn_ref, bdq_ref, bdk_ref, q_out, k_out, v_out, k_leaf = refs
    p = p_ref[...]
    aq = p[:, :512]
    ak = p[:, 512:640]
    av = p[:, 640:768]
    q = aq * lax.rsqrt(_dot(aq * aq, bdq_ref[...], HI) + 1e-6) * qn_ref[...]
    k = ak * lax.rsqrt(_dot(ak * ak, bdk_ref[...], HI) + 1e-6) * kn_ref[...]
    if rope:
        c, sa, sb = cos_ref[...], sa_ref[...], sb_ref[...]
        k = k * c + pltpu.roll(k, LANE - 16, 1) * sa + pltpu.roll(k, 16, 1) * sb
        c4, sa4, sb4 = _lanes([c] * 4), _lanes([sa] * 4), _lanes([sb] * 4)
        q = q * c4 + pltpu.roll(q, 512 - 16, 1) * sa4 + pltpu.roll(q, 16, 1) * sb4
    else:
        k_leaf[...] = k
    tr = p.shape[0]
    one_col = (lax.broadcasted_iota(jnp.int32, (tr, LANE - HEAD_DIM), 1) == 0).astype(F32)
    q = q * (HEAD_DIM ** -0.5 * LOG2E)
    q_len = jnp.sqrt(_dot(q * q, bdq_ref[...], HI) * HEAD_DIM)
    q_out[...] = _bf(_lanes([piece for h in range(ATT_HEADS)
                             for piece in (q[:, h * HEAD_DIM:(h + 1) * HEAD_DIM],
                                           one_col * q_len[:, h * HEAD_DIM:(h + 1) * HEAD_DIM])]))
    for kv in range(ATT_KV):
        sl = slice(kv * HEAD_DIM, (kv + 1) * HEAD_DIM)
        k_out[kv] = _bf(_lanes([k[:, sl], one_col]))
        v_out[kv] = _bf(_lanes([av[:, sl], one_col]))


def _block_diag(width, block, value):
    idx = np.arange(width) // block
    return jnp.asarray((idx[:, None] == idx[None, :]).astype(np.float32) * value)


def _att_prep(st, proj, q_norm, k_norm, rope_tables):
    m = proj.shape[0]
    tr = _row_tile(st)
    rope = rope_tables is not None
    full = lambda shape: pl.BlockSpec(shape, lambda i: (0, 0))
    in_specs = [pl.BlockSpec((tr, ATT_W), lambda i: (i, 0)), full((1, 512)), full((1, 128)),
                full((512, 512)), full((128, 128))]
    args = [proj, jnp.tile(q_norm, ATT_HEADS).reshape(1, 512), jnp.tile(k_norm, ATT_KV).reshape(1, 128),
            _block_diag(512, HEAD_DIM, 1.0 / HEAD_DIM), _block_diag(128, HEAD_DIM, 1.0 / HEAD_DIM)]
    if rope:
        tps = st.t // tr
        in_specs += [pl.BlockSpec((tr, LANE), lambda i: (i % tps, 0))] * 3
        args += list(rope_tables)
    kv_spec = pl.BlockSpec((ATT_KV, tr, LANE), lambda i: (0, i, 0))
    kv_shape = jax.ShapeDtypeStruct((ATT_KV, m, LANE), BF16)
    out_specs = [pl.BlockSpec((tr, ATT_HEADS * LANE), lambda i: (i, 0)), kv_spec, kv_spec]
    out_shape = [jax.ShapeDtypeStruct((m, ATT_HEADS * LANE), BF16), kv_shape, kv_shape]
    if not rope:
        out_specs.append(pl.BlockSpec((tr, 128), lambda i: (i, 0)))
        out_shape.append(jax.ShapeDtypeStruct((m, 128), F32))
    return pl.pallas_call(
        functools.partial(_att_prep_kernel, rope=rope),
        grid=(m // tr,),
        in_specs=in_specs,
        out_specs=out_specs,
        out_shape=out_shape,
        compiler_params=_cp("parallel"),
        name="att_prep",
    )(*args)


def _join_cache(st, own, cache):
    n, p, kv, hd = cache.shape
    c = jnp.transpose(cache, (2, 0, 1, 3))
    pad = jnp.zeros((kv, n, p, LANE - hd), c.dtype).at[..., 0].set(1.0)
    c = _bf(jnp.concatenate([c, pad], axis=-1))
    joined = jnp.concatenate([own.reshape(kv, n, st.t, LANE), c], axis=2)
    return joined.reshape(kv, n * (st.t + p), LANE)


def _rope_tables(t):
    half = HEAD_DIM // 2
    nf = half // 2
    freqs = ROPE_THETA ** (-jnp.arange(nf, dtype=F32) / nf)
    tt = jnp.arange(t)
    ang_r = (tt // GRID_W).astype(F32)[:, None] * freqs[None, :]
    ang_c = (tt % GRID_W).astype(F32)[:, None] * freqs[None, :]
    zero = jnp.zeros_like(ang_r)
    cos = jnp.concatenate([jnp.cos(ang_r)] * 2 + [jnp.cos(ang_c)] * 2, axis=1)
    sa = jnp.concatenate([-jnp.sin(ang_r), zero, -jnp.sin(ang_c), zero], axis=1)
    sb = jnp.concatenate([zero, jnp.sin(ang_r), zero, jnp.sin(ang_c)], axis=1)
    return tuple(jnp.tile(a, (1, LANE // HEAD_DIM)) for a in (cos, sa, sb))


def _key_bound_kernel(k_ref, o_ref):
    ones = jnp.ones((LANE, LANE), F32)
    for kv in range(ATT_KV):
        k = k_ref[kv].astype(F32)
        k = jnp.where(lax.broadcasted_iota(jnp.int32, k.shape, 1) < HEAD_DIM, k, 0.0)
        best = jnp.max(_dot(k * k, ones, HI), axis=0, keepdims=True)
        o_ref[kv] = jnp.broadcast_to(best, (SUBLANE, LANE))


def _key_bound(n, keys_per_seq, k):
    return pl.pallas_call(
        _key_bound_kernel,
        grid=(n,),
        in_specs=[pl.BlockSpec((ATT_KV, keys_per_seq, LANE), lambda b: (0, b, 0))],
        out_specs=pl.BlockSpec((None, ATT_KV, SUBLANE, LANE), lambda b: (b, 0, 0, 0)),
        out_shape=jax.ShapeDtypeStruct((n, ATT_KV, SUBLANE, LANE), F32),
        compiler_params=_cp("parallel"),
        name="key_bound",
    )(k)


def _attn_kernel(q_ref, k_ref, v_ref, kb_ref, o_ref, m_scr, acc_scr, *, tq, n_tiles):
    rows = ATT_GROUP * tq
    lane = lax.broadcasted_iota(jnp.int32, (rows, LANE), 1)
    outs = []
    for kv in range(ATT_KV):
        qs = _rows([q_ref[:, (kv * ATT_GROUP + g) * LANE:(kv * ATT_GROUP + g + 1) * LANE]
                    for g in range(ATT_GROUP)]).astype(F32)

        def key_tile(ref, j):
            return ref[kv, pl.ds(pl.multiple_of(j * KEY_TILE, KEY_TILE), KEY_TILE), :]

        def weighted_sum(q_shift, unroll):
            acc_scr[...] = jnp.zeros((rows, LANE), F32)

            def sum_body(j, carry):
                p = jnp.exp2(_dot_nt(q_shift, key_tile(k_ref, j)))
                acc_scr[...] += _dot(_bf(p), key_tile(v_ref, j))
                return carry

            lax.fori_loop(0, n_tiles, sum_body, 0, unroll=unroll)
            return acc_scr[...]

        k_len = jnp.sqrt(kb_ref[kv][0:1, :])
        unroll = max(u for u in range(1, KEY_UNROLL + 1) if n_tiles % u == 0)
        acc = weighted_sum(_bf(qs * jnp.where(lane == HEAD_DIM, -k_len, 1.0)), unroll)

        def exact_shift():
            m_scr[...] = jnp.full((rows, LANE), -jnp.inf, F32)
            q0 = _bf(jnp.where(lane == HEAD_DIM, 0.0, qs))

            def max_body(j, carry):
                sc = _dot_nt(q0, key_tile(k_ref, j))
                m_scr[...] = jnp.maximum(m_scr[...], jnp.maximum(sc[:, :LANE], sc[:, LANE:]))
                return carry

            lax.fori_loop(0, n_tiles, max_body, 0)
            row_max = jnp.max(m_scr[...], axis=-1, keepdims=True)
            return weighted_sum(_bf(jnp.where(lane == HEAD_DIM, -row_max, qs)), 1)

        row_sum_ok = jnp.min(acc[:, HEAD_DIM:HEAD_DIM + 1]) >= ATT_MIN_ROW_SUM
        acc = lax.cond(row_sum_ok, lambda: acc, exact_shift)
        o = acc[:, :HEAD_DIM] / acc[:, HEAD_DIM:HEAD_DIM + 1]
        outs += [o[g * tq:(g + 1) * tq] for g in range(ATT_GROUP)]
    o_ref[...] = _bf(_lanes(outs))


def _attention(st, q, k, v):
    tq = min(128, st.t)
    nq = st.t // tq
    keys = k.shape[1] // st.n
    assert keys % KEY_TILE == 0
    kv_spec = pl.BlockSpec((ATT_KV, keys, LANE), lambda b, i: (0, b, 0))
    rows = ATT_GROUP * tq
    return pl.pallas_call(
        functools.partial(_attn_kernel, tq=tq, n_tiles=keys // KEY_TILE),
        grid=(st.n, nq),
        in_specs=[pl.BlockSpec((tq, ATT_HEADS * LANE), lambda b, i: (b * nq + i, 0)), kv_spec, kv_spec,
                  pl.BlockSpec((None, ATT_KV, SUBLANE, LANE), lambda b, i: (b, 0, 0, 0))],
        out_specs=pl.BlockSpec((tq, 512), lambda b, i: (b * nq + i, 0)),
        out_shape=jax.ShapeDtypeStruct((st.rows, 512), BF16),
        scratch_shapes=[pltpu.VMEM((rows, LANE), F32), pltpu.VMEM((rows, LANE), F32)],
        compiler_params=_cp("parallel", "parallel"),
        name="attention",
    )(q, k, v, _key_bound(st.n, keys, k))


def _halo_specs(st, width):
    tr = _row_tile(st)
    per8 = tr // SUBLANE
    last8 = st.rows // SUBLANE - 1
    return [pl.BlockSpec((tr, width), lambda i: (i, 0)),
            pl.BlockSpec((SUBLANE, width), lambda i: (jnp.maximum(i * per8 - 1, 0), 0)),
            pl.BlockSpec((SUBLANE, width), lambda i: (jnp.minimum((i + 1) * per8, last8), 0))]


def _neighbours(cur, prev8, next8, tiles_per_seq):
    tr = cur.shape[0]
    j = pl.program_id(0) % tiles_per_seq
    pr = jnp.where(j != 0, prev8[SUBLANE - 1:SUBLANE], 0.0)
    nx = jnp.where(j != tiles_per_seq - 1, next8[0:1], 0.0)
    row = lax.broadcasted_iota(jnp.int32, cur.shape, 0)
    x_prev = jnp.where(row == 0, pr, pltpu.roll(cur, 1, 0))
    x_next = jnp.where(row == tr - 1, nx, pltpu.roll(cur, tr - 1, 0))
    return x_prev, x_next


def _dir_masks(z):
    ri = lax.broadcasted_iota(jnp.int32, (CHUNK, CHUNK), 0)
    ci = lax.broadcasted_iota(jnp.int32, (CHUNK, CHUNK), 1)
    return ((ci <= ri), (ci < ri)) if z == 0 else ((ci >= ri), (ci > ri))


def _tri_blocks(tr):
    i = np.arange(tr)
    same = (i[:, None] // CHUNK) == (i[None, :] // CHUNK)
    lower = same & (i[None, :] <= i[:, None])
    upper = same & (i[None, :] >= i[:, None])
    return jnp.asarray(lower.astype(np.float32)), jnp.asarray(upper.astype(np.float32))


def _scan_blocks(st):
    ncb = min(SCAN_CHUNKS, st.t // CHUNK)
    rb = ncb * CHUNK
    nblk = st.t // rb
    fwd = lambda b, j: (b * nblk + j, 0)
    bwd = lambda b, j: (b * nblk + nblk - 1 - j, 0)
    return ncb, rb, nblk, fwd, bwd


def _chunk_rows(step, z, ncb):
    cc = step if z == 0 else ncb - 1 - step
    return slice(cc * CHUNK, (cc + 1) * CHUNK)


def _ssd_prep_kernel(cur_ref, prev_ref, next_ref, cw_ref, cb_ref, dtb_ref, alog_ref, lo_ref, up_ref,
                     xbc_out, dt_out, cum_out, dtt_out, cumt_out, *, tiles_per_seq):
    cur = cur_ref[...]
    xc = cur[:, 512:1280]
    x_prev, x_next = _neighbours(xc, prev_ref[:, 512:1280], next_ref[:, 512:1280], tiles_per_seq)
    cw = cw_ref[...]
    conv = cb_ref[...] + x_prev * cw[0:1] + xc * cw[1:2] + x_next * cw[2:3]
    xbc_out[...] = _silu(conv)
    dt = _softplus(cur[:, 1280:1408] + dtb_ref[...])
    dt_out[...] = dt
    ld = dt * -jnp.exp(alog_ref[...])
    lane = lax.broadcasted_iota(jnp.int32, ld.shape, 1)
    cum = jnp.where(lane < SSD_HEADS, _dot(lo_ref[...], ld, HI), _dot(up_ref[...], ld, HI))
    cum_out[...] = cum
    pick = (lax.broadcasted_iota(jnp.int32, (2 * SSD_HEADS, LANE), 0)
            == lax.broadcasted_iota(jnp.int32, (2 * SSD_HEADS, LANE), 1)).astype(F32)
    dtt_out[...] = _dot_nt(pick, dt, HI)
    cumt_out[...] = _dot_nt(pick, cum, HI)


def _pad_lanes(v, width=LANE):
    v = v.reshape(1, -1)
    return jnp.pad(v, ((0, 0), (0, width - v.shape[1])))


def _ssd_prep(st, proj, conv_w, conv_b, dt_bias, a_log):
    m = proj.shape[0]
    tr = _row_tile(st)
    full = lambda shape: pl.BlockSpec(shape, lambda i: (0, 0))
    row = lambda w: pl.BlockSpec((tr, w), lambda i: (i, 0))
    col = pl.BlockSpec((2 * SSD_HEADS, tr), lambda i: (0, i))
    lower, upper = _tri_blocks(tr)
    return pl.pallas_call(
        functools.partial(_ssd_prep_kernel, tiles_per_seq=st.t // tr),
        grid=(m // tr,),
        in_specs=_halo_specs(st, SSD_W) + [full((3, 768)), full((1, 768)), full((1, LANE)), full((1, LANE)),
                                           full((tr, tr)), full((tr, tr))],
        out_specs=[row(768), row(LANE), row(LANE), col, col],
        out_shape=[jax.ShapeDtypeStruct((m, 768), F32), jax.ShapeDtypeStruct((m, LANE), F32),
                   jax.ShapeDtypeStruct((m, LANE), F32), jax.ShapeDtypeStruct((2 * SSD_HEADS, m), F32),
                   jax.ShapeDtypeStruct((2 * SSD_HEADS, m), F32)],
        compiler_params=_cp("parallel"),
        name="ssd_prep",
    )(proj, proj, proj, conv_w.T, conv_b.reshape(1, 768), _pad_lanes(dt_bias), _pad_lanes(a_log), lower, upper)


def _ssd_scan_kernel(xf_ref, xb_ref, dtf_ref, dtb_ref, cf_ref, cb_ref, dttf_ref, dttb_ref, ctf_ref, ctb_ref,
                     s0_ref, yf_ref, yb_ref, s_ref, *, ncb):
    @pl.when(pl.program_id(1) == 0)
    def _():
        s_ref[...] = s0_ref[...]

    hpg = SSD_HEADS // SSD_GROUPS
    state = {(z, g): _lanes([s_ref[z, g * hpg + hh] for hh in range(hpg)])
             for z in range(2) for g in range(SSD_GROUPS)}
    refs = ((xf_ref, dtf_ref, cf_ref, dttf_ref, ctf_ref, yf_ref), (xb_ref, dtb_ref, cb_ref, dttb_ref, ctb_ref, yb_ref))
    groups = [(z, g) for z in range(2) for g in range(SSD_GROUPS)]
    heads = [(z, h) for z in range(2) for h in range(SSD_HEADS)]
    for step in range(ncb):
        xbc, dt, cum, dtt, cumt, incl, rows = {}, {}, {}, {}, {}, {}, {}
        for z in range(2):
            x_ref, dt_ref, c_ref, dtt_ref, ct_ref, _ = refs[z]
            rows[z] = _chunk_rows(step, z, ncb)
            xbc[z], dt[z], cum[z] = x_ref[rows[z], :], dt_ref[rows[z], :], c_ref[rows[z], :]
            dtt[z], cumt[z] = dtt_ref[:, rows[z]], ct_ref[:, rows[z]]
            incl[z] = _dir_masks(z)[0]
        bmat = {(z, g): xbc[z][:, 512 + g * SSD_N:512 + (g + 1) * SSD_N] for z, g in groups}
        cmat = {(z, g): _bf(xbc[z][:, 640 + g * SSD_N:640 + (g + 1) * SSD_N]) for z, g in groups}
        cb = {k: _dot_nt(cmat[k], _bf(bmat[k])) for k in groups}
        cs = {k: _dot(cmat[k], _bf(state[k])) for k in groups}
        gcol, glast, xs = {}, {}, {}
        for z, h in heads:
            ln = z * SSD_HEADS + h
            gcol[z, h] = cum[z][:, ln:ln + 1]
            glast[z, h] = gcol[z, h][CHUNK - 1:CHUNK] if z == 0 else gcol[z, h][0:1]
            xs[z, h] = xbc[z][:, h * SSD_P:(h + 1) * SSD_P]
        inc = {}
        for z, g in groups:
            xw = []
            for h in range(g * hpg, (g + 1) * hpg):
                ln = z * SSD_HEADS + h
                xw.append(xs[z, h] * (dt[z][:, ln:ln + 1] * jnp.exp(glast[z, h] - gcol[z, h])))
            inc[z, g] = _dot_tn(_bf(bmat[z, g]), _bf(_lanes(xw)))
        att = {}
        for z, h in heads:
            ln = z * SSD_HEADS + h
            dec = jnp.exp(jnp.where(incl[z], gcol[z, h] - cumt[z][ln:ln + 1], -jnp.inf))
            att[z, h] = _bf(cb[z, h // hpg] * dec * dtt[z][ln:ln + 1])
        intra = {k: _dot(att[k], _bf(xs[k])) for k in heads}
        for z in range(2):
            outs = []
            for g in range(SSD_GROUPS):
                hs = range(g * hpg, (g + 1) * hpg)
                e_in = _lanes([jnp.broadcast_to(jnp.exp(gcol[z, h]), (CHUNK, SSD_P)) for h in hs])
                outs.append(_lanes([intra[z, h] for h in hs]) + cs[z, g] * e_in)
                e_last = _lanes([jnp.broadcast_to(jnp.exp(glast[z, h]), (1, SSD_P)) for h in hs])
                state[z, g] = state[z, g] * e_last + inc[z, g]
            refs[z][5][rows[z], :] = _lanes(outs)
    for z in range(2):
        for h in range(SSD_HEADS):
            s_ref[z, h] = state[z, h // hpg][:, (h % hpg) * SSD_P:(h % hpg + 1) * SSD_P]


def _ssd_scan(st, xbc, dt, cum, dtt, cumt, s0):
    ncb, rb, nblk, fwd, bwd = _scan_blocks(st)
    fwd_t = lambda b, j: (0, fwd(b, j)[0])
    bwd_t = lambda b, j: (0, bwd(b, j)[0])
    state_spec = pl.BlockSpec((None, 2, SSD_HEADS, SSD_N, SSD_P), lambda b, j: (b, 0, 0, 0, 0))
    rows = lambda w, m: pl.BlockSpec((rb, w), m)
    cols = lambda m: pl.BlockSpec((2 * SSD_HEADS, rb), m)
    return pl.pallas_call(
        functools.partial(_ssd_scan_kernel, ncb=ncb),
        grid=(st.n, nblk),
        in_specs=[rows(768, fwd), rows(768, bwd), rows(LANE, fwd), rows(LANE, bwd), rows(LANE, fwd), rows(LANE, bwd),
                  cols(fwd_t), cols(bwd_t), cols(fwd_t), cols(bwd_t), state_spec],
        out_specs=[rows(512, fwd), rows(512, bwd), state_spec],
        out_shape=[jax.ShapeDtypeStruct((st.rows, 512), F32), jax.ShapeDtypeStruct((st.rows, 512), F32),
                   jax.ShapeDtypeStruct((st.n, 2, SSD_HEADS, SSD_N, SSD_P), F32)],
        compiler_params=_cp("parallel", "arbitrary"),
        name="ssd_scan",
    )(xbc, xbc, dt, dt, cum, cum, dtt, dtt, cumt, cumt, s0)


def _ssd_post_kernel(yf_ref, yb_ref, xbc_ref, p_ref, d_ref, g_ref, o_ref):
    y = yf_ref[...] + yb_ref[...] + d_ref[...] * xbc_ref[:, :512]
    y = y * _silu(p_ref[:, :512])
    o_ref[...] = _bf(y * lax.rsqrt(jnp.mean(y * y, axis=-1, keepdims=True) + 1e-6) * g_ref[...])


def _ssd_post(st, y_f, y_b, xbc, proj, ssd_d, ssd_norm):
    m = proj.shape[0]
    tr = _row_tile(st)
    full = lambda shape: pl.BlockSpec(shape, lambda i: (0, 0))
    row = lambda w: pl.BlockSpec((tr, w), lambda i: (i, 0))
    return pl.pallas_call(
        _ssd_post_kernel,
        grid=(m // tr,),
        in_specs=[row(512), row(512), row(768), row(SSD_W), full((1, 512)), full((1, 512))],
        out_specs=row(512),
        out_shape=jax.ShapeDtypeStruct((m, 512), BF16),
        compiler_params=_cp("parallel"),
        name="ssd_post",
    )(y_f, y_b, xbc, proj, jnp.repeat(ssd_d, SSD_P).reshape(1, 512), ssd_norm.reshape(1, 512))


GLA_QK = GLA_HEADS * GLA_DK
GLA_SAFE_RANGE = 60.0


def _gla_prep_kernel(p_ref, g2_ref, gb_ref, lo_ref, up_ref, cum_out):
    logit = _dot(p_ref[:, 1024:1152], g2_ref[...], HI) + gb_ref[...]
    log_a = -_softplus(-logit) * (1.0 / GLA_GATE_NORM)
    cum_out[:, :GLA_QK] = _dot(lo_ref[...], log_a[:, :GLA_QK], HI)
    cum_out[:, GLA_QK:] = _dot(up_ref[...], log_a[:, GLA_QK:], HI)


def _gla_prep(st, proj, g2p, gb):
    m = proj.shape[0]
    tr = _row_tile(st)
    full = lambda shape: pl.BlockSpec(shape, lambda i: (0, 0))
    lower, upper = _tri_blocks(tr)
    return pl.pallas_call(
        _gla_prep_kernel,
        grid=(m // tr,),
        in_specs=[pl.BlockSpec((tr, GLA_W), lambda i: (i, 0)), full((LANE, 2 * GLA_QK)), full((1, 2 * GLA_QK)),
                  full((tr, tr)), full((tr, tr))],
        out_specs=pl.BlockSpec((tr, 2 * GLA_QK), lambda i: (i, 0)),
        out_shape=jax.ShapeDtypeStruct((m, 2 * GLA_QK), F32),
        compiler_params=_cp("parallel"),
        name="gla_prep",
    )(proj, g2p, gb, lower, upper)


def _gla_intra_exact(p_ref, c_ref, rows, z, seg_ref):
    r0 = rows.start
    q = p_ref[rows, 0:GLA_QK] * (GLA_DK ** -0.5)
    cum = c_ref[rows, :]
    row = lax.broadcasted_iota(jnp.int32, (CHUNK, 1), 0)
    seg = seg_ref[...]

    def body(j, acc):
        kj = p_ref[pl.ds(r0 + j, 1), GLA_QK:2 * GLA_QK]
        vj = p_ref[pl.ds(r0 + j, 1), 2 * GLA_QK:2 * GLA_QK + GLA_HEADS * GLA_DV]
        seen = (row >= j) if z == 0 else (row <= j)
        w = jnp.where(seen, q * kj * jnp.exp(jnp.minimum(cum - c_ref[pl.ds(r0 + j, 1), :], 0.0)), 0.0)
        score = _dot(w, seg, HI)
        return tuple(a + score[:, h:h + 1] * vj[:, h * GLA_DV:(h + 1) * GLA_DV] for h, a in enumerate(acc))

    zero = jnp.zeros((CHUNK, GLA_DV), F32)
    return list(lax.fori_loop(0, CHUNK, body, (zero,) * GLA_HEADS))


def _gla_scan_kernel(pf_ref, pb_ref, cf_ref, cb_ref, seg_ref, s0_ref, yf_ref, yb_ref, s_ref, *, ncb):
    @pl.when(pl.program_id(1) == 0)
    def _():
        s_ref[...] = s0_ref[...]

    refs = ((pf_ref, cf_ref, yf_ref), (pb_ref, cb_ref, yb_ref))
    heads = [(z, h) for z in range(2) for h in range(GLA_HEADS)]
    hsl = [slice(h * GLA_DK, (h + 1) * GLA_DK) for h in range(GLA_HEADS)]
    state = {(z, h): s_ref[z, h] for z, h in heads}
    chunks = [(step, z) for step in range(ncb) for z in range(2)]
    rows = {(step, z): _chunk_rows(step, z, ncb) for step, z in chunks}
    cum = {k: refs[k[1]][1][rows[k], :] for k in chunks}
    mid = {k: cum[k][CHUNK // 2:CHUNK // 2 + 1] for k in chunks}
    span = functools.reduce(jnp.maximum, [jnp.max(jnp.abs(cum[k] - mid[k])) for k in chunks])

    def v_of(k, h):
        return _bf(refs[k[1]][0][rows[k], 2 * GLA_QK + h * GLA_DV:2 * GLA_QK + (h + 1) * GLA_DV])

    def intra_factored():
        att = {}
        for k in chunks:
            p_ref = refs[k[1]][0]
            q_mid = _bf(p_ref[rows[k], 0:GLA_QK] * (GLA_DK ** -0.5) * jnp.exp(cum[k] - mid[k]))
            k_mid = _bf(p_ref[rows[k], GLA_QK:2 * GLA_QK] * jnp.exp(mid[k] - cum[k]))
            incl = _dir_masks(k[1])[0]
            for h in range(GLA_HEADS):
                att[k, h] = _bf(jnp.where(incl, _dot_nt(q_mid[:, hsl[h]], k_mid[:, hsl[h]]), 0.0))
        return [_dot(att[k, h], v_of(k, h)) for k in chunks for h in range(GLA_HEADS)]

    def intra_exact():
        out = []
        for k in chunks:
            out += _gla_intra_exact(refs[k[1]][0], refs[k[1]][1], rows[k], k[1], seg_ref)
        return out

    intra = lax.cond(span <= GLA_SAFE_RANGE, intra_factored, intra_exact)
    intra = {(k, h): intra[i * GLA_HEADS + h] for i, k in enumerate(chunks) for h in range(GLA_HEADS)}

    for step in range(ncb):
        q_in, k_out, e_last = {}, {}, {}
        for z in range(2):
            k = (step, z)
            p_ref = refs[z][0]
            glast = cum[k][CHUNK - 1:CHUNK] if z == 0 else cum[k][0:1]
            q_in[z] = _bf(p_ref[rows[k], 0:GLA_QK] * (GLA_DK ** -0.5) * jnp.exp(cum[k]))
            k_out[z] = _bf(p_ref[rows[k], GLA_QK:2 * GLA_QK] * jnp.exp(glast - cum[k]))
            e_last[z] = jnp.exp(glast)
        inter = {(z, h): _dot_nt(q_in[z][:, hsl[h]], _bf(state[z, h])) for z, h in heads}
        inc = {(z, h): _dot_tn(v_of((step, z), h), k_out[z][:, hsl[h]]) for z, h in heads}
        for z in range(2):
            refs[z][2][rows[step, z], :] = _lanes([intra[(step, z), h] + inter[z, h] for h in range(GLA_HEADS)])
            for h in range(GLA_HEADS):
                state[z, h] = state[z, h] * e_last[z][:, hsl[h]] + inc[z, h]
    for z, h in heads:
        s_ref[z, h] = state[z, h]


def _gla_scan(st, proj, cum, s0_t):
    ncb, rb, nblk, fwd, bwd = _scan_blocks(st)
    seg = _block_diag(GLA_QK, GLA_DK, 1.0)[:, ::GLA_DK]
    seg = jnp.pad(seg, ((0, 0), (0, LANE - GLA_HEADS)))
    state_spec = pl.BlockSpec((None, 2, GLA_HEADS, GLA_DV, GLA_DK), lambda b, j: (b, 0, 0, 0, 0))
    qkv_w = 2 * GLA_QK + GLA_HEADS * GLA_DV
    return pl.pallas_call(
        functools.partial(_gla_scan_kernel, ncb=ncb),
        grid=(st.n, nblk),
        in_specs=[pl.BlockSpec((rb, qkv_w), fwd), pl.BlockSpec((rb, qkv_w), bwd),
                  pl.BlockSpec((rb, GLA_QK), fwd), pl.BlockSpec((rb, GLA_QK), lambda b, j: (bwd(b, j)[0], 1)),
                  pl.BlockSpec((GLA_QK, LANE), lambda b, j: (0, 0)), state_spec],
        out_specs=[pl.BlockSpec((rb, 512), fwd), pl.BlockSpec((rb, 512), bwd), state_spec],
        out_shape=[jax.ShapeDtypeStruct((st.rows, 512), F32), jax.ShapeDtypeStruct((st.rows, 512), F32),
                   jax.ShapeDtypeStruct((st.n, 2, GLA_HEADS, GLA_DV, GLA_DK), F32)],
        compiler_params=_cp("parallel", "arbitrary"),
        name="gla_scan",
    )(proj, proj, cum, cum, seg, s0_t)


def _gla_post_kernel(yf_ref, yb_ref, p_ref, g_ref, o_ref):
    o = yf_ref[...] + yb_ref[...]
    gate = _silu(p_ref[:, 1152:1664])
    outs = []
    for h in range(GLA_HEADS):
        oh = o[:, h * GLA_DV:(h + 1) * GLA_DV]
        outs.append(oh * lax.rsqrt(jnp.mean(oh * oh, axis=-1, keepdims=True) + 1e-6) * g_ref[...])
    o_ref[...] = _bf(_lanes(outs) * gate)


def _gla_post(st, y_f, y_b, proj, gla_norm):
    m = proj.shape[0]
    tr = _row_tile(st)
    return pl.pallas_call(
        _gla_post_kernel,
        grid=(m // tr,),
        in_specs=[pl.BlockSpec((tr, 512), lambda i: (i, 0)), pl.BlockSpec((tr, 512), lambda i: (i, 0)),
                  pl.BlockSpec((tr, GLA_W), lambda i: (i, 0)), pl.BlockSpec((1, GLA_DV), lambda i: (0, 0))],
        out_specs=pl.BlockSpec((tr, 512), lambda i: (i, 0)),
        out_shape=jax.ShapeDtypeStruct((m, 512), BF16),
        compiler_params=_cp("parallel"),
        name="gla_post",
    )(y_f, y_b, proj, gla_norm.reshape(1, GLA_DV))


def _rwkv_prep_kernel(cur_ref, prev_ref, next_ref, mu_ref, w2_ref, w0_ref, a2_ref, a0_ref, g2_ref,
                      kkw_ref, ka_ref, rk_ref, bd_ref, in_out, lw_out, post_out, *, tiles_per_seq):
    cur = cur_ref[...]
    x_prev, x_next = _neighbours(cur, prev_ref[...], next_ref[...], tiles_per_seq)
    blk = cur + (0.5 * (x_prev + x_next) - cur) * mu_ref[...]
    r, k, v = blk[:, 0:512], blk[:, 512:1024], blk[:, 1024:1536]
    w_logit = w0_ref[...] + _dot(jnp.tanh(blk[:, 1536:1600]), w2_ref[...], HI)
    lw_out[...] = -RWKV_DECAY_SCALE * jax.nn.sigmoid(w_logit)
    a = jax.nn.sigmoid(a0_ref[...] + _dot(blk[:, 1600:1664], a2_ref[...], HI))
    g = _dot(jax.nn.sigmoid(blk[:, 1664:1792]), g2_ref[...], HI)
    bd = bd_ref[...]
    kk = k * kkw_ref[...]
    kk = kk * lax.rsqrt(_dot(kk * kk, bd, HI) + 1e-12)
    k2 = k * (1.0 + (a - 1.0) * ka_ref[...])
    in_out[:, 0:512] = r
    in_out[:, 512:1024] = k2
    in_out[:, 1024:1536] = v
    in_out[:, 1536:2048] = -kk
    in_out[:, 2048:2560] = kk * a
    post_out[:, 0:512] = g
    post_out[:, 512:1024] = _dot(r * k2 * rk_ref[...], bd, HI) * v


def _rwkv_prep(st, proj, mu, w2, w0, a2, a0, g2, kkw, ka, rk):
    m = proj.shape[0]
    tr = _row_tile(st)
    full = lambda shape: pl.BlockSpec(shape, lambda i: (0, 0))
    row = lambda w: pl.BlockSpec((tr, w), lambda i: (i, 0))
    vec = lambda a: a.reshape(1, -1)
    return pl.pallas_call(
        functools.partial(_rwkv_prep_kernel, tiles_per_seq=st.t // tr),
        grid=(m // tr,),
        in_specs=_halo_specs(st, RWKV_W) + [full((1, RWKV_W)), full((64, 1024)), full((1, 1024)),
                                            full((64, 512)), full((1, 512)), full((128, 512)),
                                            full((1, 512)), full((1, 512)), full((1, 512)), full((512, 512))],
        out_specs=[row(2560), row(1024), row(1024)],
        out_shape=[jax.ShapeDtypeStruct((m, 2560), F32), jax.ShapeDtypeStruct((m, 1024), F32),
                   jax.ShapeDtypeStruct((m, 1024), F32)],
        compiler_params=_cp("parallel"),
        name="rwkv_prep",
    )(proj, proj, proj, vec(mu), jnp.concatenate([w2[0], w2[1]], axis=1), vec(w0), a2, vec(a0), g2,
      vec(kkw), vec(ka), vec(rk), _block_diag(512, RWKV_HD, 1.0))


def _rwkv_chunk(x_ref, lw_ref, rows, z):
    incl, strict = _dir_masks(z)
    lw = lw_ref[rows, :]
    cum = _dot(incl.astype(F32), lw, HI)
    cum_x = cum - lw
    mid = cum[CHUNK // 2:CHUNK // 2 + 1]
    glast = cum[CHUNK - 1:CHUNK] if z == 0 else cum[0:1]
    x = x_ref[rows, :]
    r, k, v, al, be = x[:, 0:512], x[:, 512:1024], x[:, 1024:1536], x[:, 1536:2048], x[:, 2048:2560]
    e_mid, e_nmid = jnp.exp(cum - mid), jnp.exp(mid - cum)
    e_out = jnp.exp(glast - cum)
    return dict(incl=incl, strict=strict, v=v,
                r_mid=_bf(r * e_mid), a_mid=_bf(al * jnp.exp(cum_x - mid)),
                b_mid=_bf(be * e_nmid), k_mid=_bf(k * e_nmid),
                r_in=_bf(r * jnp.exp(cum)), a_in=al * jnp.exp(cum_x),
                b_out=_bf(be * e_out), k_out=_bf(k * e_out), e_last=jnp.exp(glast))


_HEAD_SLICES = [slice(h * RWKV_HD, (h + 1) * RWKV_HD) for h in range(RWKV_HEADS)]


def _rwkv_state_free(chunks, hooks):
    hooks = list(hooks)

    def run_hook():
        if hooks:
            hooks.pop(0)()

    chains = [(c, sl) for c in chunks for sl in _HEAD_SLICES]
    run_hook()
    pair = [_dot_nt(_rows([c['a_mid'][:, sl], c['r_mid'][:, sl]]), _rows([c['b_mid'][:, sl], c['k_mid'][:, sl]]))
            for c, sl in chains]
    half = CHUNK
    a_ab = [_bf(jnp.where(c['strict'], p[:half, :half], 0.0)) for (c, _), p in zip(chains, pair)]
    a_ak = [_bf(jnp.where(c['strict'], p[:half, half:], 0.0)) for (c, _), p in zip(chains, pair)]
    a_rb = [_bf(jnp.where(c['incl'], p[half:, :half], 0.0)) for (c, _), p in zip(chains, pair)]
    a_rk = [_bf(jnp.where(c['incl'], p[half:, half:], 0.0)) for (c, _), p in zip(chains, pair)]
    vb = [_bf(c['v'][:, sl]) for c, sl in chains]
    both = [_dot(_rows([ak, rk]), v) for ak, rk, v in zip(a_ak, a_rk, vb)]
    av = [x[:half] for x in both]
    o0 = [x[half:] for x in both]
    sol = [_lanes([c['a_in'][:, sl], x]) for (c, sl), x in zip(chains, av)]
    powr = a_ab
    width = 2 * RWKV_HD
    for it in range(6):
        if it < 5:
            both = [_dot(p, _lanes([_bf(s), p])) for p, s in zip(powr, sol)]
            sol = [s + x[:, :width] for s, x in zip(sol, both)]
            powr = [_bf(x[:, width:]) for x in both]
        else:
            sol = [s + _dot(p, _bf(s)) for p, s in zip(powr, sol)]
        if it in (1, 3):
            run_hook()
    while hooks:
        run_hook()
    return [dict(w=_bf(s[:, :RWKV_HD]), u0=s[:, RWKV_HD:], o0=o, a_rb=a, v=c['v'][:, sl], r_in=c['r_in'][:, sl],
                 b_out=c['b_out'][:, sl], k_out=c['k_out'][:, sl], e_last=c['e_last'][:, sl])
            for (c, sl), s, o, a in zip(chains, sol, o0, a_rb)]


def _rwkv_state_stages(res, state, write_out):
    box = {}

    def read_state():
        box['su'] = [_dot_nt(_rows([c['w'], c['r_in']]), _bf(s)) for c, s in zip(res, state)]

    def update_state():
        box['u'] = [c['u0'] + su[:CHUNK] for c, su in zip(res, box['su'])]
        state[:] = [s * c['e_last'] + _dot_tn(_bf(_rows([u, c['v']])), _rows([c['b_out'], c['k_out']]))
                    for c, s, u in zip(res, state, box['u'])]

    def emit():
        write_out([su[CHUNK:] + _dot(c['a_rb'], _bf(u)) + c['o0'] for c, su, u in zip(res, box['su'], box['u'])])

    return [read_state, update_state, emit]


def _rwkv_scan_kernel(xf_ref, xb_ref, lwf_ref, lwb_ref, s0_ref, yf_ref, yb_ref, s_ref, *, ncb):
    @pl.when(pl.program_id(1) == 0)
    def _():
        s_ref[...] = s0_ref[...]

    state = [s_ref[z, h] for z in range(2) for h in range(RWKV_HEADS)]

    def chunk_rows(step, z):
        cc = step if z == 0 else ncb - 1 - step
        return slice(cc * CHUNK, (cc + 1) * CHUNK)

    def pair_inputs(step):
        return [_rwkv_chunk(xf_ref, lwf_ref, chunk_rows(step, 0), 0),
                _rwkv_chunk(xb_ref, lwb_ref, chunk_rows(step, 1), 1)]

    def writer(step):
        def write_out(outs):
            yf_ref[chunk_rows(step, 0), :] = _lanes(outs[:RWKV_HEADS])
            yb_ref[chunk_rows(step, 1), :] = _lanes(outs[RWKV_HEADS:])
        return write_out

    res = _rwkv_state_free(pair_inputs(0), [])
    for step in range(ncb):
        hooks = _rwkv_state_stages(res, state, writer(step))
        if step + 1 < ncb:
            res = _rwkv_state_free(pair_inputs(step + 1), hooks)
        else:
            for hook in hooks:
                hook()
    for z in range(2):
        for h in range(RWKV_HEADS):
            s_ref[z, h] = state[z * RWKV_HEADS + h]


def _rwkv_scan(st, xin, lw, s0):
    ncb, rb, nblk, fwd, bwd = _scan_blocks(st)
    state_spec = pl.BlockSpec((None, 2, RWKV_HEADS, RWKV_HD, RWKV_HD), lambda b, j: (b, 0, 0, 0, 0))
    return pl.pallas_call(
        functools.partial(_rwkv_scan_kernel, ncb=ncb),
        grid=(st.n, nblk),
        in_specs=[pl.BlockSpec((rb, 2560), fwd), pl.BlockSpec((rb, 2560), bwd),
                  pl.BlockSpec((rb, 512), fwd), pl.BlockSpec((rb, 512), lambda b, j: (bwd(b, j)[0], 1)),
                  state_spec],
        out_specs=[pl.BlockSpec((rb, 512), fwd), pl.BlockSpec((rb, 512), bwd), state_spec],
        out_shape=[jax.ShapeDtypeStruct((st.rows, 512), F32), jax.ShapeDtypeStruct((st.rows, 512), F32),
                   jax.ShapeDtypeStruct((st.n, 2, RWKV_HEADS, RWKV_HD, RWKV_HD), F32)],
        compiler_params=_cp("parallel", "arbitrary"),
        name="rwkv_scan",
    )(xin, xin, lw, lw, s0)


def _rwkv_post_kernel(yf_ref, yb_ref, post_ref, lng_ref, lnb_ref, bd_ref, o_ref):
    o = yf_ref[...] + yb_ref[...]
    bd = bd_ref[...]
    d = o - _dot(o, bd, HI)
    o = d * lax.rsqrt(_dot(d * d, bd, HI) + RWKV_LN_EPS) * lng_ref[...] + lnb_ref[...]
    o_ref[...] = _bf((o + post_ref[:, 512:1024]) * post_ref[:, 0:512])


def _rwkv_post(st, y_f, y_b, post, ln_g, ln_b):
    m = post.shape[0]
    tr = _row_tile(st)
    full = lambda shape: pl.BlockSpec(shape, lambda i: (0, 0))
    row = lambda w: pl.BlockSpec((tr, w), lambda i: (i, 0))
    return pl.pallas_call(
        _rwkv_post_kernel,
        grid=(m // tr,),
        in_specs=[row(512), row(512), row(1024), full((1, 512)), full((1, 512)), full((512, 512))],
        out_specs=row(512),
        out_shape=jax.ShapeDtypeStruct((m, 512), BF16),
        compiler_params=_cp("parallel"),
        name="rwkv_post",
    )(y_f, y_b, post, ln_g.reshape(1, 512), ln_b.reshape(1, 512), _block_diag(512, RWKV_HD, 1.0 / RWKV_HD))


def _merge_kernel(h_ref, o0_ref, o1_ref, o2_ref, o3_ref, wg_ref, wb_ref, out_ref):
    h = h_ref[...]
    acc = None
    for i, o_ref in enumerate((o0_ref, o1_ref, o2_ref, o3_ref)):
        term = jax.nn.sigmoid(_dot(h, wg_ref[i])) * _dot(o_ref[...], wb_ref[i])
        acc = term if acc is None else acc + term
    out_ref[...] = _bf(acc)


def _merge(st, h, branch_outs, wg, wb):
    m, d = h.shape
    tm = _mm_tile(st)
    tn = 512
    row = lambda w: pl.BlockSpec((tm, w), lambda j, i: (i, 0))
    return pl.pallas_call(
        _merge_kernel,
        grid=(d // tn, m // tm),
        in_specs=[row(d)] + [row(BRANCH_W)] * 4 + [pl.BlockSpec((4, d, tn), lambda j, i: (0, 0, j)),
                                                    pl.BlockSpec((4, BRANCH_W, tn), lambda j, i: (0, 0, j))],
        out_specs=pl.BlockSpec((tm, tn), lambda j, i: (i, j)),
        out_shape=jax.ShapeDtypeStruct((m, d), BF16),
        compiler_params=_cp("parallel", "parallel"),
        name="merge",
    )(h, *branch_outs, wg, wb)


def _wo_kernel(m_ref, w_ref, x_ref, mod_ref, o_ref, *, gate_idx):
    o_ref[...] = x_ref[...] + mod_ref[0, gate_idx:gate_idx + 1, :] * _dot(m_ref[...], w_ref[...])


def _out_proj(st, merged, w_o, x, mod_l, gate_idx):
    m, d = x.shape
    tm = _mm_tile(st)
    return pl.pallas_call(
        functools.partial(_wo_kernel, gate_idx=gate_idx),
        grid=(m // tm,),
        in_specs=[pl.BlockSpec((tm, d), lambda i: (i, 0)), pl.BlockSpec((d, d), lambda i: (0, 0)),
                  pl.BlockSpec((tm, d), lambda i: (i, 0)), pl.BlockSpec((1, 6, d), st.group_map(tm))],
        out_specs=pl.BlockSpec((tm, d), lambda i: (i, 0)),
        out_shape=jax.ShapeDtypeStruct((m, d), F32),
        compiler_params=_cp("parallel"),
        name="out_proj",
    )(merged, w_o, x, mod_l)


def _ffn_kernel(h_ref, w1_ref, w3_ref, w2_ref, x_ref, mod_ref, o_ref, acc_ref, *, gate_idx):
    f = pl.program_id(1)

    @pl.when(f == 0)
    def _():
        acc_ref[...] = jnp.zeros_like(acc_ref)

    h = h_ref[...]
    u = _silu(_dot(h, w1_ref[...])) * _dot(h, w3_ref[...])
    acc_ref[...] += _dot(_bf(u), w2_ref[...])

    @pl.when(f == pl.num_programs(1) - 1)
    def _():
        o_ref[...] = x_ref[...] + mod_ref[0, gate_idx:gate_idx + 1, :] * acc_ref[...]


def _ffn(st, h, w1, w3, w2, x, mod_l, gate_idx):
    m, d = x.shape
    dff = w1.shape[1]
    tm = _mm_tile(st)
    tf = 512
    return pl.pallas_call(
        functools.partial(_ffn_kernel, gate_idx=gate_idx),
        grid=(m // tm, dff // tf),
        in_specs=[pl.BlockSpec((tm, d), lambda i, f: (i, 0)),
                  pl.BlockSpec((d, tf), lambda i, f: (0, f)),
                  pl.BlockSpec((d, tf), lambda i, f: (0, f)),
                  pl.BlockSpec((tf, d), lambda i, f: (f, 0)),
                  pl.BlockSpec((tm, d), lambda i, f: (i, 0)),
                  pl.BlockSpec((1, 6, d), lambda i, f: st.group_map(tm)(i))],
        out_specs=pl.BlockSpec((tm, d), lambda i, f: (i, 0)),
        out_shape=jax.ShapeDtypeStruct((m, d), F32),
        scratch_shapes=[pltpu.VMEM((tm, d), F32)],
        compiler_params=_cp("parallel", "arbitrary"),
        name="ffn",
    )(h, w1, w3, w2, x, mod_l)


def _split_w_in(w_in):
    z = lambda w: jnp.zeros(w_in.shape[:2] + (w,), w_in.dtype)
    att = w_in[..., 0:768]
    ssd = jnp.concatenate([w_in[..., 768:2064], z(SSD_W - 1296)], axis=-1)
    rwkv = w_in[..., 2064:3856]
    gla = jnp.concatenate([w_in[..., 3856:4880], w_in[..., 4880:4896], z(LANE - 16), w_in[..., 4896:5408]], axis=-1)
    return tuple(_bf(w) for w in (att, ssd, rwkv, gla))


def _block(st, x, mod_l, p, rope_tables, ctx):
    h = _normmod(st, x, p['norm1'], mod_l, 1, 0)
    proj_att = _matmul(st, h, p['w_att'])
    proj_ssd = _matmul(st, h, p['w_ssd'])
    proj_rwkv = _matmul(st, h, p['w_rwkv'])
    proj_gla = _matmul(st, h, p['w_gla'])

    v = proj_att[:, 640:768]
    if ctx is None:
        q, k_att, v_att, k = _att_prep(st, proj_att, p['q_norm'], p['k_norm'], None)
        o_att = _attention(st, q, k_att, v_att)
        s_ssd = jnp.zeros((st.n, 2, SSD_HEADS, SSD_N, SSD_P), F32)
        s_rwkv = jnp.zeros((st.n, 2, RWKV_HEADS, RWKV_HD, RWKV_HD), F32)
        s_gla_t = jnp.zeros((st.n, 2, GLA_HEADS, GLA_DV, GLA_DK), F32)
    else:
        ctx_k, ctx_v, s_ssd, s_rwkv, s_gla = ctx
        q, k_att, v_att = _att_prep(st, proj_att, p['q_norm'], p['k_norm'], rope_tables)
        k = None
        o_att = _attention(st, q, _join_cache(st, k_att, ctx_k), _join_cache(st, v_att, ctx_v))
        s_gla_t = jnp.swapaxes(s_gla, -1, -2)

    xbc, dt, cum, dtt, cumt = _ssd_prep(st, proj_ssd, p['ssd_conv_w'], p['ssd_conv_b'], p['ssd_dt_bias'],
                                        p['ssd_a_log'])
    y_ssd_f, y_ssd_b, new_ssd = _ssd_scan(st, xbc, dt, cum, dtt, cumt, s_ssd)
    o_ssd = _ssd_post(st, y_ssd_f, y_ssd_b, xbc, proj_ssd, p['ssd_d'], p['ssd_norm'])

    rin, lw, rpost = _rwkv_prep(st, proj_rwkv, p['rwkv_mu'], p['rwkv_w2'], p['rwkv_w0'], p['rwkv_a2'],
                                p['rwkv_a0'], p['rwkv_g2'], p['rwkv_kk'], p['rwkv_ka'], p['rwkv_rk'])
    y_rwkv_f, y_rwkv_b, new_rwkv = _rwkv_scan(st, rin, lw, s_rwkv)
    o_rwkv = _rwkv_post(st, y_rwkv_f, y_rwkv_b, rpost, p['rwkv_ln_g'], p['rwkv_ln_b'])

    g2 = jnp.concatenate([p['gla_g2'][0], p['gla_g2'][1]], axis=1)
    g2p = jnp.pad(g2, ((0, LANE - g2.shape[0]), (0, 0)))
    gla_cum = _gla_prep(st, proj_gla, g2p, p['gla_gb'].reshape(1, 2 * GLA_QK))
    y_gla_f, y_gla_b, new_gla_t = _gla_scan(st, proj_gla, gla_cum, s_gla_t)
    o_gla = _gla_post(st, y_gla_f, y_gla_b, proj_gla, p['gla_norm'])

    merged = _merge(st, h, (o_att, o_ssd, o_rwkv, o_gla), p['w_gate'], p['w_branch'])
    x = _out_proj(st, merged, p['w_o'], x, mod_l, 2)
    h2 = _normmod(st, x, p['norm2'], mod_l, 4, 3)
    x = _ffn(st, h2, p['ffn_w1'], p['ffn_w3'], p['ffn_w2'], x, mod_l, 5)
    return x, (k, v, new_ssd, new_rwkv, jnp.swapaxes(new_gla_t, -1, -2))


def kernel(x_prompt, x_sample, cache_attn_k, cache_attn_v, state_ssd, state_rwkv, state_gla, c, c_ctx, w_mod, b_mod, norm1, norm2, w_in, q_norm, k_norm, ssd_conv_w, ssd_conv_b, ssd_dt_bias, ssd_a_log, ssd_d, ssd_norm, rwkv_mu, rwkv_w0, rwkv_w2, rwkv_a0, rwkv_a2, rwkv_g2, rwkv_kk, rwkv_ka, rwkv_rk, rwkv_ln_g, rwkv_ln_b, gla_g2, gla_gb, gla_norm, w_gate, w_branch, w_o, ffn_w1, ffn_w3, ffn_w2, final_norm):
    nb, seq, d = x_prompt.shape
    db, dseq, _ = x_sample.shape
    depth = w_in.shape[0]
    assert d == D_MODEL and seq % CHUNK == 0 and dseq % CHUNK == 0 and 1 + db <= MOD_ROWS
    ctx_st = _Stream(nb, seq, 0, False)
    lat_st = _Stream(db, dseq, 1, True)

    cond = jnp.concatenate([c_ctx[None], c, jnp.zeros((MOD_ROWS - 1 - db, d), F32)], axis=0)
    mod = _modulation(cond, w_mod, b_mod)

    w_att, w_ssd, w_rwkv, w_gla = _split_w_in(w_in)
    w_gate_b, w_branch_b, w_o_b = _bf(w_gate), _bf(w_branch), _bf(w_o)
    w1_b, w3_b, w2_b = _bf(ffn_w1), _bf(ffn_w3), _bf(ffn_w2)

    def params_at(l):
        return dict(norm1=norm1[l], norm2=norm2[l], w_att=w_att[l], w_ssd=w_ssd[l], w_rwkv=w_rwkv[l],
                    w_gla=w_gla[l], q_norm=q_norm[l], k_norm=k_norm[l],
                    ssd_conv_w=ssd_conv_w[l], ssd_conv_b=ssd_conv_b[l], ssd_dt_bias=ssd_dt_bias[l],
                    ssd_a_log=ssd_a_log[l], ssd_d=ssd_d[l], ssd_norm=ssd_norm[l],
                    rwkv_mu=rwkv_mu[l], rwkv_w0=rwkv_w0[l], rwkv_w2=rwkv_w2[l], rwkv_a0=rwkv_a0[l],
                    rwkv_a2=rwkv_a2[l], rwkv_g2=rwkv_g2[l], rwkv_kk=rwkv_kk[l], rwkv_ka=rwkv_ka[l],
                    rwkv_rk=rwkv_rk[l], rwkv_ln_g=rwkv_ln_g[l], rwkv_ln_b=rwkv_ln_b[l],
                    gla_g2=gla_g2[l], gla_gb=gla_gb[l], gla_norm=gla_norm[l],
                    w_gate=w_gate_b[l], w_branch=w_branch_b[l], w_o=w_o_b[l],
                    ffn_w1=w1_b[l], ffn_w3=w3_b[l], ffn_w2=w2_b[l])

    xp = x_prompt.reshape(nb * seq, d)
    new_k, new_v, new_ssd, new_rwkv, new_gla = [], [], [], [], []
    for l in range(depth):
        xp, (k_l, v_l, ssd_l, rwkv_l, gla_l) = _block(ctx_st, xp, mod[l], params_at(l), None, None)
        new_k.append(k_l.reshape(nb, seq, ATT_KV, HEAD_DIM))
        new_v.append(v_l.reshape(nb, seq, ATT_KV, HEAD_DIM))
        new_ssd.append(ssd_l)
        new_rwkv.append(rwkv_l)
        new_gla.append(gla_l)

    rope_tables = _rope_tables(dseq)
    xs = x_sample.reshape(db * dseq, d)
    for l in range(depth):
        ctx = (cache_attn_k[:, l], cache_attn_v[:, l], state_ssd[:, l], state_rwkv[:, l], state_gla[:, l])
        xs, _ = _block(lat_st, xs, mod[l], params_at(l), rope_tables, ctx)

    y_prompt = _final_norm(ctx_st, xp, final_norm).reshape(nb, seq, d)
    y_sample = _final_norm(lat_st, xs, final_norm).reshape(db, dseq, d)
    return (y_prompt, y_sample, jnp.stack(new_k, axis=1), jnp.stack(new_v, axis=1),
            jnp.stack(new_ssd, axis=1), jnp.stack(new_rwkv, axis=1), jnp.stack(new_gla, axis=1))
```

```python
import functools

import jax
import jax.numpy as jnp
import numpy as np
from jax import lax
from jax.experimental import pallas as pl
from jax.experimental.pallas import tpu as pltpu

F32 = jnp.float32
BF16 = jnp.bfloat16

D_MODEL = 2048
GRID_W = 64
ATT_HEADS = 8
ATT_KV = 2
ATT_GROUP = ATT_HEADS // ATT_KV
HEAD_DIM = 64
ROPE_THETA = 10000.0
SSD_HEADS = 8
SSD_P = 64
SSD_N = 64
SSD_GROUPS = 2
RWKV_HEADS = 8
RWKV_HD = 64
RWKV_DECAY_SCALE = 0.6065306597126334
RWKV_LN_EPS = 64e-5
GLA_HEADS = 4
GLA_DK = 64
GLA_DV = 128
GLA_GATE_NORM = 16.0
CHUNK = 64
BRANCH_W = 512

ATT_W = 768
SSD_W = 1408
RWKV_W = 1792
GLA_W = 1664
LANE = 128
SUBLANE = 8
MOD_ROWS = 16
VMEM_LIMIT = 56 * 1024 * 1024
ROW_TILE = 256
MM_TILE = 512
SCAN_CHUNKS = 4
KEY_TILE = 256
KEY_UNROLL = 6
LOG2E = 1.4426950408889634
ATT_MIN_ROW_SUM = 2.0 ** -90


def _cp(*sem):
    return pltpu.CompilerParams(dimension_semantics=sem, vmem_limit_bytes=VMEM_LIMIT)


def _bf(x):
    return x.astype(BF16)


def _dot(a, b, prec=None):
    return jnp.dot(a, b, preferred_element_type=F32, precision=prec)


def _dot_nt(a, b, prec=None):
    return lax.dot_general(a, b, (((1,), (1,)), ((), ())), preferred_element_type=F32, precision=prec)


def _dot_tn(a, b, prec=None):
    return lax.dot_general(a, b, (((0,), (0,)), ((), ())), preferred_element_type=F32, precision=prec)


def _pieces(a, n):
    out = []
    for _ in range(n):
        piece = _bf(a)
        out.append(piece)
        a = a - piece.astype(F32)
    return out


def _dot_data_mask(a, mask, n):
    return functools.reduce(jnp.add, [_dot(piece, mask) for piece in _pieces(a, n)])


def _dot_mask_data(mask, a, n):
    return functools.reduce(jnp.add, [_dot(mask, piece) for piece in _pieces(a, n)])


def _dot_nt_mask_data(mask, a, n):
    return functools.reduce(jnp.add, [_dot_nt(mask, piece) for piece in _pieces(a, n)])


def _dot_split(a, b):
    a_hi, a_lo = _pieces(a, 2)
    b_hi, b_lo = _pieces(b, 2)
    return _dot(a_hi, b_hi) + (_dot(a_hi, b_lo) + _dot(a_lo, b_hi))


def _silu(x):
    return x * jax.nn.sigmoid(x)


def _softplus(x):
    return jnp.maximum(x, 0.0) + jnp.log1p(jnp.exp(-jnp.abs(x)))


def _lanes(pieces):
    return jnp.concatenate(pieces, axis=1)


def _rows(pieces):
    return jnp.concatenate(pieces, axis=0)


def _mod_kernel(c_ref, w_ref, b_ref, o_ref):
    c = c_ref[...]
    o_ref[0] = _dot(_bf(_silu(c)), _bf(w_ref[0])) + b_ref[0]


def _modulation(cond, w_mod, b_mod):
    nl, d, n6 = w_mod.shape
    tn = 1024
    out = pl.pallas_call(
        _mod_kernel,
        grid=(nl, n6 // tn),
        in_specs=[pl.BlockSpec((MOD_ROWS, d), lambda l, j: (0, 0)),
                  pl.BlockSpec((1, d, tn), lambda l, j: (l, 0, j)),
                  pl.BlockSpec((1, 1, tn), lambda l, j: (l, 0, j))],
        out_specs=pl.BlockSpec((1, MOD_ROWS, tn), lambda l, j: (l, 0, j)),
        out_shape=jax.ShapeDtypeStruct((nl, MOD_ROWS, n6), F32),
        compiler_params=_cp("parallel", "parallel"),
        name="modulation",
    )(cond, w_mod, b_mod.reshape(nl, 1, n6))
    return out.reshape(nl, MOD_ROWS, 6, d)


class _Stream:
    def __init__(self, n, t, group0, per_seq):
        self.n, self.t, self.group0, self.per_seq = n, t, group0, per_seq
        self.rows = n * t

    def group_map(self, tile):
        g0, per_seq, t = self.group0, self.per_seq, self.t
        if per_seq:
            return lambda i: (g0 + (i * tile) // t, 0, 0)
        return lambda i: (g0, 0, 0)


def _mm_tile(st):
    return min(MM_TILE, st.t)


def _row_tile(st):
    return min(ROW_TILE, st.t)


def _normmod_kernel(x_ref, g_ref, mod_ref, o_ref, *, sc_idx, sh_idx):
    x = x_ref[...]
    y = x * lax.rsqrt(jnp.mean(x * x, axis=-1, keepdims=True) + 1e-6) * g_ref[...]
    o_ref[...] = _bf(y * (1.0 + mod_ref[0, sc_idx:sc_idx + 1, :]) + mod_ref[0, sh_idx:sh_idx + 1, :])


def _normmod(st, x, gain, mod_l, sc_idx, sh_idx):
    m, d = x.shape
    tm = _mm_tile(st)
    return pl.pallas_call(
        functools.partial(_normmod_kernel, sc_idx=sc_idx, sh_idx=sh_idx),
        grid=(m // tm,),
        in_specs=[pl.BlockSpec((tm, d), lambda i: (i, 0)),
                  pl.BlockSpec((1, d), lambda i: (0, 0)),
                  pl.BlockSpec((1, 6, d), st.group_map(tm))],
        out_specs=pl.BlockSpec((tm, d), lambda i: (i, 0)),
        out_shape=jax.ShapeDtypeStruct((m, d), BF16),
        compiler_params=_cp("parallel"),
        name="normmod",
    )(x, gain.reshape(1, d), mod_l)


def _rms_kernel(x_ref, g_ref, o_ref):
    x = x_ref[...]
    o_ref[...] = x * lax.rsqrt(jnp.mean(x * x, axis=-1, keepdims=True) + 1e-6) * g_ref[...]


def _final_norm(st, x, gain):
    m, d = x.shape
    tm = _mm_tile(st)
    return pl.pallas_call(
        _rms_kernel,
        grid=(m // tm,),
        in_specs=[pl.BlockSpec((tm, d), lambda i: (i, 0)), pl.BlockSpec((1, d), lambda i: (0, 0))],
        out_specs=pl.BlockSpec((tm, d), lambda i: (i, 0)),
        out_shape=jax.ShapeDtypeStruct((m, d), F32),
        compiler_params=_cp("parallel"),
        name="final_norm",
    )(x, gain.reshape(1, d))


def _mm_kernel(x_ref, w_ref, o_ref):
    o_ref[...] = _dot(x_ref[...], w_ref[...]).astype(o_ref.dtype)


def _matmul(st, x, w, out_dtype=F32):
    m, k = x.shape
    n = w.shape[1]
    tm = _mm_tile(st)
    return pl.pallas_call(
        _mm_kernel,
        grid=(m // tm,),
        in_specs=[pl.BlockSpec((tm, k), lambda i: (i, 0)), pl.BlockSpec((k, n), lambda i: (0, 0))],
        out_specs=pl.BlockSpec((tm, n), lambda i: (i, 0)),
        out_shape=jax.ShapeDtypeStruct((m, n), out_dtype),
        compiler_params=_cp("parallel"),
        name="in_proj",
    )(x, w)


def _att_prep_kernel(*refs, rope):
    if rope:
        p_ref, qn_ref, kn_ref, bdq_ref, bdk_ref, cos_ref, sa_ref, sb_ref, q_out, k_out, v_out = refs
    else:
        p_ref, qn_ref, kn_ref, bdq_ref, bdk_ref, q_out, k_out, v_out, k_leaf = refs
    p = p_ref[...]
    aq = p[:, :512]
    ak = p[:, 512:640]
    av = p[:, 640:768]
    q = aq * lax.rsqrt(_dot_data_mask(aq * aq, bdq_ref[...], 2) + 1e-6) * qn_ref[...]
    k = ak * lax.rsqrt(_dot_data_mask(ak * ak, bdk_ref[...], 2) + 1e-6) * kn_ref[...]
    if rope:
        c, sa, sb = cos_ref[...], sa_ref[...], sb_ref[...]
        k = k * c + pltpu.roll(k, LANE - 16, 1) * sa + pltpu.roll(k, 16, 1) * sb
        c4, sa4, sb4 = _lanes([c] * 4), _lanes([sa] * 4), _lanes([sb] * 4)
        q = q * c4 + pltpu.roll(q, 512 - 16, 1) * sa4 + pltpu.roll(q, 16, 1) * sb4
    else:
        k_leaf[...] = k
    tr = p.shape[0]
    one_col = (lax.broadcasted_iota(jnp.int32, (tr, LANE - HEAD_DIM), 1) == 0).astype(F32)
    q = q * (HEAD_DIM ** -0.5 * LOG2E)
    q_len = jnp.sqrt(_dot_data_mask(q * q, bdq_ref[...], 2) * HEAD_DIM)
    q_out[...] = _bf(_lanes([piece for h in range(ATT_HEADS)
                             for piece in (q[:, h * HEAD_DIM:(h + 1) * HEAD_DIM],
                                           one_col * q_len[:, h * HEAD_DIM:(h + 1) * HEAD_DIM])]))
    for kv in range(ATT_KV):
        sl = slice(kv * HEAD_DIM, (kv + 1) * HEAD_DIM)
        k_out[kv] = _bf(_lanes([k[:, sl], one_col]))
        v_out[kv] = _bf(_lanes([av[:, sl], one_col]))


def _block_diag(width, block, value):
    idx = np.arange(width) // block
    return _bf(jnp.asarray((idx[:, None] == idx[None, :]).astype(np.float32) * value))


def _att_prep(st, proj, q_norm, k_norm, rope_tables):
    m = proj.shape[0]
    tr = _row_tile(st)
    rope = rope_tables is not None
    full = lambda shape: pl.BlockSpec(shape, lambda i: (0, 0))
    in_specs = [pl.BlockSpec((tr, ATT_W), lambda i: (i, 0)), full((1, 512)), full((1, 128)),
                full((512, 512)), full((128, 128))]
    args = [proj, jnp.tile(q_norm, ATT_HEADS).reshape(1, 512), jnp.tile(k_norm, ATT_KV).reshape(1, 128),
            _block_diag(512, HEAD_DIM, 1.0 / HEAD_DIM), _block_diag(128, HEAD_DIM, 1.0 / HEAD_DIM)]
    if rope:
        tps = st.t // tr
        in_specs += [pl.BlockSpec((tr, LANE), lambda i: (i % tps, 0))] * 3
        args += list(rope_tables)
    kv_spec = pl.BlockSpec((ATT_KV, tr, LANE), lambda i: (0, i, 0))
    kv_shape = jax.ShapeDtypeStruct((ATT_KV, m, LANE), BF16)
    out_specs = [pl.BlockSpec((tr, ATT_HEADS * LANE), lambda i: (i, 0)), kv_spec, kv_spec]
    out_shape = [jax.ShapeDtypeStruct((m, ATT_HEADS * LANE), BF16), kv_shape, kv_shape]
    if not rope:
        out_specs.append(pl.BlockSpec((tr, 128), lambda i: (i, 0)))
        out_shape.append(jax.ShapeDtypeStruct((m, 128), F32))
    return pl.pallas_call(
        functools.partial(_att_prep_kernel, rope=rope),
        grid=(m // tr,),
        in_specs=in_specs,
        out_specs=out_specs,
        out_shape=out_shape,
        compiler_params=_cp("parallel"),
        name="att_prep",
    )(*args)


def _join_cache(st, own, cache):
    n, p, kv, hd = cache.shape
    c = jnp.transpose(cache, (2, 0, 1, 3))
    pad = jnp.zeros((kv, n, p, LANE - hd), c.dtype).at[..., 0].set(1.0)
    c = _bf(jnp.concatenate([c, pad], axis=-1))
    joined = jnp.concatenate([own.reshape(kv, n, st.t, LANE), c], axis=2)
    return joined.reshape(kv, n * (st.t + p), LANE)


def _rope_tables(t):
    half = HEAD_DIM // 2
    nf = half // 2
    freqs = ROPE_THETA ** (-jnp.arange(nf, dtype=F32) / nf)
    tt = jnp.arange(t)
    ang_r = (tt // GRID_W).astype(F32)[:, None] * freqs[None, :]
    ang_c = (tt % GRID_W).astype(F32)[:, None] * freqs[None, :]
    zero = jnp.zeros_like(ang_r)
    cos = jnp.concatenate([jnp.cos(ang_r)] * 2 + [jnp.cos(ang_c)] * 2, axis=1)
    sa = jnp.concatenate([-jnp.sin(ang_r), zero, -jnp.sin(ang_c), zero], axis=1)
    sb = jnp.concatenate([zero, jnp.sin(ang_r), zero, jnp.sin(ang_c)], axis=1)
    return tuple(jnp.tile(a, (1, LANE // HEAD_DIM)) for a in (cos, sa, sb))


def _key_bound_kernel(k_ref, o_ref):
    ones = jnp.ones((LANE, LANE), BF16)
    for kv in range(ATT_KV):
        k = k_ref[kv].astype(F32)
        k = jnp.where(lax.broadcasted_iota(jnp.int32, k.shape, 1) < HEAD_DIM, k, 0.0)
        best = jnp.max(_dot_data_mask(k * k, ones, 2), axis=0, keepdims=True)
        o_ref[kv] = jnp.broadcast_to(best, (SUBLANE, LANE))


def _key_bound(n, keys_per_seq, k):
    return pl.pallas_call(
        _key_bound_kernel,
        grid=(n,),
        in_specs=[pl.BlockSpec((ATT_KV, keys_per_seq, LANE), lambda b: (0, b, 0))],
        out_specs=pl.BlockSpec((None, ATT_KV, SUBLANE, LANE), lambda b: (b, 0, 0, 0)),
        out_shape=jax.ShapeDtypeStruct((n, ATT_KV, SUBLANE, LANE), F32),
        compiler_params=_cp("parallel"),
        name="key_bound",
    )(k)


def _attn_kernel(q_ref, k_ref, v_ref, kb_ref, o_ref, m_scr, acc_scr, *, tq, n_tiles):
    rows = ATT_GROUP * tq
    lane = lax.broadcasted_iota(jnp.int32, (rows, LANE), 1)
    outs = []
    for kv in range(ATT_KV):
        qs = _rows([q_ref[:, (kv * ATT_GROUP + g) * LANE:(kv * ATT_GROUP + g + 1) * LANE]
                    for g in range(ATT_GROUP)]).astype(F32)

        def key_tile(ref, j):
            return ref[kv, pl.ds(pl.multiple_of(j * KEY_TILE, KEY_TILE), KEY_TILE), :]

        def weighted_sum(q_shift, unroll):
            acc_scr[...] = jnp.zeros((rows, LANE), F32)

            def sum_body(j, carry):
                p = jnp.exp2(_dot_nt(q_shift, key_tile(k_ref, j)))
                acc_scr[...] += _dot(_bf(p), key_tile(v_ref, j))
                return carry

            lax.fori_loop(0, n_tiles, sum_body, 0, unroll=unroll)
            return acc_scr[...]

        k_len = jnp.sqrt(kb_ref[kv][0:1, :])
        unroll = max(u for u in range(1, KEY_UNROLL + 1) if n_tiles % u == 0)
        acc = weighted_sum(_bf(qs * jnp.where(lane == HEAD_DIM, -k_len, 1.0)), unroll)

        def exact_shift():
            m_scr[...] = jnp.full((rows, LANE), -jnp.inf, F32)
            q0 = _bf(jnp.where(lane == HEAD_DIM, 0.0, qs))

            def max_body(j, carry):
                sc = _dot_nt(q0, key_tile(k_ref, j))
                m_scr[...] = jnp.maximum(m_scr[...], jnp.maximum(sc[:, :LANE], sc[:, LANE:]))
                return carry

            lax.fori_loop(0, n_tiles, max_body, 0)
            row_max = jnp.max(m_scr[...], axis=-1, keepdims=True)
            return weighted_sum(_bf(jnp.where(lane == HEAD_DIM, -row_max, qs)), 1)

        row_sum_ok = jnp.min(acc[:, HEAD_DIM:HEAD_DIM + 1]) >= ATT_MIN_ROW_SUM
        acc = lax.cond(row_sum_ok, lambda: acc, exact_shift)
        o = acc[:, :HEAD_DIM] / acc[:, HEAD_DIM:HEAD_DIM + 1]
        outs += [o[g * tq:(g + 1) * tq] for g in range(ATT_GROUP)]
    o_ref[...] = _bf(_lanes(outs))


def _attention(st, q, k, v):
    tq = min(128, st.t)
    nq = st.t // tq
    keys = k.shape[1] // st.n
    assert keys % KEY_TILE == 0
    kv_spec = pl.BlockSpec((ATT_KV, keys, LANE), lambda b, i: (0, b, 0))
    rows = ATT_GROUP * tq
    return pl.pallas_call(
        functools.partial(_attn_kernel, tq=tq, n_tiles=keys // KEY_TILE),
        grid=(st.n, nq),
        in_specs=[pl.BlockSpec((tq, ATT_HEADS * LANE), lambda b, i: (b * nq + i, 0)), kv_spec, kv_spec,
                  pl.BlockSpec((None, ATT_KV, SUBLANE, LANE), lambda b, i: (b, 0, 0, 0))],
        out_specs=pl.BlockSpec((tq, 512), lambda b, i: (b * nq + i, 0)),
        out_shape=jax.ShapeDtypeStruct((st.rows, 512), BF16),
        scratch_shapes=[pltpu.VMEM((rows, LANE), F32), pltpu.VMEM((rows, LANE), F32)],
        compiler_params=_cp("parallel", "parallel"),
        name="attention",
    )(q, k, v, _key_bound(st.n, keys, k))


def _halo_specs(st, width):
    tr = _row_tile(st)
    per8 = tr // SUBLANE
    last8 = st.rows // SUBLANE - 1
    return [pl.BlockSpec((tr, width), lambda i: (i, 0)),
            pl.BlockSpec((SUBLANE, width), lambda i: (jnp.maximum(i * per8 - 1, 0), 0)),
            pl.BlockSpec((SUBLANE, width), lambda i: (jnp.minimum((i + 1) * per8, last8), 0))]


def _neighbours(cur, prev8, next8, tiles_per_seq):
    tr = cur.shape[0]
    j = pl.program_id(0) % tiles_per_seq
    pr = jnp.where(j != 0, prev8[SUBLANE - 1:SUBLANE], 0.0)
    nx = jnp.where(j != tiles_per_seq - 1, next8[0:1], 0.0)
    row = lax.broadcasted_iota(jnp.int32, cur.shape, 0)
    x_prev = jnp.where(row == 0, pr, pltpu.roll(cur, 1, 0))
    x_next = jnp.where(row == tr - 1, nx, pltpu.roll(cur, tr - 1, 0))
    return x_prev, x_next


def _dir_masks(z):
    ri = lax.broadcasted_iota(jnp.int32, (CHUNK, CHUNK), 0)
    ci = lax.broadcasted_iota(jnp.int32, (CHUNK, CHUNK), 1)
    return ((ci <= ri), (ci < ri)) if z == 0 else ((ci >= ri), (ci > ri))


def _tri_blocks(tr):
    i = np.arange(tr)
    same = (i[:, None] // CHUNK) == (i[None, :] // CHUNK)
    lower = same & (i[None, :] <= i[:, None])
    upper = same & (i[None, :] >= i[:, None])
    return _bf(jnp.asarray(lower.astype(np.float32))), _bf(jnp.asarray(upper.astype(np.float32)))


def _scan_blocks(st):
    ncb = min(SCAN_CHUNKS, st.t // CHUNK)
    rb = ncb * CHUNK
    nblk = st.t // rb
    fwd = lambda b, j: (b * nblk + j, 0)
    bwd = lambda b, j: (b * nblk + nblk - 1 - j, 0)
    return ncb, rb, nblk, fwd, bwd


def _chunk_rows(step, z, ncb):
    cc = step if z == 0 else ncb - 1 - step
    return slice(cc * CHUNK, (cc + 1) * CHUNK)


def _ssd_prep_kernel(cur_ref, prev_ref, next_ref, cw_ref, cb_ref, dtb_ref, alog_ref, lo_ref, up_ref,
                     xbc_out, dt_out, cum_out, dtt_out, cumt_out, *, tiles_per_seq):
    cur = cur_ref[...]
    xc = cur[:, 512:1280]
    x_prev, x_next = _neighbours(xc, prev_ref[:, 512:1280], next_ref[:, 512:1280], tiles_per_seq)
    cw = cw_ref[...]
    conv = cb_ref[...] + x_prev * cw[0:1] + xc * cw[1:2] + x_next * cw[2:3]
    xbc_out[...] = _silu(conv)
    dt = _softplus(cur[:, 1280:1408] + dtb_ref[...])
    dt_out[...] = dt
    ld = dt * -jnp.exp(alog_ref[...])
    lane = lax.broadcasted_iota(jnp.int32, ld.shape, 1)
    cum = jnp.where(lane < SSD_HEADS, _dot_mask_data(lo_ref[...], ld, 3), _dot_mask_data(up_ref[...], ld, 3))
    cum_out[...] = cum
    pick = (lax.broadcasted_iota(jnp.int32, (2 * SSD_HEADS, LANE), 0)
            == lax.broadcasted_iota(jnp.int32, (2 * SSD_HEADS, LANE), 1)).astype(BF16)
    dtt_out[...] = _dot_nt_mask_data(pick, dt, 3)
    cumt_out[...] = _dot_nt_mask_data(pick, cum, 3)


def _pad_lanes(v, width=LANE):
    v = v.reshape(1, -1)
    return jnp.pad(v, ((0, 0), (0, width - v.shape[1])))


def _ssd_prep(st, proj, conv_w, conv_b, dt_bias, a_log):
    m = proj.shape[0]
    tr = _row_tile(st)
    full = lambda shape: pl.BlockSpec(shape, lambda i: (0, 0))
    row = lambda w: pl.BlockSpec((tr, w), lambda i: (i, 0))
    col = pl.BlockSpec((2 * SSD_HEADS, tr), lambda i: (0, i))
    lower, upper = _tri_blocks(tr)
    return pl.pallas_call(
        functools.partial(_ssd_prep_kernel, tiles_per_seq=st.t // tr),
        grid=(m // tr,),
        in_specs=_halo_specs(st, SSD_W) + [full((3, 768)), full((1, 768)), full((1, LANE)), full((1, LANE)),
                                           full((tr, tr)), full((tr, tr))],
        out_specs=[row(768), row(LANE), row(LANE), col, col],
        out_shape=[jax.ShapeDtypeStruct((m, 768), F32), jax.ShapeDtypeStruct((m, LANE), F32),
                   jax.ShapeDtypeStruct((m, LANE), F32), jax.ShapeDtypeStruct((2 * SSD_HEADS, m), F32),
                   jax.ShapeDtypeStruct((2 * SSD_HEADS, m), F32)],
        compiler_params=_cp("parallel"),
        name="ssd_prep",
    )(proj, proj, proj, conv_w.T, conv_b.reshape(1, 768), _pad_lanes(dt_bias), _pad_lanes(a_log), lower, upper)


def _ssd_scan_kernel(xf_ref, xb_ref, dtf_ref, dtb_ref, cf_ref, cb_ref, dttf_ref, dttb_ref, ctf_ref, ctb_ref,
                     s0_ref, yf_ref, yb_ref, s_ref, *, ncb):
    @pl.when(pl.program_id(1) == 0)
    def _():
        s_ref[...] = s0_ref[...]

    hpg = SSD_HEADS // SSD_GROUPS
    state = {(z, g): _lanes([s_ref[z, g * hpg + hh] for hh in range(hpg)])
             for z in range(2) for g in range(SSD_GROUPS)}
    refs = ((xf_ref, dtf_ref, cf_ref, dttf_ref, ctf_ref, yf_ref), (xb_ref, dtb_ref, cb_ref, dttb_ref, ctb_ref, yb_ref))
    groups = [(z, g) for z in range(2) for g in range(SSD_GROUPS)]
    heads = [(z, h) for z in range(2) for h in range(SSD_HEADS)]
    for step in range(ncb):
        xbc, dt, cum, dtt, cumt, incl, rows = {}, {}, {}, {}, {}, {}, {}
        for z in range(2):
            x_ref, dt_ref, c_ref, dtt_ref, ct_ref, _ = refs[z]
            rows[z] = _chunk_rows(step, z, ncb)
            xbc[z], dt[z], cum[z] = x_ref[rows[z], :], dt_ref[rows[z], :], c_ref[rows[z], :]
            dtt[z], cumt[z] = dtt_ref[:, rows[z]], ct_ref[:, rows[z]]
            incl[z] = _dir_masks(z)[0]
        bmat = {(z, g): xbc[z][:, 512 + g * SSD_N:512 + (g + 1) * SSD_N] for z, g in groups}
        cmat = {(z, g): _bf(xbc[z][:, 640 + g * SSD_N:640 + (g + 1) * SSD_N]) for z, g in groups}
        cb = {k: _dot_nt(cmat[k], _bf(bmat[k])) for k in groups}
        cs = {k: _dot(cmat[k], _bf(state[k])) for k in groups}
        gcol, glast, xs = {}, {}, {}
        for z, h in heads:
            ln = z * SSD_HEADS + h
            gcol[z, h] = cum[z][:, ln:ln + 1]
            glast[z, h] = gcol[z, h][CHUNK - 1:CHUNK] if z == 0 else gcol[z, h][0:1]
            xs[z, h] = xbc[z][:, h * SSD_P:(h + 1) * SSD_P]
        inc = {}
        for z, g in groups:
            xw = []
            for h in range(g * hpg, (g + 1) * hpg):
                ln = z * SSD_HEADS + h
                xw.append(xs[z, h] * (dt[z][:, ln:ln + 1] * jnp.exp(glast[z, h] - gcol[z, h])))
            inc[z, g] = _dot_tn(_bf(bmat[z, g]), _bf(_lanes(xw)))
        att = {}
        for z, h in heads:
            ln = z * SSD_HEADS + h
            dec = jnp.exp(jnp.where(incl[z], gcol[z, h] - cumt[z][ln:ln + 1], -jnp.inf))
            att[z, h] = _bf(cb[z, h // hpg] * dec * dtt[z][ln:ln + 1])
        intra = {k: _dot(att[k], _bf(xs[k])) for k in heads}
        for z in range(2):
            outs = []
            for g in range(SSD_GROUPS):
                hs = range(g * hpg, (g + 1) * hpg)
                e_in = _lanes([jnp.broadcast_to(jnp.exp(gcol[z, h]), (CHUNK, SSD_P)) for h in hs])
                outs.append(_lanes([intra[z, h] for h in hs]) + cs[z, g] * e_in)
                e_last = _lanes([jnp.broadcast_to(jnp.exp(glast[z, h]), (1, SSD_P)) for h in hs])
                state[z, g] = state[z, g] * e_last + inc[z, g]
            refs[z][5][rows[z], :] = _lanes(outs)
    for z in range(2):
        for h in range(SSD_HEADS):
            s_ref[z, h] = state[z, h // hpg][:, (h % hpg) * SSD_P:(h % hpg + 1) * SSD_P]


def _ssd_scan(st, xbc, dt, cum, dtt, cumt, s0):
    ncb, rb, nblk, fwd, bwd = _scan_blocks(st)
    fwd_t = lambda b, j: (0, fwd(b, j)[0])
    bwd_t = lambda b, j: (0, bwd(b, j)[0])
    state_spec = pl.BlockSpec((None, 2, SSD_HEADS, SSD_N, SSD_P), lambda b, j: (b, 0, 0, 0, 0))
    rows = lambda w, m: pl.BlockSpec((rb, w), m)
    cols = lambda m: pl.BlockSpec((2 * SSD_HEADS, rb), m)
    return pl.pallas_call(
        functools.partial(_ssd_scan_kernel, ncb=ncb),
        grid=(st.n, nblk),
        in_specs=[rows(768, fwd), rows(768, bwd), rows(LANE, fwd), rows(LANE, bwd), rows(LANE, fwd), rows(LANE, bwd),
                  cols(fwd_t), cols(bwd_t), cols(fwd_t), cols(bwd_t), state_spec],
        out_specs=[rows(512, fwd), rows(512, bwd), state_spec],
        out_shape=[jax.ShapeDtypeStruct((st.rows, 512), F32), jax.ShapeDtypeStruct((st.rows, 512), F32),
                   jax.ShapeDtypeStruct((st.n, 2, SSD_HEADS, SSD_N, SSD_P), F32)],
        compiler_params=_cp("parallel", "arbitrary"),
        name="ssd_scan",
    )(xbc, xbc, dt, dt, cum, cum, dtt, dtt, cumt, cumt, s0)


def _ssd_post_kernel(yf_ref, yb_ref, xbc_ref, p_ref, d_ref, g_ref, o_ref):
    y = yf_ref[...] + yb_ref[...] + d_ref[...] * xbc_ref[:, :512]
    y = y * _silu(p_ref[:, :512])
    o_ref[...] = _bf(y * lax.rsqrt(jnp.mean(y * y, axis=-1, keepdims=True) + 1e-6) * g_ref[...])


def _ssd_post(st, y_f, y_b, xbc, proj, ssd_d, ssd_norm):
    m = proj.shape[0]
    tr = _row_tile(st)
    full = lambda shape: pl.BlockSpec(shape, lambda i: (0, 0))
    row = lambda w: pl.BlockSpec((tr, w), lambda i: (i, 0))
    return pl.pallas_call(
        _ssd_post_kernel,
        grid=(m // tr,),
        in_specs=[row(512), row(512), row(768), row(SSD_W), full((1, 512)), full((1, 512))],
        out_specs=row(512),
        out_shape=jax.ShapeDtypeStruct((m, 512), BF16),
        compiler_params=_cp("parallel"),
        name="ssd_post",
    )(y_f, y_b, xbc, proj, jnp.repeat(ssd_d, SSD_P).reshape(1, 512), ssd_norm.reshape(1, 512))


GLA_QK = GLA_HEADS * GLA_DK
GLA_SAFE_RANGE = 60.0


def _gla_prep_kernel(p_ref, g2_ref, gb_ref, lo_ref, up_ref, cum_out):
    logit = _dot_split(p_ref[:, 1024:1152], g2_ref[...]) + gb_ref[...]
    log_a = -_softplus(-logit) * (1.0 / GLA_GATE_NORM)
    cum_out[:, :GLA_QK] = _dot_mask_data(lo_ref[...], log_a[:, :GLA_QK], 3)
    cum_out[:, GLA_QK:] = _dot_mask_data(up_ref[...], log_a[:, GLA_QK:], 3)


def _gla_prep(st, proj, g2p, gb):
    m = proj.shape[0]
    tr = _row_tile(st)
    full = lambda shape: pl.BlockSpec(shape, lambda i: (0, 0))
    lower, upper = _tri_blocks(tr)
    return pl.pallas_call(
        _gla_prep_kernel,
        grid=(m // tr,),
        in_specs=[pl.BlockSpec((tr, GLA_W), lambda i: (i, 0)), full((LANE, 2 * GLA_QK)), full((1, 2 * GLA_QK)),
                  full((tr, tr)), full((tr, tr))],
        out_specs=pl.BlockSpec((tr, 2 * GLA_QK), lambda i: (i, 0)),
        out_shape=jax.ShapeDtypeStruct((m, 2 * GLA_QK), F32),
        compiler_params=_cp("parallel"),
        name="gla_prep",
    )(proj, g2p, gb, lower, upper)


def _gla_intra_exact(p_ref, c_ref, rows, z, seg_ref):
    r0 = rows.start
    q = p_ref[rows, 0:GLA_QK] * (GLA_DK ** -0.5)
    cum = c_ref[rows, :]
    row = lax.broadcasted_iota(jnp.int32, (CHUNK, 1), 0)
    seg = seg_ref[...]

    def body(j, acc):
        kj = p_ref[pl.ds(r0 + j, 1), GLA_QK:2 * GLA_QK]
        vj = p_ref[pl.ds(r0 + j, 1), 2 * GLA_QK:2 * GLA_QK + GLA_HEADS * GLA_DV]
        seen = (row >= j) if z == 0 else (row <= j)
        w = jnp.where(seen, q * kj * jnp.exp(jnp.minimum(cum - c_ref[pl.ds(r0 + j, 1), :], 0.0)), 0.0)
        score = _dot_data_mask(w, seg, 3)
        return tuple(a + score[:, h:h + 1] * vj[:, h * GLA_DV:(h + 1) * GLA_DV] for h, a in enumerate(acc))

    zero = jnp.zeros((CHUNK, GLA_DV), F32)
    return list(lax.fori_loop(0, CHUNK, body, (zero,) * GLA_HEADS))


def _gla_scan_kernel(pf_ref, pb_ref, cf_ref, cb_ref, seg_ref, s0_ref, yf_ref, yb_ref, s_ref, *, ncb):
    @pl.when(pl.program_id(1) == 0)
    def _():
        s_ref[...] = s0_ref[...]

    refs = ((pf_ref, cf_ref, yf_ref), (pb_ref, cb_ref, yb_ref))
    heads = [(z, h) for z in range(2) for h in range(GLA_HEADS)]
    hsl = [slice(h * GLA_DK, (h + 1) * GLA_DK) for h in range(GLA_HEADS)]
    state = {(z, h): s_ref[z, h] for z, h in heads}
    chunks = [(step, z) for step in range(ncb) for z in range(2)]
    rows = {(step, z): _chunk_rows(step, z, ncb) for step, z in chunks}
    cum = {k: refs[k[1]][1][rows[k], :] for k in chunks}
    mid = {k: cum[k][CHUNK // 2:CHUNK // 2 + 1] for k in chunks}
    span = functools.reduce(jnp.maximum, [jnp.max(jnp.abs(cum[k] - mid[k])) for k in chunks])

    def v_of(k, h):
        return _bf(refs[k[1]][0][rows[k], 2 * GLA_QK + h * GLA_DV:2 * GLA_QK + (h + 1) * GLA_DV])

    def intra_factored():
        att = {}
        for k in chunks:
            p_ref = refs[k[1]][0]
            q_mid = _bf(p_ref[rows[k], 0:GLA_QK] * (GLA_DK ** -0.5) * jnp.exp(cum[k] - mid[k]))
            k_mid = _bf(p_ref[rows[k], GLA_QK:2 * GLA_QK] * jnp.exp(mid[k] - cum[k]))
            incl = _dir_masks(k[1])[0]
            for h in range(GLA_HEADS):
                att[k, h] = _bf(jnp.where(incl, _dot_nt(q_mid[:, hsl[h]], k_mid[:, hsl[h]]), 0.0))
        return [_dot(att[k, h], v_of(k, h)) for k in chunks for h in range(GLA_HEADS)]

    def intra_exact():
        out = []
        for k in chunks:
            out += _gla_intra_exact(refs[k[1]][0], refs[k[1]][1], rows[k], k[1], seg_ref)
        return out

    intra = lax.cond(span <= GLA_SAFE_RANGE, intra_factored, intra_exact)
    intra = {(k, h): intra[i * GLA_HEADS + h] for i, k in enumerate(chunks) for h in range(GLA_HEADS)}

    for step in range(ncb):
        q_in, k_out, e_last = {}, {}, {}
        for z in range(2):
            k = (step, z)
            p_ref = refs[z][0]
            glast = cum[k][CHUNK - 1:CHUNK] if z == 0 else cum[k][0:1]
            q_in[z] = _bf(p_ref[rows[k], 0:GLA_QK] * (GLA_DK ** -0.5) * jnp.exp(cum[k]))
            k_out[z] = _bf(p_ref[rows[k], GLA_QK:2 * GLA_QK] * jnp.exp(glast - cum[k]))
            e_last[z] = jnp.exp(glast)
        inter = {(z, h): _dot_nt(q_in[z][:, hsl[h]], _bf(state[z, h])) for z, h in heads}
        inc = {(z, h): _dot_tn(v_of((step, z), h), k_out[z][:, hsl[h]]) for z, h in heads}
        for z in range(2):
            refs[z][2][rows[step, z], :] = _lanes([intra[(step, z), h] + inter[z, h] for h in range(GLA_HEADS)])
            for h in range(GLA_HEADS):
                state[z, h] = state[z, h] * e_last[z][:, hsl[h]] + inc[z, h]
    for z, h in heads:
        s_ref[z, h] = state[z, h]


def _gla_scan(st, proj, cum, s0_t):
    ncb, rb, nblk, fwd, bwd = _scan_blocks(st)
    seg = _block_diag(GLA_QK, GLA_DK, 1.0)[:, ::GLA_DK]
    seg = jnp.pad(seg, ((0, 0), (0, LANE - GLA_HEADS)))
    state_spec = pl.BlockSpec((None, 2, GLA_HEADS, GLA_DV, GLA_DK), lambda b, j: (b, 0, 0, 0, 0))
    qkv_w = 2 * GLA_QK + GLA_HEADS * GLA_DV
    return pl.pallas_call(
        functools.partial(_gla_scan_kernel, ncb=ncb),
        grid=(st.n, nblk),
        in_specs=[pl.BlockSpec((rb, qkv_w), fwd), pl.BlockSpec((rb, qkv_w), bwd),
                  pl.BlockSpec((rb, GLA_QK), fwd), pl.BlockSpec((rb, GLA_QK), lambda b, j: (bwd(b, j)[0], 1)),
                  pl.BlockSpec((GLA_QK, LANE), lambda b, j: (0, 0)), state_spec],
        out_specs=[pl.BlockSpec((rb, 512), fwd), pl.BlockSpec((rb, 512), bwd), state_spec],
        out_shape=[jax.ShapeDtypeStruct((st.rows, 512), F32), jax.ShapeDtypeStruct((st.rows, 512), F32),
                   jax.ShapeDtypeStruct((st.n, 2, GLA_HEADS, GLA_DV, GLA_DK), F32)],
        compiler_params=_cp("parallel", "arbitrary"),
        name="gla_scan",
    )(proj, proj, cum, cum, seg, s0_t)


def _gla_post_kernel(yf_ref, yb_ref, p_ref, g_ref, o_ref):
    o = yf_ref[...] + yb_ref[...]
    gate = _silu(p_ref[:, 1152:1664])
    outs = []
    for h in range(GLA_HEADS):
        oh = o[:, h * GLA_DV:(h + 1) * GLA_DV]
        outs.append(oh * lax.rsqrt(jnp.mean(oh * oh, axis=-1, keepdims=True) + 1e-6) * g_ref[...])
    o_ref[...] = _bf(_lanes(outs) * gate)


def _gla_post(st, y_f, y_b, proj, gla_norm):
    m = proj.shape[0]
    tr = _row_tile(st)
    return pl.pallas_call(
        _gla_post_kernel,
        grid=(m // tr,),
        in_specs=[pl.BlockSpec((tr, 512), lambda i: (i, 0)), pl.BlockSpec((tr, 512), lambda i: (i, 0)),
                  pl.BlockSpec((tr, GLA_W), lambda i: (i, 0)), pl.BlockSpec((1, GLA_DV), lambda i: (0, 0))],
        out_specs=pl.BlockSpec((tr, 512), lambda i: (i, 0)),
        out_shape=jax.ShapeDtypeStruct((m, 512), BF16),
        compiler_params=_cp("parallel"),
        name="gla_post",
    )(y_f, y_b, proj, gla_norm.reshape(1, GLA_DV))


def _rwkv_prep_kernel(cur_ref, prev_ref, next_ref, mu_ref, w2_ref, w0_ref, a2_ref, a0_ref, g2_ref,
                      kkw_ref, ka_ref, rk_ref, bd_ref, in_out, lw_out, post_out, *, tiles_per_seq):
    cur = cur_ref[...]
    x_prev, x_next = _neighbours(cur, prev_ref[...], next_ref[...], tiles_per_seq)
    blk = cur + (0.5 * (x_prev + x_next) - cur) * mu_ref[...]
    r, k, v = blk[:, 0:512], blk[:, 512:1024], blk[:, 1024:1536]
    w_logit = w0_ref[...] + _dot_split(jnp.tanh(blk[:, 1536:1600]), w2_ref[...])
    lw_out[...] = -RWKV_DECAY_SCALE * jax.nn.sigmoid(w_logit)
    a = jax.nn.sigmoid(a0_ref[...] + _dot_split(blk[:, 1600:1664], a2_ref[...]))
    g = _dot_split(jax.nn.sigmoid(blk[:, 1664:1792]), g2_ref[...])
    bd = bd_ref[...]
    kk = k * kkw_ref[...]
    kk = kk * lax.rsqrt(_dot_data_mask(kk * kk, bd, 2) + 1e-12)
    k2 = k * (1.0 + (a - 1.0) * ka_ref[...])
    in_out[:, 0:512] = r
    in_out[:, 512:1024] = k2
    in_out[:, 1024:1536] = v
    in_out[:, 1536:2048] = -kk
    in_out[:, 2048:2560] = kk * a
    post_out[:, 0:512] = g
    post_out[:, 512:1024] = _dot_data_mask(r * k2 * rk_ref[...], bd, 2) * v


def _rwkv_prep(st, proj, mu, w2, w0, a2, a0, g2, kkw, ka, rk):
    m = proj.shape[0]
    tr = _row_tile(st)
    full = lambda shape: pl.BlockSpec(shape, lambda i: (0, 0))
    row = lambda w: pl.BlockSpec((tr, w), lambda i: (i, 0))
    vec = lambda a: a.reshape(1, -1)
    return pl.pallas_call(
        functools.partial(_rwkv_prep_kernel, tiles_per_seq=st.t // tr),
        grid=(m // tr,),
        in_specs=_halo_specs(st, RWKV_W) + [full((1, RWKV_W)), full((64, 1024)), full((1, 1024)),
                                            full((64, 512)), full((1, 512)), full((128, 512)),
                                            full((1, 512)), full((1, 512)), full((1, 512)), full((512, 512))],
        out_specs=[row(2560), row(1024), row(1024)],
        out_shape=[jax.ShapeDtypeStruct((m, 2560), F32), jax.ShapeDtypeStruct((m, 1024), F32),
                   jax.ShapeDtypeStruct((m, 1024), F32)],
        compiler_params=_cp("parallel"),
        name="rwkv_prep",
    )(proj, proj, proj, vec(mu), jnp.concatenate([w2[0], w2[1]], axis=1), vec(w0), a2, vec(a0), g2,
      vec(kkw), vec(ka), vec(rk), _block_diag(512, RWKV_HD, 1.0))


def _rwkv_chunk(x_ref, lw_ref, rows, z):
    incl, strict = _dir_masks(z)
    lw = lw_ref[rows, :]
    cum = _dot_mask_data(incl.astype(BF16), lw, 3)
    cum_x = cum - lw
    mid = cum[CHUNK // 2:CHUNK // 2 + 1]
    glast = cum[CHUNK - 1:CHUNK] if z == 0 else cum[0:1]
    x = x_ref[rows, :]
    r, k, v, al, be = x[:, 0:512], x[:, 512:1024], x[:, 1024:1536], x[:, 1536:2048], x[:, 2048:2560]
    e_mid, e_nmid = jnp.exp(cum - mid), jnp.exp(mid - cum)
    e_out = jnp.exp(glast - cum)
    return dict(incl=incl, strict=strict, v=v,
                r_mid=_bf(r * e_mid), a_mid=_bf(al * jnp.exp(cum_x - mid)),
                b_mid=_bf(be * e_nmid), k_mid=_bf(k * e_nmid),
                r_in=_bf(r * jnp.exp(cum)), a_in=al * jnp.exp(cum_x),
                b_out=_bf(be * e_out), k_out=_bf(k * e_out), e_last=jnp.exp(glast))


_HEAD_SLICES = [slice(h * RWKV_HD, (h + 1) * RWKV_HD) for h in range(RWKV_HEADS)]


def _rwkv_state_free(chunks, hooks):
    hooks = list(hooks)

    def run_hook():
        if hooks:
            hooks.pop(0)()

    chains = [(c, sl) for c in chunks for sl in _HEAD_SLICES]
    run_hook()
    pair = [_dot_nt(_rows([c['a_mid'][:, sl], c['r_mid'][:, sl]]), _rows([c['b_mid'][:, sl], c['k_mid'][:, sl]]))
            for c, sl in chains]
    half = CHUNK
    a_ab = [_bf(jnp.where(c['strict'], p[:half, :half], 0.0)) for (c, _), p in zip(chains, pair)]
    a_ak = [_bf(jnp.where(c['strict'], p[:half, half:], 0.0)) for (c, _), p in zip(chains, pair)]
    a_rb = [_bf(jnp.where(c['incl'], p[half:, :half], 0.0)) for (c, _), p in zip(chains, pair)]
    a_rk = [_bf(jnp.where(c['incl'], p[half:, half:], 0.0)) for (c, _), p in zip(chains, pair)]
    vb = [_bf(c['v'][:, sl]) for c, sl in chains]
    both = [_dot(_rows([ak, rk]), v) for ak, rk, v in zip(a_ak, a_rk, vb)]
    av = [x[:half] for x in both]
    o0 = [x[half:] for x in both]
    sol = [_lanes([c['a_in'][:, sl], x]) for (c, sl), x in zip(chains, av)]
    powr = a_ab
    width = 2 * RWKV_HD
    for it in range(6):
        if it < 5:
            both = [_dot(p, _lanes([_bf(s), p])) for p, s in zip(powr, sol)]
            sol = [s + x[:, :width] for s, x in zip(sol, both)]
            powr = [_bf(x[:, width:]) for x in both]
        else:
            sol = [s + _dot(p, _bf(s)) for p, s in zip(powr, sol)]
        if it in (1, 3):
            run_hook()
    while hooks:
        run_hook()
    return [dict(w=_bf(s[:, :RWKV_HD]), u0=s[:, RWKV_HD:], o0=o, a_rb=a, v=c['v'][:, sl], r_in=c['r_in'][:, sl],
                 b_out=c['b_out'][:, sl], k_out=c['k_out'][:, sl], e_last=c['e_last'][:, sl])
            for (c, sl), s, o, a in zip(chains, sol, o0, a_rb)]


def _rwkv_state_stages(res, state, write_out):
    box = {}

    def read_state():
        box['su'] = [_dot_nt(_rows([c['w'], c['r_in']]), _bf(s)) for c, s in zip(res, state)]

    def update_state():
        box['u'] = [c['u0'] + su[:CHUNK] for c, su in zip(res, box['su'])]
        state[:] = [s * c['e_last'] + _dot_tn(_bf(_rows([u, c['v']])), _rows([c['b_out'], c['k_out']]))
                    for c, s, u in zip(res, state, box['u'])]

    def emit():
        write_out([su[CHUNK:] + _dot(c['a_rb'], _bf(u)) + c['o0'] for c, su, u in zip(res, box['su'], box['u'])])

    return [read_state, update_state, emit]


def _rwkv_scan_kernel(xf_ref, xb_ref, lwf_ref, lwb_ref, s0_ref, yf_ref, yb_ref, s_ref, *, ncb):
    @pl.when(pl.program_id(1) == 0)
    def _():
        s_ref[...] = s0_ref[...]

    state = [s_ref[z, h] for z in range(2) for h in range(RWKV_HEADS)]

    def chunk_rows(step, z):
        cc = step if z == 0 else ncb - 1 - step
        return slice(cc * CHUNK, (cc + 1) * CHUNK)

    def pair_inputs(step):
        return [_rwkv_chunk(xf_ref, lwf_ref, chunk_rows(step, 0), 0),
                _rwkv_chunk(xb_ref, lwb_ref, chunk_rows(step, 1), 1)]

    def writer(step):
        def write_out(outs):
            yf_ref[chunk_rows(step, 0), :] = _lanes(outs[:RWKV_HEADS])
            yb_ref[chunk_rows(step, 1), :] = _lanes(outs[RWKV_HEADS:])
        return write_out

    res = _rwkv_state_free(pair_inputs(0), [])
    for step in range(ncb):
        hooks = _rwkv_state_stages(res, state, writer(step))
        if step + 1 < ncb:
            res = _rwkv_state_free(pair_inputs(step + 1), hooks)
        else:
            for hook in hooks:
                hook()
    for z in range(2):
        for h in range(RWKV_HEADS):
            s_ref[z, h] = state[z * RWKV_HEADS + h]


def _rwkv_scan(st, xin, lw, s0):
    ncb, rb, nblk, fwd, bwd = _scan_blocks(st)
    state_spec = pl.BlockSpec((None, 2, RWKV_HEADS, RWKV_HD, RWKV_HD), lambda b, j: (b, 0, 0, 0, 0))
    return pl.pallas_call(
        functools.partial(_rwkv_scan_kernel, ncb=ncb),
        grid=(st.n, nblk),
        in_specs=[pl.BlockSpec((rb, 2560), fwd), pl.BlockSpec((rb, 2560), bwd),
                  pl.BlockSpec((rb, 512), fwd), pl.BlockSpec((rb, 512), lambda b, j: (bwd(b, j)[0], 1)),
                  state_spec],
        out_specs=[pl.BlockSpec((rb, 512), fwd), pl.BlockSpec((rb, 512), bwd), state_spec],
        out_shape=[jax.ShapeDtypeStruct((st.rows, 512), F32), jax.ShapeDtypeStruct((st.rows, 512), F32),
                   jax.ShapeDtypeStruct((st.n, 2, RWKV_HEADS, RWKV_HD, RWKV_HD), F32)],
        compiler_params=_cp("parallel", "arbitrary"),
        name="rwkv_scan",
    )(xin, xin, lw, lw, s0)


def _rwkv_post_kernel(yf_ref, yb_ref, post_ref, lng_ref, lnb_ref, bd_ref, o_ref):
    o = yf_ref[...] + yb_ref[...]
    bd = bd_ref[...]
    d = o - _dot_data_mask(o, bd, 2)
    o = d * lax.rsqrt(_dot_data_mask(d * d, bd, 2) + RWKV_LN_EPS) * lng_ref[...] + lnb_ref[...]
    o_ref[...] = _bf((o + post_ref[:, 512:1024]) * post_ref[:, 0:512])


def _rwkv_post(st, y_f, y_b, post, ln_g, ln_b):
    m = post.shape[0]
    tr = _row_tile(st)
    full = lambda shape: pl.BlockSpec(shape, lambda i: (0, 0))
    row = lambda w: pl.BlockSpec((tr, w), lambda i: (i, 0))
    return pl.pallas_call(
        _rwkv_post_kernel,
        grid=(m // tr,),
        in_specs=[row(512), row(512), row(1024), full((1, 512)), full((1, 512)), full((512, 512))],
        out_specs=row(512),
        out_shape=jax.ShapeDtypeStruct((m, 512), BF16),
        compiler_params=_cp("parallel"),
        name="rwkv_post",
    )(y_f, y_b, post, ln_g.reshape(1, 512), ln_b.reshape(1, 512), _block_diag(512, RWKV_HD, 1.0 / RWKV_HD))


def _merge_kernel(h_ref, o0_ref, o1_ref, o2_ref, o3_ref, wg_ref, wb_ref, out_ref):
    h = h_ref[...]
    acc = None
    for i, o_ref in enumerate((o0_ref, o1_ref, o2_ref, o3_ref)):
        term = jax.nn.sigmoid(_dot(h, wg_ref[i])) * _dot(o_ref[...], wb_ref[i])
        acc = term if acc is None else acc + term
    out_ref[...] = _bf(acc)


def _merge(st, h, branch_outs, wg, wb):
    m, d = h.shape
    tm = _mm_tile(st)
    tn = 512
    row = lambda w: pl.BlockSpec((tm, w), lambda j, i: (i, 0))
    return pl.pallas_call(
        _merge_kernel,
        grid=(d // tn, m // tm),
        in_specs=[row(d)] + [row(BRANCH_W)] * 4 + [pl.BlockSpec((4, d, tn), lambda j, i: (0, 0, j)),
                                                    pl.BlockSpec((4, BRANCH_W, tn), lambda j, i: (0, 0, j))],
        out_specs=pl.BlockSpec((tm, tn), lambda j, i: (i, j)),
        out_shape=jax.ShapeDtypeStruct((m, d), BF16),
        compiler_params=_cp("parallel", "parallel"),
        name="merge",
    )(h, *branch_outs, wg, wb)


def _wo_kernel(m_ref, w_ref, x_ref, mod_ref, o_ref, *, gate_idx):
    o_ref[...] = x_ref[...] + mod_ref[0, gate_idx:gate_idx + 1, :] * _dot(m_ref[...], w_ref[...])


def _out_proj(st, merged, w_o, x, mod_l, gate_idx):
    m, d = x.shape
    tm = _mm_tile(st)
    return pl.pallas_call(
        functools.partial(_wo_kernel, gate_idx=gate_idx),
        grid=(m // tm,),
        in_specs=[pl.BlockSpec((tm, d), lambda i: (i, 0)), pl.BlockSpec((d, d), lambda i: (0, 0)),
                  pl.BlockSpec((tm, d), lambda i: (i, 0)), pl.BlockSpec((1, 6, d), st.group_map(tm))],
        out_specs=pl.BlockSpec((tm, d), lambda i: (i, 0)),
        out_shape=jax.ShapeDtypeStruct((m, d), F32),
        compiler_params=_cp("parallel"),
        name="out_proj",
    )(merged, w_o, x, mod_l)


def _ffn_kernel(h_ref, w1_ref, w3_ref, w2_ref, x_ref, mod_ref, o_ref, acc_ref, *, gate_idx):
    f = pl.program_id(1)

    @pl.when(f == 0)
    def _():
        acc_ref[...] = jnp.zeros_like(acc_ref)

    h = h_ref[...]
    u = _silu(_dot(h, w1_ref[...])) * _dot(h, w3_ref[...])
    acc_ref[...] += _dot(_bf(u), w2_ref[...])

    @pl.when(f == pl.num_programs(1) - 1)
    def _():
        o_ref[...] = x_ref[...] + mod_ref[0, gate_idx:gate_idx + 1, :] * acc_ref[...]


def _ffn(st, h, w1, w3, w2, x, mod_l, gate_idx):
    m, d = x.shape
    dff = w1.shape[1]
    tm = _mm_tile(st)
    tf = 512
    return pl.pallas_call(
        functools.partial(_ffn_kernel, gate_idx=gate_idx),
        grid=(m // tm, dff // tf),
        in_specs=[pl.BlockSpec((tm, d), lambda i, f: (i, 0)),
                  pl.BlockSpec((d, tf), lambda i, f: (0, f)),
                  pl.BlockSpec((d, tf), lambda i, f: (0, f)),
                  pl.BlockSpec((tf, d), lambda i, f: (f, 0)),
                  pl.BlockSpec((tm, d), lambda i, f: (i, 0)),
                  pl.BlockSpec((1, 6, d), lambda i, f: st.group_map(tm)(i))],
        out_specs=pl.BlockSpec((tm, d), lambda i, f: (i, 0)),
        out_shape=jax.ShapeDtypeStruct((m, d), F32),
        scratch_shapes=[pltpu.VMEM((tm, d), F32)],
        compiler_params=_cp("parallel", "arbitrary"),
        name="ffn",
    )(h, w1, w3, w2, x, mod_l)


def _split_w_in(w_in):
    z = lambda w: jnp.zeros(w_in.shape[:2] + (w,), w_in.dtype)
    att = w_in[..., 0:768]
    ssd = jnp.concatenate([w_in[..., 768:2064], z(SSD_W - 1296)], axis=-1)
    rwkv = w_in[..., 2064:3856]
    gla = jnp.concatenate([w_in[..., 3856:4880], w_in[..., 4880:4896], z(LANE - 16), w_in[..., 4896:5408]], axis=-1)
    return tuple(_bf(w) for w in (att, ssd, rwkv, gla))


def _block(st, x, mod_l, p, rope_tables, ctx):
    h = _normmod(st, x, p['norm1'], mod_l, 1, 0)
    proj_att = _matmul(st, h, p['w_att'])
    proj_ssd = _matmul(st, h, p['w_ssd'])
    proj_rwkv = _matmul(st, h, p['w_rwkv'])
    proj_gla = _matmul(st, h, p['w_gla'])

    v = proj_att[:, 640:768]
    if ctx is None:
        q, k_att, v_att, k = _att_prep(st, proj_att, p['q_norm'], p['k_norm'], None)
        o_att = _attention(st, q, k_att, v_att)
        s_ssd = jnp.zeros((st.n, 2, SSD_HEADS, SSD_N, SSD_P), F32)
        s_rwkv = jnp.zeros((st.n, 2, RWKV_HEADS, RWKV_HD, RWKV_HD), F32)
        s_gla_t = jnp.zeros((st.n, 2, GLA_HEADS, GLA_DV, GLA_DK), F32)
    else:
        ctx_k, ctx_v, s_ssd, s_rwkv, s_gla = ctx
        q, k_att, v_att = _att_prep(st, proj_att, p['q_norm'], p['k_norm'], rope_tables)
        k = None
        o_att = _attention(st, q, _join_cache(st, k_att, ctx_k), _join_cache(st, v_att, ctx_v))
        s_gla_t = jnp.swapaxes(s_gla, -1, -2)

    xbc, dt, cum, dtt, cumt = _ssd_prep(st, proj_ssd, p['ssd_conv_w'], p['ssd_conv_b'], p['ssd_dt_bias'],
                                        p['ssd_a_log'])
    y_ssd_f, y_ssd_b, new_ssd = _ssd_scan(st, xbc, dt, cum, dtt, cumt, s_ssd)
    o_ssd = _ssd_post(st, y_ssd_f, y_ssd_b, xbc, proj_ssd, p['ssd_d'], p['ssd_norm'])

    rin, lw, rpost = _rwkv_prep(st, proj_rwkv, p['rwkv_mu'], p['rwkv_w2'], p['rwkv_w0'], p['rwkv_a2'],
                                p['rwkv_a0'], p['rwkv_g2'], p['rwkv_kk'], p['rwkv_ka'], p['rwkv_rk'])
    y_rwkv_f, y_rwkv_b, new_rwkv = _rwkv_scan(st, rin, lw, s_rwkv)
    o_rwkv = _rwkv_post(st, y_rwkv_f, y_rwkv_b, rpost, p['rwkv_ln_g'], p['rwkv_ln_b'])

    g2 = jnp.concatenate([p['gla_g2'][0], p['gla_g2'][1]], axis=1)
    g2p = jnp.pad(g2, ((0, LANE - g2.shape[0]), (0, 0)))
    gla_cum = _gla_prep(st, proj_gla, g2p, p['gla_gb'].reshape(1, 2 * GLA_QK))
    y_gla_f, y_gla_b, new_gla_t = _gla_scan(st, proj_gla, gla_cum, s_gla_t)
    o_gla = _gla_post(st, y_gla_f, y_gla_b, proj_gla, p['gla_norm'])

    merged = _merge(st, h, (o_att, o_ssd, o_rwkv, o_gla), p['w_gate'], p['w_branch'])
    x = _out_proj(st, merged, p['w_o'], x, mod_l, 2)
    h2 = _normmod(st, x, p['norm2'], mod_l, 4, 3)
    x = _ffn(st, h2, p['ffn_w1'], p['ffn_w3'], p['ffn_w2'], x, mod_l, 5)
    return x, (k, v, new_ssd, new_rwkv, jnp.swapaxes(new_gla_t, -1, -2))


def kernel(x_prompt, x_sample, cache_attn_k, cache_attn_v, state_ssd, state_rwkv, state_gla, c, c_ctx, w_mod, b_mod, norm1, norm2, w_in, q_norm, k_norm, ssd_conv_w, ssd_conv_b, ssd_dt_bias, ssd_a_log, ssd_d, ssd_norm, rwkv_mu, rwkv_w0, rwkv_w2, rwkv_a0, rwkv_a2, rwkv_g2, rwkv_kk, rwkv_ka, rwkv_rk, rwkv_ln_g, rwkv_ln_b, gla_g2, gla_gb, gla_norm, w_gate, w_branch, w_o, ffn_w1, ffn_w3, ffn_w2, final_norm):
    nb, seq, d = x_prompt.shape
    db, dseq, _ = x_sample.shape
    depth = w_in.shape[0]
    assert d == D_MODEL and seq % CHUNK == 0 and dseq % CHUNK == 0 and 1 + db <= MOD_ROWS
    ctx_st = _Stream(nb, seq, 0, False)
    lat_st = _Stream(db, dseq, 1, True)

    cond = jnp.concatenate([c_ctx[None], c, jnp.zeros((MOD_ROWS - 1 - db, d), F32)], axis=0)
    mod = _modulation(cond, w_mod, b_mod)

    w_att, w_ssd, w_rwkv, w_gla = _split_w_in(w_in)
    w_gate_b, w_branch_b, w_o_b = _bf(w_gate), _bf(w_branch), _bf(w_o)
    w1_b, w3_b, w2_b = _bf(ffn_w1), _bf(ffn_w3), _bf(ffn_w2)

    def params_at(l):
        return dict(norm1=norm1[l], norm2=norm2[l], w_att=w_att[l], w_ssd=w_ssd[l], w_rwkv=w_rwkv[l],
                    w_gla=w_gla[l], q_norm=q_norm[l], k_norm=k_norm[l],
                    ssd_conv_w=ssd_conv_w[l], ssd_conv_b=ssd_conv_b[l], ssd_dt_bias=ssd_dt_bias[l],
                    ssd_a_log=ssd_a_log[l], ssd_d=ssd_d[l], ssd_norm=ssd_norm[l],
                    rwkv_mu=rwkv_mu[l], rwkv_w0=rwkv_w0[l], rwkv_w2=rwkv_w2[l], rwkv_a0=rwkv_a0[l],
                    rwkv_a2=rwkv_a2[l], rwkv_g2=rwkv_g2[l], rwkv_kk=rwkv_kk[l], rwkv_ka=rwkv_ka[l],
                    rwkv_rk=rwkv_rk[l], rwkv_ln_g=rwkv_ln_g[l], rwkv_ln_b=rwkv_ln_b[l],
                    gla_g2=gla_g2[l], gla_gb=gla_gb[l], gla_norm=gla_norm[l],
                    w_gate=w_gate_b[l], w_branch=w_branch_b[l], w_o=w_o_b[l],
                    ffn_w1=w1_b[l], ffn_w3=w3_b[l], ffn_w2=w2_b[l])

    xp = x_prompt.reshape(nb * seq, d)
    new_k, new_v, new_ssd, new_rwkv, new_gla = [], [], [], [], []
    for l in range(depth):
        xp, (k_l, v_l, ssd_l, rwkv_l, gla_l) = _block(ctx_st, xp, mod[l], params_at(l), None, None)
        new_k.append(k_l.reshape(nb, seq, ATT_KV, HEAD_DIM))
        new_v.append(v_l.reshape(nb, seq, ATT_KV, HEAD_DIM))
        new_ssd.append(ssd_l)
        new_rwkv.append(rwkv_l)
        new_gla.append(gla_l)

    rope_tables = _rope_tables(dseq)
    xs = x_sample.reshape(db * dseq, d)
    for l in range(depth):
        ctx = (cache_attn_k[:, l], cache_attn_v[:, l], state_ssd[:, l], state_rwkv[:, l], state_gla[:, l])
        xs, _ = _block(lat_st, xs, mod[l], params_at(l), rope_tables, ctx)

    y_prompt = _final_norm(ctx_st, xp, final_norm).reshape(nb, seq, d)
    y_sample = _final_norm(lat_st, xs, final_norm).reshape(db, dseq, d)
    return (y_prompt, y_sample, jnp.stack(new_k, axis=1), jnp.stack(new_v, axis=1),
            jnp.stack(new_ssd, axis=1), jnp.stack(new_rwkv, axis=1), jnp.stack(new_gla, axis=1))
```

```python
import functools

import jax
import jax.numpy as jnp
import numpy as np
from jax import lax
from jax.experimental import pallas as pl
from jax.experimental.pallas import tpu as pltpu

F32 = jnp.float32
BF16 = jnp.bfloat16

D_MODEL = 2048
GRID_W = 64
ATT_HEADS = 8
ATT_KV = 2
ATT_GROUP = ATT_HEADS // ATT_KV
HEAD_DIM = 64
ROPE_THETA = 10000.0
SSD_HEADS = 8
SSD_P = 64
SSD_N = 64
SSD_GROUPS = 2
RWKV_HEADS = 8
RWKV_HD = 64
RWKV_DECAY_SCALE = 0.6065306597126334
RWKV_LN_EPS = 64e-5
GLA_HEADS = 4
GLA_DK = 64
GLA_DV = 128
GLA_GATE_NORM = 16.0
CHUNK = 64
BRANCH_W = 512

ATT_W = 768
SSD_W = 1408
RWKV_W = 1792
GLA_W = 1664
LANE = 128
SUBLANE = 8
MOD_ROWS = 16
VMEM_LIMIT = 56 * 1024 * 1024
ROW_TILE = 256
MM_TILE = 512
SCAN_CHUNKS = 4
KEY_TILE = 256
KEY_UNROLL = 6
LOG2E = 1.4426950408889634
ATT_MIN_ROW_SUM = 2.0 ** -90


def _cp(*sem):
    return pltpu.CompilerParams(dimension_semantics=sem, vmem_limit_bytes=VMEM_LIMIT)


def _bf(x):
    return x.astype(BF16)


def _dot(a, b, prec=None):
    return jnp.dot(a, b, preferred_element_type=F32, precision=prec)


def _dot_nt(a, b, prec=None):
    return lax.dot_general(a, b, (((1,), (1,)), ((), ())), preferred_element_type=F32, precision=prec)


def _dot_tn(a, b, prec=None):
    return lax.dot_general(a, b, (((0,), (0,)), ((), ())), preferred_element_type=F32, precision=prec)


def _pieces(a, n):
    out = []
    for _ in range(n):
        piece = _bf(a)
        out.append(piece)
        a = a - piece.astype(F32)
    return out


def _dot_data_mask(a, mask, n):
    return functools.reduce(jnp.add, [_dot(piece, mask) for piece in _pieces(a, n)])


def _dot_mask_data(mask, a, n):
    return functools.reduce(jnp.add, [_dot(mask, piece) for piece in _pieces(a, n)])


def _dot_nt_mask_data(mask, a, n):
    return functools.reduce(jnp.add, [_dot_nt(mask, piece) for piece in _pieces(a, n)])


def _dot_split(a, b):
    a_hi, a_lo = _pieces(a, 2)
    b_hi, b_lo = _pieces(b, 2)
    return _dot(a_hi, b_hi) + (_dot(a_hi, b_lo) + _dot(a_lo, b_hi))


def _silu(x):
    return x * jax.nn.sigmoid(x)


def _softplus(x):
    return jnp.maximum(x, 0.0) + jnp.log1p(jnp.exp(-jnp.abs(x)))


def _lanes(pieces):
    return jnp.concatenate(pieces, axis=1)


def _rows(pieces):
    return jnp.concatenate(pieces, axis=0)


def _mod_kernel(c_ref, w_ref, b_ref, o_ref):
    c = c_ref[...]
    o_ref[0] = _dot(_bf(_silu(c)), _bf(w_ref[0])) + b_ref[0]


def _modulation(cond, w_mod, b_mod):
    nl, d, n6 = w_mod.shape
    tn = 1024
    out = pl.pallas_call(
        _mod_kernel,
        grid=(nl, n6 // tn),
        in_specs=[pl.BlockSpec((MOD_ROWS, d), lambda l, j: (0, 0)),
                  pl.BlockSpec((1, d, tn), lambda l, j: (l, 0, j)),
                  pl.BlockSpec((1, 1, tn), lambda l, j: (l, 0, j))],
        out_specs=pl.BlockSpec((1, MOD_ROWS, tn), lambda l, j: (l, 0, j)),
        out_shape=jax.ShapeDtypeStruct((nl, MOD_ROWS, n6), F32),
        compiler_params=_cp("parallel", "parallel"),
        name="modulation",
    )(cond, w_mod, b_mod.reshape(nl, 1, n6))
    return out.reshape(nl, MOD_ROWS, 6, d)


class _Stream:
    def __init__(self, n, t, group0, per_seq):
        self.n, self.t, self.group0, self.per_seq = n, t, group0, per_seq
        self.rows = n * t

    def group_map(self, tile):
        g0, per_seq, t = self.group0, self.per_seq, self.t
        if per_seq:
            return lambda i: (g0 + (i * tile) // t, 0, 0)
        return lambda i: (g0, 0, 0)


def _mm_tile(st):
    return min(MM_TILE, st.t)


def _row_tile(st):
    return min(ROW_TILE, st.t)


def _rms(x, gain):
    return x * lax.rsqrt(jnp.mean(x * x, axis=-1, keepdims=True) + 1e-6) * gain


def _norm_mod(x, gain, mod_ref, sc_idx, sh_idx):
    return _bf(_rms(x, gain) * (1.0 + mod_ref[0, sc_idx:sc_idx + 1, :]) + mod_ref[0, sh_idx:sh_idx + 1, :])


def _in_proj_a_kernel(x_ref, g_ref, mod_ref, wa_ref, ws_ref, h_out, pa_out, ps_out):
    h = _norm_mod(x_ref[...], g_ref[...], mod_ref, 1, 0)
    h_out[...] = h
    pa_out[...] = _dot(h, wa_ref[...])
    ps_out[...] = _dot(h, ws_ref[...])


def _in_proj_a(st, x, gain, mod_l, w_att, w_ssd):
    m, d = x.shape
    tm = _mm_tile(st)
    row = lambda w: pl.BlockSpec((tm, w), lambda i: (i, 0))
    full = lambda a: pl.BlockSpec(a.shape, lambda i: (0, 0))
    return pl.pallas_call(
        _in_proj_a_kernel,
        grid=(m // tm,),
        in_specs=[row(d), pl.BlockSpec((1, d), lambda i: (0, 0)), pl.BlockSpec((1, 6, d), st.group_map(tm)),
                  full(w_att), full(w_ssd)],
        out_specs=[row(d), row(ATT_W), row(SSD_W)],
        out_shape=[jax.ShapeDtypeStruct((m, d), BF16), jax.ShapeDtypeStruct((m, ATT_W), F32),
                   jax.ShapeDtypeStruct((m, SSD_W), F32)],
        compiler_params=_cp("parallel"),
        name="in_proj_a",
    )(x, gain.reshape(1, d), mod_l, w_att, w_ssd)


def _in_proj_b_kernel(h_ref, wr_ref, wg_ref, pr_out, pg_out):
    h = h_ref[...]
    pr_out[...] = _dot(h, wr_ref[...])
    pg_out[...] = _dot(h, wg_ref[...])


def _in_proj_b(st, h, w_rwkv, w_gla):
    m, d = h.shape
    tm = _mm_tile(st)
    row = lambda w: pl.BlockSpec((tm, w), lambda i: (i, 0))
    full = lambda a: pl.BlockSpec(a.shape, lambda i: (0, 0))
    return pl.pallas_call(
        _in_proj_b_kernel,
        grid=(m // tm,),
        in_specs=[row(d), full(w_rwkv), full(w_gla)],
        out_specs=[row(RWKV_W), row(GLA_W)],
        out_shape=[jax.ShapeDtypeStruct((m, RWKV_W), F32), jax.ShapeDtypeStruct((m, GLA_W), F32)],
        compiler_params=_cp("parallel"),
        name="in_proj_b",
    )(h, w_rwkv, w_gla)


def _att_prep_kernel(*refs, rope):
    if rope:
        p_ref, qn_ref, kn_ref, bdq_ref, bdk_ref, cos_ref, sa_ref, sb_ref, q_out, k_out, v_out = refs
    else:
        p_ref, qn_ref, kn_ref, bdq_ref, bdk_ref, q_out, k_out, v_out, k_leaf = refs
    p = p_ref[...]
    aq = p[:, :512]
    ak = p[:, 512:640]
    av = p[:, 640:768]
    q = aq * lax.rsqrt(_dot_data_mask(aq * aq, bdq_ref[...], 2) + 1e-6) * qn_ref[...]
    k = ak * lax.rsqrt(_dot_data_mask(ak * ak, bdk_ref[...], 2) + 1e-6) * kn_ref[...]
    if rope:
        c, sa, sb = cos_ref[...], sa_ref[...], sb_ref[...]
        k = k * c + pltpu.roll(k, LANE - 16, 1) * sa + pltpu.roll(k, 16, 1) * sb
        c4, sa4, sb4 = _lanes([c] * 4), _lanes([sa] * 4), _lanes([sb] * 4)
        q = q * c4 + pltpu.roll(q, 512 - 16, 1) * sa4 + pltpu.roll(q, 16, 1) * sb4
    else:
        k_leaf[...] = k
    tr = p.shape[0]
    one_col = (lax.broadcasted_iota(jnp.int32, (tr, LANE - HEAD_DIM), 1) == 0).astype(F32)
    q = q * (HEAD_DIM ** -0.5 * LOG2E)
    q_len = jnp.sqrt(_dot_data_mask(q * q, bdq_ref[...], 2) * HEAD_DIM)
    q_out[...] = _bf(_lanes([piece for h in range(ATT_HEADS)
                             for piece in (q[:, h * HEAD_DIM:(h + 1) * HEAD_DIM],
                                           one_col * q_len[:, h * HEAD_DIM:(h + 1) * HEAD_DIM])]))
    for kv in range(ATT_KV):
        sl = slice(kv * HEAD_DIM, (kv + 1) * HEAD_DIM)
        k_out[kv] = _bf(_lanes([k[:, sl], one_col]))
        v_out[kv] = _bf(_lanes([av[:, sl], one_col]))


def _block_diag(width, block, value):
    idx = np.arange(width) // block
    return _bf(jnp.asarray((idx[:, None] == idx[None, :]).astype(np.float32) * value))


def _att_prep(st, proj, q_norm, k_norm, rope_tables):
    m = proj.shape[0]
    tr = _row_tile(st)
    rope = rope_tables is not None
    full = lambda shape: pl.BlockSpec(shape, lambda i: (0, 0))
    in_specs = [pl.BlockSpec((tr, ATT_W), lambda i: (i, 0)), full((1, 512)), full((1, 128)),
                full((512, 512)), full((128, 128))]
    args = [proj, jnp.tile(q_norm, ATT_HEADS).reshape(1, 512), jnp.tile(k_norm, ATT_KV).reshape(1, 128),
            _block_diag(512, HEAD_DIM, 1.0 / HEAD_DIM), _block_diag(128, HEAD_DIM, 1.0 / HEAD_DIM)]
    if rope:
        tps = st.t // tr
        in_specs += [pl.BlockSpec((tr, LANE), lambda i: (i % tps, 0))] * 3
        args += list(rope_tables)
    kv_spec = pl.BlockSpec((ATT_KV, tr, LANE), lambda i: (0, i, 0))
    kv_shape = jax.ShapeDtypeStruct((ATT_KV, m, LANE), BF16)
    out_specs = [pl.BlockSpec((tr, ATT_HEADS * LANE), lambda i: (i, 0)), kv_spec, kv_spec]
    out_shape = [jax.ShapeDtypeStruct((m, ATT_HEADS * LANE), BF16), kv_shape, kv_shape]
    if not rope:
        out_specs.append(pl.BlockSpec((tr, 128), lambda i: (i, 0)))
        out_shape.append(jax.ShapeDtypeStruct((m, 128), F32))
    return pl.pallas_call(
        functools.partial(_att_prep_kernel, rope=rope),
        grid=(m // tr,),
        in_specs=in_specs,
        out_specs=out_specs,
        out_shape=out_shape,
        compiler_params=_cp("parallel"),
        name="att_prep",
    )(*args)


def _join_cache(st, own, cache):
    n, p, kv, hd = cache.shape
    c = jnp.transpose(cache, (2, 0, 1, 3))
    pad = jnp.zeros((kv, n, p, LANE - hd), c.dtype).at[..., 0].set(1.0)
    c = _bf(jnp.concatenate([c, pad], axis=-1))
    joined = jnp.concatenate([own.reshape(kv, n, st.t, LANE), c], axis=2)
    return joined.reshape(kv, n * (st.t + p), LANE)


def _rope_tables(t):
    half = HEAD_DIM // 2
    nf = half // 2
    freqs = ROPE_THETA ** (-jnp.arange(nf, dtype=F32) / nf)
    tt = jnp.arange(t)
    ang_r = (tt // GRID_W).astype(F32)[:, None] * freqs[None, :]
    ang_c = (tt % GRID_W).astype(F32)[:, None] * freqs[None, :]
    zero = jnp.zeros_like(ang_r)
    cos = jnp.concatenate([jnp.cos(ang_r)] * 2 + [jnp.cos(ang_c)] * 2, axis=1)
    sa = jnp.concatenate([-jnp.sin(ang_r), zero, -jnp.sin(ang_c), zero], axis=1)
    sb = jnp.concatenate([zero, jnp.sin(ang_r), zero, jnp.sin(ang_c)], axis=1)
    return tuple(jnp.tile(a, (1, LANE // HEAD_DIM)) for a in (cos, sa, sb))


def _key_bound_kernel(k_ref, o_ref):
    ones = jnp.ones((LANE, LANE), BF16)
    for kv in range(ATT_KV):
        k = k_ref[kv].astype(F32)
        k = jnp.where(lax.broadcasted_iota(jnp.int32, k.shape, 1) < HEAD_DIM, k, 0.0)
        best = jnp.max(_dot_data_mask(k * k, ones, 2), axis=0, keepdims=True)
        o_ref[kv] = jnp.broadcast_to(best, (SUBLANE, LANE))


def _key_bound(n, keys_per_seq, k):
    return pl.pallas_call(
        _key_bound_kernel,
        grid=(n,),
        in_specs=[pl.BlockSpec((ATT_KV, keys_per_seq, LANE), lambda b: (0, b, 0))],
        out_specs=pl.BlockSpec((None, ATT_KV, SUBLANE, LANE), lambda b: (b, 0, 0, 0)),
        out_shape=jax.ShapeDtypeStruct((n, ATT_KV, SUBLANE, LANE), F32),
        compiler_params=_cp("parallel"),
        name="key_bound",
    )(k)


def _attn_kernel(q_ref, k_ref, v_ref, kb_ref, o_ref, m_scr, acc_scr, *, tq, n_tiles):
    rows = ATT_GROUP * tq
    lane = lax.broadcasted_iota(jnp.int32, (rows, LANE), 1)
    outs = []
    for kv in range(ATT_KV):
        qs = _rows([q_ref[:, (kv * ATT_GROUP + g) * LANE:(kv * ATT_GROUP + g + 1) * LANE]
                    for g in range(ATT_GROUP)]).astype(F32)

        def key_tile(ref, j):
            return ref[kv, pl.ds(pl.multiple_of(j * KEY_TILE, KEY_TILE), KEY_TILE), :]

        def weighted_sum(q_shift, unroll):
            acc_scr[...] = jnp.zeros((rows, LANE), F32)

            def sum_body(j, carry):
                p = jnp.exp2(_dot_nt(q_shift, key_tile(k_ref, j)))
                acc_scr[...] += _dot(_bf(p), key_tile(v_ref, j))
                return carry

            lax.fori_loop(0, n_tiles, sum_body, 0, unroll=unroll)
            return acc_scr[...]

        k_len = jnp.sqrt(kb_ref[kv][0:1, :])
        unroll = max(u for u in range(1, KEY_UNROLL + 1) if n_tiles % u == 0)
        acc = weighted_sum(_bf(qs * jnp.where(lane == HEAD_DIM, -k_len, 1.0)), unroll)

        def exact_shift():
            m_scr[...] = jnp.full((rows, LANE), -jnp.inf, F32)
            q0 = _bf(jnp.where(lane == HEAD_DIM, 0.0, qs))

            def max_body(j, carry):
                sc = _dot_nt(q0, key_tile(k_ref, j))
                m_scr[...] = jnp.maximum(m_scr[...], jnp.maximum(sc[:, :LANE], sc[:, LANE:]))
                return carry

            lax.fori_loop(0, n_tiles, max_body, 0)
            row_max = jnp.max(m_scr[...], axis=-1, keepdims=True)
            return weighted_sum(_bf(jnp.where(lane == HEAD_DIM, -row_max, qs)), 1)

        row_sum_ok = jnp.min(acc[:, HEAD_DIM:HEAD_DIM + 1]) >= ATT_MIN_ROW_SUM
        acc = lax.cond(row_sum_ok, lambda: acc, exact_shift)
        o = acc[:, :HEAD_DIM] / acc[:, HEAD_DIM:HEAD_DIM + 1]
        outs += [o[g * tq:(g + 1) * tq] for g in range(ATT_GROUP)]
    o_ref[...] = _bf(_lanes(outs))


def _attention(st, q, k, v):
    tq = min(128, st.t)
    nq = st.t // tq
    keys = k.shape[1] // st.n
    assert keys % KEY_TILE == 0
    kv_spec = pl.BlockSpec((ATT_KV, keys, LANE), lambda b, i: (0, b, 0))
    rows = ATT_GROUP * tq
    return pl.pallas_call(
        functools.partial(_attn_kernel, tq=tq, n_tiles=keys // KEY_TILE),
        grid=(st.n, nq),
        in_specs=[pl.BlockSpec((tq, ATT_HEADS * LANE), lambda b, i: (b * nq + i, 0)), kv_spec, kv_spec,
                  pl.BlockSpec((None, ATT_KV, SUBLANE, LANE), lambda b, i: (b, 0, 0, 0))],
        out_specs=pl.BlockSpec((tq, 512), lambda b, i: (b * nq + i, 0)),
        out_shape=jax.ShapeDtypeStruct((st.rows, 512), BF16),
        scratch_shapes=[pltpu.VMEM((rows, LANE), F32), pltpu.VMEM((rows, LANE), F32)],
        compiler_params=_cp("parallel", "parallel"),
        name="attention",
    )(q, k, v, _key_bound(st.n, keys, k))


def _halo_specs(st, width):
    tr = _row_tile(st)
    per8 = tr // SUBLANE
    last8 = st.rows // SUBLANE - 1
    return [pl.BlockSpec((tr, width), lambda i: (i, 0)),
            pl.BlockSpec((SUBLANE, width), lambda i: (jnp.maximum(i * per8 - 1, 0), 0)),
            pl.BlockSpec((SUBLANE, width), lambda i: (jnp.minimum((i + 1) * per8, last8), 0))]


def _neighbours(cur, prev8, next8, tiles_per_seq):
    tr = cur.shape[0]
    j = pl.program_id(0) % tiles_per_seq
    pr = jnp.where(j != 0, prev8[SUBLANE - 1:SUBLANE], 0.0)
    nx = jnp.where(j != tiles_per_seq - 1, next8[0:1], 0.0)
    row = lax.broadcasted_iota(jnp.int32, cur.shape, 0)
    x_prev = jnp.where(row == 0, pr, pltpu.roll(cur, 1, 0))
    x_next = jnp.where(row == tr - 1, nx, pltpu.roll(cur, tr - 1, 0))
    return x_prev, x_next


def _dir_masks(z):
    ri = lax.broadcasted_iota(jnp.int32, (CHUNK, CHUNK), 0)
    ci = lax.broadcasted_iota(jnp.int32, (CHUNK, CHUNK), 1)
    return ((ci <= ri), (ci < ri)) if z == 0 else ((ci >= ri), (ci > ri))


def _tri_blocks(tr):
    i = np.arange(tr)
    same = (i[:, None] // CHUNK) == (i[None, :] // CHUNK)
    lower = same & (i[None, :] <= i[:, None])
    upper = same & (i[None, :] >= i[:, None])
    return _bf(jnp.asarray(lower.astype(np.float32))), _bf(jnp.asarray(upper.astype(np.float32)))


def _scan_blocks(st):
    ncb = min(SCAN_CHUNKS, st.t // CHUNK)
    rb = ncb * CHUNK
    nblk = st.t // rb
    fwd = lambda b, j: (b * nblk + j, 0)
    bwd = lambda b, j: (b * nblk + nblk - 1 - j, 0)
    return ncb, rb, nblk, fwd, bwd


def _chunk_rows(step, z, ncb):
    cc = step if z == 0 else ncb - 1 - step
    return slice(cc * CHUNK, (cc + 1) * CHUNK)


def _ssd_prep_kernel(cur_ref, prev_ref, next_ref, cw_ref, cb_ref, dtb_ref, alog_ref, lo_ref, up_ref,
                     xbc_out, dt_out, cum_out, dtt_out, cumt_out, *, tiles_per_seq):
    cur = cur_ref[...]
    xc = cur[:, 512:1280]
    x_prev, x_next = _neighbours(xc, prev_ref[:, 512:1280], next_ref[:, 512:1280], tiles_per_seq)
    cw = cw_ref[...]
    conv = cb_ref[...] + x_prev * cw[0:1] + xc * cw[1:2] + x_next * cw[2:3]
    xbc_out[...] = _silu(conv)
    dt = _softplus(cur[:, 1280:1408] + dtb_ref[...])
    dt_out[...] = dt
    ld = dt * -jnp.exp(alog_ref[...])
    lane = lax.broadcasted_iota(jnp.int32, ld.shape, 1)
    cum = jnp.where(lane < SSD_HEADS, _dot_mask_data(lo_ref[...], ld, 3), _dot_mask_data(up_ref[...], ld, 3))
    cum_out[...] = cum
    pick = (lax.broadcasted_iota(jnp.int32, (2 * SSD_HEADS, LANE), 0)
            == lax.broadcasted_iota(jnp.int32, (2 * SSD_HEADS, LANE), 1)).astype(BF16)
    dtt_out[...] = _dot_nt_mask_data(pick, dt, 3)
    cumt_out[...] = _dot_nt_mask_data(pick, cum, 3)


def _pad_lanes(v, width=LANE):
    v = v.reshape(1, -1)
    return jnp.pad(v, ((0, 0), (0, width - v.shape[1])))


def _ssd_prep(st, proj, conv_w, conv_b, dt_bias, a_log):
    m = proj.shape[0]
    tr = _row_tile(st)
    full = lambda shape: pl.BlockSpec(shape, lambda i: (0, 0))
    row = lambda w: pl.BlockSpec((tr, w), lambda i: (i, 0))
    col = pl.BlockSpec((2 * SSD_HEADS, tr), lambda i: (0, i))
    lower, upper = _tri_blocks(tr)
    return pl.pallas_call(
        functools.partial(_ssd_prep_kernel, tiles_per_seq=st.t // tr),
        grid=(m // tr,),
        in_specs=_halo_specs(st, SSD_W) + [full((3, 768)), full((1, 768)), full((1, LANE)), full((1, LANE)),
                                           full((tr, tr)), full((tr, tr))],
        out_specs=[row(768), row(LANE), row(LANE), col, col],
        out_shape=[jax.ShapeDtypeStruct((m, 768), F32), jax.ShapeDtypeStruct((m, LANE), F32),
                   jax.ShapeDtypeStruct((m, LANE), F32), jax.ShapeDtypeStruct((2 * SSD_HEADS, m), F32),
                   jax.ShapeDtypeStruct((2 * SSD_HEADS, m), F32)],
        compiler_params=_cp("parallel"),
        name="ssd_prep",
    )(proj, proj, proj, conv_w.T, conv_b.reshape(1, 768), _pad_lanes(dt_bias), _pad_lanes(a_log), lower, upper)


def _ssd_scan_kernel(xf_ref, xb_ref, dtf_ref, dtb_ref, cf_ref, cb_ref, dttf_ref, dttb_ref, ctf_ref, ctb_ref,
                     s0_ref, yf_ref, yb_ref, s_ref, *, ncb):
    @pl.when(pl.program_id(1) == 0)
    def _():
        s_ref[...] = s0_ref[...]

    hpg = SSD_HEADS // SSD_GROUPS
    state = {(z, g): _lanes([s_ref[z, g * hpg + hh] for hh in range(hpg)])
             for z in range(2) for g in range(SSD_GROUPS)}
    refs = ((xf_ref, dtf_ref, cf_ref, dttf_ref, ctf_ref, yf_ref), (xb_ref, dtb_ref, cb_ref, dttb_ref, ctb_ref, yb_ref))
    groups = [(z, g) for z in range(2) for g in range(SSD_GROUPS)]
    heads = [(z, h) for z in range(2) for h in range(SSD_HEADS)]
    for step in range(ncb):
        xbc, dt, cum, dtt, cumt, incl, rows = {}, {}, {}, {}, {}, {}, {}
        for z in range(2):
            x_ref, dt_ref, c_ref, dtt_ref, ct_ref, _ = refs[z]
            rows[z] = _chunk_rows(step, z, ncb)
            xbc[z], dt[z], cum[z] = x_ref[rows[z], :], dt_ref[rows[z], :], c_ref[rows[z], :]
            dtt[z], cumt[z] = dtt_ref[:, rows[z]], ct_ref[:, rows[z]]
            incl[z] = _dir_masks(z)[0]
        bmat = {(z, g): xbc[z][:, 512 + g * SSD_N:512 + (g + 1) * SSD_N] for z, g in groups}
        cmat = {(z, g): _bf(xbc[z][:, 640 + g * SSD_N:640 + (g + 1) * SSD_N]) for z, g in groups}
        cb = {k: _dot_nt(cmat[k], _bf(bmat[k])) for k in groups}
        cs = {k: _dot(cmat[k], _bf(state[k])) for k in groups}
        gcol, glast, xs = {}, {}, {}
        for z, h in heads:
            ln = z * SSD_HEADS + h
            gcol[z, h] = cum[z][:, ln:ln + 1]
            glast[z, h] = gcol[z, h][CHUNK - 1:CHUNK] if z == 0 else gcol[z, h][0:1]
            xs[z, h] = xbc[z][:, h * SSD_P:(h + 1) * SSD_P]
        inc = {}
        for z, g in groups:
            xw = []
            for h in range(g * hpg, (g + 1) * hpg):
                ln = z * SSD_HEADS + h
                xw.append(xs[z, h] * (dt[z][:, ln:ln + 1] * jnp.exp(glast[z, h] - gcol[z, h])))
            inc[z, g] = _dot_tn(_bf(bmat[z, g]), _bf(_lanes(xw)))
        att = {}
        for z, h in heads:
            ln = z * SSD_HEADS + h
            dec = jnp.exp(jnp.where(incl[z], gcol[z, h] - cumt[z][ln:ln + 1], -jnp.inf))
            att[z, h] = _bf(cb[z, h // hpg] * dec * dtt[z][ln:ln + 1])
        intra = {k: _dot(att[k], _bf(xs[k])) for k in heads}
        for z in range(2):
            outs = []
            for g in range(SSD_GROUPS):
                hs = range(g * hpg, (g + 1) * hpg)
                e_in = _lanes([jnp.broadcast_to(jnp.exp(gcol[z, h]), (CHUNK, SSD_P)) for h in hs])
                outs.append(_lanes([intra[z, h] for h in hs]) + cs[z, g] * e_in)
                e_last = _lanes([jnp.broadcast_to(jnp.exp(glast[z, h]), (1, SSD_P)) for h in hs])
                state[z, g] = state[z, g] * e_last + inc[z, g]
            refs[z][5][rows[z], :] = _lanes(outs)
    for z in range(2):
        for h in range(SSD_HEADS):
            s_ref[z, h] = state[z, h // hpg][:, (h % hpg) * SSD_P:(h % hpg + 1) * SSD_P]


def _ssd_scan(st, xbc, dt, cum, dtt, cumt, s0):
    ncb, rb, nblk, fwd, bwd = _scan_blocks(st)
    fwd_t = lambda b, j: (0, fwd(b, j)[0])
    bwd_t = lambda b, j: (0, bwd(b, j)[0])
    state_spec = pl.BlockSpec((None, 2, SSD_HEADS, SSD_N, SSD_P), lambda b, j: (b, 0, 0, 0, 0))
    rows = lambda w, m: pl.BlockSpec((rb, w), m)
    cols = lambda m: pl.BlockSpec((2 * SSD_HEADS, rb), m)
    return pl.pallas_call(
        functools.partial(_ssd_scan_kernel, ncb=ncb),
        grid=(st.n, nblk),
        in_specs=[rows(768, fwd), rows(768, bwd), rows(LANE, fwd), rows(LANE, bwd), rows(LANE, fwd), rows(LANE, bwd),
                  cols(fwd_t), cols(bwd_t), cols(fwd_t), cols(bwd_t), state_spec],
        out_specs=[rows(512, fwd), rows(512, bwd), state_spec],
        out_shape=[jax.ShapeDtypeStruct((st.rows, 512), F32), jax.ShapeDtypeStruct((st.rows, 512), F32),
                   jax.ShapeDtypeStruct((st.n, 2, SSD_HEADS, SSD_N, SSD_P), F32)],
        compiler_params=_cp("parallel", "arbitrary"),
        name="ssd_scan",
    )(xbc, xbc, dt, dt, cum, cum, dtt, dtt, cumt, cumt, s0)


def _ssd_post_kernel(yf_ref, yb_ref, xbc_ref, p_ref, d_ref, g_ref, o_ref):
    y = yf_ref[...] + yb_ref[...] + d_ref[...] * xbc_ref[:, :512]
    y = y * _silu(p_ref[:, :512])
    o_ref[...] = _bf(y * lax.rsqrt(jnp.mean(y * y, axis=-1, keepdims=True) + 1e-6) * g_ref[...])


def _ssd_post(st, y_f, y_b, xbc, proj, ssd_d, ssd_norm):
    m = proj.shape[0]
    tr = _row_tile(st)
    full = lambda shape: pl.BlockSpec(shape, lambda i: (0, 0))
    row = lambda w: pl.BlockSpec((tr, w), lambda i: (i, 0))
    return pl.pallas_call(
        _ssd_post_kernel,
        grid=(m // tr,),
        in_specs=[row(512), row(512), row(768), row(SSD_W), full((1, 512)), full((1, 512))],
        out_specs=row(512),
        out_shape=jax.ShapeDtypeStruct((m, 512), BF16),
        compiler_params=_cp("parallel"),
        name="ssd_post",
    )(y_f, y_b, xbc, proj, jnp.repeat(ssd_d, SSD_P).reshape(1, 512), ssd_norm.reshape(1, 512))


GLA_QK = GLA_HEADS * GLA_DK
GLA_SAFE_RANGE = 60.0


def _gla_prep_kernel(p_ref, g2_ref, gb_ref, lo_ref, up_ref, cum_out):
    logit = _dot_split(p_ref[:, 1024:1152], g2_ref[...]) + gb_ref[...]
    log_a = -_softplus(-logit) * (1.0 / GLA_GATE_NORM)
    cum_out[:, :GLA_QK] = _dot_mask_data(lo_ref[...], log_a[:, :GLA_QK], 3)
    cum_out[:, GLA_QK:] = _dot_mask_data(up_ref[...], log_a[:, GLA_QK:], 3)


def _gla_prep(st, proj, g2p, gb):
    m = proj.shape[0]
    tr = _row_tile(st)
    full = lambda shape: pl.BlockSpec(shape, lambda i: (0, 0))
    lower, upper = _tri_blocks(tr)
    return pl.pallas_call(
        _gla_prep_kernel,
        grid=(m // tr,),
        in_specs=[pl.BlockSpec((tr, GLA_W), lambda i: (i, 0)), full((LANE, 2 * GLA_QK)), full((1, 2 * GLA_QK)),
                  full((tr, tr)), full((tr, tr))],
        out_specs=pl.BlockSpec((tr, 2 * GLA_QK), lambda i: (i, 0)),
        out_shape=jax.ShapeDtypeStruct((m, 2 * GLA_QK), F32),
        compiler_params=_cp("parallel"),
        name="gla_prep",
    )(proj, g2p, gb, lower, upper)


def _gla_intra_exact(p_ref, c_ref, rows, z, seg_ref):
    r0 = rows.start
    q = p_ref[rows, 0:GLA_QK] * (GLA_DK ** -0.5)
    cum = c_ref[rows, :]
    row = lax.broadcasted_iota(jnp.int32, (CHUNK, 1), 0)
    seg = seg_ref[...]

    def body(j, acc):
        kj = p_ref[pl.ds(r0 + j, 1), GLA_QK:2 * GLA_QK]
        vj = p_ref[pl.ds(r0 + j, 1), 2 * GLA_QK:2 * GLA_QK + GLA_HEADS * GLA_DV]
        seen = (row >= j) if z == 0 else (row <= j)
        w = jnp.where(seen, q * kj * jnp.exp(jnp.minimum(cum - c_ref[pl.ds(r0 + j, 1), :], 0.0)), 0.0)
        score = _dot_data_mask(w, seg, 3)
        return tuple(a + score[:, h:h + 1] * vj[:, h * GLA_DV:(h + 1) * GLA_DV] for h, a in enumerate(acc))

    zero = jnp.zeros((CHUNK, GLA_DV), F32)
    return list(lax.fori_loop(0, CHUNK, body, (zero,) * GLA_HEADS))


def _gla_scan_kernel(pf_ref, pb_ref, cf_ref, cb_ref, seg_ref, s0_ref, yf_ref, yb_ref, s_ref, *, ncb):
    @pl.when(pl.program_id(1) == 0)
    def _():
        s_ref[...] = s0_ref[...]

    refs = ((pf_ref, cf_ref, yf_ref), (pb_ref, cb_ref, yb_ref))
    heads = [(z, h) for z in range(2) for h in range(GLA_HEADS)]
    hsl = [slice(h * GLA_DK, (h + 1) * GLA_DK) for h in range(GLA_HEADS)]
    state = {(z, h): s_ref[z, h] for z, h in heads}
    chunks = [(step, z) for step in range(ncb) for z in range(2)]
    rows = {(step, z): _chunk_rows(step, z, ncb) for step, z in chunks}
    cum = {k: refs[k[1]][1][rows[k], :] for k in chunks}
    mid = {k: cum[k][CHUNK // 2:CHUNK // 2 + 1] for k in chunks}
    span = functools.reduce(jnp.maximum, [jnp.max(jnp.abs(cum[k] - mid[k])) for k in chunks])

    def v_of(k, h):
        return _bf(refs[k[1]][0][rows[k], 2 * GLA_QK + h * GLA_DV:2 * GLA_QK + (h + 1) * GLA_DV])

    def intra_factored():
        att = {}
        for k in chunks:
            p_ref = refs[k[1]][0]
            q_mid = _bf(p_ref[rows[k], 0:GLA_QK] * (GLA_DK ** -0.5) * jnp.exp(cum[k] - mid[k]))
            k_mid = _bf(p_ref[rows[k], GLA_QK:2 * GLA_QK] * jnp.exp(mid[k] - cum[k]))
            incl = _dir_masks(k[1])[0]
            for h in range(GLA_HEADS):
                att[k, h] = _bf(jnp.where(incl, _dot_nt(q_mid[:, hsl[h]], k_mid[:, hsl[h]]), 0.0))
        return [_dot(att[k, h], v_of(k, h)) for k in chunks for h in range(GLA_HEADS)]

    def intra_exact():
        out = []
        for k in chunks:
            out += _gla_intra_exact(refs[k[1]][0], refs[k[1]][1], rows[k], k[1], seg_ref)
        return out

    intra = lax.cond(span <= GLA_SAFE_RANGE, intra_factored, intra_exact)
    intra = {(k, h): intra[i * GLA_HEADS + h] for i, k in enumerate(chunks) for h in range(GLA_HEADS)}

    for step in range(ncb):
        q_in, k_out, e_last = {}, {}, {}
        for z in range(2):
            k = (step, z)
            p_ref = refs[z][0]
            glast = cum[k][CHUNK - 1:CHUNK] if z == 0 else cum[k][0:1]
            q_in[z] = _bf(p_ref[rows[k], 0:GLA_QK] * (GLA_DK ** -0.5) * jnp.exp(cum[k]))
            k_out[z] = _bf(p_ref[rows[k], GLA_QK:2 * GLA_QK] * jnp.exp(glast - cum[k]))
            e_last[z] = jnp.exp(glast)
        inter = {(z, h): _dot_nt(q_in[z][:, hsl[h]], _bf(state[z, h])) for z, h in heads}
        inc = {(z, h): _dot_tn(v_of((step, z), h), k_out[z][:, hsl[h]]) for z, h in heads}
        for z in range(2):
            refs[z][2][rows[step, z], :] = _lanes([intra[(step, z), h] + inter[z, h] for h in range(GLA_HEADS)])
            for h in range(GLA_HEADS):
                state[z, h] = state[z, h] * e_last[z][:, hsl[h]] + inc[z, h]
    for z, h in heads:
        s_ref[z, h] = state[z, h]


def _gla_scan(st, proj, cum, s0_t):
    ncb, rb, nblk, fwd, bwd = _scan_blocks(st)
    seg = _block_diag(GLA_QK, GLA_DK, 1.0)[:, ::GLA_DK]
    seg = jnp.pad(seg, ((0, 0), (0, LANE - GLA_HEADS)))
    state_spec = pl.BlockSpec((None, 2, GLA_HEADS, GLA_DV, GLA_DK), lambda b, j: (b, 0, 0, 0, 0))
    qkv_w = 2 * GLA_QK + GLA_HEADS * GLA_DV
    return pl.pallas_call(
        functools.partial(_gla_scan_kernel, ncb=ncb),
        grid=(st.n, nblk),
        in_specs=[pl.BlockSpec((rb, qkv_w), fwd), pl.BlockSpec((rb, qkv_w), bwd),
                  pl.BlockSpec((rb, GLA_QK), fwd), pl.BlockSpec((rb, GLA_QK), lambda b, j: (bwd(b, j)[0], 1)),
                  pl.BlockSpec((GLA_QK, LANE), lambda b, j: (0, 0)), state_spec],
        out_specs=[pl.BlockSpec((rb, 512), fwd), pl.BlockSpec((rb, 512), bwd), state_spec],
        out_shape=[jax.ShapeDtypeStruct((st.rows, 512), F32), jax.ShapeDtypeStruct((st.rows, 512), F32),
                   jax.ShapeDtypeStruct((st.n, 2, GLA_HEADS, GLA_DV, GLA_DK), F32)],
        compiler_params=_cp("parallel", "arbitrary"),
        name="gla_scan",
    )(proj, proj, cum, cum, seg, s0_t)


def _gla_post_kernel(yf_ref, yb_ref, p_ref, g_ref, o_ref):
    o = yf_ref[...] + yb_ref[...]
    gate = _silu(p_ref[:, 1152:1664])
    outs = []
    for h in range(GLA_HEADS):
        oh = o[:, h * GLA_DV:(h + 1) * GLA_DV]
        outs.append(oh * lax.rsqrt(jnp.mean(oh * oh, axis=-1, keepdims=True) + 1e-6) * g_ref[...])
    o_ref[...] = _bf(_lanes(outs) * gate)


def _gla_post(st, y_f, y_b, proj, gla_norm):
    m = proj.shape[0]
    tr = _row_tile(st)
    return pl.pallas_call(
        _gla_post_kernel,
        grid=(m // tr,),
        in_specs=[pl.BlockSpec((tr, 512), lambda i: (i, 0)), pl.BlockSpec((tr, 512), lambda i: (i, 0)),
                  pl.BlockSpec((tr, GLA_W), lambda i: (i, 0)), pl.BlockSpec((1, GLA_DV), lambda i: (0, 0))],
        out_specs=pl.BlockSpec((tr, 512), lambda i: (i, 0)),
        out_shape=jax.ShapeDtypeStruct((m, 512), BF16),
        compiler_params=_cp("parallel"),
        name="gla_post",
    )(y_f, y_b, proj, gla_norm.reshape(1, GLA_DV))


def _rwkv_prep_kernel(cur_ref, prev_ref, next_ref, mu_ref, w2_ref, w0_ref, a2_ref, a0_ref, g2_ref,
                      kkw_ref, ka_ref, rk_ref, bd_ref, in_out, lw_out, post_out, *, tiles_per_seq):
    cur = cur_ref[...]
    x_prev, x_next = _neighbours(cur, prev_ref[...], next_ref[...], tiles_per_seq)
    blk = cur + (0.5 * (x_prev + x_next) - cur) * mu_ref[...]
    r, k, v = blk[:, 0:512], blk[:, 512:1024], blk[:, 1024:1536]
    w_logit = w0_ref[...] + _dot_split(jnp.tanh(blk[:, 1536:1600]), w2_ref[...])
    lw_out[...] = -RWKV_DECAY_SCALE * jax.nn.sigmoid(w_logit)
    a = jax.nn.sigmoid(a0_ref[...] + _dot_split(blk[:, 1600:1664], a2_ref[...]))
    g = _dot_split(jax.nn.sigmoid(blk[:, 1664:1792]), g2_ref[...])
    bd = bd_ref[...]
    kk = k * kkw_ref[...]
    kk = kk * lax.rsqrt(_dot_data_mask(kk * kk, bd, 2) + 1e-12)
    k2 = k * (1.0 + (a - 1.0) * ka_ref[...])
    in_out[:, 0:512] = r
    in_out[:, 512:1024] = k2
    in_out[:, 1024:1536] = v
    in_out[:, 1536:2048] = -kk
    in_out[:, 2048:2560] = kk * a
    post_out[:, 0:512] = g
    post_out[:, 512:1024] = _dot_data_mask(r * k2 * rk_ref[...], bd, 2) * v


def _rwkv_prep(st, proj, mu, w2, w0, a2, a0, g2, kkw, ka, rk):
    m = proj.shape[0]
    tr = _row_tile(st)
    full = lambda shape: pl.BlockSpec(shape, lambda i: (0, 0))
    row = lambda w: pl.BlockSpec((tr, w), lambda i: (i, 0))
    vec = lambda a: a.reshape(1, -1)
    return pl.pallas_call(
        functools.partial(_rwkv_prep_kernel, tiles_per_seq=st.t // tr),
        grid=(m // tr,),
        in_specs=_halo_specs(st, RWKV_W) + [full((1, RWKV_W)), full((64, 1024)), full((1, 1024)),
                                            full((64, 512)), full((1, 512)), full((128, 512)),
                                            full((1, 512)), full((1, 512)), full((1, 512)), full((512, 512))],
        out_specs=[row(2560), row(1024), row(1024)],
        out_shape=[jax.ShapeDtypeStruct((m, 2560), F32), jax.ShapeDtypeStruct((m, 1024), F32),
                   jax.ShapeDtypeStruct((m, 1024), F32)],
        compiler_params=_cp("parallel"),
        name="rwkv_prep",
    )(proj, proj, proj, vec(mu), jnp.concatenate([w2[0], w2[1]], axis=1), vec(w0), a2, vec(a0), g2,
      vec(kkw), vec(ka), vec(rk), _block_diag(512, RWKV_HD, 1.0))


def _rwkv_chunk(x_ref, lw_ref, rows, z):
    incl, strict = _dir_masks(z)
    lw = lw_ref[rows, :]
    cum = _dot_mask_data(incl.astype(BF16), lw, 3)
    cum_x = cum - lw
    mid = cum[CHUNK // 2:CHUNK // 2 + 1]
    glast = cum[CHUNK - 1:CHUNK] if z == 0 else cum[0:1]
    x = x_ref[rows, :]
    r, k, v, al, be = x[:, 0:512], x[:, 512:1024], x[:, 1024:1536], x[:, 1536:2048], x[:, 2048:2560]
    e_mid, e_nmid = jnp.exp(cum - mid), jnp.exp(mid - cum)
    e_out = jnp.exp(glast - cum)
    return dict(incl=incl, strict=strict, v=v,
                r_mid=_bf(r * e_mid), a_mid=_bf(al * jnp.exp(cum_x - mid)),
                b_mid=_bf(be * e_nmid), k_mid=_bf(k * e_nmid),
                r_in=_bf(r * jnp.exp(cum)), a_in=al * jnp.exp(cum_x),
                b_out=_bf(be * e_out), k_out=_bf(k * e_out), e_last=jnp.exp(glast))


_HEAD_SLICES = [slice(h * RWKV_HD, (h + 1) * RWKV_HD) for h in range(RWKV_HEADS)]


def _rwkv_state_free(chunks, hooks):
    hooks = list(hooks)

    def run_hook():
        if hooks:
            hooks.pop(0)()

    chains = [(c, sl) for c in chunks for sl in _HEAD_SLICES]
    run_hook()
    pair = [_dot_nt(_rows([c['a_mid'][:, sl], c['r_mid'][:, sl]]), _rows([c['b_mid'][:, sl], c['k_mid'][:, sl]]))
            for c, sl in chains]
    half = CHUNK
    a_ab = [_bf(jnp.where(c['strict'], p[:half, :half], 0.0)) for (c, _), p in zip(chains, pair)]
    a_ak = [_bf(jnp.where(c['strict'], p[:half, half:], 0.0)) for (c, _), p in zip(chains, pair)]
    a_rb = [_bf(jnp.where(c['incl'], p[half:, :half], 0.0)) for (c, _), p in zip(chains, pair)]
    a_rk = [_bf(jnp.where(c['incl'], p[half:, half:], 0.0)) for (c, _), p in zip(chains, pair)]
    vb = [_bf(c['v'][:, sl]) for c, sl in chains]
    both = [_dot(_rows([ak, rk]), v) for ak, rk, v in zip(a_ak, a_rk, vb)]
    av = [x[:half] for x in both]
    o0 = [x[half:] for x in both]
    sol = [_lanes([c['a_in'][:, sl], x]) for (c, sl), x in zip(chains, av)]
    powr = a_ab
    width = 2 * RWKV_HD
    for it in range(6):
        if it < 5:
            both = [_dot(p, _lanes([_bf(s), p])) for p, s in zip(powr, sol)]
            sol = [s + x[:, :width] for s, x in zip(sol, both)]
            powr = [_bf(x[:, width:]) for x in both]
        else:
            sol = [s + _dot(p, _bf(s)) for p, s in zip(powr, sol)]
        if it in (1, 3):
            run_hook()
    while hooks:
        run_hook()
    return [dict(w=_bf(s[:, :RWKV_HD]), u0=s[:, RWKV_HD:], o0=o, a_rb=a, v=c['v'][:, sl], r_in=c['r_in'][:, sl],
                 b_out=c['b_out'][:, sl], k_out=c['k_out'][:, sl], e_last=c['e_last'][:, sl])
            for (c, sl), s, o, a in zip(chains, sol, o0, a_rb)]


def _rwkv_state_stages(res, state, write_out):
    box = {}

    def read_state():
        box['su'] = [_dot_nt(_rows([c['w'], c['r_in']]), _bf(s)) for c, s in zip(res, state)]

    def update_state():
        box['u'] = [c['u0'] + su[:CHUNK] for c, su in zip(res, box['su'])]
        state[:] = [s * c['e_last'] + _dot_tn(_bf(_rows([u, c['v']])), _rows([c['b_out'], c['k_out']]))
                    for c, s, u in zip(res, state, box['u'])]

    def emit():
        write_out([su[CHUNK:] + _dot(c['a_rb'], _bf(u)) + c['o0'] for c, su, u in zip(res, box['su'], box['u'])])

    return [read_state, update_state, emit]


def _rwkv_scan_kernel(xf_ref, xb_ref, lwf_ref, lwb_ref, s0_ref, yf_ref, yb_ref, s_ref, *, ncb):
    @pl.when(pl.program_id(1) == 0)
    def _():
        s_ref[...] = s0_ref[...]

    state = [s_ref[z, h] for z in range(2) for h in range(RWKV_HEADS)]

    def chunk_rows(step, z):
        cc = step if z == 0 else ncb - 1 - step
        return slice(cc * CHUNK, (cc + 1) * CHUNK)

    def pair_inputs(step):
        return [_rwkv_chunk(xf_ref, lwf_ref, chunk_rows(step, 0), 0),
                _rwkv_chunk(xb_ref, lwb_ref, chunk_rows(step, 1), 1)]

    def writer(step):
        def write_out(outs):
            yf_ref[chunk_rows(step, 0), :] = _lanes(outs[:RWKV_HEADS])
            yb_ref[chunk_rows(step, 1), :] = _lanes(outs[RWKV_HEADS:])
        return write_out

    res = _rwkv_state_free(pair_inputs(0), [])
    for step in range(ncb):
        hooks = _rwkv_state_stages(res, state, writer(step))
        if step + 1 < ncb:
            res = _rwkv_state_free(pair_inputs(step + 1), hooks)
        else:
            for hook in hooks:
                hook()
    for z in range(2):
        for h in range(RWKV_HEADS):
            s_ref[z, h] = state[z * RWKV_HEADS + h]


def _rwkv_scan(st, xin, lw, s0):
    ncb, rb, nblk, fwd, bwd = _scan_blocks(st)
    state_spec = pl.BlockSpec((None, 2, RWKV_HEADS, RWKV_HD, RWKV_HD), lambda b, j: (b, 0, 0, 0, 0))
    return pl.pallas_call(
        functools.partial(_rwkv_scan_kernel, ncb=ncb),
        grid=(st.n, nblk),
        in_specs=[pl.BlockSpec((rb, 2560), fwd), pl.BlockSpec((rb, 2560), bwd),
                  pl.BlockSpec((rb, 512), fwd), pl.BlockSpec((rb, 512), lambda b, j: (bwd(b, j)[0], 1)),
                  state_spec],
        out_specs=[pl.BlockSpec((rb, 512), fwd), pl.BlockSpec((rb, 512), bwd), state_spec],
        out_shape=[jax.ShapeDtypeStruct((st.rows, 512), F32), jax.ShapeDtypeStruct((st.rows, 512), F32),
                   jax.ShapeDtypeStruct((st.n, 2, RWKV_HEADS, RWKV_HD, RWKV_HD), F32)],
        compiler_params=_cp("parallel", "arbitrary"),
        name="rwkv_scan",
    )(xin, xin, lw, lw, s0)


def _rwkv_post_kernel(yf_ref, yb_ref, post_ref, lng_ref, lnb_ref, bd_ref, o_ref):
    o = yf_ref[...] + yb_ref[...]
    bd = bd_ref[...]
    d = o - _dot_data_mask(o, bd, 2)
    o = d * lax.rsqrt(_dot_data_mask(d * d, bd, 2) + RWKV_LN_EPS) * lng_ref[...] + lnb_ref[...]
    o_ref[...] = _bf((o + post_ref[:, 512:1024]) * post_ref[:, 0:512])


def _rwkv_post(st, y_f, y_b, post, ln_g, ln_b):
    m = post.shape[0]
    tr = _row_tile(st)
    full = lambda shape: pl.BlockSpec(shape, lambda i: (0, 0))
    row = lambda w: pl.BlockSpec((tr, w), lambda i: (i, 0))
    return pl.pallas_call(
        _rwkv_post_kernel,
        grid=(m // tr,),
        in_specs=[row(512), row(512), row(1024), full((1, 512)), full((1, 512)), full((512, 512))],
        out_specs=row(512),
        out_shape=jax.ShapeDtypeStruct((m, 512), BF16),
        compiler_params=_cp("parallel"),
        name="rwkv_post",
    )(y_f, y_b, post, ln_g.reshape(1, 512), ln_b.reshape(1, 512), _block_diag(512, RWKV_HD, 1.0 / RWKV_HD))


def _merge_kernel(h_ref, o0_ref, o1_ref, o2_ref, o3_ref, wg_ref, wb_ref, out_ref):
    h = h_ref[...]
    acc = None
    for i, o_ref in enumerate((o0_ref, o1_ref, o2_ref, o3_ref)):
        term = jax.nn.sigmoid(_dot(h, wg_ref[i])) * _dot(o_ref[...], wb_ref[i])
        acc = term if acc is None else acc + term
    out_ref[...] = _bf(acc)


def _merge(st, h, branch_outs, wg, wb):
    m, d = h.shape
    tm = _mm_tile(st)
    tn = 512
    row = lambda w: pl.BlockSpec((tm, w), lambda j, i: (i, 0))
    return pl.pallas_call(
        _merge_kernel,
        grid=(d // tn, m // tm),
        in_specs=[row(d)] + [row(BRANCH_W)] * 4 + [pl.BlockSpec((4, d, tn), lambda j, i: (0, 0, j)),
                                                    pl.BlockSpec((4, BRANCH_W, tn), lambda j, i: (0, 0, j))],
        out_specs=pl.BlockSpec((tm, tn), lambda j, i: (i, j)),
        out_shape=jax.ShapeDtypeStruct((m, d), BF16),
        compiler_params=_cp("parallel", "parallel"),
        name="merge",
    )(h, *branch_outs, wg, wb)


def _wo_kernel(m_ref, w_ref, x_ref, mod_ref, g_ref, o_ref, h_out):
    x = x_ref[...] + mod_ref[0, 2:3, :] * _dot(m_ref[...], w_ref[...])
    o_ref[...] = x
    h_out[...] = _norm_mod(x, g_ref[...], mod_ref, 4, 3)


def _out_proj(st, merged, w_o, x, mod_l, gain2):
    m, d = x.shape
    tm = _mm_tile(st)
    row = pl.BlockSpec((tm, d), lambda i: (i, 0))
    return pl.pallas_call(
        _wo_kernel,
        grid=(m // tm,),
        in_specs=[row, pl.BlockSpec((d, d), lambda i: (0, 0)), row, pl.BlockSpec((1, 6, d), st.group_map(tm)),
                  pl.BlockSpec((1, d), lambda i: (0, 0))],
        out_specs=[row, row],
        out_shape=[jax.ShapeDtypeStruct((m, d), F32), jax.ShapeDtypeStruct((m, d), BF16)],
        compiler_params=_cp("parallel"),
        name="out_proj",
    )(merged, w_o, x, mod_l, gain2.reshape(1, d))


def _ffn_kernel(h_ref, w1_ref, w3_ref, w2_ref, x_ref, mod_ref, fg_ref, o_ref, acc_ref, *, final):
    f = pl.program_id(1)

    @pl.when(f == 0)
    def _():
        acc_ref[...] = jnp.zeros_like(acc_ref)

    h = h_ref[...]
    u = _silu(_dot(h, w1_ref[...])) * _dot(h, w3_ref[...])
    acc_ref[...] += _dot(_bf(u), w2_ref[...])

    @pl.when(f == pl.num_programs(1) - 1)
    def _():
        x = x_ref[...] + mod_ref[0, 5:6, :] * acc_ref[...]
        o_ref[...] = _rms(x, fg_ref[...]) if final else x


def _ffn(st, h, w1, w3, w2, x, mod_l, final_gain, final):
    m, d = x.shape
    dff = w1.shape[1]
    tm = _mm_tile(st)
    tf = 512
    return pl.pallas_call(
        functools.partial(_ffn_kernel, final=final),
        grid=(m // tm, dff // tf),
        in_specs=[pl.BlockSpec((tm, d), lambda i, f: (i, 0)),
                  pl.BlockSpec((d, tf), lambda i, f: (0, f)),
                  pl.BlockSpec((d, tf), lambda i, f: (0, f)),
                  pl.BlockSpec((tf, d), lambda i, f: (f, 0)),
                  pl.BlockSpec((tm, d), lambda i, f: (i, 0)),
                  pl.BlockSpec((1, 6, d), lambda i, f: st.group_map(tm)(i)),
                  pl.BlockSpec((1, d), lambda i, f: (0, 0))],
        out_specs=pl.BlockSpec((tm, d), lambda i, f: (i, 0)),
        out_shape=jax.ShapeDtypeStruct((m, d), F32),
        scratch_shapes=[pltpu.VMEM((tm, d), F32)],
        compiler_params=_cp("parallel", "arbitrary"),
        name="ffn",
    )(h, w1, w3, w2, x, mod_l, final_gain.reshape(1, d))


def _split_w_in(w_in):
    z = lambda w: jnp.zeros(w_in.shape[:2] + (w,), w_in.dtype)
    att = w_in[..., 0:768]
    ssd = jnp.concatenate([w_in[..., 768:2064], z(SSD_W - 1296)], axis=-1)
    rwkv = w_in[..., 2064:3856]
    gla = jnp.concatenate([w_in[..., 3856:4880], w_in[..., 4880:4896], z(LANE - 16), w_in[..., 4896:5408]], axis=-1)
    return tuple(_bf(w) for w in (att, ssd, rwkv, gla))


def _block(st, x, mod_l, p, rope_tables, ctx, final_gain, final):
    h, proj_att, proj_ssd = _in_proj_a(st, x, p['norm1'], mod_l, p['w_att'], p['w_ssd'])
    proj_rwkv, proj_gla = _in_proj_b(st, h, p['w_rwkv'], p['w_gla'])

    v = proj_att[:, 640:768]
    if ctx is None:
        q, k_att, v_att, k = _att_prep(st, proj_att, p['q_norm'], p['k_norm'], None)
        o_att = _attention(st, q, k_att, v_att)
        s_ssd = jnp.zeros((st.n, 2, SSD_HEADS, SSD_N, SSD_P), F32)
        s_rwkv = jnp.zeros((st.n, 2, RWKV_HEADS, RWKV_HD, RWKV_HD), F32)
        s_gla_t = jnp.zeros((st.n, 2, GLA_HEADS, GLA_DV, GLA_DK), F32)
    else:
        ctx_k, ctx_v, s_ssd, s_rwkv, s_gla = ctx
        q, k_att, v_att = _att_prep(st, proj_att, p['q_norm'], p['k_norm'], rope_tables)
        k = None
        o_att = _attention(st, q, _join_cache(st, k_att, ctx_k), _join_cache(st, v_att, ctx_v))
        s_gla_t = jnp.swapaxes(s_gla, -1, -2)

    xbc, dt, cum, dtt, cumt = _ssd_prep(st, proj_ssd, p['ssd_conv_w'], p['ssd_conv_b'], p['ssd_dt_bias'],
                                        p['ssd_a_log'])
    y_ssd_f, y_ssd_b, new_ssd = _ssd_scan(st, xbc, dt, cum, dtt, cumt, s_ssd)
    o_ssd = _ssd_post(st, y_ssd_f, y_ssd_b, xbc, proj_ssd, p['ssd_d'], p['ssd_norm'])

    rin, lw, rpost = _rwkv_prep(st, proj_rwkv, p['rwkv_mu'], p['rwkv_w2'], p['rwkv_w0'], p['rwkv_a2'],
                                p['rwkv_a0'], p['rwkv_g2'], p['rwkv_kk'], p['rwkv_ka'], p['rwkv_rk'])
    y_rwkv_f, y_rwkv_b, new_rwkv = _rwkv_scan(st, rin, lw, s_rwkv)
    o_rwkv = _rwkv_post(st, y_rwkv_f, y_rwkv_b, rpost, p['rwkv_ln_g'], p['rwkv_ln_b'])

    g2 = jnp.concatenate([p['gla_g2'][0], p['gla_g2'][1]], axis=1)
    g2p = jnp.pad(g2, ((0, LANE - g2.shape[0]), (0, 0)))
    gla_cum = _gla_prep(st, proj_gla, g2p, p['gla_gb'].reshape(1, 2 * GLA_QK))
    y_gla_f, y_gla_b, new_gla_t = _gla_scan(st, proj_gla, gla_cum, s_gla_t)
    o_gla = _gla_post(st, y_gla_f, y_gla_b, proj_gla, p['gla_norm'])

    merged = _merge(st, h, (o_att, o_ssd, o_rwkv, o_gla), p['w_gate'], p['w_branch'])
    x, h2 = _out_proj(st, merged, p['w_o'], x, mod_l, p['norm2'])
    x = _ffn(st, h2, p['ffn_w1'], p['ffn_w3'], p['ffn_w2'], x, mod_l, final_gain, final)
    return x, (k, v, new_ssd, new_rwkv, jnp.swapaxes(new_gla_t, -1, -2))


def kernel(x_prompt, x_sample, cache_attn_k, cache_attn_v, state_ssd, state_rwkv, state_gla, c, c_ctx, w_mod, b_mod, norm1, norm2, w_in, q_norm, k_norm, ssd_conv_w, ssd_conv_b, ssd_dt_bias, ssd_a_log, ssd_d, ssd_norm, rwkv_mu, rwkv_w0, rwkv_w2, rwkv_a0, rwkv_a2, rwkv_g2, rwkv_kk, rwkv_ka, rwkv_rk, rwkv_ln_g, rwkv_ln_b, gla_g2, gla_gb, gla_norm, w_gate, w_branch, w_o, ffn_w1, ffn_w3, ffn_w2, final_norm):
    nb, seq, d = x_prompt.shape
    db, dseq, _ = x_sample.shape
    depth = w_in.shape[0]
    assert d == D_MODEL and seq % CHUNK == 0 and dseq % CHUNK == 0 and 1 + db <= MOD_ROWS
    ctx_st = _Stream(nb, seq, 0, False)
    lat_st = _Stream(db, dseq, 1, True)

    cond = jnp.concatenate([c_ctx[None], c, jnp.zeros((MOD_ROWS - 1 - db, d), F32)], axis=0)
    mod = _modulation(cond, w_mod, b_mod)

    w_att, w_ssd, w_rwkv, w_gla = _split_w_in(w_in)
    w_gate_b, w_branch_b, w_o_b = _bf(w_gate), _bf(w_branch), _bf(w_o)
    w1_b, w3_b, w2_b = _bf(ffn_w1), _bf(ffn_w3), _bf(ffn_w2)

    def params_at(l):
        return dict(norm1=norm1[l], norm2=norm2[l], w_att=w_att[l], w_ssd=w_ssd[l], w_rwkv=w_rwkv[l],
                    w_gla=w_gla[l], q_norm=q_norm[l], k_norm=k_norm[l],
                    ssd_conv_w=ssd_conv_w[l], ssd_conv_b=ssd_conv_b[l], ssd_dt_bias=ssd_dt_bias[l],
                    ssd_a_log=ssd_a_log[l], ssd_d=ssd_d[l], ssd_norm=ssd_norm[l],
                    rwkv_mu=rwkv_mu[l], rwkv_w0=rwkv_w0[l], rwkv_w2=rwkv_w2[l], rwkv_a0=rwkv_a0[l],
                    rwkv_a2=rwkv_a2[l], rwkv_g2=rwkv_g2[l], rwkv_kk=rwkv_kk[l], rwkv_ka=rwkv_ka[l],
                    rwkv_rk=rwkv_rk[l], rwkv_ln_g=rwkv_ln_g[l], rwkv_ln_b=rwkv_ln_b[l],
                    gla_g2=gla_g2[l], gla_gb=gla_gb[l], gla_norm=gla_norm[l],
                    w_gate=w_gate_b[l], w_branch=w_branch_b[l], w_o=w_o_b[l],
                    ffn_w1=w1_b[l], ffn_w3=w3_b[l], ffn_w2=w2_b[l])

    xp = x_prompt.reshape(nb * seq, d)
    new_k, new_v, new_ssd, new_rwkv, new_gla = [], [], [], [], []
    for l in range(depth):
        xp, (k_l, v_l, ssd_l, rwkv_l, gla_l) = _block(ctx_st, xp, mod[l], params_at(l), None, None, final_norm,
                                                      l == depth - 1)
        new_k.append(k_l.reshape(nb, seq, ATT_KV, HEAD_DIM))
        new_v.append(v_l.reshape(nb, seq, ATT_KV, HEAD_DIM))
        new_ssd.append(ssd_l)
        new_rwkv.append(rwkv_l)
        new_gla.append(gla_l)

    rope_tables = _rope_tables(dseq)
    xs = x_sample.reshape(db * dseq, d)
    for l in range(depth):
        ctx = (cache_attn_k[:, l], cache_attn_v[:, l], state_ssd[:, l], state_rwkv[:, l], state_gla[:, l])
        xs, _ = _block(lat_st, xs, mod[l], params_at(l), rope_tables, ctx, final_norm, l == depth - 1)

    y_prompt = xp.reshape(nb, seq, d)
    y_sample = xs.reshape(db, dseq, d)
    return (y_prompt, y_sample, jnp.stack(new_k, axis=1), jnp.stack(new_v, axis=1),
            jnp.stack(new_ssd, axis=1), jnp.stack(new_rwkv, axis=1), jnp.stack(new_gla, axis=1))
```

```python
import functools

import jax
import jax.numpy as jnp
import numpy as np
from jax import lax
from jax.experimental import pallas as pl
from jax.experimental.pallas import tpu as pltpu

F32 = jnp.float32
BF16 = jnp.bfloat16

D_MODEL = 2048
GRID_W = 64
ATT_HEADS = 8
ATT_KV = 2
ATT_GROUP = ATT_HEADS // ATT_KV
HEAD_DIM = 64
ROPE_THETA = 10000.0
SSD_HEADS = 8
SSD_P = 64
SSD_N = 64
SSD_GROUPS = 2
RWKV_HEADS = 8
RWKV_HD = 64
RWKV_DECAY_SCALE = 0.6065306597126334
RWKV_LN_EPS = 64e-5
GLA_HEADS = 4
GLA_DK = 64
GLA_DV = 128
GLA_GATE_NORM = 16.0
CHUNK = 64
BRANCH_W = 512

ATT_W = 768
SSD_W = 1408
RWKV_W = 1792
GLA_W = 1664
LANE = 128
SUBLANE = 8
MOD_ROWS = 16
VMEM_LIMIT = 56 * 1024 * 1024
ROW_TILE = 256
MM_TILE = 512
SCAN_CHUNKS = 4
KEY_TILE = 256
KEY_UNROLL = 6
LOG2E = 1.4426950408889634
ATT_MIN_ROW_SUM = 2.0 ** -90


def _cp(*sem):
    return pltpu.CompilerParams(dimension_semantics=sem, vmem_limit_bytes=VMEM_LIMIT)


def _bf(x):
    return x.astype(BF16)


def _dot(a, b, prec=None):
    return jnp.dot(a, b, preferred_element_type=F32, precision=prec)


def _dot_nt(a, b, prec=None):
    return lax.dot_general(a, b, (((1,), (1,)), ((), ())), preferred_element_type=F32, precision=prec)


def _dot_tn(a, b, prec=None):
    return lax.dot_general(a, b, (((0,), (0,)), ((), ())), preferred_element_type=F32, precision=prec)


def _pieces(a, n):
    out = []
    for _ in range(n):
        piece = _bf(a)
        out.append(piece)
        a = a - piece.astype(F32)
    return out


def _dot_data_mask(a, mask, n):
    return functools.reduce(jnp.add, [_dot(piece, mask) for piece in _pieces(a, n)])


def _dot_mask_data(mask, a, n):
    return functools.reduce(jnp.add, [_dot(mask, piece) for piece in _pieces(a, n)])


def _dot_nt_mask_data(mask, a, n):
    return functools.reduce(jnp.add, [_dot_nt(mask, piece) for piece in _pieces(a, n)])


def _dot_split(a, b):
    a_hi, a_lo = _pieces(a, 2)
    b_hi, b_lo = _pieces(b, 2)
    return _dot(a_hi, b_hi) + (_dot(a_hi, b_lo) + _dot(a_lo, b_hi))


def _silu(x):
    return x * jax.nn.sigmoid(x)


def _softplus(x):
    return jnp.maximum(x, 0.0) + jnp.log1p(jnp.exp(-jnp.abs(x)))


def _lanes(pieces):
    return jnp.concatenate(pieces, axis=1)


def _rows(pieces):
    return jnp.concatenate(pieces, axis=0)


def _mod_kernel(c_ref, w_ref, b_ref, o_ref):
    c = c_ref[...]
    o_ref[0] = _dot(_bf(_silu(c)), _bf(w_ref[0])) + b_ref[0]


def _modulation(cond, w_mod, b_mod):
    nl, d, n6 = w_mod.shape
    tn = 1024
    out = pl.pallas_call(
        _mod_kernel,
        grid=(nl, n6 // tn),
        in_specs=[pl.BlockSpec((MOD_ROWS, d), lambda l, j: (0, 0)),
                  pl.BlockSpec((1, d, tn), lambda l, j: (l, 0, j)),
                  pl.BlockSpec((1, 1, tn), lambda l, j: (l, 0, j))],
        out_specs=pl.BlockSpec((1, MOD_ROWS, tn), lambda l, j: (l, 0, j)),
        out_shape=jax.ShapeDtypeStruct((nl, MOD_ROWS, n6), F32),
        compiler_params=_cp("parallel", "parallel"),
        name="modulation",
    )(cond, w_mod, b_mod.reshape(nl, 1, n6))
    return out.reshape(nl, MOD_ROWS, 6, d)


class _Stream:
    def __init__(self, n, t, group0, per_seq):
        self.n, self.t, self.group0, self.per_seq = n, t, group0, per_seq
        self.rows = n * t

    def group_map(self, tile):
        g0, per_seq, t = self.group0, self.per_seq, self.t
        if per_seq:
            return lambda i: (g0 + (i * tile) // t, 0, 0)
        return lambda i: (g0, 0, 0)


def _mm_tile(st):
    return min(MM_TILE, st.t)


def _row_tile(st):
    return min(ROW_TILE, st.t)


def _rms(x, gain):
    return x * lax.rsqrt(jnp.mean(x * x, axis=-1, keepdims=True) + 1e-6) * gain


def _norm_mod(x, gain, mod_ref, sc_idx, sh_idx):
    return _bf(_rms(x, gain) * (1.0 + mod_ref[0, sc_idx:sc_idx + 1, :]) + mod_ref[0, sh_idx:sh_idx + 1, :])


def _in_proj_a_kernel(x_ref, g_ref, mod_ref, wa_ref, ws_ref, h_out, pa_out, ps_out):
    h = _norm_mod(x_ref[...], g_ref[...], mod_ref, 1, 0)
    h_out[...] = h
    pa_out[...] = _dot(h, wa_ref[...])
    ps_out[...] = _dot(h, ws_ref[...])


def _in_proj_a(st, x, gain, mod_l, w_att, w_ssd):
    m, d = x.shape
    tm = _mm_tile(st)
    row = lambda w: pl.BlockSpec((tm, w), lambda i: (i, 0))
    full = lambda a: pl.BlockSpec(a.shape, lambda i: (0, 0))
    return pl.pallas_call(
        _in_proj_a_kernel,
        grid=(m // tm,),
        in_specs=[row(d), pl.BlockSpec((1, d), lambda i: (0, 0)), pl.BlockSpec((1, 6, d), st.group_map(tm)),
                  full(w_att), full(w_ssd)],
        out_specs=[row(d), row(ATT_W), row(SSD_W)],
        out_shape=[jax.ShapeDtypeStruct((m, d), BF16), jax.ShapeDtypeStruct((m, ATT_W), F32),
                   jax.ShapeDtypeStruct((m, SSD_W), F32)],
        compiler_params=_cp("parallel"),
        name="in_proj_a",
    )(x, gain.reshape(1, d), mod_l, w_att, w_ssd)


def _in_proj_b_kernel(h_ref, wr_ref, wg_ref, pr_out, pg_out):
    h = h_ref[...]
    pr_out[...] = _dot(h, wr_ref[...])
    pg_out[...] = _dot(h, wg_ref[...])


def _in_proj_b(st, h, w_rwkv, w_gla):
    m, d = h.shape
    tm = _mm_tile(st)
    row = lambda w: pl.BlockSpec((tm, w), lambda i: (i, 0))
    full = lambda a: pl.BlockSpec(a.shape, lambda i: (0, 0))
    return pl.pallas_call(
        _in_proj_b_kernel,
        grid=(m // tm,),
        in_specs=[row(d), full(w_rwkv), full(w_gla)],
        out_specs=[row(RWKV_W), row(GLA_W)],
        out_shape=[jax.ShapeDtypeStruct((m, RWKV_W), F32), jax.ShapeDtypeStruct((m, GLA_W), F32)],
        compiler_params=_cp("parallel"),
        name="in_proj_b",
    )(h, w_rwkv, w_gla)


def _att_prep_kernel(*refs, rope):
    if rope:
        p_ref, qn_ref, kn_ref, bdq_ref, bdk_ref, cos_ref, sa_ref, sb_ref, q_out, k_out, v_out = refs
    else:
        p_ref, qn_ref, kn_ref, bdq_ref, bdk_ref, q_out, k_out, v_out, k_leaf = refs
    p = p_ref[...]
    aq = p[:, :512]
    ak = p[:, 512:640]
    av = p[:, 640:768]
    q = aq * lax.rsqrt(_dot_data_mask(aq * aq, bdq_ref[...], 2) + 1e-6) * qn_ref[...]
    k = ak * lax.rsqrt(_dot_data_mask(ak * ak, bdk_ref[...], 2) + 1e-6) * kn_ref[...]
    if rope:
        c, sa, sb = cos_ref[...], sa_ref[...], sb_ref[...]
        k = k * c + pltpu.roll(k, LANE - 16, 1) * sa + pltpu.roll(k, 16, 1) * sb
        c4, sa4, sb4 = _lanes([c] * 4), _lanes([sa] * 4), _lanes([sb] * 4)
        q = q * c4 + pltpu.roll(q, 512 - 16, 1) * sa4 + pltpu.roll(q, 16, 1) * sb4
    else:
        k_leaf[...] = k
    tr = p.shape[0]
    one_col = (lax.broadcasted_iota(jnp.int32, (tr, LANE - HEAD_DIM), 1) == 0).astype(F32)
    q = q * (HEAD_DIM ** -0.5 * LOG2E)
    q_len = jnp.sqrt(_dot_data_mask(q * q, bdq_ref[...], 2) * HEAD_DIM)
    q_out[...] = _bf(_lanes([piece for h in range(ATT_HEADS)
                             for piece in (q[:, h * HEAD_DIM:(h + 1) * HEAD_DIM],
                                           one_col * q_len[:, h * HEAD_DIM:(h + 1) * HEAD_DIM])]))
    for kv in range(ATT_KV):
        sl = slice(kv * HEAD_DIM, (kv + 1) * HEAD_DIM)
        k_out[kv] = _bf(_lanes([k[:, sl], one_col]))
        v_out[kv] = _bf(_lanes([av[:, sl], one_col]))


def _block_diag(width, block, value):
    idx = np.arange(width) // block
    return _bf(jnp.asarray((idx[:, None] == idx[None, :]).astype(np.float32) * value))


def _att_prep(st, proj, q_norm, k_norm, rope_tables):
    m = proj.shape[0]
    tr = _row_tile(st)
    rope = rope_tables is not None
    full = lambda shape: pl.BlockSpec(shape, lambda i: (0, 0))
    in_specs = [pl.BlockSpec((tr, ATT_W), lambda i: (i, 0)), full((1, 512)), full((1, 128)),
                full((512, 512)), full((128, 128))]
    args = [proj, jnp.tile(q_norm, ATT_HEADS).reshape(1, 512), jnp.tile(k_norm, ATT_KV).reshape(1, 128),
            _block_diag(512, HEAD_DIM, 1.0 / HEAD_DIM), _block_diag(128, HEAD_DIM, 1.0 / HEAD_DIM)]
    if rope:
        tps = st.t // tr
        in_specs += [pl.BlockSpec((tr, LANE), lambda i: (i % tps, 0))] * 3
        args += list(rope_tables)
    kv_spec = pl.BlockSpec((ATT_KV, tr, LANE), lambda i: (0, i, 0))
    kv_shape = jax.ShapeDtypeStruct((ATT_KV, m, LANE), BF16)
    out_specs = [pl.BlockSpec((tr, ATT_HEADS * LANE), lambda i: (i, 0)), kv_spec, kv_spec]
    out_shape = [jax.ShapeDtypeStruct((m, ATT_HEADS * LANE), BF16), kv_shape, kv_shape]
    if not rope:
        out_specs.append(pl.BlockSpec((tr, 128), lambda i: (i, 0)))
        out_shape.append(jax.ShapeDtypeStruct((m, 128), F32))
    return pl.pallas_call(
        functools.partial(_att_prep_kernel, rope=rope),
        grid=(m // tr,),
        in_specs=in_specs,
        out_specs=out_specs,
        out_shape=out_shape,
        compiler_params=_cp("parallel"),
        name="att_prep",
    )(*args)


def _join_cache(st, own, cache):
    n, p, kv, hd = cache.shape
    c = jnp.transpose(cache, (2, 0, 1, 3))
    pad = jnp.zeros((kv, n, p, LANE - hd), c.dtype).at[..., 0].set(1.0)
    c = _bf(jnp.concatenate([c, pad], axis=-1))
    joined = jnp.concatenate([own.reshape(kv, n, st.t, LANE), c], axis=2)
    return joined.reshape(kv, n * (st.t + p), LANE)


def _rope_tables(t):
    half = HEAD_DIM // 2
    nf = half // 2
    freqs = ROPE_THETA ** (-jnp.arange(nf, dtype=F32) / nf)
    tt = jnp.arange(t)
    ang_r = (tt // GRID_W).astype(F32)[:, None] * freqs[None, :]
    ang_c = (tt % GRID_W).astype(F32)[:, None] * freqs[None, :]
    zero = jnp.zeros_like(ang_r)
    cos = jnp.concatenate([jnp.cos(ang_r)] * 2 + [jnp.cos(ang_c)] * 2, axis=1)
    sa = jnp.concatenate([-jnp.sin(ang_r), zero, -jnp.sin(ang_c), zero], axis=1)
    sb = jnp.concatenate([zero, jnp.sin(ang_r), zero, jnp.sin(ang_c)], axis=1)
    return tuple(jnp.tile(a, (1, LANE // HEAD_DIM)) for a in (cos, sa, sb))


def _key_bound_kernel(k_ref, o_ref):
    ones = jnp.ones((LANE, LANE), BF16)
    for kv in range(ATT_KV):
        k = k_ref[kv].astype(F32)
        k = jnp.where(lax.broadcasted_iota(jnp.int32, k.shape, 1) < HEAD_DIM, k, 0.0)
        best = jnp.max(_dot_data_mask(k * k, ones, 2), axis=0, keepdims=True)
        o_ref[kv] = jnp.broadcast_to(best, (SUBLANE, LANE))


def _key_bound(n, keys_per_seq, k):
    return pl.pallas_call(
        _key_bound_kernel,
        grid=(n,),
        in_specs=[pl.BlockSpec((ATT_KV, keys_per_seq, LANE), lambda b: (0, b, 0))],
        out_specs=pl.BlockSpec((None, ATT_KV, SUBLANE, LANE), lambda b: (b, 0, 0, 0)),
        out_shape=jax.ShapeDtypeStruct((n, ATT_KV, SUBLANE, LANE), F32),
        compiler_params=_cp("parallel"),
        name="key_bound",
    )(k)


def _attn_kernel(q_ref, k_ref, v_ref, kb_ref, o_ref, m_scr, acc_scr, *, tq, n_tiles):
    rows = ATT_GROUP * tq
    lane = lax.broadcasted_iota(jnp.int32, (rows, LANE), 1)
    outs = []
    for kv in range(ATT_KV):
        qs = _rows([q_ref[:, (kv * ATT_GROUP + g) * LANE:(kv * ATT_GROUP + g + 1) * LANE]
                    for g in range(ATT_GROUP)]).astype(F32)

        def key_tile(ref, j):
            return ref[kv, pl.ds(pl.multiple_of(j * KEY_TILE, KEY_TILE), KEY_TILE), :]

        def weighted_sum(q_shift, unroll):
            acc_scr[...] = jnp.zeros((rows, LANE), F32)

            def sum_body(j, carry):
                p = jnp.exp2(_dot_nt(q_shift, key_tile(k_ref, j)))
                acc_scr[...] += _dot(_bf(p), key_tile(v_ref, j))
                return carry

            lax.fori_loop(0, n_tiles, sum_body, 0, unroll=unroll)
            return acc_scr[...]

        k_len = jnp.sqrt(kb_ref[kv][0:1, :])
        unroll = max(u for u in range(1, KEY_UNROLL + 1) if n_tiles % u == 0)
        acc = weighted_sum(_bf(qs * jnp.where(lane == HEAD_DIM, -k_len, 1.0)), unroll)

        def exact_shift():
            m_scr[...] = jnp.full((rows, LANE), -jnp.inf, F32)
            q0 = _bf(jnp.where(lane == HEAD_DIM, 0.0, qs))

            def max_body(j, carry):
                sc = _dot_nt(q0, key_tile(k_ref, j))
                m_scr[...] = jnp.maximum(m_scr[...], jnp.maximum(sc[:, :LANE], sc[:, LANE:]))
                return carry

            lax.fori_loop(0, n_tiles, max_body, 0)
            row_max = jnp.max(m_scr[...], axis=-1, keepdims=True)
            return weighted_sum(_bf(jnp.where(lane == HEAD_DIM, -row_max, qs)), 1)

        row_sum_ok = jnp.min(acc[:, HEAD_DIM:HEAD_DIM + 1]) >= ATT_MIN_ROW_SUM
        acc = lax.cond(row_sum_ok, lambda: acc, exact_shift)
        o = acc[:, :HEAD_DIM] / acc[:, HEAD_DIM:HEAD_DIM + 1]
        outs += [o[g * tq:(g + 1) * tq] for g in range(ATT_GROUP)]
    o_ref[...] = _bf(_lanes(outs))


def _attention(st, q, k, v):
    tq = min(128, st.t)
    nq = st.t // tq
    keys = k.shape[1] // st.n
    assert keys % KEY_TILE == 0
    kv_spec = pl.BlockSpec((ATT_KV, keys, LANE), lambda b, i: (0, b, 0))
    rows = ATT_GROUP * tq
    return pl.pallas_call(
        functools.partial(_attn_kernel, tq=tq, n_tiles=keys // KEY_TILE),
        grid=(st.n, nq),
        in_specs=[pl.BlockSpec((tq, ATT_HEADS * LANE), lambda b, i: (b * nq + i, 0)), kv_spec, kv_spec,
                  pl.BlockSpec((None, ATT_KV, SUBLANE, LANE), lambda b, i: (b, 0, 0, 0))],
        out_specs=pl.BlockSpec((tq, 512), lambda b, i: (b * nq + i, 0)),
        out_shape=jax.ShapeDtypeStruct((st.rows, 512), BF16),
        scratch_shapes=[pltpu.VMEM((rows, LANE), F32), pltpu.VMEM((rows, LANE), F32)],
        compiler_params=_cp("parallel", "parallel"),
        name="attention",
    )(q, k, v, _key_bound(st.n, keys, k))


def _halo_specs(st, width):
    tr = _row_tile(st)
    per8 = tr // SUBLANE
    last8 = st.rows // SUBLANE - 1
    return [pl.BlockSpec((tr, width), lambda i: (i, 0)),
            pl.BlockSpec((SUBLANE, width), lambda i: (jnp.maximum(i * per8 - 1, 0), 0)),
            pl.BlockSpec((SUBLANE, width), lambda i: (jnp.minimum((i + 1) * per8, last8), 0))]


def _neighbours(cur, prev8, next8, tiles_per_seq):
    tr = cur.shape[0]
    j = pl.program_id(0) % tiles_per_seq
    pr = jnp.where(j != 0, prev8[SUBLANE - 1:SUBLANE], 0.0)
    nx = jnp.where(j != tiles_per_seq - 1, next8[0:1], 0.0)
    row = lax.broadcasted_iota(jnp.int32, cur.shape, 0)
    x_prev = jnp.where(row == 0, pr, pltpu.roll(cur, 1, 0))
    x_next = jnp.where(row == tr - 1, nx, pltpu.roll(cur, tr - 1, 0))
    return x_prev, x_next


def _dir_masks(z):
    ri = lax.broadcasted_iota(jnp.int32, (CHUNK, CHUNK), 0)
    ci = lax.broadcasted_iota(jnp.int32, (CHUNK, CHUNK), 1)
    return ((ci <= ri), (ci < ri)) if z == 0 else ((ci >= ri), (ci > ri))


def _tri_blocks(tr):
    i = np.arange(tr)
    same = (i[:, None] // CHUNK) == (i[None, :] // CHUNK)
    lower = same & (i[None, :] <= i[:, None])
    upper = same & (i[None, :] >= i[:, None])
    return _bf(jnp.asarray(lower.astype(np.float32))), _bf(jnp.asarray(upper.astype(np.float32)))


def _scan_blocks(st):
    ncb = min(SCAN_CHUNKS, st.t // CHUNK)
    rb = ncb * CHUNK
    nblk = st.t // rb
    fwd = lambda b, j: (b * nblk + j, 0)
    bwd = lambda b, j: (b * nblk + nblk - 1 - j, 0)
    return ncb, rb, nblk, fwd, bwd


def _chunk_rows(step, z, ncb):
    cc = step if z == 0 else ncb - 1 - step
    return slice(cc * CHUNK, (cc + 1) * CHUNK)


def _ssd_prep_kernel(cur_ref, prev_ref, next_ref, cw_ref, cb_ref, dtb_ref, alog_ref, lo_ref, up_ref,
                     xbc_out, dt_out, cum_out, dtt_out, cumt_out, *, tiles_per_seq):
    cur = cur_ref[...]
    xc = cur[:, 512:1280]
    x_prev, x_next = _neighbours(xc, prev_ref[:, 512:1280], next_ref[:, 512:1280], tiles_per_seq)
    cw = cw_ref[...]
    conv = cb_ref[...] + x_prev * cw[0:1] + xc * cw[1:2] + x_next * cw[2:3]
    xbc_out[...] = _silu(conv)
    dt = _softplus(cur[:, 1280:1408] + dtb_ref[...])
    dt_out[...] = dt
    ld = dt * -jnp.exp(alog_ref[...])
    lane = lax.broadcasted_iota(jnp.int32, ld.shape, 1)
    cum = jnp.where(lane < SSD_HEADS, _dot_mask_data(lo_ref[...], ld, 3), _dot_mask_data(up_ref[...], ld, 3))
    cum_out[...] = cum
    pick = (lax.broadcasted_iota(jnp.int32, (2 * SSD_HEADS, LANE), 0)
            == lax.broadcasted_iota(jnp.int32, (2 * SSD_HEADS, LANE), 1)).astype(BF16)
    dtt_out[...] = _dot_nt_mask_data(pick, dt, 3)
    cumt_out[...] = _dot_nt_mask_data(pick, cum, 3)


def _pad_lanes(v, width=LANE):
    v = v.reshape(1, -1)
    return jnp.pad(v, ((0, 0), (0, width - v.shape[1])))


def _ssd_prep(st, proj, conv_w, conv_b, dt_bias, a_log):
    m = proj.shape[0]
    tr = _row_tile(st)
    full = lambda shape: pl.BlockSpec(shape, lambda i: (0, 0))
    row = lambda w: pl.BlockSpec((tr, w), lambda i: (i, 0))
    col = pl.BlockSpec((2 * SSD_HEADS, tr), lambda i: (0, i))
    lower, upper = _tri_blocks(tr)
    return pl.pallas_call(
        functools.partial(_ssd_prep_kernel, tiles_per_seq=st.t // tr),
        grid=(m // tr,),
        in_specs=_halo_specs(st, SSD_W) + [full((3, 768)), full((1, 768)), full((1, LANE)), full((1, LANE)),
                                           full((tr, tr)), full((tr, tr))],
        out_specs=[row(768), row(LANE), row(LANE), col, col],
        out_shape=[jax.ShapeDtypeStruct((m, 768), F32), jax.ShapeDtypeStruct((m, LANE), F32),
                   jax.ShapeDtypeStruct((m, LANE), F32), jax.ShapeDtypeStruct((2 * SSD_HEADS, m), F32),
                   jax.ShapeDtypeStruct((2 * SSD_HEADS, m), F32)],
        compiler_params=_cp("parallel"),
        name="ssd_prep",
    )(proj, proj, proj, conv_w.T, conv_b.reshape(1, 768), _pad_lanes(dt_bias), _pad_lanes(a_log), lower, upper)


def _ssd_scan_kernel(xf_ref, xb_ref, dtf_ref, dtb_ref, cf_ref, cb_ref, dttf_ref, dttb_ref, ctf_ref, ctb_ref,
                     sel_ref, s0_ref, yf_ref, yb_ref, s_ref, *, ncb):
    @pl.when(pl.program_id(1) == 0)
    def _():
        s_ref[...] = s0_ref[...]

    hpg = SSD_HEADS // SSD_GROUPS
    state = {(z, g): _lanes([s_ref[z, g * hpg + hh] for hh in range(hpg)])
             for z in range(2) for g in range(SSD_GROUPS)}
    refs = ((xf_ref, dtf_ref, cf_ref, dttf_ref, ctf_ref, yf_ref), (xb_ref, dtb_ref, cb_ref, dttb_ref, ctb_ref, yb_ref))
    groups = [(z, g) for z in range(2) for g in range(SSD_GROUPS)]
    heads = [(z, h) for z in range(2) for h in range(SSD_HEADS)]
    for step in range(ncb):
        xbc, dt, cum, dtt, cumt, incl, rows = {}, {}, {}, {}, {}, {}, {}
        for z in range(2):
            x_ref, dt_ref, c_ref, dtt_ref, ct_ref, _ = refs[z]
            rows[z] = _chunk_rows(step, z, ncb)
            xbc[z] = x_ref[rows[z], :]
            dt[z] = _dot_data_mask(dt_ref[rows[z], :], sel_ref[z], 3)
            cum[z] = _dot_data_mask(c_ref[rows[z], :], sel_ref[z], 3)
            dtt[z], cumt[z] = dtt_ref[:, rows[z]], ct_ref[:, rows[z]]
            incl[z] = _dir_masks(z)[0]
        bmat = {(z, g): xbc[z][:, 512 + g * SSD_N:512 + (g + 1) * SSD_N] for z, g in groups}
        cmat = {(z, g): _bf(xbc[z][:, 640 + g * SSD_N:640 + (g + 1) * SSD_N]) for z, g in groups}
        cb = {k: _dot_nt(cmat[k], _bf(bmat[k])) for k in groups}
        cs = {k: _dot(cmat[k], _bf(state[k])) for k in groups}
        gw = hpg * SSD_P
        glast = {z: cum[z][CHUNK - 1:CHUNK] if z == 0 else cum[z][0:1] for z in range(2)}
        xs = {z: xbc[z][:, :SSD_HEADS * SSD_P] for z in range(2)}
        xw = {z: xs[z] * (dt[z] * jnp.exp(glast[z] - cum[z])) for z in range(2)}
        inc = {(z, g): _dot_tn(_bf(bmat[z, g]), _bf(xw[z][:, g * gw:(g + 1) * gw])) for z, g in groups}
        att = {}
        for z, h in heads:
            ln = z * SSD_HEADS + h
            gcol = cum[z][:, h * SSD_P:(h + 1) * SSD_P]
            dec = jnp.exp(jnp.where(incl[z], gcol - cumt[z][ln:ln + 1], -jnp.inf))
            att[z, h] = _bf(cb[z, h // hpg] * dec * dtt[z][ln:ln + 1])
        intra = {(z, h): _dot(att[z, h], _bf(xs[z][:, h * SSD_P:(h + 1) * SSD_P])) for z, h in heads}
        for z in range(2):
            e_in = jnp.exp(cum[z])
            e_last = jnp.exp(glast[z])
            outs = []
            for g in range(SSD_GROUPS):
                hs = range(g * hpg, (g + 1) * hpg)
                outs.append(_lanes([intra[z, h] for h in hs]) + cs[z, g] * e_in[:, g * gw:(g + 1) * gw])
                state[z, g] = state[z, g] * e_last[:, g * gw:(g + 1) * gw] + inc[z, g]
            refs[z][5][rows[z], :] = _lanes(outs)
    for z in range(2):
        for h in range(SSD_HEADS):
            s_ref[z, h] = state[z, h // hpg][:, (h % hpg) * SSD_P:(h % hpg + 1) * SSD_P]


def _ssd_scan(st, xbc, dt, cum, dtt, cumt, s0):
    ncb, rb, nblk, fwd, bwd = _scan_blocks(st)
    fwd_t = lambda b, j: (0, fwd(b, j)[0])
    bwd_t = lambda b, j: (0, bwd(b, j)[0])
    state_spec = pl.BlockSpec((None, 2, SSD_HEADS, SSD_N, SSD_P), lambda b, j: (b, 0, 0, 0, 0))
    rows = lambda w, m: pl.BlockSpec((rb, w), m)
    cols = lambda m: pl.BlockSpec((2 * SSD_HEADS, rb), m)
    lane = np.arange(LANE)[None, :, None]
    col = np.arange(SSD_HEADS * SSD_P)[None, None, :]
    sel = _bf(jnp.asarray((lane == np.arange(2)[:, None, None] * SSD_HEADS + col // SSD_P).astype(np.float32)))
    return pl.pallas_call(
        functools.partial(_ssd_scan_kernel, ncb=ncb),
        grid=(st.n, nblk),
        in_specs=[rows(768, fwd), rows(768, bwd), rows(LANE, fwd), rows(LANE, bwd), rows(LANE, fwd), rows(LANE, bwd),
                  cols(fwd_t), cols(bwd_t), cols(fwd_t), cols(bwd_t),
                  pl.BlockSpec(sel.shape, lambda b, j: (0, 0, 0)), state_spec],
        out_specs=[rows(512, fwd), rows(512, bwd), state_spec],
        out_shape=[jax.ShapeDtypeStruct((st.rows, 512), F32), jax.ShapeDtypeStruct((st.rows, 512), F32),
                   jax.ShapeDtypeStruct((st.n, 2, SSD_HEADS, SSD_N, SSD_P), F32)],
        compiler_params=_cp("parallel", "arbitrary"),
        name="ssd_scan",
    )(xbc, xbc, dt, dt, cum, cum, dtt, dtt, cumt, cumt, sel, s0)


def _ssd_post_kernel(yf_ref, yb_ref, xbc_ref, p_ref, d_ref, g_ref, o_ref):
    y = yf_ref[...] + yb_ref[...] + d_ref[...] * xbc_ref[:, :512]
    y = y * _silu(p_ref[:, :512])
    o_ref[...] = _bf(y * lax.rsqrt(jnp.mean(y * y, axis=-1, keepdims=True) + 1e-6) * g_ref[...])


def _ssd_post(st, y_f, y_b, xbc, proj, ssd_d, ssd_norm):
    m = proj.shape[0]
    tr = _row_tile(st)
    full = lambda shape: pl.BlockSpec(shape, lambda i: (0, 0))
    row = lambda w: pl.BlockSpec((tr, w), lambda i: (i, 0))
    return pl.pallas_call(
        _ssd_post_kernel,
        grid=(m // tr,),
        in_specs=[row(512), row(512), row(512), row(512), full((1, 512)), full((1, 512))],
        out_specs=row(512),
        out_shape=jax.ShapeDtypeStruct((m, 512), BF16),
        compiler_params=_cp("parallel"),
        name="ssd_post",
    )(y_f, y_b, xbc, proj, jnp.repeat(ssd_d, SSD_P).reshape(1, 512), ssd_norm.reshape(1, 512))


GLA_QK = GLA_HEADS * GLA_DK
GLA_SAFE_RANGE = 60.0


def _gla_prep_kernel(p_ref, g2_ref, gb_ref, lo_ref, up_ref, cum_out):
    logit = _dot_split(p_ref[...], g2_ref[...]) + gb_ref[...]
    log_a = -_softplus(-logit) * (1.0 / GLA_GATE_NORM)
    cum_out[:, :GLA_QK] = _dot_mask_data(lo_ref[...], log_a[:, :GLA_QK], 3)
    cum_out[:, GLA_QK:] = _dot_mask_data(up_ref[...], log_a[:, GLA_QK:], 3)


def _gla_prep(st, proj, g2p, gb):
    m = proj.shape[0]
    tr = _row_tile(st)
    full = lambda shape: pl.BlockSpec(shape, lambda i: (0, 0))
    lower, upper = _tri_blocks(tr)
    return pl.pallas_call(
        _gla_prep_kernel,
        grid=(m // tr,),
        in_specs=[pl.BlockSpec((tr, LANE), lambda i: (i, (GLA_W - LANE) // LANE)), full((LANE, 2 * GLA_QK)),
                  full((1, 2 * GLA_QK)),
                  full((tr, tr)), full((tr, tr))],
        out_specs=pl.BlockSpec((tr, 2 * GLA_QK), lambda i: (i, 0)),
        out_shape=jax.ShapeDtypeStruct((m, 2 * GLA_QK), F32),
        compiler_params=_cp("parallel"),
        name="gla_prep",
    )(proj, g2p, gb, lower, upper)


def _gla_intra_exact(p_ref, c_ref, rows, z, seg_ref):
    r0 = rows.start
    q = p_ref[rows, 0:GLA_QK] * (GLA_DK ** -0.5)
    cum = c_ref[rows, :]
    row = lax.broadcasted_iota(jnp.int32, (CHUNK, 1), 0)
    seg = seg_ref[...]

    def body(j, acc):
        kj = p_ref[pl.ds(r0 + j, 1), GLA_QK:2 * GLA_QK]
        vj = p_ref[pl.ds(r0 + j, 1), 2 * GLA_QK:2 * GLA_QK + GLA_HEADS * GLA_DV]
        seen = (row >= j) if z == 0 else (row <= j)
        w = jnp.where(seen, q * kj * jnp.exp(jnp.minimum(cum - c_ref[pl.ds(r0 + j, 1), :], 0.0)), 0.0)
        score = _dot_data_mask(w, seg, 3)
        return tuple(a + score[:, h:h + 1] * vj[:, h * GLA_DV:(h + 1) * GLA_DV] for h, a in enumerate(acc))

    zero = jnp.zeros((CHUNK, GLA_DV), F32)
    return list(lax.fori_loop(0, CHUNK, body, (zero,) * GLA_HEADS))


def _gla_scan_kernel(pf_ref, pb_ref, cf_ref, cb_ref, seg_ref, s0_ref, yf_ref, yb_ref, s_ref, *, ncb):
    @pl.when(pl.program_id(1) == 0)
    def _():
        s_ref[...] = s0_ref[...]

    refs = ((pf_ref, cf_ref, yf_ref), (pb_ref, cb_ref, yb_ref))
    heads = [(z, h) for z in range(2) for h in range(GLA_HEADS)]
    hsl = [slice(h * GLA_DK, (h + 1) * GLA_DK) for h in range(GLA_HEADS)]
    state = {(z, h): s_ref[z, h] for z, h in heads}
    chunks = [(step, z) for step in range(ncb) for z in range(2)]
    rows = {(step, z): _chunk_rows(step, z, ncb) for step, z in chunks}
    cum = {k: refs[k[1]][1][rows[k], :] for k in chunks}
    mid = {k: cum[k][CHUNK // 2:CHUNK // 2 + 1] for k in chunks}
    span = functools.reduce(jnp.maximum, [jnp.max(jnp.abs(cum[k] - mid[k])) for k in chunks])

    def v_of(k, h):
        return _bf(refs[k[1]][0][rows[k], 2 * GLA_QK + h * GLA_DV:2 * GLA_QK + (h + 1) * GLA_DV])

    def intra_factored():
        att = {}
        for k in chunks:
            p_ref = refs[k[1]][0]
            q_mid = _bf(p_ref[rows[k], 0:GLA_QK] * (GLA_DK ** -0.5) * jnp.exp(cum[k] - mid[k]))
            k_mid = _bf(p_ref[rows[k], GLA_QK:2 * GLA_QK] * jnp.exp(mid[k] - cum[k]))
            incl = _dir_masks(k[1])[0]
            for h in range(GLA_HEADS):
                att[k, h] = _bf(jnp.where(incl, _dot_nt(q_mid[:, hsl[h]], k_mid[:, hsl[h]]), 0.0))
        return [_dot(att[k, h], v_of(k, h)) for k in chunks for h in range(GLA_HEADS)]

    def intra_exact():
        out = []
        for k in chunks:
            out += _gla_intra_exact(refs[k[1]][0], refs[k[1]][1], rows[k], k[1], seg_ref)
        return out

    intra = lax.cond(span <= GLA_SAFE_RANGE, intra_factored, intra_exact)
    intra = {(k, h): intra[i * GLA_HEADS + h] for i, k in enumerate(chunks) for h in range(GLA_HEADS)}

    for step in range(ncb):
        q_in, k_out, e_last = {}, {}, {}
        for z in range(2):
            k = (step, z)
            p_ref = refs[z][0]
            glast = cum[k][CHUNK - 1:CHUNK] if z == 0 else cum[k][0:1]
            q_in[z] = _bf(p_ref[rows[k], 0:GLA_QK] * (GLA_DK ** -0.5) * jnp.exp(cum[k]))
            k_out[z] = _bf(p_ref[rows[k], GLA_QK:2 * GLA_QK] * jnp.exp(glast - cum[k]))
            e_last[z] = jnp.exp(glast)
        inter = {(z, h): _dot_nt(q_in[z][:, hsl[h]], _bf(state[z, h])) for z, h in heads}
        inc = {(z, h): _dot_tn(v_of((step, z), h), k_out[z][:, hsl[h]]) for z, h in heads}
        for z in range(2):
            refs[z][2][rows[step, z], :] = _lanes([intra[(step, z), h] + inter[z, h] for h in range(GLA_HEADS)])
            for h in range(GLA_HEADS):
                state[z, h] = state[z, h] * e_last[z][:, hsl[h]] + inc[z, h]
    for z, h in heads:
        s_ref[z, h] = state[z, h]


def _gla_scan(st, proj, cum, s0_t):
    ncb, rb, nblk, fwd, bwd = _scan_blocks(st)
    seg = _block_diag(GLA_QK, GLA_DK, 1.0)[:, ::GLA_DK]
    seg = jnp.pad(seg, ((0, 0), (0, LANE - GLA_HEADS)))
    state_spec = pl.BlockSpec((None, 2, GLA_HEADS, GLA_DV, GLA_DK), lambda b, j: (b, 0, 0, 0, 0))
    qkv_w = 2 * GLA_QK + GLA_HEADS * GLA_DV
    return pl.pallas_call(
        functools.partial(_gla_scan_kernel, ncb=ncb),
        grid=(st.n, nblk),
        in_specs=[pl.BlockSpec((rb, qkv_w), fwd), pl.BlockSpec((rb, qkv_w), bwd),
                  pl.BlockSpec((rb, GLA_QK), fwd), pl.BlockSpec((rb, GLA_QK), lambda b, j: (bwd(b, j)[0], 1)),
                  pl.BlockSpec((GLA_QK, LANE), lambda b, j: (0, 0)), state_spec],
        out_specs=[pl.BlockSpec((rb, 512), fwd), pl.BlockSpec((rb, 512), bwd), state_spec],
        out_shape=[jax.ShapeDtypeStruct((st.rows, 512), F32), jax.ShapeDtypeStruct((st.rows, 512), F32),
                   jax.ShapeDtypeStruct((st.n, 2, GLA_HEADS, GLA_DV, GLA_DK), F32)],
        compiler_params=_cp("parallel", "arbitrary"),
        name="gla_scan",
    )(proj, proj, cum, cum, seg, s0_t)


def _gla_post_kernel(yf_ref, yb_ref, p_ref, g_ref, o_ref):
    o = yf_ref[...] + yb_ref[...]
    gate = _silu(p_ref[...])
    outs = []
    for h in range(GLA_HEADS):
        oh = o[:, h * GLA_DV:(h + 1) * GLA_DV]
        outs.append(oh * lax.rsqrt(jnp.mean(oh * oh, axis=-1, keepdims=True) + 1e-6) * g_ref[...])
    o_ref[...] = _bf(_lanes(outs) * gate)


def _gla_post(st, y_f, y_b, proj, gla_norm):
    m = proj.shape[0]
    tr = _row_tile(st)
    return pl.pallas_call(
        _gla_post_kernel,
        grid=(m // tr,),
        in_specs=[pl.BlockSpec((tr, 512), lambda i: (i, 0)), pl.BlockSpec((tr, 512), lambda i: (i, 0)),
                  pl.BlockSpec((tr, 512), lambda i: (i, 2)), pl.BlockSpec((1, GLA_DV), lambda i: (0, 0))],
        out_specs=pl.BlockSpec((tr, 512), lambda i: (i, 0)),
        out_shape=jax.ShapeDtypeStruct((m, 512), BF16),
        compiler_params=_cp("parallel"),
        name="gla_post",
    )(y_f, y_b, proj, gla_norm.reshape(1, GLA_DV))


def _rwkv_prep_kernel(cur_ref, prev_ref, next_ref, mu_ref, w2_ref, w0_ref, a2_ref, a0_ref, g2_ref,
                      kkw_ref, ka_ref, rk_ref, bd_ref, in_out, lw_out, post_out, *, tiles_per_seq):
    cur = cur_ref[...]
    x_prev, x_next = _neighbours(cur, prev_ref[...], next_ref[...], tiles_per_seq)
    blk = cur + (0.5 * (x_prev + x_next) - cur) * mu_ref[...]
    r, k, v = blk[:, 0:512], blk[:, 512:1024], blk[:, 1024:1536]
    w_logit = w0_ref[...] + _dot_split(jnp.tanh(blk[:, 1536:1600]), w2_ref[...])
    lw_out[...] = -RWKV_DECAY_SCALE * jax.nn.sigmoid(w_logit)
    a = jax.nn.sigmoid(a0_ref[...] + _dot_split(blk[:, 1600:1664], a2_ref[...]))
    g = _dot_split(jax.nn.sigmoid(blk[:, 1664:1792]), g2_ref[...])
    bd = bd_ref[...]
    kk = k * kkw_ref[...]
    kk = kk * lax.rsqrt(_dot_data_mask(kk * kk, bd, 2) + 1e-12)
    k2 = k * (1.0 + (a - 1.0) * ka_ref[...])
    in_out[:, 0:512] = r
    in_out[:, 512:1024] = k2
    in_out[:, 1024:1536] = v
    in_out[:, 1536:2048] = -kk
    in_out[:, 2048:2560] = kk * a
    post_out[:, 0:512] = g
    post_out[:, 512:1024] = _dot_data_mask(r * k2 * rk_ref[...], bd, 2) * v


def _rwkv_prep(st, proj, mu, w2, w0, a2, a0, g2, kkw, ka, rk):
    m = proj.shape[0]
    tr = _row_tile(st)
    full = lambda shape: pl.BlockSpec(shape, lambda i: (0, 0))
    row = lambda w: pl.BlockSpec((tr, w), lambda i: (i, 0))
    vec = lambda a: a.reshape(1, -1)
    return pl.pallas_call(
        functools.partial(_rwkv_prep_kernel, tiles_per_seq=st.t // tr),
        grid=(m // tr,),
        in_specs=_halo_specs(st, RWKV_W) + [full((1, RWKV_W)), full((64, 1024)), full((1, 1024)),
                                            full((64, 512)), full((1, 512)), full((128, 512)),
                                            full((1, 512)), full((1, 512)), full((1, 512)), full((512, 512))],
        out_specs=[row(2560), row(1024), row(1024)],
        out_shape=[jax.ShapeDtypeStruct((m, 2560), F32), jax.ShapeDtypeStruct((m, 1024), F32),
                   jax.ShapeDtypeStruct((m, 1024), F32)],
        compiler_params=_cp("parallel"),
        name="rwkv_prep",
    )(proj, proj, proj, vec(mu), jnp.concatenate([w2[0], w2[1]], axis=1), vec(w0), a2, vec(a0), g2,
      vec(kkw), vec(ka), vec(rk), _block_diag(512, RWKV_HD, 1.0))


def _rwkv_chunk(x_ref, lw_ref, rows, z):
    incl, strict = _dir_masks(z)
    lw = lw_ref[rows, :]
    cum = _dot_mask_data(incl.astype(BF16), lw, 3)
    cum_x = cum - lw
    mid = cum[CHUNK // 2:CHUNK // 2 + 1]
    glast = cum[CHUNK - 1:CHUNK] if z == 0 else cum[0:1]
    x = x_ref[rows, :]
    r, k, v, al, be = x[:, 0:512], x[:, 512:1024], x[:, 1024:1536], x[:, 1536:2048], x[:, 2048:2560]
    e_mid, e_nmid = jnp.exp(cum - mid), jnp.exp(mid - cum)
    e_out = jnp.exp(glast - cum)
    return dict(incl=incl, strict=strict, v=v,
                r_mid=_bf(r * e_mid), a_mid=_bf(al * jnp.exp(cum_x - mid)),
                b_mid=_bf(be * e_nmid), k_mid=_bf(k * e_nmid),
                r_in=_bf(r * jnp.exp(cum)), a_in=al * jnp.exp(cum_x),
                b_out=_bf(be * e_out), k_out=_bf(k * e_out), e_last=jnp.exp(glast))


_HEAD_SLICES = [slice(h * RWKV_HD, (h + 1) * RWKV_HD) for h in range(RWKV_HEADS)]


def _rwkv_state_free(chunks, hooks):
    hooks = list(hooks)

    def run_hook():
        if hooks:
            hooks.pop(0)()

    chains = [(c, sl) for c in chunks for sl in _HEAD_SLICES]
    run_hook()
    pair = [_dot_nt(_rows([c['a_mid'][:, sl], c['r_mid'][:, sl]]), _rows([c['b_mid'][:, sl], c['k_mid'][:, sl]]))
            for c, sl in chains]
    half = CHUNK
    a_ab = [_bf(jnp.where(c['strict'], p[:half, :half], 0.0)) for (c, _), p in zip(chains, pair)]
    a_ak = [_bf(jnp.where(c['strict'], p[:half, half:], 0.0)) for (c, _), p in zip(chains, pair)]
    a_rb = [_bf(jnp.where(c['incl'], p[half:, :half], 0.0)) for (c, _), p in zip(chains, pair)]
    a_rk = [_bf(jnp.where(c['incl'], p[half:, half:], 0.0)) for (c, _), p in zip(chains, pair)]
    vb = [_bf(c['v'][:, sl]) for c, sl in chains]
    both = [_dot(_rows([ak, rk]), v) for ak, rk, v in zip(a_ak, a_rk, vb)]
    av = [x[:half] for x in both]
    o0 = [x[half:] for x in both]
    sol = [_lanes([c['a_in'][:, sl], x]) for (c, sl), x in zip(chains, av)]
    powr = a_ab
    width = 2 * RWKV_HD
    for it in range(6):
        if it < 5:
            both = [_dot(p, _lanes([_bf(s), p])) for p, s in zip(powr, sol)]
            sol = [s + x[:, :width] for s, x in zip(sol, both)]
            powr = [_bf(x[:, width:]) for x in both]
        else:
            sol = [s + _dot(p, _bf(s)) for p, s in zip(powr, sol)]
        if it in (1, 3):
            run_hook()
    while hooks:
        run_hook()
    return [dict(w=_bf(s[:, :RWKV_HD]), u0=s[:, RWKV_HD:], o0=o, a_rb=a, v=c['v'][:, sl], r_in=c['r_in'][:, sl],
                 b_out=c['b_out'][:, sl], k_out=c['k_out'][:, sl], e_last=c['e_last'][:, sl])
            for (c, sl), s, o, a in zip(chains, sol, o0, a_rb)]


def _rwkv_state_stages(res, state, write_out):
    box = {}

    def read_state():
        box['su'] = [_dot_nt(_rows([c['w'], c['r_in']]), _bf(s)) for c, s in zip(res, state)]

    def update_state():
        box['u'] = [c['u0'] + su[:CHUNK] for c, su in zip(res, box['su'])]
        state[:] = [s * c['e_last'] + _dot_tn(_bf(_rows([u, c['v']])), _rows([c['b_out'], c['k_out']]))
                    for c, s, u in zip(res, state, box['u'])]

    def emit():
        write_out([su[CHUNK:] + _dot(c['a_rb'], _bf(u)) + c['o0'] for c, su, u in zip(res, box['su'], box['u'])])

    return [read_state, update_state, emit]


def _rwkv_scan_kernel(xf_ref, xb_ref, lwf_ref, lwb_ref, s0_ref, yf_ref, yb_ref, s_ref, *, ncb):
    @pl.when(pl.program_id(1) == 0)
    def _():
        s_ref[...] = s0_ref[...]

    state = [s_ref[z, h] for z in range(2) for h in range(RWKV_HEADS)]

    def chunk_rows(step, z):
        cc = step if z == 0 else ncb - 1 - step
        return slice(cc * CHUNK, (cc + 1) * CHUNK)

    def pair_inputs(step):
        return [_rwkv_chunk(xf_ref, lwf_ref, chunk_rows(step, 0), 0),
                _rwkv_chunk(xb_ref, lwb_ref, chunk_rows(step, 1), 1)]

    def writer(step):
        def write_out(outs):
            yf_ref[chunk_rows(step, 0), :] = _lanes(outs[:RWKV_HEADS])
            yb_ref[chunk_rows(step, 1), :] = _lanes(outs[RWKV_HEADS:])
        return write_out

    res = _rwkv_state_free(pair_inputs(0), [])
    for step in range(ncb):
        hooks = _rwkv_state_stages(res, state, writer(step))
        if step + 1 < ncb:
            res = _rwkv_state_free(pair_inputs(step + 1), hooks)
        else:
            for hook in hooks:
                hook()
    for z in range(2):
        for h in range(RWKV_HEADS):
            s_ref[z, h] = state[z * RWKV_HEADS + h]


def _rwkv_scan(st, xin, lw, s0):
    ncb, rb, nblk, fwd, bwd = _scan_blocks(st)
    state_spec = pl.BlockSpec((None, 2, RWKV_HEADS, RWKV_HD, RWKV_HD), lambda b, j: (b, 0, 0, 0, 0))
    return pl.pallas_call(
        functools.partial(_rwkv_scan_kernel, ncb=ncb),
        grid=(st.n, nblk),
        in_specs=[pl.BlockSpec((rb, 2560), fwd), pl.BlockSpec((rb, 2560), bwd),
                  pl.BlockSpec((rb, 512), fwd), pl.BlockSpec((rb, 512), lambda b, j: (bwd(b, j)[0], 1)),
                  state_spec],
        out_specs=[pl.BlockSpec((rb, 512), fwd), pl.BlockSpec((rb, 512), bwd), state_spec],
        out_shape=[jax.ShapeDtypeStruct((st.rows, 512), F32), jax.ShapeDtypeStruct((st.rows, 512), F32),
                   jax.ShapeDtypeStruct((st.n, 2, RWKV_HEADS, RWKV_HD, RWKV_HD), F32)],
        compiler_params=_cp("parallel", "arbitrary"),
        name="rwkv_scan",
    )(xin, xin, lw, lw, s0)


def _rwkv_post_kernel(yf_ref, yb_ref, post_ref, lng_ref, lnb_ref, bd_ref, o_ref):
    o = yf_ref[...] + yb_ref[...]
    bd = bd_ref[...]
    d = o - _dot_data_mask(o, bd, 2)
    o = d * lax.rsqrt(_dot_data_mask(d * d, bd, 2) + RWKV_LN_EPS) * lng_ref[...] + lnb_ref[...]
    o_ref[...] = _bf((o + post_ref[:, 512:1024]) * post_ref[:, 0:512])


def _rwkv_post(st, y_f, y_b, post, ln_g, ln_b):
    m = post.shape[0]
    tr = _row_tile(st)
    full = lambda shape: pl.BlockSpec(shape, lambda i: (0, 0))
    row = lambda w: pl.BlockSpec((tr, w), lambda i: (i, 0))
    return pl.pallas_call(
        _rwkv_post_kernel,
        grid=(m // tr,),
        in_specs=[row(512), row(512), row(1024), full((1, 512)), full((1, 512)), full((512, 512))],
        out_specs=row(512),
        out_shape=jax.ShapeDtypeStruct((m, 512), BF16),
        compiler_params=_cp("parallel"),
        name="rwkv_post",
    )(y_f, y_b, post, ln_g.reshape(1, 512), ln_b.reshape(1, 512), _block_diag(512, RWKV_HD, 1.0 / RWKV_HD))


def _merge_kernel(h_ref, o0_ref, o1_ref, o2_ref, o3_ref, wg_ref, wb_ref, out_ref):
    h = h_ref[...]
    acc = None
    for i, o_ref in enumerate((o0_ref, o1_ref, o2_ref, o3_ref)):
        term = jax.nn.sigmoid(_dot(h, wg_ref[i])) * _dot(o_ref[...], wb_ref[i])
        acc = term if acc is None else acc + term
    out_ref[...] = _bf(acc)


def _merge(st, h, branch_outs, wg, wb):
    m, d = h.shape
    tm = _mm_tile(st)
    tn = 512
    row = lambda w: pl.BlockSpec((tm, w), lambda j, i: (i, 0))
    return pl.pallas_call(
        _merge_kernel,
        grid=(d // tn, m // tm),
        in_specs=[row(d)] + [row(BRANCH_W)] * 4 + [pl.BlockSpec((4, d, tn), lambda j, i: (0, 0, j)),
                                                    pl.BlockSpec((4, BRANCH_W, tn), lambda j, i: (0, 0, j))],
        out_specs=pl.BlockSpec((tm, tn), lambda j, i: (i, j)),
        out_shape=jax.ShapeDtypeStruct((m, d), BF16),
        compiler_params=_cp("parallel", "parallel"),
        name="merge",
    )(h, *branch_outs, wg, wb)


def _wo_kernel(m_ref, w_ref, x_ref, mod_ref, g_ref, o_ref, h_out):
    x = x_ref[...] + mod_ref[0, 2:3, :] * _dot(m_ref[...], w_ref[...])
    o_ref[...] = x
    h_out[...] = _norm_mod(x, g_ref[...], mod_ref, 4, 3)


def _out_proj(st, merged, w_o, x, mod_l, gain2):
    m, d = x.shape
    tm = _mm_tile(st)
    row = pl.BlockSpec((tm, d), lambda i: (i, 0))
    return pl.pallas_call(
        _wo_kernel,
        grid=(m // tm,),
        in_specs=[row, pl.BlockSpec((d, d), lambda i: (0, 0)), row, pl.BlockSpec((1, 6, d), st.group_map(tm)),
                  pl.BlockSpec((1, d), lambda i: (0, 0))],
        out_specs=[row, row],
        out_shape=[jax.ShapeDtypeStruct((m, d), F32), jax.ShapeDtypeStruct((m, d), BF16)],
        compiler_params=_cp("parallel"),
        name="out_proj",
    )(merged, w_o, x, mod_l, gain2.reshape(1, d))


def _ffn_kernel(h_ref, w1_ref, w3_ref, w2_ref, x_ref, mod_ref, fg_ref, o_ref, acc_ref, *, final):
    f = pl.program_id(1)

    @pl.when(f == 0)
    def _():
        acc_ref[...] = jnp.zeros_like(acc_ref)

    h = h_ref[...]
    u = _silu(_dot(h, w1_ref[...])) * _dot(h, w3_ref[...])
    acc_ref[...] += _dot(_bf(u), w2_ref[...])

    @pl.when(f == pl.num_programs(1) - 1)
    def _():
        x = x_ref[...] + mod_ref[0, 5:6, :] * acc_ref[...]
        o_ref[...] = _rms(x, fg_ref[...]) if final else x


def _ffn(st, h, w1, w3, w2, x, mod_l, final_gain, final):
    m, d = x.shape
    dff = w1.shape[1]
    tm = _mm_tile(st)
    tf = 512
    return pl.pallas_call(
        functools.partial(_ffn_kernel, final=final),
        grid=(m // tm, dff // tf),
        in_specs=[pl.BlockSpec((tm, d), lambda i, f: (i, 0)),
                  pl.BlockSpec((d, tf), lambda i, f: (0, f)),
                  pl.BlockSpec((d, tf), lambda i, f: (0, f)),
                  pl.BlockSpec((tf, d), lambda i, f: (f, 0)),
                  pl.BlockSpec((tm, d), lambda i, f: (i, 0)),
                  pl.BlockSpec((1, 6, d), lambda i, f: st.group_map(tm)(i)),
                  pl.BlockSpec((1, d), lambda i, f: (0, 0))],
        out_specs=pl.BlockSpec((tm, d), lambda i, f: (i, 0)),
        out_shape=jax.ShapeDtypeStruct((m, d), F32),
        scratch_shapes=[pltpu.VMEM((tm, d), F32)],
        compiler_params=_cp("parallel", "arbitrary"),
        name="ffn",
    )(h, w1, w3, w2, x, mod_l, final_gain.reshape(1, d))


def _split_w_in(w_in):
    z = lambda w: jnp.zeros(w_in.shape[:2] + (w,), w_in.dtype)
    att = w_in[..., 0:768]
    ssd = jnp.concatenate([w_in[..., 768:2064], z(SSD_W - 1296)], axis=-1)
    rwkv = w_in[..., 2064:3856]
    gla = jnp.concatenate([w_in[..., 3856:4880], w_in[..., 4896:5408], w_in[..., 4880:4896], z(LANE - 16)], axis=-1)
    return tuple(_bf(w) for w in (att, ssd, rwkv, gla))


def _block(st, x, mod_l, p, rope_tables, ctx, final_gain, final):
    h, proj_att, proj_ssd = _in_proj_a(st, x, p['norm1'], mod_l, p['w_att'], p['w_ssd'])
    proj_rwkv, proj_gla = _in_proj_b(st, h, p['w_rwkv'], p['w_gla'])

    v = proj_att[:, 640:768]
    if ctx is None:
        q, k_att, v_att, k = _att_prep(st, proj_att, p['q_norm'], p['k_norm'], None)
        o_att = _attention(st, q, k_att, v_att)
        s_ssd = jnp.zeros((st.n, 2, SSD_HEADS, SSD_N, SSD_P), F32)
        s_rwkv = jnp.zeros((st.n, 2, RWKV_HEADS, RWKV_HD, RWKV_HD), F32)
        s_gla_t = jnp.zeros((st.n, 2, GLA_HEADS, GLA_DV, GLA_DK), F32)
    else:
        ctx_k, ctx_v, s_ssd, s_rwkv, s_gla = ctx
        q, k_att, v_att = _att_prep(st, proj_att, p['q_norm'], p['k_norm'], rope_tables)
        k = None
        o_att = _attention(st, q, _join_cache(st, k_att, ctx_k), _join_cache(st, v_att, ctx_v))
        s_gla_t = jnp.swapaxes(s_gla, -1, -2)

    xbc, dt, cum, dtt, cumt = _ssd_prep(st, proj_ssd, p['ssd_conv_w'], p['ssd_conv_b'], p['ssd_dt_bias'],
                                        p['ssd_a_log'])
    y_ssd_f, y_ssd_b, new_ssd = _ssd_scan(st, xbc, dt, cum, dtt, cumt, s_ssd)
    o_ssd = _ssd_post(st, y_ssd_f, y_ssd_b, xbc, proj_ssd, p['ssd_d'], p['ssd_norm'])

    rin, lw, rpost = _rwkv_prep(st, proj_rwkv, p['rwkv_mu'], p['rwkv_w2'], p['rwkv_w0'], p['rwkv_a2'],
                                p['rwkv_a0'], p['rwkv_g2'], p['rwkv_kk'], p['rwkv_ka'], p['rwkv_rk'])
    y_rwkv_f, y_rwkv_b, new_rwkv = _rwkv_scan(st, rin, lw, s_rwkv)
    o_rwkv = _rwkv_post(st, y_rwkv_f, y_rwkv_b, rpost, p['rwkv_ln_g'], p['rwkv_ln_b'])

    g2 = jnp.concatenate([p['gla_g2'][0], p['gla_g2'][1]], axis=1)
    g2p = jnp.pad(g2, ((0, LANE - g2.shape[0]), (0, 0)))
    gla_cum = _gla_prep(st, proj_gla, g2p, p['gla_gb'].reshape(1, 2 * GLA_QK))
    y_gla_f, y_gla_b, new_gla_t = _gla_scan(st, proj_gla, gla_cum, s_gla_t)
    o_gla = _gla_post(st, y_gla_f, y_gla_b, proj_gla, p['gla_norm'])

    merged = _merge(st, h, (o_att, o_ssd, o_rwkv, o_gla), p['w_gate'], p['w_branch'])
    x, h2 = _out_proj(st, merged, p['w_o'], x, mod_l, p['norm2'])
    x = _ffn(st, h2, p['ffn_w1'], p['ffn_w3'], p['ffn_w2'], x, mod_l, final_gain, final)
    return x, (k, v, new_ssd, new_rwkv, jnp.swapaxes(new_gla_t, -1, -2))


def kernel(x_prompt, x_sample, cache_attn_k, cache_attn_v, state_ssd, state_rwkv, state_gla, c, c_ctx, w_mod, b_mod, norm1, norm2, w_in, q_norm, k_norm, ssd_conv_w, ssd_conv_b, ssd_dt_bias, ssd_a_log, ssd_d, ssd_norm, rwkv_mu, rwkv_w0, rwkv_w2, rwkv_a0, rwkv_a2, rwkv_g2, rwkv_kk, rwkv_ka, rwkv_rk, rwkv_ln_g, rwkv_ln_b, gla_g2, gla_gb, gla_norm, w_gate, w_branch, w_o, ffn_w1, ffn_w3, ffn_w2, final_norm):
    nb, seq, d = x_prompt.shape
    db, dseq, _ = x_sample.shape
    depth = w_in.shape[0]
    assert d == D_MODEL and seq % CHUNK == 0 and dseq % CHUNK == 0 and 1 + db <= MOD_ROWS
    ctx_st = _Stream(nb, seq, 0, False)
    lat_st = _Stream(db, dseq, 1, True)

    cond = jnp.concatenate([c_ctx[None], c, jnp.zeros((MOD_ROWS - 1 - db, d), F32)], axis=0)
    mod = _modulation(cond, w_mod, b_mod)

    w_att, w_ssd, w_rwkv, w_gla = _split_w_in(w_in)
    w_gate_b, w_branch_b, w_o_b = _bf(w_gate), _bf(w_branch), _bf(w_o)
    w1_b, w3_b, w2_b = _bf(ffn_w1), _bf(ffn_w3), _bf(ffn_w2)

    def params_at(l):
        return dict(norm1=norm1[l], norm2=norm2[l], w_att=w_att[l], w_ssd=w_ssd[l], w_rwkv=w_rwkv[l],
                    w_gla=w_gla[l], q_norm=q_norm[l], k_norm=k_norm[l],
                    ssd_conv_w=ssd_conv_w[l], ssd_conv_b=ssd_conv_b[l], ssd_dt_bias=ssd_dt_bias[l],
                    ssd_a_log=ssd_a_log[l], ssd_d=ssd_d[l], ssd_norm=ssd_norm[l],
                    rwkv_mu=rwkv_mu[l], rwkv_w0=rwkv_w0[l], rwkv_w2=rwkv_w2[l], rwkv_a0=rwkv_a0[l],
                    rwkv_a2=rwkv_a2[l], rwkv_g2=rwkv_g2[l], rwkv_kk=rwkv_kk[l], rwkv_ka=rwkv_ka[l],
                    rwkv_rk=rwkv_rk[l], rwkv_ln_g=rwkv_ln_g[l], rwkv_ln_b=rwkv_ln_b[l],
                    gla_g2=gla_g2[l], gla_gb=gla_gb[l], gla_norm=gla_norm[l],
                    w_gate=w_gate_b[l], w_branch=w_branch_b[l], w_o=w_o_b[l],
                    ffn_w1=w1_b[l], ffn_w3=w3_b[l], ffn_w2=w2_b[l])

    xp = x_prompt.reshape(nb * seq, d)
    new_k, new_v, new_ssd, new_rwkv, new_gla = [], [], [], [], []
    for l in range(depth):
        xp, (k_l, v_l, ssd_l, rwkv_l, gla_l) = _block(ctx_st, xp, mod[l], params_at(l), None, None, final_norm,
                                                      l == depth - 1)
        new_k.append(k_l.reshape(nb, seq, ATT_KV, HEAD_DIM))
        new_v.append(v_l.reshape(nb, seq, ATT_KV, HEAD_DIM))
        new_ssd.append(ssd_l)
        new_rwkv.append(rwkv_l)
        new_gla.append(gla_l)

    rope_tables = _rope_tables(dseq)
    xs = x_sample.reshape(db * dseq, d)
    for l in range(depth):
        ctx = (cache_attn_k[:, l], cache_attn_v[:, l], state_ssd[:, l], state_rwkv[:, l], state_gla[:, l])
        xs, _ = _block(lat_st, xs, mod[l], params_at(l), rope_tables, ctx, final_norm, l == depth - 1)

    y_prompt = xp.reshape(nb, seq, d)
    y_sample = xs.reshape(db, dseq, d)
    return (y_prompt, y_sample, jnp.stack(new_k, axis=1), jnp.stack(new_v, axis=1),
            jnp.stack(new_ssd, axis=1), jnp.stack(new_rwkv, axis=1), jnp.stack(new_gla, axis=1))
```

```python
import functools

import jax
import jax.numpy as jnp
import numpy as np
from jax import lax
from jax.experimental import pallas as pl
from jax.experimental.pallas import tpu as pltpu

F32 = jnp.float32
BF16 = jnp.bfloat16

D_MODEL = 2048
GRID_W = 64
ATT_HEADS = 8
ATT_KV = 2
ATT_GROUP = ATT_HEADS // ATT_KV
HEAD_DIM = 64
ROPE_THETA = 10000.0
SSD_HEADS = 8
SSD_P = 64
SSD_N = 64
SSD_GROUPS = 2
RWKV_HEADS = 8
RWKV_HD = 64
RWKV_DECAY_SCALE = 0.6065306597126334
RWKV_LN_EPS = 64e-5
GLA_HEADS = 4
GLA_DK = 64
GLA_DV = 128
GLA_GATE_NORM = 16.0
CHUNK = 64
BRANCH_W = 512

ATT_W = 768
SSD_W = 1408
RWKV_W = 1792
GLA_W = 1664
LANE = 128
SUBLANE = 8
MOD_ROWS = 16
VMEM_LIMIT = 56 * 1024 * 1024
ROW_TILE = 256
MM_TILE = 512
SCAN_CHUNKS = 4
KEY_TILE = 256
ATT_Q_TILE = 256
KEY_UNROLL = 6
LOG2E = 1.4426950408889634
ATT_MIN_ROW_SUM = 2.0 ** -90


def _cp(*sem):
    return pltpu.CompilerParams(dimension_semantics=sem, vmem_limit_bytes=VMEM_LIMIT)


def _bf(x):
    return x.astype(BF16)


def _dot(a, b, prec=None):
    return jnp.dot(a, b, preferred_element_type=F32, precision=prec)


def _dot_nt(a, b, prec=None):
    return lax.dot_general(a, b, (((1,), (1,)), ((), ())), preferred_element_type=F32, precision=prec)


def _dot_tn(a, b, prec=None):
    return lax.dot_general(a, b, (((0,), (0,)), ((), ())), preferred_element_type=F32, precision=prec)


def _pieces(a, n):
    out = []
    for _ in range(n):
        piece = _bf(a)
        out.append(piece)
        a = a - piece.astype(F32)
    return out


def _dot_data_mask(a, mask, n):
    return functools.reduce(jnp.add, [_dot(piece, mask) for piece in _pieces(a, n)])


def _dot_mask_data(mask, a, n):
    return functools.reduce(jnp.add, [_dot(mask, piece) for piece in _pieces(a, n)])


def _dot_nt_mask_data(mask, a, n):
    return functools.reduce(jnp.add, [_dot_nt(mask, piece) for piece in _pieces(a, n)])


def _dot_split(a, b):
    a_hi, a_lo = _pieces(a, 2)
    b_hi, b_lo = _pieces(b, 2)
    return _dot(a_hi, b_hi) + (_dot(a_hi, b_lo) + _dot(a_lo, b_hi))


def _silu(x):
    return x * jax.nn.sigmoid(x)


def _softplus(x):
    return jnp.maximum(x, 0.0) + jnp.log1p(jnp.exp(-jnp.abs(x)))


def _lanes(pieces):
    return jnp.concatenate(pieces, axis=1)


def _rows(pieces):
    return jnp.concatenate(pieces, axis=0)


def _mod_kernel(c_ref, w_ref, b_ref, o_ref):
    c = c_ref[...]
    o_ref[0] = _dot(_bf(_silu(c)), _bf(w_ref[0])) + b_ref[0]


def _modulation(cond, w_mod, b_mod):
    nl, d, n6 = w_mod.shape
    tn = 1024
    out = pl.pallas_call(
        _mod_kernel,
        grid=(nl, n6 // tn),
        in_specs=[pl.BlockSpec((MOD_ROWS, d), lambda l, j: (0, 0)),
                  pl.BlockSpec((1, d, tn), lambda l, j: (l, 0, j)),
                  pl.BlockSpec((1, 1, tn), lambda l, j: (l, 0, j))],
        out_specs=pl.BlockSpec((1, MOD_ROWS, tn), lambda l, j: (l, 0, j)),
        out_shape=jax.ShapeDtypeStruct((nl, MOD_ROWS, n6), F32),
        compiler_params=_cp("parallel", "parallel"),
        name="modulation",
    )(cond, w_mod, b_mod.reshape(nl, 1, n6))
    return out.reshape(nl, MOD_ROWS, 6, d)


class _Stream:
    def __init__(self, n, t, group0, per_seq):
        self.n, self.t, self.group0, self.per_seq = n, t, group0, per_seq
        self.rows = n * t

    def group_map(self, tile):
        g0, per_seq, t = self.group0, self.per_seq, self.t
        if per_seq:
            return lambda i: (g0 + (i * tile) // t, 0, 0)
        return lambda i: (g0, 0, 0)


def _mm_tile(st):
    return min(MM_TILE, st.t)


def _row_tile(st):
    return min(ROW_TILE, st.t)


def _rms(x, gain):
    return x * lax.rsqrt(jnp.mean(x * x, axis=-1, keepdims=True) + 1e-6) * gain


def _norm_mod(x, gain, mod_ref, sc_idx, sh_idx):
    return _bf(_rms(x, gain) * (1.0 + mod_ref[0, sc_idx:sc_idx + 1, :]) + mod_ref[0, sh_idx:sh_idx + 1, :])


def _in_proj_a_kernel(x_ref, g_ref, mod_ref, wa_ref, ws_ref, h_out, pa_out, ps_out):
    h = _norm_mod(x_ref[...], g_ref[...], mod_ref, 1, 0)
    h_out[...] = h
    pa_out[...] = _dot(h, wa_ref[...])
    ps_out[...] = _dot(h, ws_ref[...])


def _in_proj_a(st, x, gain, mod_l, w_att, w_ssd):
    m, d = x.shape
    tm = _mm_tile(st)
    row = lambda w: pl.BlockSpec((tm, w), lambda i: (i, 0))
    full = lambda a: pl.BlockSpec(a.shape, lambda i: (0, 0))
    return pl.pallas_call(
        _in_proj_a_kernel,
        grid=(m // tm,),
        in_specs=[row(d), pl.BlockSpec((1, d), lambda i: (0, 0)), pl.BlockSpec((1, 6, d), st.group_map(tm)),
                  full(w_att), full(w_ssd)],
        out_specs=[row(d), row(ATT_W), row(SSD_W)],
        out_shape=[jax.ShapeDtypeStruct((m, d), BF16), jax.ShapeDtypeStruct((m, ATT_W), F32),
                   jax.ShapeDtypeStruct((m, SSD_W), F32)],
        compiler_params=_cp("parallel"),
        name="in_proj_a",
    )(x, gain.reshape(1, d), mod_l, w_att, w_ssd)


def _in_proj_b_kernel(h_ref, wr_ref, wg_ref, pr_out, pg_out):
    h = h_ref[...]
    pr_out[...] = _dot(h, wr_ref[...])
    pg_out[...] = _dot(h, wg_ref[...])


def _in_proj_b(st, h, w_rwkv, w_gla):
    m, d = h.shape
    tm = _mm_tile(st)
    row = lambda w: pl.BlockSpec((tm, w), lambda i: (i, 0))
    full = lambda a: pl.BlockSpec(a.shape, lambda i: (0, 0))
    return pl.pallas_call(
        _in_proj_b_kernel,
        grid=(m // tm,),
        in_specs=[row(d), full(w_rwkv), full(w_gla)],
        out_specs=[row(RWKV_W), row(GLA_W)],
        out_shape=[jax.ShapeDtypeStruct((m, RWKV_W), F32), jax.ShapeDtypeStruct((m, GLA_W), F32)],
        compiler_params=_cp("parallel"),
        name="in_proj_b",
    )(h, w_rwkv, w_gla)


def _att_prep_kernel(*refs, rope):
    if rope:
        p_ref, qn_ref, kn_ref, bdq_ref, bdk_ref, cos_ref, sa_ref, sb_ref, q_out, k_out, v_out = refs
    else:
        p_ref, qn_ref, kn_ref, bdq_ref, bdk_ref, q_out, k_out, v_out, k_leaf = refs
    p = p_ref[...]
    aq = p[:, :512]
    ak = p[:, 512:640]
    av = p[:, 640:768]
    q = aq * lax.rsqrt(_dot_data_mask(aq * aq, bdq_ref[...], 2) + 1e-6) * qn_ref[...]
    k = ak * lax.rsqrt(_dot_data_mask(ak * ak, bdk_ref[...], 2) + 1e-6) * kn_ref[...]
    if rope:
        c, sa, sb = cos_ref[...], sa_ref[...], sb_ref[...]
        k = k * c + pltpu.roll(k, LANE - 16, 1) * sa + pltpu.roll(k, 16, 1) * sb
        c4, sa4, sb4 = _lanes([c] * 4), _lanes([sa] * 4), _lanes([sb] * 4)
        q = q * c4 + pltpu.roll(q, 512 - 16, 1) * sa4 + pltpu.roll(q, 16, 1) * sb4
    else:
        k_leaf[...] = k
    tr = p.shape[0]
    one_col = (lax.broadcasted_iota(jnp.int32, (tr, LANE - HEAD_DIM), 1) == 0).astype(F32)
    q = q * (HEAD_DIM ** -0.5 * LOG2E)
    q_len = jnp.sqrt(_dot_data_mask(q * q, bdq_ref[...], 2) * HEAD_DIM)
    q_out[...] = _bf(_lanes([piece for h in range(ATT_HEADS)
                             for piece in (q[:, h * HEAD_DIM:(h + 1) * HEAD_DIM],
                                           one_col * q_len[:, h * HEAD_DIM:(h + 1) * HEAD_DIM])]))
    for kv in range(ATT_KV):
        sl = slice(kv * HEAD_DIM, (kv + 1) * HEAD_DIM)
        k_out[kv] = _bf(_lanes([k[:, sl], one_col]))
        v_out[kv] = _bf(_lanes([av[:, sl], one_col]))


def _block_diag(width, block, value):
    idx = np.arange(width) // block
    return _bf(jnp.asarray((idx[:, None] == idx[None, :]).astype(np.float32) * value))


def _att_prep(st, proj, q_norm, k_norm, rope_tables):
    m = proj.shape[0]
    tr = _row_tile(st)
    rope = rope_tables is not None
    full = lambda shape: pl.BlockSpec(shape, lambda i: (0, 0))
    in_specs = [pl.BlockSpec((tr, ATT_W), lambda i: (i, 0)), full((1, 512)), full((1, 128)),
                full((512, 512)), full((128, 128))]
    args = [proj, jnp.tile(q_norm, ATT_HEADS).reshape(1, 512), jnp.tile(k_norm, ATT_KV).reshape(1, 128),
            _block_diag(512, HEAD_DIM, 1.0 / HEAD_DIM), _block_diag(128, HEAD_DIM, 1.0 / HEAD_DIM)]
    if rope:
        tps = st.t // tr
        in_specs += [pl.BlockSpec((tr, LANE), lambda i: (i % tps, 0))] * 3
        args += list(rope_tables)
    kv_spec = pl.BlockSpec((ATT_KV, tr, LANE), lambda i: (0, i, 0))
    kv_shape = jax.ShapeDtypeStruct((ATT_KV, m, LANE), BF16)
    out_specs = [pl.BlockSpec((tr, ATT_HEADS * LANE), lambda i: (i, 0)), kv_spec, kv_spec]
    out_shape = [jax.ShapeDtypeStruct((m, ATT_HEADS * LANE), BF16), kv_shape, kv_shape]
    if not rope:
        out_specs.append(pl.BlockSpec((tr, 128), lambda i: (i, 0)))
        out_shape.append(jax.ShapeDtypeStruct((m, 128), F32))
    return pl.pallas_call(
        functools.partial(_att_prep_kernel, rope=rope),
        grid=(m // tr,),
        in_specs=in_specs,
        out_specs=out_specs,
        out_shape=out_shape,
        compiler_params=_cp("parallel"),
        name="att_prep",
    )(*args)


def _join_cache(st, own, cache):
    n, p, kv, hd = cache.shape
    c = jnp.transpose(cache, (2, 0, 1, 3))
    pad = jnp.zeros((kv, n, p, LANE - hd), c.dtype).at[..., 0].set(1.0)
    c = _bf(jnp.concatenate([c, pad], axis=-1))
    joined = jnp.concatenate([own.reshape(kv, n, st.t, LANE), c], axis=2)
    return joined.reshape(kv, n * (st.t + p), LANE)


def _rope_tables(t):
    half = HEAD_DIM // 2
    nf = half // 2
    freqs = ROPE_THETA ** (-jnp.arange(nf, dtype=F32) / nf)
    tt = jnp.arange(t)
    ang_r = (tt // GRID_W).astype(F32)[:, None] * freqs[None, :]
    ang_c = (tt % GRID_W).astype(F32)[:, None] * freqs[None, :]
    zero = jnp.zeros_like(ang_r)
    cos = jnp.concatenate([jnp.cos(ang_r)] * 2 + [jnp.cos(ang_c)] * 2, axis=1)
    sa = jnp.concatenate([-jnp.sin(ang_r), zero, -jnp.sin(ang_c), zero], axis=1)
    sb = jnp.concatenate([zero, jnp.sin(ang_r), zero, jnp.sin(ang_c)], axis=1)
    return tuple(jnp.tile(a, (1, LANE // HEAD_DIM)) for a in (cos, sa, sb))


def _key_bound_kernel(k_ref, o_ref):
    ones = jnp.ones((LANE, LANE), BF16)
    for kv in range(ATT_KV):
        k = k_ref[kv].astype(F32)
        k = jnp.where(lax.broadcasted_iota(jnp.int32, k.shape, 1) < HEAD_DIM, k, 0.0)
        best = jnp.max(_dot_data_mask(k * k, ones, 2), axis=0, keepdims=True)
        o_ref[kv] = jnp.broadcast_to(best, (SUBLANE, LANE))


def _key_bound(n, keys_per_seq, k):
    return pl.pallas_call(
        _key_bound_kernel,
        grid=(n,),
        in_specs=[pl.BlockSpec((ATT_KV, keys_per_seq, LANE), lambda b: (0, b, 0))],
        out_specs=pl.BlockSpec((None, ATT_KV, SUBLANE, LANE), lambda b: (b, 0, 0, 0)),
        out_shape=jax.ShapeDtypeStruct((n, ATT_KV, SUBLANE, LANE), F32),
        compiler_params=_cp("parallel"),
        name="key_bound",
    )(k)


def _attn_kernel(q_ref, k_ref, v_ref, kb_ref, o_ref, m_scr, acc_scr, *, tq, n_tiles):
    rows = ATT_GROUP * tq
    lane = lax.broadcasted_iota(jnp.int32, (rows, LANE), 1)
    outs = []
    for kv in range(ATT_KV):
        qs = _rows([q_ref[:, (kv * ATT_GROUP + g) * LANE:(kv * ATT_GROUP + g + 1) * LANE]
                    for g in range(ATT_GROUP)]).astype(F32)

        def key_tile(ref, j):
            return ref[kv, pl.ds(pl.multiple_of(j * KEY_TILE, KEY_TILE), KEY_TILE), :]

        def weighted_sum(q_shift, unroll):
            acc_scr[...] = jnp.zeros((rows, LANE), F32)

            def sum_body(j, carry):
                p = jnp.exp2(_dot_nt(q_shift, key_tile(k_ref, j)))
                acc_scr[...] += _dot(_bf(p), key_tile(v_ref, j))
                return carry

            lax.fori_loop(0, n_tiles, sum_body, 0, unroll=unroll)
            return acc_scr[...]

        k_len = jnp.sqrt(kb_ref[kv][0:1, :])
        unroll = max(u for u in range(1, KEY_UNROLL + 1) if n_tiles % u == 0)
        acc = weighted_sum(_bf(qs * jnp.where(lane == HEAD_DIM, -k_len, 1.0)), unroll)

        def exact_shift():
            m_scr[...] = jnp.full((rows, LANE), -jnp.inf, F32)
            q0 = _bf(jnp.where(lane == HEAD_DIM, 0.0, qs))

            def max_body(j, carry):
                sc = _dot_nt(q0, key_tile(k_ref, j))
                m_scr[...] = jnp.maximum(m_scr[...], jnp.maximum(sc[:, :LANE], sc[:, LANE:]))
                return carry

            lax.fori_loop(0, n_tiles, max_body, 0)
            row_max = jnp.max(m_scr[...], axis=-1, keepdims=True)
            return weighted_sum(_bf(jnp.where(lane == HEAD_DIM, -row_max, qs)), 1)

        row_sum_ok = jnp.min(acc[:, HEAD_DIM:HEAD_DIM + 1]) >= ATT_MIN_ROW_SUM
        acc = lax.cond(row_sum_ok, lambda: acc, exact_shift)
        o = acc[:, :HEAD_DIM] / acc[:, HEAD_DIM:HEAD_DIM + 1]
        outs += [o[g * tq:(g + 1) * tq] for g in range(ATT_GROUP)]
    o_ref[...] = _bf(_lanes(outs))


def _attention(st, q, k, v):
    tq = min(ATT_Q_TILE, st.t)
    nq = st.t // tq
    keys = k.shape[1] // st.n
    assert keys % KEY_TILE == 0
    kv_spec = pl.BlockSpec((ATT_KV, keys, LANE), lambda b, i: (0, b, 0))
    rows = ATT_GROUP * tq
    return pl.pallas_call(
        functools.partial(_attn_kernel, tq=tq, n_tiles=keys // KEY_TILE),
        grid=(st.n, nq),
        in_specs=[pl.BlockSpec((tq, ATT_HEADS * LANE), lambda b, i: (b * nq + i, 0)), kv_spec, kv_spec,
                  pl.BlockSpec((None, ATT_KV, SUBLANE, LANE), lambda b, i: (b, 0, 0, 0))],
        out_specs=pl.BlockSpec((tq, 512), lambda b, i: (b * nq + i, 0)),
        out_shape=jax.ShapeDtypeStruct((st.rows, 512), BF16),
        scratch_shapes=[pltpu.VMEM((rows, LANE), F32), pltpu.VMEM((rows, LANE), F32)],
        compiler_params=_cp("parallel", "parallel"),
        name="attention",
    )(q, k, v, _key_bound(st.n, keys, k))


def _halo_specs(st, width):
    tr = _row_tile(st)
    per8 = tr // SUBLANE
    last8 = st.rows // SUBLANE - 1
    return [pl.BlockSpec((tr, width), lambda i: (i, 0)),
            pl.BlockSpec((SUBLANE, width), lambda i: (jnp.maximum(i * per8 - 1, 0), 0)),
            pl.BlockSpec((SUBLANE, width), lambda i: (jnp.minimum((i + 1) * per8, last8), 0))]


def _neighbours(cur, prev8, next8, tiles_per_seq):
    tr = cur.shape[0]
    j = pl.program_id(0) % tiles_per_seq
    pr = jnp.where(j != 0, prev8[SUBLANE - 1:SUBLANE], 0.0)
    nx = jnp.where(j != tiles_per_seq - 1, next8[0:1], 0.0)
    row = lax.broadcasted_iota(jnp.int32, cur.shape, 0)
    x_prev = jnp.where(row == 0, pr, pltpu.roll(cur, 1, 0))
    x_next = jnp.where(row == tr - 1, nx, pltpu.roll(cur, tr - 1, 0))
    return x_prev, x_next


def _dir_masks(z):
    ri = lax.broadcasted_iota(jnp.int32, (CHUNK, CHUNK), 0)
    ci = lax.broadcasted_iota(jnp.int32, (CHUNK, CHUNK), 1)
    return ((ci <= ri), (ci < ri)) if z == 0 else ((ci >= ri), (ci > ri))


def _tri_blocks(tr):
    i = np.arange(tr)
    same = (i[:, None] // CHUNK) == (i[None, :] // CHUNK)
    lower = same & (i[None, :] <= i[:, None])
    upper = same & (i[None, :] >= i[:, None])
    return _bf(jnp.asarray(lower.astype(np.float32))), _bf(jnp.asarray(upper.astype(np.float32)))


def _scan_blocks(st):
    ncb = min(SCAN_CHUNKS, st.t // CHUNK)
    rb = ncb * CHUNK
    nblk = st.t // rb
    fwd = lambda b, j: (b * nblk + j, 0)
    bwd = lambda b, j: (b * nblk + nblk - 1 - j, 0)
    return ncb, rb, nblk, fwd, bwd


def _chunk_rows(step, z, ncb):
    cc = step if z == 0 else ncb - 1 - step
    return slice(cc * CHUNK, (cc + 1) * CHUNK)


def _ssd_prep_kernel(cur_ref, prev_ref, next_ref, cw_ref, cb_ref, dtb_ref, alog_ref, lo_ref, up_ref,
                     xbc_out, dt_out, cum_out, dtt_out, cumt_out, *, tiles_per_seq):
    cur = cur_ref[...]
    xc = cur[:, 512:1280]
    x_prev, x_next = _neighbours(xc, prev_ref[:, 512:1280], next_ref[:, 512:1280], tiles_per_seq)
    cw = cw_ref[...]
    conv = cb_ref[...] + x_prev * cw[0:1] + xc * cw[1:2] + x_next * cw[2:3]
    xbc_out[...] = _silu(conv)
    dt = _softplus(cur[:, 1280:1408] + dtb_ref[...])
    dt_out[...] = dt
    ld = dt * -jnp.exp(alog_ref[...])
    lane = lax.broadcasted_iota(jnp.int32, ld.shape, 1)
    cum = jnp.where(lane < SSD_HEADS, _dot_mask_data(lo_ref[...], ld, 3), _dot_mask_data(up_ref[...], ld, 3))
    cum_out[...] = cum
    pick = (lax.broadcasted_iota(jnp.int32, (2 * SSD_HEADS, LANE), 0)
            == lax.broadcasted_iota(jnp.int32, (2 * SSD_HEADS, LANE), 1)).astype(BF16)
    dtt_out[...] = _dot_nt_mask_data(pick, dt, 3)
    cumt_out[...] = _dot_nt_mask_data(pick, cum, 3)


def _pad_lanes(v, width=LANE):
    v = v.reshape(1, -1)
    return jnp.pad(v, ((0, 0), (0, width - v.shape[1])))


def _ssd_prep(st, proj, conv_w, conv_b, dt_bias, a_log):
    m = proj.shape[0]
    tr = _row_tile(st)
    full = lambda shape: pl.BlockSpec(shape, lambda i: (0, 0))
    row = lambda w: pl.BlockSpec((tr, w), lambda i: (i, 0))
    col = pl.BlockSpec((2 * SSD_HEADS, tr), lambda i: (0, i))
    lower, upper = _tri_blocks(tr)
    return pl.pallas_call(
        functools.partial(_ssd_prep_kernel, tiles_per_seq=st.t // tr),
        grid=(m // tr,),
        in_specs=_halo_specs(st, SSD_W) + [full((3, 768)), full((1, 768)), full((1, LANE)), full((1, LANE)),
                                           full((tr, tr)), full((tr, tr))],
        out_specs=[row(768), row(LANE), row(LANE), col, col],
        out_shape=[jax.ShapeDtypeStruct((m, 768), F32), jax.ShapeDtypeStruct((m, LANE), F32),
                   jax.ShapeDtypeStruct((m, LANE), F32), jax.ShapeDtypeStruct((2 * SSD_HEADS, m), F32),
                   jax.ShapeDtypeStruct((2 * SSD_HEADS, m), F32)],
        compiler_params=_cp("parallel"),
        name="ssd_prep",
    )(proj, proj, proj, conv_w.T, conv_b.reshape(1, 768), _pad_lanes(dt_bias), _pad_lanes(a_log), lower, upper)


def _ssd_scan_kernel(xf_ref, xb_ref, dtf_ref, dtb_ref, cf_ref, cb_ref, dttf_ref, dttb_ref, ctf_ref, ctb_ref,
                     sel_ref, s0_ref, yf_ref, yb_ref, s_ref, *, ncb):
    @pl.when(pl.program_id(1) == 0)
    def _():
        s_ref[...] = s0_ref[...]

    hpg = SSD_HEADS // SSD_GROUPS
    state = {(z, g): _lanes([s_ref[z, g * hpg + hh] for hh in range(hpg)])
             for z in range(2) for g in range(SSD_GROUPS)}
    refs = ((xf_ref, dtf_ref, cf_ref, dttf_ref, ctf_ref, yf_ref), (xb_ref, dtb_ref, cb_ref, dttb_ref, ctb_ref, yb_ref))
    groups = [(z, g) for z in range(2) for g in range(SSD_GROUPS)]
    heads = [(z, h) for z in range(2) for h in range(SSD_HEADS)]
    for step in range(ncb):
        xbc, dt, cum, dtt, cumt, incl, rows = {}, {}, {}, {}, {}, {}, {}
        for z in range(2):
            x_ref, dt_ref, c_ref, dtt_ref, ct_ref, _ = refs[z]
            rows[z] = _chunk_rows(step, z, ncb)
            xbc[z] = x_ref[rows[z], :]
            dt[z] = _dot_data_mask(dt_ref[rows[z], :], sel_ref[z], 3)
            cum[z] = _dot_data_mask(c_ref[rows[z], :], sel_ref[z], 3)
            dtt[z], cumt[z] = dtt_ref[:, rows[z]], ct_ref[:, rows[z]]
            incl[z] = _dir_masks(z)[0]
        bmat = {(z, g): xbc[z][:, 512 + g * SSD_N:512 + (g + 1) * SSD_N] for z, g in groups}
        cmat = {(z, g): _bf(xbc[z][:, 640 + g * SSD_N:640 + (g + 1) * SSD_N]) for z, g in groups}
        cb = {k: _dot_nt(cmat[k], _bf(bmat[k])) for k in groups}
        cs = {k: _dot(cmat[k], _bf(state[k])) for k in groups}
        gw = hpg * SSD_P
        glast = {z: cum[z][CHUNK - 1:CHUNK] if z == 0 else cum[z][0:1] for z in range(2)}
        xs = {z: xbc[z][:, :SSD_HEADS * SSD_P] for z in range(2)}
        xw = {z: xs[z] * (dt[z] * jnp.exp(glast[z] - cum[z])) for z in range(2)}
        inc = {(z, g): _dot_tn(_bf(bmat[z, g]), _bf(xw[z][:, g * gw:(g + 1) * gw])) for z, g in groups}
        att = {}
        for z, h in heads:
            ln = z * SSD_HEADS + h
            gcol = cum[z][:, h * SSD_P:(h + 1) * SSD_P]
            dec = jnp.exp(jnp.where(incl[z], gcol - cumt[z][ln:ln + 1], -jnp.inf))
            att[z, h] = _bf(cb[z, h // hpg] * dec * dtt[z][ln:ln + 1])
        intra = {(z, h): _dot(att[z, h], _bf(xs[z][:, h * SSD_P:(h + 1) * SSD_P])) for z, h in heads}
        for z in range(2):
            e_in = jnp.exp(cum[z])
            e_last = jnp.exp(glast[z])
            outs = []
            for g in range(SSD_GROUPS):
                hs = range(g * hpg, (g + 1) * hpg)
                outs.append(_lanes([intra[z, h] for h in hs]) + cs[z, g] * e_in[:, g * gw:(g + 1) * gw])
                state[z, g] = state[z, g] * e_last[:, g * gw:(g + 1) * gw] + inc[z, g]
            refs[z][5][rows[z], :] = _bf(_lanes(outs))
    for z in range(2):
        for h in range(SSD_HEADS):
            s_ref[z, h] = state[z, h // hpg][:, (h % hpg) * SSD_P:(h % hpg + 1) * SSD_P]


def _ssd_scan(st, xbc, dt, cum, dtt, cumt, s0):
    ncb, rb, nblk, fwd, bwd = _scan_blocks(st)
    fwd_t = lambda b, j: (0, fwd(b, j)[0])
    bwd_t = lambda b, j: (0, bwd(b, j)[0])
    state_spec = pl.BlockSpec((None, 2, SSD_HEADS, SSD_N, SSD_P), lambda b, j: (b, 0, 0, 0, 0))
    rows = lambda w, m: pl.BlockSpec((rb, w), m)
    cols = lambda m: pl.BlockSpec((2 * SSD_HEADS, rb), m)
    lane = np.arange(LANE)[None, :, None]
    col = np.arange(SSD_HEADS * SSD_P)[None, None, :]
    sel = _bf(jnp.asarray((lane == np.arange(2)[:, None, None] * SSD_HEADS + col // SSD_P).astype(np.float32)))
    return pl.pallas_call(
        functools.partial(_ssd_scan_kernel, ncb=ncb),
        grid=(st.n, nblk),
        in_specs=[rows(768, fwd), rows(768, bwd), rows(LANE, fwd), rows(LANE, bwd), rows(LANE, fwd), rows(LANE, bwd),
                  cols(fwd_t), cols(bwd_t), cols(fwd_t), cols(bwd_t),
                  pl.BlockSpec(sel.shape, lambda b, j: (0, 0, 0)), state_spec],
        out_specs=[rows(512, fwd), rows(512, bwd), state_spec],
        out_shape=[jax.ShapeDtypeStruct((st.rows, 512), BF16), jax.ShapeDtypeStruct((st.rows, 512), BF16),
                   jax.ShapeDtypeStruct((st.n, 2, SSD_HEADS, SSD_N, SSD_P), F32)],
        compiler_params=_cp("parallel", "arbitrary"),
        name="ssd_scan",
    )(xbc, xbc, dt, dt, cum, cum, dtt, dtt, cumt, cumt, sel, s0)


def _ssd_post_kernel(yf_ref, yb_ref, xbc_ref, p_ref, d_ref, g_ref, o_ref):
    y = yf_ref[...].astype(F32) + yb_ref[...].astype(F32) + d_ref[...] * xbc_ref[:, :512]
    y = y * _silu(p_ref[:, :512])
    o_ref[...] = _bf(y * lax.rsqrt(jnp.mean(y * y, axis=-1, keepdims=True) + 1e-6) * g_ref[...])


def _ssd_post(st, y_f, y_b, xbc, proj, ssd_d, ssd_norm):
    m = proj.shape[0]
    tr = _row_tile(st)
    full = lambda shape: pl.BlockSpec(shape, lambda i: (0, 0))
    row = lambda w: pl.BlockSpec((tr, w), lambda i: (i, 0))
    return pl.pallas_call(
        _ssd_post_kernel,
        grid=(m // tr,),
        in_specs=[row(512), row(512), row(512), row(512), full((1, 512)), full((1, 512))],
        out_specs=row(512),
        out_shape=jax.ShapeDtypeStruct((m, 512), BF16),
        compiler_params=_cp("parallel"),
        name="ssd_post",
    )(y_f, y_b, xbc, proj, jnp.repeat(ssd_d, SSD_P).reshape(1, 512), ssd_norm.reshape(1, 512))


GLA_QK = GLA_HEADS * GLA_DK
GLA_SAFE_RANGE = 60.0


def _gla_prep_kernel(p_ref, g2_ref, gb_ref, lo_ref, up_ref, cum_out):
    logit = _dot_split(p_ref[...], g2_ref[...]) + gb_ref[...]
    log_a = -_softplus(-logit) * (1.0 / GLA_GATE_NORM)
    cum_out[:, :GLA_QK] = _dot_mask_data(lo_ref[...], log_a[:, :GLA_QK], 3)
    cum_out[:, GLA_QK:] = _dot_mask_data(up_ref[...], log_a[:, GLA_QK:], 3)


def _gla_prep(st, proj, g2p, gb):
    m = proj.shape[0]
    tr = _row_tile(st)
    full = lambda shape: pl.BlockSpec(shape, lambda i: (0, 0))
    lower, upper = _tri_blocks(tr)
    return pl.pallas_call(
        _gla_prep_kernel,
        grid=(m // tr,),
        in_specs=[pl.BlockSpec((tr, LANE), lambda i: (i, (GLA_W - LANE) // LANE)), full((LANE, 2 * GLA_QK)),
                  full((1, 2 * GLA_QK)),
                  full((tr, tr)), full((tr, tr))],
        out_specs=pl.BlockSpec((tr, 2 * GLA_QK), lambda i: (i, 0)),
        out_shape=jax.ShapeDtypeStruct((m, 2 * GLA_QK), F32),
        compiler_params=_cp("parallel"),
        name="gla_prep",
    )(proj, g2p, gb, lower, upper)


def _gla_intra_exact(p_ref, c_ref, rows, z, seg_ref):
    r0 = rows.start
    q = p_ref[rows, 0:GLA_QK] * (GLA_DK ** -0.5)
    cum = c_ref[rows, :]
    row = lax.broadcasted_iota(jnp.int32, (CHUNK, 1), 0)
    seg = seg_ref[...]

    def body(j, acc):
        kj = p_ref[pl.ds(r0 + j, 1), GLA_QK:2 * GLA_QK]
        vj = p_ref[pl.ds(r0 + j, 1), 2 * GLA_QK:2 * GLA_QK + GLA_HEADS * GLA_DV]
        seen = (row >= j) if z == 0 else (row <= j)
        w = jnp.where(seen, q * kj * jnp.exp(jnp.minimum(cum - c_ref[pl.ds(r0 + j, 1), :], 0.0)), 0.0)
        score = _dot_data_mask(w, seg, 3)
        return tuple(a + score[:, h:h + 1] * vj[:, h * GLA_DV:(h + 1) * GLA_DV] for h, a in enumerate(acc))

    zero = jnp.zeros((CHUNK, GLA_DV), F32)
    return list(lax.fori_loop(0, CHUNK, body, (zero,) * GLA_HEADS))


def _gla_scan_kernel(pf_ref, pb_ref, cf_ref, cb_ref, seg_ref, s0_ref, yf_ref, yb_ref, s_ref, *, ncb):
    @pl.when(pl.program_id(1) == 0)
    def _():
        s_ref[...] = s0_ref[...]

    refs = ((pf_ref, cf_ref, yf_ref), (pb_ref, cb_ref, yb_ref))
    heads = [(z, h) for z in range(2) for h in range(GLA_HEADS)]
    hsl = [slice(h * GLA_DK, (h + 1) * GLA_DK) for h in range(GLA_HEADS)]
    state = {(z, h): s_ref[z, h] for z, h in heads}
    chunks = [(step, z) for step in range(ncb) for z in range(2)]
    rows = {(step, z): _chunk_rows(step, z, ncb) for step, z in chunks}
    cum = {k: refs[k[1]][1][rows[k], :] for k in chunks}
    mid = {k: cum[k][CHUNK // 2:CHUNK // 2 + 1] for k in chunks}
    span = functools.reduce(jnp.maximum, [jnp.max(jnp.abs(cum[k] - mid[k])) for k in chunks])

    def v_of(k, h):
        return _bf(refs[k[1]][0][rows[k], 2 * GLA_QK + h * GLA_DV:2 * GLA_QK + (h + 1) * GLA_DV])

    def intra_factored():
        att = {}
        for k in chunks:
            p_ref = refs[k[1]][0]
            q_mid = _bf(p_ref[rows[k], 0:GLA_QK] * (GLA_DK ** -0.5) * jnp.exp(cum[k] - mid[k]))
            k_mid = _bf(p_ref[rows[k], GLA_QK:2 * GLA_QK] * jnp.exp(mid[k] - cum[k]))
            incl = _dir_masks(k[1])[0]
            for h in range(GLA_HEADS):
                att[k, h] = _bf(jnp.where(incl, _dot_nt(q_mid[:, hsl[h]], k_mid[:, hsl[h]]), 0.0))
        return [_dot(att[k, h], v_of(k, h)) for k in chunks for h in range(GLA_HEADS)]

    def intra_exact():
        out = []
        for k in chunks:
            out += _gla_intra_exact(refs[k[1]][0], refs[k[1]][1], rows[k], k[1], seg_ref)
        return out

    intra = lax.cond(span <= GLA_SAFE_RANGE, intra_factored, intra_exact)
    intra = {(k, h): intra[i * GLA_HEADS + h] for i, k in enumerate(chunks) for h in range(GLA_HEADS)}

    for step in range(ncb):
        q_in, k_out, e_last = {}, {}, {}
        for z in range(2):
            k = (step, z)
            p_ref = refs[z][0]
            glast = cum[k][CHUNK - 1:CHUNK] if z == 0 else cum[k][0:1]
            q_in[z] = _bf(p_ref[rows[k], 0:GLA_QK] * (GLA_DK ** -0.5) * jnp.exp(cum[k]))
            k_out[z] = _bf(p_ref[rows[k], GLA_QK:2 * GLA_QK] * jnp.exp(glast - cum[k]))
            e_last[z] = jnp.exp(glast)
        inter = {(z, h): _dot_nt(q_in[z][:, hsl[h]], _bf(state[z, h])) for z, h in heads}
        inc = {(z, h): _dot_tn(v_of((step, z), h), k_out[z][:, hsl[h]]) for z, h in heads}
        for z in range(2):
            refs[z][2][rows[step, z], :] = _bf(_lanes([intra[(step, z), h] + inter[z, h] for h in range(GLA_HEADS)]))
            for h in range(GLA_HEADS):
                state[z, h] = state[z, h] * e_last[z][:, hsl[h]] + inc[z, h]
    for z, h in heads:
        s_ref[z, h] = state[z, h]


def _gla_scan(st, proj, cum, s0_t):
    ncb, rb, nblk, fwd, bwd = _scan_blocks(st)
    seg = _block_diag(GLA_QK, GLA_DK, 1.0)[:, ::GLA_DK]
    seg = jnp.pad(seg, ((0, 0), (0, LANE - GLA_HEADS)))
    state_spec = pl.BlockSpec((None, 2, GLA_HEADS, GLA_DV, GLA_DK), lambda b, j: (b, 0, 0, 0, 0))
    qkv_w = 2 * GLA_QK + GLA_HEADS * GLA_DV
    return pl.pallas_call(
        functools.partial(_gla_scan_kernel, ncb=ncb),
        grid=(st.n, nblk),
        in_specs=[pl.BlockSpec((rb, qkv_w), fwd), pl.BlockSpec((rb, qkv_w), bwd),
                  pl.BlockSpec((rb, GLA_QK), fwd), pl.BlockSpec((rb, GLA_QK), lambda b, j: (bwd(b, j)[0], 1)),
                  pl.BlockSpec((GLA_QK, LANE), lambda b, j: (0, 0)), state_spec],
        out_specs=[pl.BlockSpec((rb, 512), fwd), pl.BlockSpec((rb, 512), bwd), state_spec],
        out_shape=[jax.ShapeDtypeStruct((st.rows, 512), BF16), jax.ShapeDtypeStruct((st.rows, 512), BF16),
                   jax.ShapeDtypeStruct((st.n, 2, GLA_HEADS, GLA_DV, GLA_DK), F32)],
        compiler_params=_cp("parallel", "arbitrary"),
        name="gla_scan",
    )(proj, proj, cum, cum, seg, s0_t)


def _gla_post_kernel(yf_ref, yb_ref, p_ref, g_ref, o_ref):
    o = yf_ref[...].astype(F32) + yb_ref[...].astype(F32)
    gate = _silu(p_ref[...])
    outs = []
    for h in range(GLA_HEADS):
        oh = o[:, h * GLA_DV:(h + 1) * GLA_DV]
        outs.append(oh * lax.rsqrt(jnp.mean(oh * oh, axis=-1, keepdims=True) + 1e-6) * g_ref[...])
    o_ref[...] = _bf(_lanes(outs) * gate)


def _gla_post(st, y_f, y_b, proj, gla_norm):
    m = proj.shape[0]
    tr = _row_tile(st)
    return pl.pallas_call(
        _gla_post_kernel,
        grid=(m // tr,),
        in_specs=[pl.BlockSpec((tr, 512), lambda i: (i, 0)), pl.BlockSpec((tr, 512), lambda i: (i, 0)),
                  pl.BlockSpec((tr, 512), lambda i: (i, 2)), pl.BlockSpec((1, GLA_DV), lambda i: (0, 0))],
        out_specs=pl.BlockSpec((tr, 512), lambda i: (i, 0)),
        out_shape=jax.ShapeDtypeStruct((m, 512), BF16),
        compiler_params=_cp("parallel"),
        name="gla_post",
    )(y_f, y_b, proj, gla_norm.reshape(1, GLA_DV))


def _rwkv_prep_kernel(cur_ref, prev_ref, next_ref, mu_ref, w2_ref, w0_ref, a2_ref, a0_ref, g2_ref,
                      kkw_ref, ka_ref, rk_ref, bd_ref, in_out, lw_out, post_out, *, tiles_per_seq):
    cur = cur_ref[...]
    x_prev, x_next = _neighbours(cur, prev_ref[...], next_ref[...], tiles_per_seq)
    blk = cur + (0.5 * (x_prev + x_next) - cur) * mu_ref[...]
    r, k, v = blk[:, 0:512], blk[:, 512:1024], blk[:, 1024:1536]
    w_logit = w0_ref[...] + _dot_split(jnp.tanh(blk[:, 1536:1600]), w2_ref[...])
    lw_out[...] = -RWKV_DECAY_SCALE * jax.nn.sigmoid(w_logit)
    a = jax.nn.sigmoid(a0_ref[...] + _dot_split(blk[:, 1600:1664], a2_ref[...]))
    g = _dot_split(jax.nn.sigmoid(blk[:, 1664:1792]), g2_ref[...])
    bd = bd_ref[...]
    kk = k * kkw_ref[...]
    kk = kk * lax.rsqrt(_dot_data_mask(kk * kk, bd, 2) + 1e-12)
    k2 = k * (1.0 + (a - 1.0) * ka_ref[...])
    in_out[:, 0:512] = _bf(r)
    in_out[:, 512:1024] = _bf(k2)
    in_out[:, 1024:1536] = _bf(v)
    in_out[:, 1536:2048] = _bf(-kk)
    in_out[:, 2048:2560] = _bf(kk * a)
    post_out[:, 0:512] = _bf(g)
    post_out[:, 512:1024] = _bf(_dot_data_mask(r * k2 * rk_ref[...], bd, 2) * v)


def _rwkv_prep(st, proj, mu, w2, w0, a2, a0, g2, kkw, ka, rk):
    m = proj.shape[0]
    tr = _row_tile(st)
    full = lambda shape: pl.BlockSpec(shape, lambda i: (0, 0))
    row = lambda w: pl.BlockSpec((tr, w), lambda i: (i, 0))
    vec = lambda a: a.reshape(1, -1)
    return pl.pallas_call(
        functools.partial(_rwkv_prep_kernel, tiles_per_seq=st.t // tr),
        grid=(m // tr,),
        in_specs=_halo_specs(st, RWKV_W) + [full((1, RWKV_W)), full((64, 1024)), full((1, 1024)),
                                            full((64, 512)), full((1, 512)), full((128, 512)),
                                            full((1, 512)), full((1, 512)), full((1, 512)), full((512, 512))],
        out_specs=[row(2560), row(1024), row(1024)],
        out_shape=[jax.ShapeDtypeStruct((m, 2560), BF16), jax.ShapeDtypeStruct((m, 1024), F32),
                   jax.ShapeDtypeStruct((m, 1024), BF16)],
        compiler_params=_cp("parallel"),
        name="rwkv_prep",
    )(proj, proj, proj, vec(mu), jnp.concatenate([w2[0], w2[1]], axis=1), vec(w0), a2, vec(a0), g2,
      vec(kkw), vec(ka), vec(rk), _block_diag(512, RWKV_HD, 1.0))


def _rwkv_chunk(x_ref, lw_ref, rows, z):
    incl, strict = _dir_masks(z)
    lw = lw_ref[rows, :]
    cum = _dot_mask_data(incl.astype(BF16), lw, 3)
    cum_x = cum - lw
    mid = cum[CHUNK // 2:CHUNK // 2 + 1]
    glast = cum[CHUNK - 1:CHUNK] if z == 0 else cum[0:1]
    x = x_ref[rows, :]
    r, k, v, al, be = x[:, 0:512], x[:, 512:1024], x[:, 1024:1536], x[:, 1536:2048], x[:, 2048:2560]
    e_mid, e_nmid = jnp.exp(cum - mid), jnp.exp(mid - cum)
    e_out = jnp.exp(glast - cum)
    return dict(incl=incl, strict=strict, v=v,
                r_mid=_bf(r * e_mid), a_mid=_bf(al * jnp.exp(cum_x - mid)),
                b_mid=_bf(be * e_nmid), k_mid=_bf(k * e_nmid),
                r_in=_bf(r * jnp.exp(cum)), a_in=al * jnp.exp(cum_x),
                b_out=_bf(be * e_out), k_out=_bf(k * e_out), e_last=jnp.exp(glast))


_HEAD_SLICES = [slice(h * RWKV_HD, (h + 1) * RWKV_HD) for h in range(RWKV_HEADS)]


def _rwkv_state_free(chunks, hooks):
    hooks = list(hooks)

    def run_hook():
        if hooks:
            hooks.pop(0)()

    chains = [(c, sl) for c in chunks for sl in _HEAD_SLICES]
    run_hook()
    pair = [_dot_nt(_rows([c['a_mid'][:, sl], c['r_mid'][:, sl]]), _rows([c['b_mid'][:, sl], c['k_mid'][:, sl]]))
            for c, sl in chains]
    half = CHUNK
    a_ab = [_bf(jnp.where(c['strict'], p[:half, :half], 0.0)) for (c, _), p in zip(chains, pair)]
    a_ak = [_bf(jnp.where(c['strict'], p[:half, half:], 0.0)) for (c, _), p in zip(chains, pair)]
    a_rb = [_bf(jnp.where(c['incl'], p[half:, :half], 0.0)) for (c, _), p in zip(chains, pair)]
    a_rk = [_bf(jnp.where(c['incl'], p[half:, half:], 0.0)) for (c, _), p in zip(chains, pair)]
    vb = [_bf(c['v'][:, sl]) for c, sl in chains]
    both = [_dot(_rows([ak, rk]), v) for ak, rk, v in zip(a_ak, a_rk, vb)]
    av = [x[:half] for x in both]
    o0 = [x[half:] for x in both]
    sol = [_lanes([c['a_in'][:, sl], x]) for (c, sl), x in zip(chains, av)]
    powr = a_ab
    width = 2 * RWKV_HD
    for it in range(6):
        if it < 5:
            both = [_dot(p, _lanes([_bf(s), p])) for p, s in zip(powr, sol)]
            sol = [s + x[:, :width] for s, x in zip(sol, both)]
            powr = [_bf(x[:, width:]) for x in both]
        else:
            sol = [s + _dot(p, _bf(s)) for p, s in zip(powr, sol)]
        if it in (1, 3):
            run_hook()
    while hooks:
        run_hook()
    return [dict(w=_bf(s[:, :RWKV_HD]), u0=s[:, RWKV_HD:], o0=o, a_rb=a, v=c['v'][:, sl], r_in=c['r_in'][:, sl],
                 b_out=c['b_out'][:, sl], k_out=c['k_out'][:, sl], e_last=c['e_last'][:, sl])
            for (c, sl), s, o, a in zip(chains, sol, o0, a_rb)]


def _rwkv_state_stages(res, state, write_out):
    box = {}

    def read_state():
        box['su'] = [_dot_nt(_rows([c['w'], c['r_in']]), _bf(s)) for c, s in zip(res, state)]

    def update_state():
        box['u'] = [c['u0'] + su[:CHUNK] for c, su in zip(res, box['su'])]
        state[:] = [s * c['e_last'] + _dot_tn(_rows([_bf(u), _bf(c['v'])]), _rows([c['b_out'], c['k_out']]))
                    for c, s, u in zip(res, state, box['u'])]

    def emit():
        write_out([su[CHUNK:] + _dot(c['a_rb'], _bf(u)) + c['o0'] for c, su, u in zip(res, box['su'], box['u'])])

    return [read_state, update_state, emit]


def _rwkv_scan_kernel(xf_ref, xb_ref, lwf_ref, lwb_ref, s0_ref, yf_ref, yb_ref, s_ref, *, ncb):
    @pl.when(pl.program_id(1) == 0)
    def _():
        s_ref[...] = s0_ref[...]

    state = [s_ref[z, h] for z in range(2) for h in range(RWKV_HEADS)]

    def chunk_rows(step, z):
        cc = step if z == 0 else ncb - 1 - step
        return slice(cc * CHUNK, (cc + 1) * CHUNK)

    def pair_inputs(step):
        return [_rwkv_chunk(xf_ref, lwf_ref, chunk_rows(step, 0), 0),
                _rwkv_chunk(xb_ref, lwb_ref, chunk_rows(step, 1), 1)]

    def writer(step):
        def write_out(outs):
            yf_ref[chunk_rows(step, 0), :] = _bf(_lanes(outs[:RWKV_HEADS]))
            yb_ref[chunk_rows(step, 1), :] = _bf(_lanes(outs[RWKV_HEADS:]))
        return write_out

    res = _rwkv_state_free(pair_inputs(0), [])
    for step in range(ncb):
        hooks = _rwkv_state_stages(res, state, writer(step))
        if step + 1 < ncb:
            res = _rwkv_state_free(pair_inputs(step + 1), hooks)
        else:
            for hook in hooks:
                hook()
    for z in range(2):
        for h in range(RWKV_HEADS):
            s_ref[z, h] = state[z * RWKV_HEADS + h]


def _rwkv_scan(st, xin, lw, s0):
    ncb, rb, nblk, fwd, bwd = _scan_blocks(st)
    state_spec = pl.BlockSpec((None, 2, RWKV_HEADS, RWKV_HD, RWKV_HD), lambda b, j: (b, 0, 0, 0, 0))
    return pl.pallas_call(
        functools.partial(_rwkv_scan_kernel, ncb=ncb),
        grid=(st.n, nblk),
        in_specs=[pl.BlockSpec((rb, 2560), fwd), pl.BlockSpec((rb, 2560), bwd),
                  pl.BlockSpec((rb, 512), fwd), pl.BlockSpec((rb, 512), lambda b, j: (bwd(b, j)[0], 1)),
                  state_spec],
        out_specs=[pl.BlockSpec((rb, 512), fwd), pl.BlockSpec((rb, 512), bwd), state_spec],
        out_shape=[jax.ShapeDtypeStruct((st.rows, 512), BF16), jax.ShapeDtypeStruct((st.rows, 512), BF16),
                   jax.ShapeDtypeStruct((st.n, 2, RWKV_HEADS, RWKV_HD, RWKV_HD), F32)],
        compiler_params=_cp("parallel", "arbitrary"),
        name="rwkv_scan",
    )(xin, xin, lw, lw, s0)


def _rwkv_post_kernel(yf_ref, yb_ref, post_ref, lng_ref, lnb_ref, bd_ref, o_ref):
    o = yf_ref[...].astype(F32) + yb_ref[...].astype(F32)
    bd = bd_ref[...]
    d = o - _dot_data_mask(o, bd, 2)
    o = d * lax.rsqrt(_dot_data_mask(d * d, bd, 2) + RWKV_LN_EPS) * lng_ref[...] + lnb_ref[...]
    o_ref[...] = _bf((o + post_ref[:, 512:1024].astype(F32)) * post_ref[:, 0:512].astype(F32))


def _rwkv_post(st, y_f, y_b, post, ln_g, ln_b):
    m = post.shape[0]
    tr = _row_tile(st)
    full = lambda shape: pl.BlockSpec(shape, lambda i: (0, 0))
    row = lambda w: pl.BlockSpec((tr, w), lambda i: (i, 0))
    return pl.pallas_call(
        _rwkv_post_kernel,
        grid=(m // tr,),
        in_specs=[row(512), row(512), row(1024), full((1, 512)), full((1, 512)), full((512, 512))],
        out_specs=row(512),
        out_shape=jax.ShapeDtypeStruct((m, 512), BF16),
        compiler_params=_cp("parallel"),
        name="rwkv_post",
    )(y_f, y_b, post, ln_g.reshape(1, 512), ln_b.reshape(1, 512), _block_diag(512, RWKV_HD, 1.0 / RWKV_HD))


def _merge_kernel(h_ref, o0_ref, o1_ref, o2_ref, o3_ref, wg_ref, wb_ref, out_ref):
    h = h_ref[...]
    acc = None
    for i, o_ref in enumerate((o0_ref, o1_ref, o2_ref, o3_ref)):
        term = jax.nn.sigmoid(_dot(h, wg_ref[i])) * _dot(o_ref[...], wb_ref[i])
        acc = term if acc is None else acc + term
    out_ref[...] = _bf(acc)


def _merge(st, h, branch_outs, wg, wb):
    m, d = h.shape
    tm = _mm_tile(st)
    tn = 512
    row = lambda w: pl.BlockSpec((tm, w), lambda j, i: (i, 0))
    return pl.pallas_call(
        _merge_kernel,
        grid=(d // tn, m // tm),
        in_specs=[row(d)] + [row(BRANCH_W)] * 4 + [pl.BlockSpec((4, d, tn), lambda j, i: (0, 0, j)),
                                                    pl.BlockSpec((4, BRANCH_W, tn), lambda j, i: (0, 0, j))],
        out_specs=pl.BlockSpec((tm, tn), lambda j, i: (i, j)),
        out_shape=jax.ShapeDtypeStruct((m, d), BF16),
        compiler_params=_cp("parallel", "parallel"),
        name="merge",
    )(h, *branch_outs, wg, wb)


def _wo_kernel(m_ref, w_ref, x_ref, mod_ref, g_ref, o_ref, h_out):
    x = x_ref[...] + mod_ref[0, 2:3, :] * _dot(m_ref[...], w_ref[...])
    o_ref[...] = x
    h_out[...] = _norm_mod(x, g_ref[...], mod_ref, 4, 3)


def _out_proj(st, merged, w_o, x, mod_l, gain2):
    m, d = x.shape
    tm = _mm_tile(st)
    row = pl.BlockSpec((tm, d), lambda i: (i, 0))
    return pl.pallas_call(
        _wo_kernel,
        grid=(m // tm,),
        in_specs=[row, pl.BlockSpec((d, d), lambda i: (0, 0)), row, pl.BlockSpec((1, 6, d), st.group_map(tm)),
                  pl.BlockSpec((1, d), lambda i: (0, 0))],
        out_specs=[row, row],
        out_shape=[jax.ShapeDtypeStruct((m, d), F32), jax.ShapeDtypeStruct((m, d), BF16)],
        compiler_params=_cp("parallel"),
        name="out_proj",
    )(merged, w_o, x, mod_l, gain2.reshape(1, d))


def _ffn_kernel(h_ref, w1_ref, w3_ref, w2_ref, x_ref, mod_ref, fg_ref, o_ref, acc_ref, *, final):
    f = pl.program_id(1)

    @pl.when(f == 0)
    def _():
        acc_ref[...] = jnp.zeros_like(acc_ref)

    h = h_ref[...]
    u = _silu(_dot(h, w1_ref[...])) * _dot(h, w3_ref[...])
    acc_ref[...] += _dot(_bf(u), w2_ref[...])

    @pl.when(f == pl.num_programs(1) - 1)
    def _():
        x = x_ref[...] + mod_ref[0, 5:6, :] * acc_ref[...]
        o_ref[...] = _rms(x, fg_ref[...]) if final else x


def _ffn(st, h, w1, w3, w2, x, mod_l, final_gain, final):
    m, d = x.shape
    dff = w1.shape[1]
    tm = _mm_tile(st)
    tf = 512
    return pl.pallas_call(
        functools.partial(_ffn_kernel, final=final),
        grid=(m // tm, dff // tf),
        in_specs=[pl.BlockSpec((tm, d), lambda i, f: (i, 0)),
                  pl.BlockSpec((d, tf), lambda i, f: (0, f)),
                  pl.BlockSpec((d, tf), lambda i, f: (0, f)),
                  pl.BlockSpec((tf, d), lambda i, f: (f, 0)),
                  pl.BlockSpec((tm, d), lambda i, f: (i, 0)),
                  pl.BlockSpec((1, 6, d), lambda i, f: st.group_map(tm)(i)),
                  pl.BlockSpec((1, d), lambda i, f: (0, 0))],
        out_specs=pl.BlockSpec((tm, d), lambda i, f: (i, 0)),
        out_shape=jax.ShapeDtypeStruct((m, d), F32),
        scratch_shapes=[pltpu.VMEM((tm, d), F32)],
        compiler_params=_cp("parallel", "arbitrary"),
        name="ffn",
    )(h, w1, w3, w2, x, mod_l, final_gain.reshape(1, d))


def _split_w_in(w_in):
    z = lambda w: jnp.zeros(w_in.shape[:2] + (w,), w_in.dtype)
    att = w_in[..., 0:768]
    ssd = jnp.concatenate([w_in[..., 768:2064], z(SSD_W - 1296)], axis=-1)
    rwkv = w_in[..., 2064:3856]
    gla = jnp.concatenate([w_in[..., 3856:4880], w_in[..., 4896:5408], w_in[..., 4880:4896], z(LANE - 16)], axis=-1)
    return tuple(_bf(w) for w in (att, ssd, rwkv, gla))


def _block(st, x, mod_l, p, rope_tables, ctx, final_gain, final):
    h, proj_att, proj_ssd = _in_proj_a(st, x, p['norm1'], mod_l, p['w_att'], p['w_ssd'])
    proj_rwkv, proj_gla = _in_proj_b(st, h, p['w_rwkv'], p['w_gla'])

    v = proj_att[:, 640:768]
    if ctx is None:
        q, k_att, v_att, k = _att_prep(st, proj_att, p['q_norm'], p['k_norm'], None)
        o_att = _attention(st, q, k_att, v_att)
        s_ssd = jnp.zeros((st.n, 2, SSD_HEADS, SSD_N, SSD_P), F32)
        s_rwkv = jnp.zeros((st.n, 2, RWKV_HEADS, RWKV_HD, RWKV_HD), F32)
        s_gla_t = jnp.zeros((st.n, 2, GLA_HEADS, GLA_DV, GLA_DK), F32)
    else:
        ctx_k, ctx_v, s_ssd, s_rwkv, s_gla = ctx
        q, k_att, v_att = _att_prep(st, proj_att, p['q_norm'], p['k_norm'], rope_tables)
        k = None
        o_att = _attention(st, q, _join_cache(st, k_att, ctx_k), _join_cache(st, v_att, ctx_v))
        s_gla_t = jnp.swapaxes(s_gla, -1, -2)

    xbc, dt, cum, dtt, cumt = _ssd_prep(st, proj_ssd, p['ssd_conv_w'], p['ssd_conv_b'], p['ssd_dt_bias'],
                                        p['ssd_a_log'])
    y_ssd_f, y_ssd_b, new_ssd = _ssd_scan(st, xbc, dt, cum, dtt, cumt, s_ssd)
    o_ssd = _ssd_post(st, y_ssd_f, y_ssd_b, xbc, proj_ssd, p['ssd_d'], p['ssd_norm'])

    rin, lw, rpost = _rwkv_prep(st, proj_rwkv, p['rwkv_mu'], p['rwkv_w2'], p['rwkv_w0'], p['rwkv_a2'],
                                p['rwkv_a0'], p['rwkv_g2'], p['rwkv_kk'], p['rwkv_ka'], p['rwkv_rk'])
    y_rwkv_f, y_rwkv_b, new_rwkv = _rwkv_scan(st, rin, lw, s_rwkv)
    o_rwkv = _rwkv_post(st, y_rwkv_f, y_rwkv_b, rpost, p['rwkv_ln_g'], p['rwkv_ln_b'])

    g2 = jnp.concatenate([p['gla_g2'][0], p['gla_g2'][1]], axis=1)
    g2p = jnp.pad(g2, ((0, LANE - g2.shape[0]), (0, 0)))
    gla_cum = _gla_prep(st, proj_gla, g2p, p['gla_gb'].reshape(1, 2 * GLA_QK))
    y_gla_f, y_gla_b, new_gla_t = _gla_scan(st, proj_gla, gla_cum, s_gla_t)
    o_gla = _gla_post(st, y_gla_f, y_gla_b, proj_gla, p['gla_norm'])

    merged = _merge(st, h, (o_att, o_ssd, o_rwkv, o_gla), p['w_gate'], p['w_branch'])
    x, h2 = _out_proj(st, merged, p['w_o'], x, mod_l, p['norm2'])
    x = _ffn(st, h2, p['ffn_w1'], p['ffn_w3'], p['ffn_w2'], x, mod_l, final_gain, final)
    return x, (k, v, new_ssd, new_rwkv, jnp.swapaxes(new_gla_t, -1, -2))


def kernel(x_prompt, x_sample, cache_attn_k, cache_attn_v, state_ssd, state_rwkv, state_gla, c, c_ctx, w_mod, b_mod, norm1, norm2, w_in, q_norm, k_norm, ssd_conv_w, ssd_conv_b, ssd_dt_bias, ssd_a_log, ssd_d, ssd_norm, rwkv_mu, rwkv_w0, rwkv_w2, rwkv_a0, rwkv_a2, rwkv_g2, rwkv_kk, rwkv_ka, rwkv_rk, rwkv_ln_g, rwkv_ln_b, gla_g2, gla_gb, gla_norm, w_gate, w_branch, w_o, ffn_w1, ffn_w3, ffn_w2, final_norm):
    nb, seq, d = x_prompt.shape
    db, dseq, _ = x_sample.shape
    depth = w_in.shape[0]
    assert d == D_MODEL and seq % CHUNK == 0 and dseq % CHUNK == 0 and 1 + db <= MOD_ROWS
    ctx_st = _Stream(nb, seq, 0, False)
    lat_st = _Stream(db, dseq, 1, True)

    cond = jnp.concatenate([c_ctx[None], c, jnp.zeros((MOD_ROWS - 1 - db, d), F32)], axis=0)
    mod = _modulation(cond, w_mod, b_mod)

    w_att, w_ssd, w_rwkv, w_gla = _split_w_in(w_in)
    w_gate_b, w_branch_b, w_o_b = _bf(w_gate), _bf(w_branch), _bf(w_o)
    w1_b, w3_b, w2_b = _bf(ffn_w1), _bf(ffn_w3), _bf(ffn_w2)

    def params_at(l):
        return dict(norm1=norm1[l], norm2=norm2[l], w_att=w_att[l], w_ssd=w_ssd[l], w_rwkv=w_rwkv[l],
                    w_gla=w_gla[l], q_norm=q_norm[l], k_norm=k_norm[l],
                    ssd_conv_w=ssd_conv_w[l], ssd_conv_b=ssd_conv_b[l], ssd_dt_bias=ssd_dt_bias[l],
                    ssd_a_log=ssd_a_log[l], ssd_d=ssd_d[l], ssd_norm=ssd_norm[l],
                    rwkv_mu=rwkv_mu[l], rwkv_w0=rwkv_w0[l], rwkv_w2=rwkv_w2[l], rwkv_a0=rwkv_a0[l],
                    rwkv_a2=rwkv_a2[l], rwkv_g2=rwkv_g2[l], rwkv_kk=rwkv_kk[l], rwkv_ka=rwkv_ka[l],
                    rwkv_rk=rwkv_rk[l], rwkv_ln_g=rwkv_ln_g[l], rwkv_ln_b=rwkv_ln_b[l],
                    gla_g2=gla_g2[l], gla_gb=gla_gb[l], gla_norm=gla_norm[l],
                    w_gate=w_gate_b[l], w_branch=w_branch_b[l], w_o=w_o_b[l],
                    ffn_w1=w1_b[l], ffn_w3=w3_b[l], ffn_w2=w2_b[l])

    xp = x_prompt.reshape(nb * seq, d)
    new_k, new_v, new_ssd, new_rwkv, new_gla = [], [], [], [], []
    for l in range(depth):
        xp, (k_l, v_l, ssd_l, rwkv_l, gla_l) = _block(ctx_st, xp, mod[l], params_at(l), None, None, final_norm,
                                                      l == depth - 1)
        new_k.append(k_l.reshape(nb, seq, ATT_KV, HEAD_DIM))
        new_v.append(v_l.reshape(nb, seq, ATT_KV, HEAD_DIM))
        new_ssd.append(ssd_l)
        new_rwkv.append(rwkv_l)
        new_gla.append(gla_l)

    rope_tables = _rope_tables(dseq)
    xs = x_sample.reshape(db * dseq, d)
    for l in range(depth):
        ctx = (cache_attn_k[:, l], cache_attn_v[:, l], state_ssd[:, l], state_rwkv[:, l], state_gla[:, l])
        xs, _ = _block(lat_st, xs, mod[l], params_at(l), rope_tables, ctx, final_norm, l == depth - 1)

    y_prompt = xp.reshape(nb, seq, d)
    y_sample = xs.reshape(db, dseq, d)
    return (y_prompt, y_sample, jnp.stack(new_k, axis=1), jnp.stack(new_v, axis=1),
            jnp.stack(new_ssd, axis=1), jnp.stack(new_rwkv, axis=1), jnp.stack(new_gla, axis=1))
```

```python
import functools

import jax
import jax.numpy as jnp
import numpy as np
from jax import lax
from jax.experimental import pallas as pl
from jax.experimental.pallas import tpu as pltpu

F32 = jnp.float32
BF16 = jnp.bfloat16

D_MODEL = 2048
GRID_W = 64
ATT_HEADS = 8
ATT_KV = 2
ATT_GROUP = ATT_HEADS // ATT_KV
HEAD_DIM = 64
ROPE_THETA = 10000.0
SSD_HEADS = 8
SSD_P = 64
SSD_N = 64
SSD_GROUPS = 2
RWKV_HEADS = 8
RWKV_HD = 64
RWKV_DECAY_SCALE = 0.6065306597126334
RWKV_LN_EPS = 64e-5
GLA_HEADS = 4
GLA_DK = 64
GLA_DV = 128
GLA_GATE_NORM = 16.0
CHUNK = 64
BRANCH_W = 512

ATT_W = 768
SSD_W = 1408
RWKV_W = 1792
GLA_W = 1664
LANE = 128
SUBLANE = 8
MOD_ROWS = 16
VMEM_LIMIT = 56 * 1024 * 1024
ROW_TILE = 256
MM_TILE = 512
SCAN_CHUNKS = 4
KEY_TILE = 256
ATT_Q_TILE = 256
KEY_UNROLL = 6
LOG2E = 1.4426950408889634
ATT_MIN_ROW_SUM = 2.0 ** -90


def _cp(*sem):
    return pltpu.CompilerParams(dimension_semantics=sem, vmem_limit_bytes=VMEM_LIMIT)


def _bf(x):
    return x.astype(BF16)


def _dot(a, b):
    return jnp.dot(a, b, preferred_element_type=F32)


def _dot_nt(a, b):
    return lax.dot_general(a, b, (((1,), (1,)), ((), ())), preferred_element_type=F32)


def _dot_tn(a, b):
    return lax.dot_general(a, b, (((0,), (0,)), ((), ())), preferred_element_type=F32)


def _pieces(a, n):
    out = []
    for _ in range(n):
        piece = _bf(a)
        out.append(piece)
        a = a - piece.astype(F32)
    return out


def _dot_data_mask(a, mask, n):
    return functools.reduce(jnp.add, [_dot(piece, mask) for piece in _pieces(a, n)])


def _dot_mask_data(mask, a, n):
    return functools.reduce(jnp.add, [_dot(mask, piece) for piece in _pieces(a, n)])


def _dot_nt_mask_data(mask, a, n):
    return functools.reduce(jnp.add, [_dot_nt(mask, piece) for piece in _pieces(a, n)])


def _dot_split(a, b):
    a_hi, a_lo = _pieces(a, 2)
    b_hi, b_lo = _pieces(b, 2)
    return _dot(a_hi, b_hi) + (_dot(a_hi, b_lo) + _dot(a_lo, b_hi))


def _silu(x):
    return x * jax.nn.sigmoid(x)


def _softplus(x):
    return jnp.maximum(x, 0.0) + jnp.log1p(jnp.exp(-jnp.abs(x)))


def _lanes(pieces):
    return jnp.concatenate(pieces, axis=1)


def _rows(pieces):
    return jnp.concatenate(pieces, axis=0)


def _mod_kernel(c_ref, w_ref, b_ref, o_ref):
    c = c_ref[...]
    o_ref[0] = _dot(_bf(_silu(c)), _bf(w_ref[0])) + b_ref[0]


def _modulation(cond, w_mod, b_mod):
    nl, d, n6 = w_mod.shape
    tn = 1024
    out = pl.pallas_call(
        _mod_kernel,
        grid=(nl, n6 // tn),
        in_specs=[pl.BlockSpec((MOD_ROWS, d), lambda l, j: (0, 0)),
                  pl.BlockSpec((1, d, tn), lambda l, j: (l, 0, j)),
                  pl.BlockSpec((1, 1, tn), lambda l, j: (l, 0, j))],
        out_specs=pl.BlockSpec((1, MOD_ROWS, tn), lambda l, j: (l, 0, j)),
        out_shape=jax.ShapeDtypeStruct((nl, MOD_ROWS, n6), F32),
        compiler_params=_cp("parallel", "parallel"),
        name="modulation",
    )(cond, w_mod, b_mod.reshape(nl, 1, n6))
    return out.reshape(nl, MOD_ROWS, 6, d)


class _Stream:
    def __init__(self, n, t, group0, per_seq):
        self.n, self.t, self.group0, self.per_seq = n, t, group0, per_seq
        self.rows = n * t

    def group_map(self, tile):
        g0, per_seq, t = self.group0, self.per_seq, self.t
        if per_seq:
            return lambda i: (g0 + (i * tile) // t, 0, 0)
        return lambda i: (g0, 0, 0)


def _mm_tile(st):
    limit = st.t if st.per_seq else st.rows
    return min(MM_TILE, limit)


def _row_tile(st):
    return min(ROW_TILE, st.t)


def _rms(x, gain):
    return x * lax.rsqrt(jnp.mean(x * x, axis=-1, keepdims=True) + 1e-6) * gain


def _norm_mod(x, gain, mod_ref, sc_idx, sh_idx):
    return _bf(_rms(x, gain) * (1.0 + mod_ref[0, sc_idx:sc_idx + 1, :]) + mod_ref[0, sh_idx:sh_idx + 1, :])


def _in_proj_a_kernel(x_ref, g_ref, mod_ref, wa_ref, ws_ref, h_out, pa_out, ps_out):
    h = _norm_mod(x_ref[...], g_ref[...], mod_ref, 1, 0)
    h_out[...] = h
    pa_out[...] = _dot(h, wa_ref[...])
    ps_out[...] = _dot(h, ws_ref[...])


def _in_proj_a(st, x, gain, mod_l, w_att, w_ssd):
    m, d = x.shape
    tm = _mm_tile(st)
    row = lambda w: pl.BlockSpec((tm, w), lambda i: (i, 0))
    full = lambda a: pl.BlockSpec(a.shape, lambda i: (0, 0))
    return pl.pallas_call(
        _in_proj_a_kernel,
        grid=(m // tm,),
        in_specs=[row(d), pl.BlockSpec((1, d), lambda i: (0, 0)), pl.BlockSpec((1, 6, d), st.group_map(tm)),
                  full(w_att), full(w_ssd)],
        out_specs=[row(d), row(ATT_W), row(SSD_W)],
        out_shape=[jax.ShapeDtypeStruct((m, d), BF16), jax.ShapeDtypeStruct((m, ATT_W), F32),
                   jax.ShapeDtypeStruct((m, SSD_W), F32)],
        compiler_params=_cp("parallel"),
        name="in_proj_a",
    )(x, gain.reshape(1, d), mod_l, w_att, w_ssd)


def _in_proj_b_kernel(h_ref, wr_ref, wg_ref, pr_out, pg_out):
    h = h_ref[...]
    pr_out[...] = _dot(h, wr_ref[...])
    pg_out[...] = _dot(h, wg_ref[...])


def _in_proj_b(st, h, w_rwkv, w_gla):
    m, d = h.shape
    tm = _mm_tile(st)
    row = lambda w: pl.BlockSpec((tm, w), lambda i: (i, 0))
    full = lambda a: pl.BlockSpec(a.shape, lambda i: (0, 0))
    return pl.pallas_call(
        _in_proj_b_kernel,
        grid=(m // tm,),
        in_specs=[row(d), full(w_rwkv), full(w_gla)],
        out_specs=[row(RWKV_W), row(GLA_W)],
        out_shape=[jax.ShapeDtypeStruct((m, RWKV_W), F32), jax.ShapeDtypeStruct((m, GLA_W), F32)],
        compiler_params=_cp("parallel"),
        name="in_proj_b",
    )(h, w_rwkv, w_gla)


def _att_prep_kernel(*refs, rope):
    if rope:
        p_ref, qn_ref, kn_ref, bdq_ref, bdk_ref, cos_ref, sa_ref, sb_ref, q_out, k_out, v_out = refs
    else:
        p_ref, qn_ref, kn_ref, bdq_ref, bdk_ref, q_out, k_out, v_out, k_leaf = refs
    p = p_ref[...]
    aq = p[:, :512]
    ak = p[:, 512:640]
    av = p[:, 640:768]
    q = aq * lax.rsqrt(_dot_data_mask(aq * aq, bdq_ref[...], 2) + 1e-6) * qn_ref[...]
    k = ak * lax.rsqrt(_dot_data_mask(ak * ak, bdk_ref[...], 2) + 1e-6) * kn_ref[...]
    if rope:
        c, sa, sb = cos_ref[...], sa_ref[...], sb_ref[...]
        k = k * c + pltpu.roll(k, LANE - 16, 1) * sa + pltpu.roll(k, 16, 1) * sb
        c4, sa4, sb4 = _lanes([c] * 4), _lanes([sa] * 4), _lanes([sb] * 4)
        q = q * c4 + pltpu.roll(q, 512 - 16, 1) * sa4 + pltpu.roll(q, 16, 1) * sb4
    else:
        k_leaf[...] = k
    tr = p.shape[0]
    one_col = (lax.broadcasted_iota(jnp.int32, (tr, LANE - HEAD_DIM), 1) == 0).astype(F32)
    q = q * (HEAD_DIM ** -0.5 * LOG2E)
    q_len = jnp.sqrt(_dot_data_mask(q * q, bdq_ref[...], 2) * HEAD_DIM)
    q_out[...] = _bf(_lanes([piece for h in range(ATT_HEADS)
                             for piece in (q[:, h * HEAD_DIM:(h + 1) * HEAD_DIM],
                                           one_col * q_len[:, h * HEAD_DIM:(h + 1) * HEAD_DIM])]))
    for kv in range(ATT_KV):
        sl = slice(kv * HEAD_DIM, (kv + 1) * HEAD_DIM)
        k_out[kv] = _bf(_lanes([k[:, sl], one_col]))
        v_out[kv] = _bf(_lanes([av[:, sl], one_col]))


def _block_diag(width, block, value):
    idx = np.arange(width) // block
    return _bf(jnp.asarray((idx[:, None] == idx[None, :]).astype(np.float32) * value))


def _att_prep(st, proj, q_norm, k_norm, rope_tables):
    m = proj.shape[0]
    tr = _row_tile(st)
    rope = rope_tables is not None
    full = lambda shape: pl.BlockSpec(shape, lambda i: (0, 0))
    in_specs = [pl.BlockSpec((tr, ATT_W), lambda i: (i, 0)), full((1, 512)), full((1, 128)),
                full((512, 512)), full((128, 128))]
    args = [proj, jnp.tile(q_norm, ATT_HEADS).reshape(1, 512), jnp.tile(k_norm, ATT_KV).reshape(1, 128),
            _block_diag(512, HEAD_DIM, 1.0 / HEAD_DIM), _block_diag(128, HEAD_DIM, 1.0 / HEAD_DIM)]
    if rope:
        tps = st.t // tr
        in_specs += [pl.BlockSpec((tr, LANE), lambda i: (i % tps, 0))] * 3
        args += list(rope_tables)
    kv_spec = pl.BlockSpec((ATT_KV, tr, LANE), lambda i: (0, i, 0))
    kv_shape = jax.ShapeDtypeStruct((ATT_KV, m, LANE), BF16)
    out_specs = [pl.BlockSpec((tr, ATT_HEADS * LANE), lambda i: (i, 0)), kv_spec, kv_spec]
    out_shape = [jax.ShapeDtypeStruct((m, ATT_HEADS * LANE), BF16), kv_shape, kv_shape]
    if not rope:
        out_specs.append(pl.BlockSpec((tr, 128), lambda i: (i, 0)))
        out_shape.append(jax.ShapeDtypeStruct((m, 128), F32))
    return pl.pallas_call(
        functools.partial(_att_prep_kernel, rope=rope),
        grid=(m // tr,),
        in_specs=in_specs,
        out_specs=out_specs,
        out_shape=out_shape,
        compiler_params=_cp("parallel"),
        name="att_prep",
    )(*args)


def _join_cache(st, own, cache):
    n, p, kv, hd = cache.shape
    c = jnp.transpose(cache, (2, 0, 1, 3))
    pad = jnp.zeros((kv, n, p, LANE - hd), c.dtype).at[..., 0].set(1.0)
    c = _bf(jnp.concatenate([c, pad], axis=-1))
    joined = jnp.concatenate([own.reshape(kv, n, st.t, LANE), c], axis=2)
    return joined.reshape(kv, n * (st.t + p), LANE)


def _rope_tables(t):
    half = HEAD_DIM // 2
    nf = half // 2
    freqs = ROPE_THETA ** (-jnp.arange(nf, dtype=F32) / nf)
    tt = jnp.arange(t)
    ang_r = (tt // GRID_W).astype(F32)[:, None] * freqs[None, :]
    ang_c = (tt % GRID_W).astype(F32)[:, None] * freqs[None, :]
    zero = jnp.zeros_like(ang_r)
    cos = jnp.concatenate([jnp.cos(ang_r)] * 2 + [jnp.cos(ang_c)] * 2, axis=1)
    sa = jnp.concatenate([-jnp.sin(ang_r), zero, -jnp.sin(ang_c), zero], axis=1)
    sb = jnp.concatenate([zero, jnp.sin(ang_r), zero, jnp.sin(ang_c)], axis=1)
    return tuple(jnp.tile(a, (1, LANE // HEAD_DIM)) for a in (cos, sa, sb))


def _key_bound_kernel(k_ref, o_ref):
    ones = jnp.ones((LANE, LANE), BF16)
    for kv in range(ATT_KV):
        k = k_ref[kv].astype(F32)
        k = jnp.where(lax.broadcasted_iota(jnp.int32, k.shape, 1) < HEAD_DIM, k, 0.0)
        best = jnp.max(_dot_data_mask(k * k, ones, 2), axis=0, keepdims=True)
        o_ref[kv] = jnp.broadcast_to(best, (SUBLANE, LANE))


def _key_bound(n, keys_per_seq, k):
    return pl.pallas_call(
        _key_bound_kernel,
        grid=(n,),
        in_specs=[pl.BlockSpec((ATT_KV, keys_per_seq, LANE), lambda b: (0, b, 0))],
        out_specs=pl.BlockSpec((None, ATT_KV, SUBLANE, LANE), lambda b: (b, 0, 0, 0)),
        out_shape=jax.ShapeDtypeStruct((n, ATT_KV, SUBLANE, LANE), F32),
        compiler_params=_cp("parallel"),
        name="key_bound",
    )(k)


def _attn_kernel(q_ref, k_ref, v_ref, kb_ref, o_ref, m_scr, acc_scr, *, tq, n_tiles):
    rows = ATT_GROUP * tq
    lane = lax.broadcasted_iota(jnp.int32, (rows, LANE), 1)
    outs = []
    for kv in range(ATT_KV):
        qs = _rows([q_ref[:, (kv * ATT_GROUP + g) * LANE:(kv * ATT_GROUP + g + 1) * LANE]
                    for g in range(ATT_GROUP)]).astype(F32)

        def key_tile(ref, j):
            return ref[kv, pl.ds(pl.multiple_of(j * KEY_TILE, KEY_TILE), KEY_TILE), :]

        def weighted_sum(q_shift, unroll):
            acc_scr[...] = jnp.zeros((rows, LANE), F32)

            def sum_body(j, carry):
                p = jnp.exp2(_dot_nt(q_shift, key_tile(k_ref, j)))
                acc_scr[...] += _dot(_bf(p), key_tile(v_ref, j))
                return carry

            lax.fori_loop(0, n_tiles, sum_body, 0, unroll=unroll)
            return acc_scr[...]

        k_len = jnp.sqrt(kb_ref[kv][0:1, :])
        unroll = max(u for u in range(1, KEY_UNROLL + 1) if n_tiles % u == 0)
        acc = weighted_sum(_bf(qs * jnp.where(lane == HEAD_DIM, -k_len, 1.0)), unroll)

        def exact_shift():
            m_scr[...] = jnp.full((rows, LANE), -jnp.inf, F32)
            q0 = _bf(jnp.where(lane == HEAD_DIM, 0.0, qs))

            def max_body(j, carry):
                sc = _dot_nt(q0, key_tile(k_ref, j))
                m_scr[...] = jnp.maximum(m_scr[...], jnp.maximum(sc[:, :LANE], sc[:, LANE:]))
                return carry

            lax.fori_loop(0, n_tiles, max_body, 0)
            row_max = jnp.max(m_scr[...], axis=-1, keepdims=True)
            return weighted_sum(_bf(jnp.where(lane == HEAD_DIM, -row_max, qs)), 1)

        row_sum_ok = jnp.min(acc[:, HEAD_DIM:HEAD_DIM + 1]) >= ATT_MIN_ROW_SUM
        acc = lax.cond(row_sum_ok, lambda: acc, exact_shift)
        o = acc[:, :HEAD_DIM] / acc[:, HEAD_DIM:HEAD_DIM + 1]
        outs += [o[g * tq:(g + 1) * tq] for g in range(ATT_GROUP)]
    o_ref[...] = _bf(_lanes(outs))


def _attention(st, q, k, v):
    tq = min(ATT_Q_TILE, st.t)
    nq = st.t // tq
    keys = k.shape[1] // st.n
    assert keys % KEY_TILE == 0
    kv_spec = pl.BlockSpec((ATT_KV, keys, LANE), lambda b, i: (0, b, 0))
    rows = ATT_GROUP * tq
    return pl.pallas_call(
        functools.partial(_attn_kernel, tq=tq, n_tiles=keys // KEY_TILE),
        grid=(st.n, nq),
        in_specs=[pl.BlockSpec((tq, ATT_HEADS * LANE), lambda b, i: (b * nq + i, 0)), kv_spec, kv_spec,
                  pl.BlockSpec((None, ATT_KV, SUBLANE, LANE), lambda b, i: (b, 0, 0, 0))],
        out_specs=pl.BlockSpec((tq, 512), lambda b, i: (b * nq + i, 0)),
        out_shape=jax.ShapeDtypeStruct((st.rows, 512), BF16),
        scratch_shapes=[pltpu.VMEM((rows, LANE), F32), pltpu.VMEM((rows, LANE), F32)],
        compiler_params=_cp("parallel", "parallel"),
        name="attention",
    )(q, k, v, _key_bound(st.n, keys, k))


def _halo_specs(st, width):
    tr = _row_tile(st)
    per8 = tr // SUBLANE
    last8 = st.rows // SUBLANE - 1
    return [pl.BlockSpec((tr, width), lambda i: (i, 0)),
            pl.BlockSpec((SUBLANE, width), lambda i: (jnp.maximum(i * per8 - 1, 0), 0)),
            pl.BlockSpec((SUBLANE, width), lambda i: (jnp.minimum((i + 1) * per8, last8), 0))]


def _neighbours(cur, prev8, next8, tiles_per_seq):
    tr = cur.shape[0]
    j = pl.program_id(0) % tiles_per_seq
    pr = jnp.where(j != 0, prev8[SUBLANE - 1:SUBLANE], 0.0)
    nx = jnp.where(j != tiles_per_seq - 1, next8[0:1], 0.0)
    row = lax.broadcasted_iota(jnp.int32, cur.shape, 0)
    x_prev = jnp.where(row == 0, pr, pltpu.roll(cur, 1, 0))
    x_next = jnp.where(row == tr - 1, nx, pltpu.roll(cur, tr - 1, 0))
    return x_prev, x_next


def _dir_masks(z):
    ri = lax.broadcasted_iota(jnp.int32, (CHUNK, CHUNK), 0)
    ci = lax.broadcasted_iota(jnp.int32, (CHUNK, CHUNK), 1)
    return ((ci <= ri), (ci < ri)) if z == 0 else ((ci >= ri), (ci > ri))


def _tri_blocks(tr):
    i = np.arange(tr)
    same = (i[:, None] // CHUNK) == (i[None, :] // CHUNK)
    lower = same & (i[None, :] <= i[:, None])
    upper = same & (i[None, :] >= i[:, None])
    return _bf(jnp.asarray(lower.astype(np.float32))), _bf(jnp.asarray(upper.astype(np.float32)))


def _scan_blocks(st):
    ncb = min(SCAN_CHUNKS, st.t // CHUNK)
    rb = ncb * CHUNK
    nblk = st.t // rb
    fwd = lambda b, j: (b * nblk + j, 0)
    bwd = lambda b, j: (b * nblk + nblk - 1 - j, 0)
    return ncb, rb, nblk, fwd, bwd


def _chunk_rows(step, z, ncb):
    cc = step if z == 0 else ncb - 1 - step
    return slice(cc * CHUNK, (cc + 1) * CHUNK)


def _ssd_prep_kernel(cur_ref, prev_ref, next_ref, cw_ref, cb_ref, dtb_ref, alog_ref, lo_ref, up_ref,
                     xbc_out, dt_out, cum_out, dtt_out, cumt_out, *, tiles_per_seq):
    cur = cur_ref[...]
    xc = cur[:, 512:1280]
    x_prev, x_next = _neighbours(xc, prev_ref[:, 512:1280], next_ref[:, 512:1280], tiles_per_seq)
    cw = cw_ref[...]
    conv = cb_ref[...] + x_prev * cw[0:1] + xc * cw[1:2] + x_next * cw[2:3]
    xbc_out[...] = _silu(conv)
    dt = _softplus(cur[:, 1280:1408] + dtb_ref[...])
    dt_out[...] = dt
    ld = dt * -jnp.exp(alog_ref[...])
    lane = lax.broadcasted_iota(jnp.int32, ld.shape, 1)
    cum = jnp.where(lane < SSD_HEADS, _dot_mask_data(lo_ref[...], ld, 3), _dot_mask_data(up_ref[...], ld, 3))
    cum_out[...] = cum
    pick = (lax.broadcasted_iota(jnp.int32, (2 * SSD_HEADS, LANE), 0)
            == lax.broadcasted_iota(jnp.int32, (2 * SSD_HEADS, LANE), 1)).astype(BF16)
    dtt_out[...] = _dot_nt_mask_data(pick, dt, 3)
    cumt_out[...] = _dot_nt_mask_data(pick, cum, 3)


def _pad_lanes(v, width=LANE):
    v = v.reshape(1, -1)
    return jnp.pad(v, ((0, 0), (0, width - v.shape[1])))


def _ssd_prep(st, proj, conv_w, conv_b, dt_bias, a_log):
    m = proj.shape[0]
    tr = _row_tile(st)
    full = lambda shape: pl.BlockSpec(shape, lambda i: (0, 0))
    row = lambda w: pl.BlockSpec((tr, w), lambda i: (i, 0))
    col = pl.BlockSpec((2 * SSD_HEADS, tr), lambda i: (0, i))
    lower, upper = _tri_blocks(tr)
    return pl.pallas_call(
        functools.partial(_ssd_prep_kernel, tiles_per_seq=st.t // tr),
        grid=(m // tr,),
        in_specs=_halo_specs(st, SSD_W) + [full((3, 768)), full((1, 768)), full((1, LANE)), full((1, LANE)),
                                           full((tr, tr)), full((tr, tr))],
        out_specs=[row(768), row(LANE), row(LANE), col, col],
        out_shape=[jax.ShapeDtypeStruct((m, 768), F32), jax.ShapeDtypeStruct((m, LANE), F32),
                   jax.ShapeDtypeStruct((m, LANE), F32), jax.ShapeDtypeStruct((2 * SSD_HEADS, m), F32),
                   jax.ShapeDtypeStruct((2 * SSD_HEADS, m), F32)],
        compiler_params=_cp("parallel"),
        name="ssd_prep",
    )(proj, proj, proj, conv_w.T, conv_b.reshape(1, 768), _pad_lanes(dt_bias), _pad_lanes(a_log), lower, upper)


def _ssd_scan_kernel(xf_ref, xb_ref, dtf_ref, dtb_ref, cf_ref, cb_ref, dttf_ref, dttb_ref, ctf_ref, ctb_ref,
                     sel_ref, s0_ref, yf_ref, yb_ref, s_ref, *, ncb):
    @pl.when(pl.program_id(1) == 0)
    def _():
        s_ref[...] = s0_ref[...]

    hpg = SSD_HEADS // SSD_GROUPS
    state = {(z, g): _lanes([s_ref[z, g * hpg + hh] for hh in range(hpg)])
             for z in range(2) for g in range(SSD_GROUPS)}
    refs = ((xf_ref, dtf_ref, cf_ref, dttf_ref, ctf_ref, yf_ref), (xb_ref, dtb_ref, cb_ref, dttb_ref, ctb_ref, yb_ref))
    groups = [(z, g) for z in range(2) for g in range(SSD_GROUPS)]
    heads = [(z, h) for z in range(2) for h in range(SSD_HEADS)]
    for step in range(ncb):
        xbc, dt, cum, dtt, cumt, incl, rows = {}, {}, {}, {}, {}, {}, {}
        for z in range(2):
            x_ref, dt_ref, c_ref, dtt_ref, ct_ref, _ = refs[z]
            rows[z] = _chunk_rows(step, z, ncb)
            xbc[z] = x_ref[rows[z], :]
            dt[z] = _dot_data_mask(dt_ref[rows[z], :], sel_ref[z], 3)
            cum[z] = _dot_data_mask(c_ref[rows[z], :], sel_ref[z], 3)
            dtt[z], cumt[z] = dtt_ref[:, rows[z]], ct_ref[:, rows[z]]
            incl[z] = _dir_masks(z)[0]
        bmat = {(z, g): xbc[z][:, 512 + g * SSD_N:512 + (g + 1) * SSD_N] for z, g in groups}
        cmat = {(z, g): _bf(xbc[z][:, 640 + g * SSD_N:640 + (g + 1) * SSD_N]) for z, g in groups}
        cb = {k: _dot_nt(cmat[k], _bf(bmat[k])) for k in groups}
        cs = {k: _dot(cmat[k], _bf(state[k])) for k in groups}
        gw = hpg * SSD_P
        glast = {z: cum[z][CHUNK - 1:CHUNK] if z == 0 else cum[z][0:1] for z in range(2)}
        xs = {z: xbc[z][:, :SSD_HEADS * SSD_P] for z in range(2)}
        xw = {z: xs[z] * (dt[z] * jnp.exp(glast[z] - cum[z])) for z in range(2)}
        inc = {(z, g): _dot_tn(_bf(bmat[z, g]), _bf(xw[z][:, g * gw:(g + 1) * gw])) for z, g in groups}
        att = {}
        for z, h in heads:
            ln = z * SSD_HEADS + h
            gcol = cum[z][:, h * SSD_P:(h + 1) * SSD_P]
            dec = jnp.exp(jnp.where(incl[z], gcol - cumt[z][ln:ln + 1], -jnp.inf))
            att[z, h] = _bf(cb[z, h // hpg] * dec * dtt[z][ln:ln + 1])
        intra = {(z, h): _dot(att[z, h], _bf(xs[z][:, h * SSD_P:(h + 1) * SSD_P])) for z, h in heads}
        for z in range(2):
            e_in = jnp.exp(cum[z])
            e_last = jnp.exp(glast[z])
            outs = []
            for g in range(SSD_GROUPS):
                hs = range(g * hpg, (g + 1) * hpg)
                outs.append(_lanes([intra[z, h] for h in hs]) + cs[z, g] * e_in[:, g * gw:(g + 1) * gw])
                state[z, g] = state[z, g] * e_last[:, g * gw:(g + 1) * gw] + inc[z, g]
            refs[z][5][rows[z], :] = _bf(_lanes(outs))
    for z in range(2):
        for h in range(SSD_HEADS):
            s_ref[z, h] = state[z, h // hpg][:, (h % hpg) * SSD_P:(h % hpg + 1) * SSD_P]


def _ssd_scan(st, xbc, dt, cum, dtt, cumt, s0):
    ncb, rb, nblk, fwd, bwd = _scan_blocks(st)
    fwd_t = lambda b, j: (0, fwd(b, j)[0])
    bwd_t = lambda b, j: (0, bwd(b, j)[0])
    state_spec = pl.BlockSpec((None, 2, SSD_HEADS, SSD_N, SSD_P), lambda b, j: (b, 0, 0, 0, 0))
    rows = lambda w, m: pl.BlockSpec((rb, w), m)
    cols = lambda m: pl.BlockSpec((2 * SSD_HEADS, rb), m)
    lane = np.arange(LANE)[None, :, None]
    col = np.arange(SSD_HEADS * SSD_P)[None, None, :]
    sel = _bf(jnp.asarray((lane == np.arange(2)[:, None, None] * SSD_HEADS + col // SSD_P).astype(np.float32)))
    return pl.pallas_call(
        functools.partial(_ssd_scan_kernel, ncb=ncb),
        grid=(st.n, nblk),
        in_specs=[rows(768, fwd), rows(768, bwd), rows(LANE, fwd), rows(LANE, bwd), rows(LANE, fwd), rows(LANE, bwd),
                  cols(fwd_t), cols(bwd_t), cols(fwd_t), cols(bwd_t),
                  pl.BlockSpec(sel.shape, lambda b, j: (0, 0, 0)), state_spec],
        out_specs=[rows(512, fwd), rows(512, bwd), state_spec],
        out_shape=[jax.ShapeDtypeStruct((st.rows, 512), BF16), jax.ShapeDtypeStruct((st.rows, 512), BF16),
                   jax.ShapeDtypeStruct((st.n, 2, SSD_HEADS, SSD_N, SSD_P), F32)],
        compiler_params=_cp("parallel", "arbitrary"),
        name="ssd_scan",
    )(xbc, xbc, dt, dt, cum, cum, dtt, dtt, cumt, cumt, sel, s0)


def _ssd_post_kernel(yf_ref, yb_ref, xbc_ref, p_ref, d_ref, g_ref, o_ref):
    y = yf_ref[...].astype(F32) + yb_ref[...].astype(F32) + d_ref[...] * xbc_ref[:, :512]
    y = y * _silu(p_ref[:, :512])
    o_ref[...] = _bf(y * lax.rsqrt(jnp.mean(y * y, axis=-1, keepdims=True) + 1e-6) * g_ref[...])


def _ssd_post(st, y_f, y_b, xbc, proj, ssd_d, ssd_norm):
    m = proj.shape[0]
    tr = _row_tile(st)
    full = lambda shape: pl.BlockSpec(shape, lambda i: (0, 0))
    row = lambda w: pl.BlockSpec((tr, w), lambda i: (i, 0))
    return pl.pallas_call(
        _ssd_post_kernel,
        grid=(m // tr,),
        in_specs=[row(512), row(512), row(512), row(512), full((1, 512)), full((1, 512))],
        out_specs=row(512),
        out_shape=jax.ShapeDtypeStruct((m, 512), BF16),
        compiler_params=_cp("parallel"),
        name="ssd_post",
    )(y_f, y_b, xbc, proj, jnp.repeat(ssd_d, SSD_P).reshape(1, 512), ssd_norm.reshape(1, 512))


GLA_QK = GLA_HEADS * GLA_DK
GLA_SAFE_RANGE = 60.0


def _gla_prep_kernel(p_ref, g2_ref, gb_ref, lo_ref, up_ref, cum_out):
    logit = _dot_split(p_ref[...], g2_ref[...]) + gb_ref[...]
    log_a = -_softplus(-logit) * (1.0 / GLA_GATE_NORM)
    cum_out[:, :GLA_QK] = _dot_mask_data(lo_ref[...], log_a[:, :GLA_QK], 3)
    cum_out[:, GLA_QK:] = _dot_mask_data(up_ref[...], log_a[:, GLA_QK:], 3)


def _gla_prep(st, proj, g2p, gb):
    m = proj.shape[0]
    tr = _row_tile(st)
    full = lambda shape: pl.BlockSpec(shape, lambda i: (0, 0))
    lower, upper = _tri_blocks(tr)
    return pl.pallas_call(
        _gla_prep_kernel,
        grid=(m // tr,),
        in_specs=[pl.BlockSpec((tr, LANE), lambda i: (i, (GLA_W - LANE) // LANE)), full((LANE, 2 * GLA_QK)),
                  full((1, 2 * GLA_QK)),
                  full((tr, tr)), full((tr, tr))],
        out_specs=pl.BlockSpec((tr, 2 * GLA_QK), lambda i: (i, 0)),
        out_shape=jax.ShapeDtypeStruct((m, 2 * GLA_QK), F32),
        compiler_params=_cp("parallel"),
        name="gla_prep",
    )(proj, g2p, gb, lower, upper)


def _gla_intra_exact(p_ref, c_ref, rows, z, seg_ref):
    r0 = rows.start
    q = p_ref[rows, 0:GLA_QK] * (GLA_DK ** -0.5)
    cum = c_ref[rows, :]
    row = lax.broadcasted_iota(jnp.int32, (CHUNK, 1), 0)
    seg = seg_ref[...]

    def body(j, acc):
        kj = p_ref[pl.ds(r0 + j, 1), GLA_QK:2 * GLA_QK]
        vj = p_ref[pl.ds(r0 + j, 1), 2 * GLA_QK:2 * GLA_QK + GLA_HEADS * GLA_DV]
        seen = (row >= j) if z == 0 else (row <= j)
        w = jnp.where(seen, q * kj * jnp.exp(jnp.minimum(cum - c_ref[pl.ds(r0 + j, 1), :], 0.0)), 0.0)
        score = _dot_data_mask(w, seg, 3)
        return tuple(a + score[:, h:h + 1] * vj[:, h * GLA_DV:(h + 1) * GLA_DV] for h, a in enumerate(acc))

    zero = jnp.zeros((CHUNK, GLA_DV), F32)
    return list(lax.fori_loop(0, CHUNK, body, (zero,) * GLA_HEADS))


def _gla_scan_kernel(pf_ref, pb_ref, cf_ref, cb_ref, seg_ref, s0_ref, yf_ref, yb_ref, s_ref, *, ncb):
    @pl.when(pl.program_id(1) == 0)
    def _():
        s_ref[...] = s0_ref[...]

    refs = ((pf_ref, cf_ref, yf_ref), (pb_ref, cb_ref, yb_ref))
    heads = [(z, h) for z in range(2) for h in range(GLA_HEADS)]
    hsl = [slice(h * GLA_DK, (h + 1) * GLA_DK) for h in range(GLA_HEADS)]
    state = {(z, h): s_ref[z, h] for z, h in heads}
    chunks = [(step, z) for step in range(ncb) for z in range(2)]
    rows = {(step, z): _chunk_rows(step, z, ncb) for step, z in chunks}
    cum = {k: refs[k[1]][1][rows[k], :] for k in chunks}
    mid = {k: cum[k][CHUNK // 2:CHUNK // 2 + 1] for k in chunks}
    span = functools.reduce(jnp.maximum, [jnp.max(jnp.abs(cum[k] - mid[k])) for k in chunks])

    def v_of(k, h):
        return _bf(refs[k[1]][0][rows[k], 2 * GLA_QK + h * GLA_DV:2 * GLA_QK + (h + 1) * GLA_DV])

    def intra_factored():
        att = {}
        for k in chunks:
            p_ref = refs[k[1]][0]
            q_mid = _bf(p_ref[rows[k], 0:GLA_QK] * (GLA_DK ** -0.5) * jnp.exp(cum[k] - mid[k]))
            k_mid = _bf(p_ref[rows[k], GLA_QK:2 * GLA_QK] * jnp.exp(mid[k] - cum[k]))
            incl = _dir_masks(k[1])[0]
            for h in range(GLA_HEADS):
                att[k, h] = _bf(jnp.where(incl, _dot_nt(q_mid[:, hsl[h]], k_mid[:, hsl[h]]), 0.0))
        return [_dot(att[k, h], v_of(k, h)) for k in chunks for h in range(GLA_HEADS)]

    def intra_exact():
        out = []
        for k in chunks:
            out += _gla_intra_exact(refs[k[1]][0], refs[k[1]][1], rows[k], k[1], seg_ref)
        return out

    intra = lax.cond(span <= GLA_SAFE_RANGE, intra_factored, intra_exact)
    intra = {(k, h): intra[i * GLA_HEADS + h] for i, k in enumerate(chunks) for h in range(GLA_HEADS)}

    for step in range(ncb):
        q_in, k_out, e_last = {}, {}, {}
        for z in range(2):
            k = (step, z)
            p_ref = refs[z][0]
            glast = cum[k][CHUNK - 1:CHUNK] if z == 0 else cum[k][0:1]
            q_in[z] = _bf(p_ref[rows[k], 0:GLA_QK] * (GLA_DK ** -0.5) * jnp.exp(cum[k]))
            k_out[z] = _bf(p_ref[rows[k], GLA_QK:2 * GLA_QK] * jnp.exp(glast - cum[k]))
            e_last[z] = jnp.exp(glast)
        inter = {(z, h): _dot_nt(q_in[z][:, hsl[h]], _bf(state[z, h])) for z, h in heads}
        inc = {(z, h): _dot_tn(v_of((step, z), h), k_out[z][:, hsl[h]]) for z, h in heads}
        for z in range(2):
            refs[z][2][rows[step, z], :] = _bf(_lanes([intra[(step, z), h] + inter[z, h] for h in range(GLA_HEADS)]))
            for h in range(GLA_HEADS):
                state[z, h] = state[z, h] * e_last[z][:, hsl[h]] + inc[z, h]
    for z, h in heads:
        s_ref[z, h] = state[z, h]


def _gla_scan(st, proj, cum, s0_t):
    ncb, rb, nblk, fwd, bwd = _scan_blocks(st)
    seg = _block_diag(GLA_QK, GLA_DK, 1.0)[:, ::GLA_DK]
    seg = jnp.pad(seg, ((0, 0), (0, LANE - GLA_HEADS)))
    state_spec = pl.BlockSpec((None, 2, GLA_HEADS, GLA_DV, GLA_DK), lambda b, j: (b, 0, 0, 0, 0))
    qkv_w = 2 * GLA_QK + GLA_HEADS * GLA_DV
    return pl.pallas_call(
        functools.partial(_gla_scan_kernel, ncb=ncb),
        grid=(st.n, nblk),
        in_specs=[pl.BlockSpec((rb, qkv_w), fwd), pl.BlockSpec((rb, qkv_w), bwd),
                  pl.BlockSpec((rb, GLA_QK), fwd), pl.BlockSpec((rb, GLA_QK), lambda b, j: (bwd(b, j)[0], 1)),
                  pl.BlockSpec((GLA_QK, LANE), lambda b, j: (0, 0)), state_spec],
        out_specs=[pl.BlockSpec((rb, 512), fwd), pl.BlockSpec((rb, 512), bwd), state_spec],
        out_shape=[jax.ShapeDtypeStruct((st.rows, 512), BF16), jax.ShapeDtypeStruct((st.rows, 512), BF16),
                   jax.ShapeDtypeStruct((st.n, 2, GLA_HEADS, GLA_DV, GLA_DK), F32)],
        compiler_params=_cp("parallel", "arbitrary"),
        name="gla_scan",
    )(proj, proj, cum, cum, seg, s0_t)


def _gla_post_kernel(yf_ref, yb_ref, p_ref, g_ref, o_ref):
    o = yf_ref[...].astype(F32) + yb_ref[...].astype(F32)
    gate = _silu(p_ref[...])
    outs = []
    for h in range(GLA_HEADS):
        oh = o[:, h * GLA_DV:(h + 1) * GLA_DV]
        outs.append(oh * lax.rsqrt(jnp.mean(oh * oh, axis=-1, keepdims=True) + 1e-6) * g_ref[...])
    o_ref[...] = _bf(_lanes(outs) * gate)


def _gla_post(st, y_f, y_b, proj, gla_norm):
    m = proj.shape[0]
    tr = _row_tile(st)
    return pl.pallas_call(
        _gla_post_kernel,
        grid=(m // tr,),
        in_specs=[pl.BlockSpec((tr, 512), lambda i: (i, 0)), pl.BlockSpec((tr, 512), lambda i: (i, 0)),
                  pl.BlockSpec((tr, 512), lambda i: (i, 2)), pl.BlockSpec((1, GLA_DV), lambda i: (0, 0))],
        out_specs=pl.BlockSpec((tr, 512), lambda i: (i, 0)),
        out_shape=jax.ShapeDtypeStruct((m, 512), BF16),
        compiler_params=_cp("parallel"),
        name="gla_post",
    )(y_f, y_b, proj, gla_norm.reshape(1, GLA_DV))


def _rwkv_prep_kernel(cur_ref, prev_ref, next_ref, mu_ref, w2_ref, w0_ref, a2_ref, a0_ref, g2_ref,
                      kkw_ref, ka_ref, rk_ref, bd_ref, in_out, lw_out, post_out, *, tiles_per_seq):
    cur = cur_ref[...]
    x_prev, x_next = _neighbours(cur, prev_ref[...], next_ref[...], tiles_per_seq)
    blk = cur + (0.5 * (x_prev + x_next) - cur) * mu_ref[...]
    r, k, v = blk[:, 0:512], blk[:, 512:1024], blk[:, 1024:1536]
    w_logit = w0_ref[...] + _dot_split(jnp.tanh(blk[:, 1536:1600]), w2_ref[...])
    lw_out[...] = -RWKV_DECAY_SCALE * jax.nn.sigmoid(w_logit)
    a = jax.nn.sigmoid(a0_ref[...] + _dot_split(blk[:, 1600:1664], a2_ref[...]))
    g = _dot_split(jax.nn.sigmoid(blk[:, 1664:1792]), g2_ref[...])
    bd = bd_ref[...]
    kk = k * kkw_ref[...]
    kk = kk * lax.rsqrt(_dot_data_mask(kk * kk, bd, 2) + 1e-12)
    k2 = k * (1.0 + (a - 1.0) * ka_ref[...])
    in_out[:, 0:512] = _bf(r)
    in_out[:, 512:1024] = _bf(k2)
    in_out[:, 1024:1536] = _bf(v)
    in_out[:, 1536:2048] = _bf(-kk)
    in_out[:, 2048:2560] = _bf(kk * a)
    post_out[:, 0:512] = _bf(g)
    post_out[:, 512:1024] = _bf(_dot_data_mask(r * k2 * rk_ref[...], bd, 2) * v)


def _rwkv_prep(st, proj, mu, w2, w0, a2, a0, g2, kkw, ka, rk):
    m = proj.shape[0]
    tr = _row_tile(st)
    full = lambda shape: pl.BlockSpec(shape, lambda i: (0, 0))
    row = lambda w: pl.BlockSpec((tr, w), lambda i: (i, 0))
    vec = lambda a: a.reshape(1, -1)
    return pl.pallas_call(
        functools.partial(_rwkv_prep_kernel, tiles_per_seq=st.t // tr),
        grid=(m // tr,),
        in_specs=_halo_specs(st, RWKV_W) + [full((1, RWKV_W)), full((64, 1024)), full((1, 1024)),
                                            full((64, 512)), full((1, 512)), full((128, 512)),
                                            full((1, 512)), full((1, 512)), full((1, 512)), full((512, 512))],
        out_specs=[row(2560), row(1024), row(1024)],
        out_shape=[jax.ShapeDtypeStruct((m, 2560), BF16), jax.ShapeDtypeStruct((m, 1024), F32),
                   jax.ShapeDtypeStruct((m, 1024), BF16)],
        compiler_params=_cp("parallel"),
        name="rwkv_prep",
    )(proj, proj, proj, vec(mu), jnp.concatenate([w2[0], w2[1]], axis=1), vec(w0), a2, vec(a0), g2,
      vec(kkw), vec(ka), vec(rk), _block_diag(512, RWKV_HD, 1.0))


def _rwkv_chunk(x_ref, lw_ref, rows, z):
    incl, strict = _dir_masks(z)
    lw = lw_ref[rows, :]
    cum = _dot_mask_data(incl.astype(BF16), lw, 3)
    cum_x = cum - lw
    mid = cum[CHUNK // 2:CHUNK // 2 + 1]
    glast = cum[CHUNK - 1:CHUNK] if z == 0 else cum[0:1]
    x = x_ref[rows, :]
    r, k, v, al, be = x[:, 0:512], x[:, 512:1024], x[:, 1024:1536], x[:, 1536:2048], x[:, 2048:2560]
    e_mid, e_nmid = jnp.exp(cum - mid), jnp.exp(mid - cum)
    e_out = jnp.exp(glast - cum)
    return dict(incl=incl, strict=strict, v=v,
                r_mid=_bf(r * e_mid), a_mid=_bf(al * jnp.exp(cum_x - mid)),
                b_mid=_bf(be * e_nmid), k_mid=_bf(k * e_nmid),
                r_in=_bf(r * jnp.exp(cum)), a_in=al * jnp.exp(cum_x),
                b_out=_bf(be * e_out), k_out=_bf(k * e_out), e_last=jnp.exp(glast))


_HEAD_SLICES = [slice(h * RWKV_HD, (h + 1) * RWKV_HD) for h in range(RWKV_HEADS)]


def _rwkv_state_free(chunks, hooks):
    hooks = list(hooks)

    def run_hook():
        if hooks:
            hooks.pop(0)()

    chains = [(c, sl) for c in chunks for sl in _HEAD_SLICES]
    run_hook()
    pair = [_dot_nt(_rows([c['a_mid'][:, sl], c['r_mid'][:, sl]]), _rows([c['b_mid'][:, sl], c['k_mid'][:, sl]]))
            for c, sl in chains]
    half = CHUNK
    a_ab = [_bf(jnp.where(c['strict'], p[:half, :half], 0.0)) for (c, _), p in zip(chains, pair)]
    a_ak = [_bf(jnp.where(c['strict'], p[:half, half:], 0.0)) for (c, _), p in zip(chains, pair)]
    a_rb = [_bf(jnp.where(c['incl'], p[half:, :half], 0.0)) for (c, _), p in zip(chains, pair)]
    a_rk = [_bf(jnp.where(c['incl'], p[half:, half:], 0.0)) for (c, _), p in zip(chains, pair)]
    vb = [_bf(c['v'][:, sl]) for c, sl in chains]
    both = [_dot(_rows([ak, rk]), v) for ak, rk, v in zip(a_ak, a_rk, vb)]
    av = [x[:half] for x in both]
    o0 = [x[half:] for x in both]
    sol = [_lanes([c['a_in'][:, sl], x]) for (c, sl), x in zip(chains, av)]
    powr = a_ab
    width = 2 * RWKV_HD
    for it in range(6):
        if it < 5:
            both = [_dot(p, _lanes([_bf(s), p])) for p, s in zip(powr, sol)]
            sol = [s + x[:, :width] for s, x in zip(sol, both)]
            powr = [_bf(x[:, width:]) for x in both]
        else:
            sol = [s + _dot(p, _bf(s)) for p, s in zip(powr, sol)]
        if it in (1, 3):
            run_hook()
    while hooks:
        run_hook()
    return [dict(w=_bf(s[:, :RWKV_HD]), u0=s[:, RWKV_HD:], o0=o, a_rb=a, v=c['v'][:, sl], r_in=c['r_in'][:, sl],
                 b_out=c['b_out'][:, sl], k_out=c['k_out'][:, sl], e_last=c['e_last'][:, sl])
            for (c, sl), s, o, a in zip(chains, sol, o0, a_rb)]


def _rwkv_state_stages(res, state, write_out):
    box = {}

    def read_state():
        box['su'] = [_dot_nt(_rows([c['w'], c['r_in']]), _bf(s)) for c, s in zip(res, state)]

    def update_state():
        box['u'] = [c['u0'] + su[:CHUNK] for c, su in zip(res, box['su'])]
        state[:] = [s * c['e_last'] + _dot_tn(_rows([_bf(u), _bf(c['v'])]), _rows([c['b_out'], c['k_out']]))
                    for c, s, u in zip(res, state, box['u'])]

    def emit():
        write_out([su[CHUNK:] + _dot(c['a_rb'], _bf(u)) + c['o0'] for c, su, u in zip(res, box['su'], box['u'])])

    return [read_state, update_state, emit]


def _rwkv_scan_kernel(xf_ref, xb_ref, lwf_ref, lwb_ref, s0_ref, yf_ref, yb_ref, s_ref, *, ncb):
    @pl.when(pl.program_id(1) == 0)
    def _():
        s_ref[...] = s0_ref[...]

    state = [s_ref[z, h] for z in range(2) for h in range(RWKV_HEADS)]

    def chunk_rows(step, z):
        cc = step if z == 0 else ncb - 1 - step
        return slice(cc * CHUNK, (cc + 1) * CHUNK)

    def pair_inputs(step):
        return [_rwkv_chunk(xf_ref, lwf_ref, chunk_rows(step, 0), 0),
                _rwkv_chunk(xb_ref, lwb_ref, chunk_rows(step, 1), 1)]

    def writer(step):
        def write_out(outs):
            yf_ref[chunk_rows(step, 0), :] = _bf(_lanes(outs[:RWKV_HEADS]))
            yb_ref[chunk_rows(step, 1), :] = _bf(_lanes(outs[RWKV_HEADS:]))
        return write_out

    res = _rwkv_state_free(pair_inputs(0), [])
    for step in range(ncb):
        hooks = _rwkv_state_stages(res, state, writer(step))
        if step + 1 < ncb:
            res = _rwkv_state_free(pair_inputs(step + 1), hooks)
        else:
            for hook in hooks:
                hook()
    for z in range(2):
        for h in range(RWKV_HEADS):
            s_ref[z, h] = state[z * RWKV_HEADS + h]


def _rwkv_scan(st, xin, lw, s0):
    ncb, rb, nblk, fwd, bwd = _scan_blocks(st)
    state_spec = pl.BlockSpec((None, 2, RWKV_HEADS, RWKV_HD, RWKV_HD), lambda b, j: (b, 0, 0, 0, 0))
    return pl.pallas_call(
        functools.partial(_rwkv_scan_kernel, ncb=ncb),
        grid=(st.n, nblk),
        in_specs=[pl.BlockSpec((rb, 2560), fwd), pl.BlockSpec((rb, 2560), bwd),
                  pl.BlockSpec((rb, 512), fwd), pl.BlockSpec((rb, 512), lambda b, j: (bwd(b, j)[0], 1)),
                  state_spec],
        out_specs=[pl.BlockSpec((rb, 512), fwd), pl.BlockSpec((rb, 512), bwd), state_spec],
        out_shape=[jax.ShapeDtypeStruct((st.rows, 512), BF16), jax.ShapeDtypeStruct((st.rows, 512), BF16),
                   jax.ShapeDtypeStruct((st.n, 2, RWKV_HEADS, RWKV_HD, RWKV_HD), F32)],
        compiler_params=_cp("parallel", "arbitrary"),
        name="rwkv_scan",
    )(xin, xin, lw, lw, s0)


def _rwkv_post_kernel(yf_ref, yb_ref, post_ref, lng_ref, lnb_ref, bd_ref, o_ref):
    o = yf_ref[...].astype(F32) + yb_ref[...].astype(F32)
    bd = bd_ref[...]
    d = o - _dot_data_mask(o, bd, 2)
    o = d * lax.rsqrt(_dot_data_mask(d * d, bd, 2) + RWKV_LN_EPS) * lng_ref[...] + lnb_ref[...]
    o_ref[...] = _bf((o + post_ref[:, 512:1024].astype(F32)) * post_ref[:, 0:512].astype(F32))


def _rwkv_post(st, y_f, y_b, post, ln_g, ln_b):
    m = post.shape[0]
    tr = _row_tile(st)
    full = lambda shape: pl.BlockSpec(shape, lambda i: (0, 0))
    row = lambda w: pl.BlockSpec((tr, w), lambda i: (i, 0))
    return pl.pallas_call(
        _rwkv_post_kernel,
        grid=(m // tr,),
        in_specs=[row(512), row(512), row(1024), full((1, 512)), full((1, 512)), full((512, 512))],
        out_specs=row(512),
        out_shape=jax.ShapeDtypeStruct((m, 512), BF16),
        compiler_params=_cp("parallel"),
        name="rwkv_post",
    )(y_f, y_b, post, ln_g.reshape(1, 512), ln_b.reshape(1, 512), _block_diag(512, RWKV_HD, 1.0 / RWKV_HD))


def _merge_kernel(h_ref, o0_ref, o1_ref, o2_ref, o3_ref, wg_ref, wb_ref, out_ref):
    h = h_ref[...]
    acc = None
    for i, o_ref in enumerate((o0_ref, o1_ref, o2_ref, o3_ref)):
        term = jax.nn.sigmoid(_dot(h, wg_ref[i])) * _dot(o_ref[...], wb_ref[i])
        acc = term if acc is None else acc + term
    out_ref[...] = _bf(acc)


def _merge(st, h, branch_outs, wg, wb):
    m, d = h.shape
    tm = _mm_tile(st)
    tn = 512
    row = lambda w: pl.BlockSpec((tm, w), lambda j, i: (i, 0))
    return pl.pallas_call(
        _merge_kernel,
        grid=(d // tn, m // tm),
        in_specs=[row(d)] + [row(BRANCH_W)] * 4 + [pl.BlockSpec((4, d, tn), lambda j, i: (0, 0, j)),
                                                    pl.BlockSpec((4, BRANCH_W, tn), lambda j, i: (0, 0, j))],
        out_specs=pl.BlockSpec((tm, tn), lambda j, i: (i, j)),
        out_shape=jax.ShapeDtypeStruct((m, d), BF16),
        compiler_params=_cp("parallel", "parallel"),
        name="merge",
    )(h, *branch_outs, wg, wb)


def _wo_kernel(m_ref, w_ref, x_ref, mod_ref, g_ref, o_ref, h_out):
    x = x_ref[...] + mod_ref[0, 2:3, :] * _dot(m_ref[...], w_ref[...])
    o_ref[...] = x
    h_out[...] = _norm_mod(x, g_ref[...], mod_ref, 4, 3)


def _out_proj(st, merged, w_o, x, mod_l, gain2):
    m, d = x.shape
    tm = _mm_tile(st)
    row = pl.BlockSpec((tm, d), lambda i: (i, 0))
    return pl.pallas_call(
        _wo_kernel,
        grid=(m // tm,),
        in_specs=[row, pl.BlockSpec((d, d), lambda i: (0, 0)), row, pl.BlockSpec((1, 6, d), st.group_map(tm)),
                  pl.BlockSpec((1, d), lambda i: (0, 0))],
        out_specs=[row, row],
        out_shape=[jax.ShapeDtypeStruct((m, d), F32), jax.ShapeDtypeStruct((m, d), BF16)],
        compiler_params=_cp("parallel"),
        name="out_proj",
    )(merged, w_o, x, mod_l, gain2.reshape(1, d))


def _ffn_kernel(h_ref, w1_ref, w3_ref, w2_ref, x_ref, mod_ref, fg_ref, o_ref, acc_ref, *, final):
    f = pl.program_id(1)

    @pl.when(f == 0)
    def _():
        acc_ref[...] = jnp.zeros_like(acc_ref)

    h = h_ref[...]
    u = _silu(_dot(h, w1_ref[...])) * _dot(h, w3_ref[...])
    acc_ref[...] += _dot(_bf(u), w2_ref[...])

    @pl.when(f == pl.num_programs(1) - 1)
    def _():
        x = x_ref[...] + mod_ref[0, 5:6, :] * acc_ref[...]
        o_ref[...] = _rms(x, fg_ref[...]) if final else x


def _ffn(st, h, w1, w3, w2, x, mod_l, final_gain, final):
    m, d = x.shape
    dff = w1.shape[1]
    tm = _mm_tile(st)
    tf = 512
    return pl.pallas_call(
        functools.partial(_ffn_kernel, final=final),
        grid=(m // tm, dff // tf),
        in_specs=[pl.BlockSpec((tm, d), lambda i, f: (i, 0)),
                  pl.BlockSpec((d, tf), lambda i, f: (0, f)),
                  pl.BlockSpec((d, tf), lambda i, f: (0, f)),
                  pl.BlockSpec((tf, d), lambda i, f: (f, 0)),
                  pl.BlockSpec((tm, d), lambda i, f: (i, 0)),
                  pl.BlockSpec((1, 6, d), lambda i, f: st.group_map(tm)(i)),
                  pl.BlockSpec((1, d), lambda i, f: (0, 0))],
        out_specs=pl.BlockSpec((tm, d), lambda i, f: (i, 0)),
        out_shape=jax.ShapeDtypeStruct((m, d), F32),
        scratch_shapes=[pltpu.VMEM((tm, d), F32)],
        compiler_params=_cp("parallel", "arbitrary"),
        name="ffn",
    )(h, w1, w3, w2, x, mod_l, final_gain.reshape(1, d))


def _split_w_in(w_in):
    z = lambda w: jnp.zeros(w_in.shape[:2] + (w,), w_in.dtype)
    att = w_in[..., 0:768]
    ssd = jnp.concatenate([w_in[..., 768:2064], z(SSD_W - 1296)], axis=-1)
    rwkv = w_in[..., 2064:3856]
    gla = jnp.concatenate([w_in[..., 3856:4880], w_in[..., 4896:5408], w_in[..., 4880:4896], z(LANE - 16)], axis=-1)
    return tuple(_bf(w) for w in (att, ssd, rwkv, gla))


def _block(st, x, mod_l, p, rope_tables, ctx, final_gain, final):
    h, proj_att, proj_ssd = _in_proj_a(st, x, p['norm1'], mod_l, p['w_att'], p['w_ssd'])
    proj_rwkv, proj_gla = _in_proj_b(st, h, p['w_rwkv'], p['w_gla'])

    v = proj_att[:, 640:768]
    if ctx is None:
        q, k_att, v_att, k = _att_prep(st, proj_att, p['q_norm'], p['k_norm'], None)
        o_att = _attention(st, q, k_att, v_att)
        s_ssd = jnp.zeros((st.n, 2, SSD_HEADS, SSD_N, SSD_P), F32)
        s_rwkv = jnp.zeros((st.n, 2, RWKV_HEADS, RWKV_HD, RWKV_HD), F32)
        s_gla_t = jnp.zeros((st.n, 2, GLA_HEADS, GLA_DV, GLA_DK), F32)
    else:
        ctx_k, ctx_v, s_ssd, s_rwkv, s_gla = ctx
        q, k_att, v_att = _att_prep(st, proj_att, p['q_norm'], p['k_norm'], rope_tables)
        k = None
        o_att = _attention(st, q, _join_cache(st, k_att, ctx_k), _join_cache(st, v_att, ctx_v))
        s_gla_t = jnp.swapaxes(s_gla, -1, -2)

    xbc, dt, cum, dtt, cumt = _ssd_prep(st, proj_ssd, p['ssd_conv_w'], p['ssd_conv_b'], p['ssd_dt_bias'],
                                        p['ssd_a_log'])
    y_ssd_f, y_ssd_b, new_ssd = _ssd_scan(st, xbc, dt, cum, dtt, cumt, s_ssd)
    o_ssd = _ssd_post(st, y_ssd_f, y_ssd_b, xbc, proj_ssd, p['ssd_d'], p['ssd_norm'])

    rin, lw, rpost = _rwkv_prep(st, proj_rwkv, p['rwkv_mu'], p['rwkv_w2'], p['rwkv_w0'], p['rwkv_a2'],
                                p['rwkv_a0'], p['rwkv_g2'], p['rwkv_kk'], p['rwkv_ka'], p['rwkv_rk'])
    y_rwkv_f, y_rwkv_b, new_rwkv = _rwkv_scan(st, rin, lw, s_rwkv)
    o_rwkv = _rwkv_post(st, y_rwkv_f, y_rwkv_b, rpost, p['rwkv_ln_g'], p['rwkv_ln_b'])

    g2 = jnp.concatenate([p['gla_g2'][0], p['gla_g2'][1]], axis=1)
    g2p = jnp.pad(g2, ((0, LANE - g2.shape[0]), (0, 0)))
    gla_cum = _gla_prep(st, proj_gla, g2p, p['gla_gb'].reshape(1, 2 * GLA_QK))
    y_gla_f, y_gla_b, new_gla_t = _gla_scan(st, proj_gla, gla_cum, s_gla_t)
    o_gla = _gla_post(st, y_gla_f, y_gla_b, proj_gla, p['gla_norm'])

    merged = _merge(st, h, (o_att, o_ssd, o_rwkv, o_gla), p['w_gate'], p['w_branch'])
    x, h2 = _out_proj(st, merged, p['w_o'], x, mod_l, p['norm2'])
    x = _ffn(st, h2, p['ffn_w1'], p['ffn_w3'], p['ffn_w2'], x, mod_l, final_gain, final)
    return x, (k, v, new_ssd, new_rwkv, jnp.swapaxes(new_gla_t, -1, -2))


def kernel(x_prompt, x_sample, cache_attn_k, cache_attn_v, state_ssd, state_rwkv, state_gla, c, c_ctx, w_mod, b_mod, norm1, norm2, w_in, q_norm, k_norm, ssd_conv_w, ssd_conv_b, ssd_dt_bias, ssd_a_log, ssd_d, ssd_norm, rwkv_mu, rwkv_w0, rwkv_w2, rwkv_a0, rwkv_a2, rwkv_g2, rwkv_kk, rwkv_ka, rwkv_rk, rwkv_ln_g, rwkv_ln_b, gla_g2, gla_gb, gla_norm, w_gate, w_branch, w_o, ffn_w1, ffn_w3, ffn_w2, final_norm):
    nb, seq, d = x_prompt.shape
    db, dseq, _ = x_sample.shape
    depth = w_in.shape[0]
    assert d == D_MODEL and seq % CHUNK == 0 and dseq % CHUNK == 0 and 1 + db <= MOD_ROWS
    ctx_st = _Stream(nb, seq, 0, False)
    lat_st = _Stream(db, dseq, 1, True)

    cond = jnp.concatenate([c_ctx[None], c, jnp.zeros((MOD_ROWS - 1 - db, d), F32)], axis=0)
    mod = _modulation(cond, w_mod, b_mod)

    w_att, w_ssd, w_rwkv, w_gla = _split_w_in(w_in)
    w_gate_b, w_branch_b, w_o_b = _bf(w_gate), _bf(w_branch), _bf(w_o)
    w1_b, w3_b, w2_b = _bf(ffn_w1), _bf(ffn_w3), _bf(ffn_w2)

    def params_at(l):
        return dict(norm1=norm1[l], norm2=norm2[l], w_att=w_att[l], w_ssd=w_ssd[l], w_rwkv=w_rwkv[l],
                    w_gla=w_gla[l], q_norm=q_norm[l], k_norm=k_norm[l],
                    ssd_conv_w=ssd_conv_w[l], ssd_conv_b=ssd_conv_b[l], ssd_dt_bias=ssd_dt_bias[l],
                    ssd_a_log=ssd_a_log[l], ssd_d=ssd_d[l], ssd_norm=ssd_norm[l],
                    rwkv_mu=rwkv_mu[l], rwkv_w0=rwkv_w0[l], rwkv_w2=rwkv_w2[l], rwkv_a0=rwkv_a0[l],
                    rwkv_a2=rwkv_a2[l], rwkv_g2=rwkv_g2[l], rwkv_kk=rwkv_kk[l], rwkv_ka=rwkv_ka[l],
                    rwkv_rk=rwkv_rk[l], rwkv_ln_g=rwkv_ln_g[l], rwkv_ln_b=rwkv_ln_b[l],
                    gla_g2=gla_g2[l], gla_gb=gla_gb[l], gla_norm=gla_norm[l],
                    w_gate=w_gate_b[l], w_branch=w_branch_b[l], w_o=w_o_b[l],
                    ffn_w1=w1_b[l], ffn_w3=w3_b[l], ffn_w2=w2_b[l])

    xp = x_prompt.reshape(nb * seq, d)
    new_k, new_v, new_ssd, new_rwkv, new_gla = [], [], [], [], []
    for l in range(depth):
        xp, (k_l, v_l, ssd_l, rwkv_l, gla_l) = _block(ctx_st, xp, mod[l], params_at(l), None, None, final_norm,
                                                      l == depth - 1)
        new_k.append(k_l.reshape(nb, seq, ATT_KV, HEAD_DIM))
        new_v.append(v_l.reshape(nb, seq, ATT_KV, HEAD_DIM))
        new_ssd.append(ssd_l)
        new_rwkv.append(rwkv_l)
        new_gla.append(gla_l)

    rope_tables = _rope_tables(dseq)
    xs = x_sample.reshape(db * dseq, d)
    for l in range(depth):
        ctx = (cache_attn_k[:, l], cache_attn_v[:, l], state_ssd[:, l], state_rwkv[:, l], state_gla[:, l])
        xs, _ = _block(lat_st, xs, mod[l], params_at(l), rope_tables, ctx, final_norm, l == depth - 1)

    y_prompt = xp.reshape(nb, seq, d)
    y_sample = xs.reshape(db, dseq, d)
    return (y_prompt, y_sample, jnp.stack(new_k, axis=1), jnp.stack(new_v, axis=1),
            jnp.stack(new_ssd, axis=1), jnp.stack(new_rwkv, axis=1), jnp.stack(new_gla, axis=1))
```

```python
import functools

import jax
import jax.numpy as jnp
import numpy as np
from jax import lax
from jax.experimental import pallas as pl
from jax.experimental.pallas import tpu as pltpu

F32 = jnp.float32
BF16 = jnp.bfloat16

D_MODEL = 2048
GRID_W = 64
ATT_HEADS = 8
ATT_KV = 2
ATT_GROUP = ATT_HEADS // ATT_KV
HEAD_DIM = 64
ROPE_THETA = 10000.0
SSD_HEADS = 8
SSD_P = 64
SSD_N = 64
SSD_GROUPS = 2
RWKV_HEADS = 8
RWKV_HD = 64
RWKV_DECAY_SCALE = 0.6065306597126334
RWKV_LN_EPS = 64e-5
GLA_HEADS = 4
GLA_DK = 64
GLA_DV = 128
GLA_GATE_NORM = 16.0
CHUNK = 64
BRANCH_W = 512

ATT_W = 768
SSD_W = 1408
RWKV_W = 1792
GLA_W = 1664
LANE = 128
SUBLANE = 8
MOD_ROWS = 16
VMEM_LIMIT = 56 * 1024 * 1024
ROW_TILE = 256
MM_TILE = 512
SCAN_CHUNKS = 4
KEY_TILE = 256
ATT_Q_TILE = 256
KEY_UNROLL = 6
LOG2E = 1.4426950408889634
ATT_MIN_ROW_SUM = 2.0 ** -90


def _cp(*sem):
    return pltpu.CompilerParams(dimension_semantics=sem, vmem_limit_bytes=VMEM_LIMIT)


def _bf(x):
    return x.astype(BF16)


def _dot(a, b):
    return jnp.dot(a, b, preferred_element_type=F32)


def _dot_nt(a, b):
    return lax.dot_general(a, b, (((1,), (1,)), ((), ())), preferred_element_type=F32)


def _dot_tn(a, b):
    return lax.dot_general(a, b, (((0,), (0,)), ((), ())), preferred_element_type=F32)


def _pieces(a, n):
    out = []
    for _ in range(n):
        piece = _bf(a)
        out.append(piece)
        a = a - piece.astype(F32)
    return out


def _dot_data_mask(a, mask, n):
    return functools.reduce(jnp.add, [_dot(piece, mask) for piece in _pieces(a, n)])


def _dot_mask_data(mask, a, n):
    return functools.reduce(jnp.add, [_dot(mask, piece) for piece in _pieces(a, n)])


def _dot_nt_mask_data(mask, a, n):
    return functools.reduce(jnp.add, [_dot_nt(mask, piece) for piece in _pieces(a, n)])


def _dot_split(a, b):
    a_hi, a_lo = _pieces(a, 2)
    b_hi, b_lo = _pieces(b, 2)
    return _dot(a_hi, b_hi) + (_dot(a_hi, b_lo) + _dot(a_lo, b_hi))


def _silu(x):
    return x * jax.nn.sigmoid(x)


def _softplus(x):
    return jnp.maximum(x, 0.0) + jnp.log1p(jnp.exp(-jnp.abs(x)))


def _lanes(pieces):
    return jnp.concatenate(pieces, axis=1)


def _rows(pieces):
    return jnp.concatenate(pieces, axis=0)


def _mod_kernel(c_ref, w_ref, b_ref, o_ref):
    c = c_ref[...]
    o_ref[0] = _dot(_bf(_silu(c)), _bf(w_ref[0])) + b_ref[0]


def _modulation(cond, w_mod, b_mod):
    nl, d, n6 = w_mod.shape
    tn = 1024
    out = pl.pallas_call(
        _mod_kernel,
        grid=(nl, n6 // tn),
        in_specs=[pl.BlockSpec((MOD_ROWS, d), lambda l, j: (0, 0)),
                  pl.BlockSpec((1, d, tn), lambda l, j: (l, 0, j)),
                  pl.BlockSpec((1, 1, tn), lambda l, j: (l, 0, j))],
        out_specs=pl.BlockSpec((1, MOD_ROWS, tn), lambda l, j: (l, 0, j)),
        out_shape=jax.ShapeDtypeStruct((nl, MOD_ROWS, n6), F32),
        compiler_params=_cp("parallel", "parallel"),
        name="modulation",
    )(cond, w_mod, b_mod.reshape(nl, 1, n6))
    return out.reshape(nl, MOD_ROWS, 6, d)


class _Stream:
    def __init__(self, n, t, group0, per_seq):
        self.n, self.t, self.group0, self.per_seq = n, t, group0, per_seq
        self.rows = n * t

    def group_map(self, tile):
        g0, per_seq, t = self.group0, self.per_seq, self.t
        if per_seq:
            return lambda i: (g0 + (i * tile) // t, 0, 0)
        return lambda i: (g0, 0, 0)


def _mm_tile(st):
    limit = st.t if st.per_seq else st.rows
    return min(MM_TILE, limit)


def _row_tile(st):
    return min(ROW_TILE, st.t)


def _rms(x, gain):
    return x * lax.rsqrt(jnp.mean(x * x, axis=-1, keepdims=True) + 1e-6) * gain


def _norm_mod(x, gain, mod_ref, sc_idx, sh_idx):
    return _bf(_rms(x, gain) * (1.0 + mod_ref[0, sc_idx:sc_idx + 1, :]) + mod_ref[0, sh_idx:sh_idx + 1, :])


def _in_proj_a_kernel(x_ref, g_ref, mod_ref, wa_ref, ws_ref, h_out, pa_out, ps_out):
    h = _norm_mod(x_ref[...], g_ref[...], mod_ref, 1, 0)
    h_out[...] = h
    pa_out[...] = _dot(h, wa_ref[...])
    ps_out[...] = _dot(h, ws_ref[...])


def _in_proj_a(st, x, gain, mod_l, w_att, w_ssd):
    m, d = x.shape
    tm = _mm_tile(st)
    row = lambda w: pl.BlockSpec((tm, w), lambda i: (i, 0))
    full = lambda a: pl.BlockSpec(a.shape, lambda i: (0, 0))
    return pl.pallas_call(
        _in_proj_a_kernel,
        grid=(m // tm,),
        in_specs=[row(d), pl.BlockSpec((1, d), lambda i: (0, 0)), pl.BlockSpec((1, 6, d), st.group_map(tm)),
                  full(w_att), full(w_ssd)],
        out_specs=[row(d), row(ATT_W), row(SSD_W)],
        out_shape=[jax.ShapeDtypeStruct((m, d), BF16), jax.ShapeDtypeStruct((m, ATT_W), F32),
                   jax.ShapeDtypeStruct((m, SSD_W), F32)],
        compiler_params=_cp("parallel"),
        name="in_proj_a",
    )(x, gain.reshape(1, d), mod_l, w_att, w_ssd)


def _in_proj_b_kernel(h_ref, wr_ref, wg_ref, pr_out, pg_out):
    h = h_ref[...]
    pr_out[...] = _dot(h, wr_ref[...])
    pg_out[...] = _dot(h, wg_ref[...])


def _in_proj_b(st, h, w_rwkv, w_gla):
    m, d = h.shape
    tm = _mm_tile(st)
    row = lambda w: pl.BlockSpec((tm, w), lambda i: (i, 0))
    full = lambda a: pl.BlockSpec(a.shape, lambda i: (0, 0))
    return pl.pallas_call(
        _in_proj_b_kernel,
        grid=(m // tm,),
        in_specs=[row(d), full(w_rwkv), full(w_gla)],
        out_specs=[row(RWKV_W), row(GLA_W)],
        out_shape=[jax.ShapeDtypeStruct((m, RWKV_W), F32), jax.ShapeDtypeStruct((m, GLA_W), F32)],
        compiler_params=_cp("parallel"),
        name="in_proj_b",
    )(h, w_rwkv, w_gla)


def _att_prep_kernel(*refs, rope):
    if rope:
        p_ref, qn_ref, kn_ref, bdq_ref, bdk_ref, cos_ref, sa_ref, sb_ref, q_out, k_out, v_out = refs
    else:
        p_ref, qn_ref, kn_ref, bdq_ref, bdk_ref, q_out, k_out, v_out, k_leaf = refs
    p = p_ref[...]
    aq = p[:, :512]
    ak = p[:, 512:640]
    av = p[:, 640:768]
    q = aq * lax.rsqrt(_dot_data_mask(aq * aq, bdq_ref[...], 2) + 1e-6) * qn_ref[...]
    k = ak * lax.rsqrt(_dot_data_mask(ak * ak, bdk_ref[...], 2) + 1e-6) * kn_ref[...]
    if rope:
        c, sa, sb = cos_ref[...], sa_ref[...], sb_ref[...]
        k = k * c + pltpu.roll(k, LANE - 16, 1) * sa + pltpu.roll(k, 16, 1) * sb
        c4, sa4, sb4 = _lanes([c] * 4), _lanes([sa] * 4), _lanes([sb] * 4)
        q = q * c4 + pltpu.roll(q, 512 - 16, 1) * sa4 + pltpu.roll(q, 16, 1) * sb4
    else:
        k_leaf[...] = k
    tr = p.shape[0]
    spare = lax.broadcasted_iota(jnp.int32, (tr, LANE - HEAD_DIM), 1)
    one_col = (spare == 0).astype(F32)
    two_cols = (spare < 2).astype(F32)
    q = q * (HEAD_DIM ** -0.5 * LOG2E)
    q_len = jnp.sqrt(_dot_data_mask(q * q, bdq_ref[...], 2) * HEAD_DIM)
    q_out[...] = _bf(_lanes([piece for h in range(ATT_HEADS)
                             for piece in (q[:, h * HEAD_DIM:(h + 1) * HEAD_DIM],
                                           one_col * q_len[:, h * HEAD_DIM:(h + 1) * HEAD_DIM])]))
    for kv in range(ATT_KV):
        sl = slice(kv * HEAD_DIM, (kv + 1) * HEAD_DIM)
        k_out[kv] = _bf(_lanes([k[:, sl], two_cols]))
        v_out[kv] = _bf(_lanes([av[:, sl], one_col]))


def _block_diag(width, block, value):
    idx = np.arange(width) // block
    return _bf(jnp.asarray((idx[:, None] == idx[None, :]).astype(np.float32) * value))


def _att_prep(st, proj, q_norm, k_norm, rope_tables):
    m = proj.shape[0]
    tr = _row_tile(st)
    rope = rope_tables is not None
    full = lambda shape: pl.BlockSpec(shape, lambda i: (0, 0))
    in_specs = [pl.BlockSpec((tr, ATT_W), lambda i: (i, 0)), full((1, 512)), full((1, 128)),
                full((512, 512)), full((128, 128))]
    args = [proj, jnp.tile(q_norm, ATT_HEADS).reshape(1, 512), jnp.tile(k_norm, ATT_KV).reshape(1, 128),
            _block_diag(512, HEAD_DIM, 1.0 / HEAD_DIM), _block_diag(128, HEAD_DIM, 1.0 / HEAD_DIM)]
    if rope:
        tps = st.t // tr
        in_specs += [pl.BlockSpec((tr, LANE), lambda i: (i % tps, 0))] * 3
        args += list(rope_tables)
    kv_spec = pl.BlockSpec((ATT_KV, tr, LANE), lambda i: (0, i, 0))
    kv_shape = jax.ShapeDtypeStruct((ATT_KV, m, LANE), BF16)
    out_specs = [pl.BlockSpec((tr, ATT_HEADS * LANE), lambda i: (i, 0)), kv_spec, kv_spec]
    out_shape = [jax.ShapeDtypeStruct((m, ATT_HEADS * LANE), BF16), kv_shape, kv_shape]
    if not rope:
        out_specs.append(pl.BlockSpec((tr, 128), lambda i: (i, 0)))
        out_shape.append(jax.ShapeDtypeStruct((m, 128), F32))
    return pl.pallas_call(
        functools.partial(_att_prep_kernel, rope=rope),
        grid=(m // tr,),
        in_specs=in_specs,
        out_specs=out_specs,
        out_shape=out_shape,
        compiler_params=_cp("parallel"),
        name="att_prep",
    )(*args)


def _join_cache(st, own, cache, ones):
    n, p, kv, hd = cache.shape
    c = jnp.transpose(cache, (2, 0, 1, 3))
    pad = jnp.zeros((kv, n, p, LANE - hd), c.dtype).at[..., :ones].set(1.0)
    c = _bf(jnp.concatenate([c, pad], axis=-1))
    joined = jnp.concatenate([own.reshape(kv, n, st.t, LANE), c], axis=2)
    return joined.reshape(kv, n * (st.t + p), LANE)


def _rope_tables(t):
    half = HEAD_DIM // 2
    nf = half // 2
    freqs = ROPE_THETA ** (-jnp.arange(nf, dtype=F32) / nf)
    tt = jnp.arange(t)
    ang_r = (tt // GRID_W).astype(F32)[:, None] * freqs[None, :]
    ang_c = (tt % GRID_W).astype(F32)[:, None] * freqs[None, :]
    zero = jnp.zeros_like(ang_r)
    cos = jnp.concatenate([jnp.cos(ang_r)] * 2 + [jnp.cos(ang_c)] * 2, axis=1)
    sa = jnp.concatenate([-jnp.sin(ang_r), zero, -jnp.sin(ang_c), zero], axis=1)
    sb = jnp.concatenate([zero, jnp.sin(ang_r), zero, jnp.sin(ang_c)], axis=1)
    return tuple(jnp.tile(a, (1, LANE // HEAD_DIM)) for a in (cos, sa, sb))


def _key_bound_kernel(k_ref, o_ref):
    ones = jnp.ones((LANE, LANE), BF16)
    for kv in range(ATT_KV):
        k = k_ref[kv].astype(F32)
        k = jnp.where(lax.broadcasted_iota(jnp.int32, k.shape, 1) < HEAD_DIM, k, 0.0)
        best = jnp.max(_dot_data_mask(k * k, ones, 2), axis=0, keepdims=True)
        o_ref[kv] = jnp.broadcast_to(best, (SUBLANE, LANE))


def _key_bound(n, keys_per_seq, k):
    return pl.pallas_call(
        _key_bound_kernel,
        grid=(n,),
        in_specs=[pl.BlockSpec((ATT_KV, keys_per_seq, LANE), lambda b: (0, b, 0))],
        out_specs=pl.BlockSpec((None, ATT_KV, SUBLANE, LANE), lambda b: (b, 0, 0, 0)),
        out_shape=jax.ShapeDtypeStruct((n, ATT_KV, SUBLANE, LANE), F32),
        compiler_params=_cp("parallel"),
        name="key_bound",
    )(k)


def _attn_kernel(q_ref, k_ref, v_ref, kb_ref, o_ref, m_scr, acc_scr, *, tq, n_tiles):
    rows = ATT_GROUP * tq
    lane = lax.broadcasted_iota(jnp.int32, (rows, LANE), 1)
    outs = []
    for kv in range(ATT_KV):
        qs = _rows([q_ref[:, (kv * ATT_GROUP + g) * LANE:(kv * ATT_GROUP + g + 1) * LANE]
                    for g in range(ATT_GROUP)]).astype(F32)

        def key_tile(ref, j):
            return ref[kv, pl.ds(pl.multiple_of(j * KEY_TILE, KEY_TILE), KEY_TILE), :]

        def weighted_sum(q_shift, unroll):
            acc_scr[...] = jnp.zeros((rows, LANE), F32)

            def sum_body(j, carry):
                p = jnp.exp2(_dot_nt(q_shift, key_tile(k_ref, j)))
                acc_scr[...] += _dot(_bf(p), key_tile(v_ref, j))
                return carry

            lax.fori_loop(0, n_tiles, sum_body, 0, unroll=unroll)
            return acc_scr[...]

        k_len = jnp.sqrt(kb_ref[kv][0:1, :])
        unroll = max(u for u in range(1, KEY_UNROLL + 1) if n_tiles % u == 0)
        acc = weighted_sum(_bf(qs * jnp.where(lane == HEAD_DIM, -k_len, 1.0)), unroll)

        def exact_shift():
            m_scr[...] = jnp.full((rows, LANE), -jnp.inf, F32)
            q0 = _bf(jnp.where(lane >= HEAD_DIM, 0.0, qs))

            def max_body(j, carry):
                sc = _dot_nt(q0, key_tile(k_ref, j))
                m_scr[...] = jnp.maximum(m_scr[...], jnp.maximum(sc[:, :LANE], sc[:, LANE:]))
                return carry

            lax.fori_loop(0, n_tiles, max_body, 0)
            row_max = jnp.max(m_scr[...], axis=-1, keepdims=True)
            hi = _bf(row_max).astype(F32)
            shifted = jnp.where(lane == HEAD_DIM, -hi, jnp.where(lane == HEAD_DIM + 1, hi - row_max, qs))
            return weighted_sum(_bf(shifted), 1)

        row_sum = acc[:, HEAD_DIM:HEAD_DIM + 1]
        row_sum_ok = jnp.logical_and(jnp.min(row_sum) >= ATT_MIN_ROW_SUM, jnp.max(row_sum) <= 1.0 / ATT_MIN_ROW_SUM)
        acc = lax.cond(row_sum_ok, lambda: acc, exact_shift)
        o = acc[:, :HEAD_DIM] / acc[:, HEAD_DIM:HEAD_DIM + 1]
        outs += [o[g * tq:(g + 1) * tq] for g in range(ATT_GROUP)]
    o_ref[...] = _bf(_lanes(outs))


def _attention(st, q, k, v):
    tq = min(ATT_Q_TILE, st.t)
    nq = st.t // tq
    keys = k.shape[1] // st.n
    assert keys % KEY_TILE == 0
    kv_spec = pl.BlockSpec((ATT_KV, keys, LANE), lambda b, i: (0, b, 0))
    rows = ATT_GROUP * tq
    return pl.pallas_call(
        functools.partial(_attn_kernel, tq=tq, n_tiles=keys // KEY_TILE),
        grid=(st.n, nq),
        in_specs=[pl.BlockSpec((tq, ATT_HEADS * LANE), lambda b, i: (b * nq + i, 0)), kv_spec, kv_spec,
                  pl.BlockSpec((None, ATT_KV, SUBLANE, LANE), lambda b, i: (b, 0, 0, 0))],
        out_specs=pl.BlockSpec((tq, 512), lambda b, i: (b * nq + i, 0)),
        out_shape=jax.ShapeDtypeStruct((st.rows, 512), BF16),
        scratch_shapes=[pltpu.VMEM((rows, LANE), F32), pltpu.VMEM((rows, LANE), F32)],
        compiler_params=_cp("parallel", "parallel"),
        name="attention",
    )(q, k, v, _key_bound(st.n, keys, k))


def _halo_specs(st, width):
    tr = _row_tile(st)
    per8 = tr // SUBLANE
    last8 = st.rows // SUBLANE - 1
    return [pl.BlockSpec((tr, width), lambda i: (i, 0)),
            pl.BlockSpec((SUBLANE, width), lambda i: (jnp.maximum(i * per8 - 1, 0), 0)),
            pl.BlockSpec((SUBLANE, width), lambda i: (jnp.minimum((i + 1) * per8, last8), 0))]


def _neighbours(cur, prev8, next8, tiles_per_seq):
    tr = cur.shape[0]
    j = pl.program_id(0) % tiles_per_seq
    pr = jnp.where(j != 0, prev8[SUBLANE - 1:SUBLANE], 0.0)
    nx = jnp.where(j != tiles_per_seq - 1, next8[0:1], 0.0)
    row = lax.broadcasted_iota(jnp.int32, cur.shape, 0)
    x_prev = jnp.where(row == 0, pr, pltpu.roll(cur, 1, 0))
    x_next = jnp.where(row == tr - 1, nx, pltpu.roll(cur, tr - 1, 0))
    return x_prev, x_next


def _dir_masks(z):
    ri = lax.broadcasted_iota(jnp.int32, (CHUNK, CHUNK), 0)
    ci = lax.broadcasted_iota(jnp.int32, (CHUNK, CHUNK), 1)
    return ((ci <= ri), (ci < ri)) if z == 0 else ((ci >= ri), (ci > ri))


def _tri_blocks(tr):
    i = np.arange(tr)
    same = (i[:, None] // CHUNK) == (i[None, :] // CHUNK)
    lower = same & (i[None, :] <= i[:, None])
    upper = same & (i[None, :] >= i[:, None])
    return _bf(jnp.asarray(lower.astype(np.float32))), _bf(jnp.asarray(upper.astype(np.float32)))


def _scan_blocks(st):
    ncb = min(SCAN_CHUNKS, st.t // CHUNK)
    rb = ncb * CHUNK
    nblk = st.t // rb
    fwd = lambda b, j: (b * nblk + j, 0)
    bwd = lambda b, j: (b * nblk + nblk - 1 - j, 0)
    return ncb, rb, nblk, fwd, bwd


def _chunk_rows(step, z, ncb):
    cc = step if z == 0 else ncb - 1 - step
    return slice(cc * CHUNK, (cc + 1) * CHUNK)


def _ssd_prep_kernel(cur_ref, prev_ref, next_ref, cw_ref, cb_ref, dtb_ref, alog_ref, lo_ref, up_ref,
                     xbc_out, dt_out, cum_out, dtt_out, cumt_out, *, tiles_per_seq):
    cur = cur_ref[...]
    xc = cur[:, 512:1280]
    x_prev, x_next = _neighbours(xc, prev_ref[:, 512:1280], next_ref[:, 512:1280], tiles_per_seq)
    cw = cw_ref[...]
    conv = cb_ref[...] + x_prev * cw[0:1] + xc * cw[1:2] + x_next * cw[2:3]
    xbc_out[...] = _silu(conv)
    dt = _softplus(cur[:, 1280:1408] + dtb_ref[...])
    dt_out[...] = dt
    ld = dt * -jnp.exp(alog_ref[...])
    lane = lax.broadcasted_iota(jnp.int32, ld.shape, 1)
    cum = jnp.where(lane < SSD_HEADS, _dot_mask_data(lo_ref[...], ld, 3), _dot_mask_data(up_ref[...], ld, 3))
    cum_out[...] = cum
    pick = (lax.broadcasted_iota(jnp.int32, (2 * SSD_HEADS, LANE), 0)
            == lax.broadcasted_iota(jnp.int32, (2 * SSD_HEADS, LANE), 1)).astype(BF16)
    dtt_out[...] = _dot_nt_mask_data(pick, dt, 3)
    cumt_out[...] = _dot_nt_mask_data(pick, cum, 3)


def _pad_lanes(v, width=LANE):
    v = v.reshape(1, -1)
    return jnp.pad(v, ((0, 0), (0, width - v.shape[1])))


def _ssd_prep(st, proj, conv_w, conv_b, dt_bias, a_log):
    m = proj.shape[0]
    tr = _row_tile(st)
    full = lambda shape: pl.BlockSpec(shape, lambda i: (0, 0))
    row = lambda w: pl.BlockSpec((tr, w), lambda i: (i, 0))
    col = pl.BlockSpec((2 * SSD_HEADS, tr), lambda i: (0, i))
    lower, upper = _tri_blocks(tr)
    return pl.pallas_call(
        functools.partial(_ssd_prep_kernel, tiles_per_seq=st.t // tr),
        grid=(m // tr,),
        in_specs=_halo_specs(st, SSD_W) + [full((3, 768)), full((1, 768)), full((1, LANE)), full((1, LANE)),
                                           full((tr, tr)), full((tr, tr))],
        out_specs=[row(768), row(LANE), row(LANE), col, col],
        out_shape=[jax.ShapeDtypeStruct((m, 768), F32), jax.ShapeDtypeStruct((m, LANE), F32),
                   jax.ShapeDtypeStruct((m, LANE), F32), jax.ShapeDtypeStruct((2 * SSD_HEADS, m), F32),
                   jax.ShapeDtypeStruct((2 * SSD_HEADS, m), F32)],
        compiler_params=_cp("parallel"),
        name="ssd_prep",
    )(proj, proj, proj, conv_w.T, conv_b.reshape(1, 768), _pad_lanes(dt_bias), _pad_lanes(a_log), lower, upper)


def _ssd_scan_kernel(xf_ref, xb_ref, dtf_ref, dtb_ref, cf_ref, cb_ref, dttf_ref, dttb_ref, ctf_ref, ctb_ref,
                     sel_ref, s0_ref, yf_ref, yb_ref, s_ref, *, ncb):
    @pl.when(pl.program_id(1) == 0)
    def _():
        s_ref[...] = s0_ref[...]

    hpg = SSD_HEADS // SSD_GROUPS
    state = {(z, g): _lanes([s_ref[z, g * hpg + hh] for hh in range(hpg)])
             for z in range(2) for g in range(SSD_GROUPS)}
    refs = ((xf_ref, dtf_ref, cf_ref, dttf_ref, ctf_ref, yf_ref), (xb_ref, dtb_ref, cb_ref, dttb_ref, ctb_ref, yb_ref))
    groups = [(z, g) for z in range(2) for g in range(SSD_GROUPS)]
    heads = [(z, h) for z in range(2) for h in range(SSD_HEADS)]
    for step in range(ncb):
        xbc, dt, cum, dtt, cumt, incl, rows = {}, {}, {}, {}, {}, {}, {}
        for z in range(2):
            x_ref, dt_ref, c_ref, dtt_ref, ct_ref, _ = refs[z]
            rows[z] = _chunk_rows(step, z, ncb)
            xbc[z] = x_ref[rows[z], :]
            dt[z] = _dot_data_mask(dt_ref[rows[z], :], sel_ref[z], 3)
            cum[z] = _dot_data_mask(c_ref[rows[z], :], sel_ref[z], 3)
            dtt[z], cumt[z] = dtt_ref[:, rows[z]], ct_ref[:, rows[z]]
            incl[z] = _dir_masks(z)[0]
        bmat = {(z, g): xbc[z][:, 512 + g * SSD_N:512 + (g + 1) * SSD_N] for z, g in groups}
        cmat = {(z, g): _bf(xbc[z][:, 640 + g * SSD_N:640 + (g + 1) * SSD_N]) for z, g in groups}
        cb = {k: _dot_nt(cmat[k], _bf(bmat[k])) for k in groups}
        cs = {k: _dot(cmat[k], _bf(state[k])) for k in groups}
        gw = hpg * SSD_P
        glast = {z: cum[z][CHUNK - 1:CHUNK] if z == 0 else cum[z][0:1] for z in range(2)}
        xs = {z: xbc[z][:, :SSD_HEADS * SSD_P] for z in range(2)}
        xw = {z: xs[z] * (dt[z] * jnp.exp(glast[z] - cum[z])) for z in range(2)}
        inc = {(z, g): _dot_tn(_bf(bmat[z, g]), _bf(xw[z][:, g * gw:(g + 1) * gw])) for z, g in groups}
        att = {}
        for z, h in heads:
            ln = z * SSD_HEADS + h
            gcol = cum[z][:, h * SSD_P:(h + 1) * SSD_P]
            dec = jnp.exp(jnp.where(incl[z], gcol - cumt[z][ln:ln + 1], -jnp.inf))
            att[z, h] = _bf(cb[z, h // hpg] * dec * dtt[z][ln:ln + 1])
        intra = {(z, h): _dot(att[z, h], _bf(xs[z][:, h * SSD_P:(h + 1) * SSD_P])) for z, h in heads}
        for z in range(2):
            e_in = jnp.exp(cum[z])
            e_last = jnp.exp(glast[z])
            outs = []
            for g in range(SSD_GROUPS):
                hs = range(g * hpg, (g + 1) * hpg)
                outs.append(_lanes([intra[z, h] for h in hs]) + cs[z, g] * e_in[:, g * gw:(g + 1) * gw])
                state[z, g] = state[z, g] * e_last[:, g * gw:(g + 1) * gw] + inc[z, g]
            refs[z][5][rows[z], :] = _bf(_lanes(outs))
    for z in range(2):
        for h in range(SSD_HEADS):
            s_ref[z, h] = state[z, h // hpg][:, (h % hpg) * SSD_P:(h % hpg + 1) * SSD_P]


def _ssd_scan(st, xbc, dt, cum, dtt, cumt, s0):
    ncb, rb, nblk, fwd, bwd = _scan_blocks(st)
    fwd_t = lambda b, j: (0, fwd(b, j)[0])
    bwd_t = lambda b, j: (0, bwd(b, j)[0])
    state_spec = pl.BlockSpec((None, 2, SSD_HEADS, SSD_N, SSD_P), lambda b, j: (b, 0, 0, 0, 0))
    rows = lambda w, m: pl.BlockSpec((rb, w), m)
    cols = lambda m: pl.BlockSpec((2 * SSD_HEADS, rb), m)
    lane = np.arange(LANE)[None, :, None]
    col = np.arange(SSD_HEADS * SSD_P)[None, None, :]
    sel = _bf(jnp.asarray((lane == np.arange(2)[:, None, None] * SSD_HEADS + col // SSD_P).astype(np.float32)))
    return pl.pallas_call(
        functools.partial(_ssd_scan_kernel, ncb=ncb),
        grid=(st.n, nblk),
        in_specs=[rows(768, fwd), rows(768, bwd), rows(LANE, fwd), rows(LANE, bwd), rows(LANE, fwd), rows(LANE, bwd),
                  cols(fwd_t), cols(bwd_t), cols(fwd_t), cols(bwd_t),
                  pl.BlockSpec(sel.shape, lambda b, j: (0, 0, 0)), state_spec],
        out_specs=[rows(512, fwd), rows(512, bwd), state_spec],
        out_shape=[jax.ShapeDtypeStruct((st.rows, 512), BF16), jax.ShapeDtypeStruct((st.rows, 512), BF16),
                   jax.ShapeDtypeStruct((st.n, 2, SSD_HEADS, SSD_N, SSD_P), F32)],
        compiler_params=_cp("parallel", "arbitrary"),
        name="ssd_scan",
    )(xbc, xbc, dt, dt, cum, cum, dtt, dtt, cumt, cumt, sel, s0)


def _ssd_post_kernel(yf_ref, yb_ref, xbc_ref, p_ref, d_ref, g_ref, o_ref):
    y = yf_ref[...].astype(F32) + yb_ref[...].astype(F32) + d_ref[...] * xbc_ref[:, :512]
    y = y * _silu(p_ref[:, :512])
    o_ref[...] = _bf(y * lax.rsqrt(jnp.mean(y * y, axis=-1, keepdims=True) + 1e-6) * g_ref[...])


def _ssd_post(st, y_f, y_b, xbc, proj, ssd_d, ssd_norm):
    m = proj.shape[0]
    tr = _row_tile(st)
    full = lambda shape: pl.BlockSpec(shape, lambda i: (0, 0))
    row = lambda w: pl.BlockSpec((tr, w), lambda i: (i, 0))
    return pl.pallas_call(
        _ssd_post_kernel,
        grid=(m // tr,),
        in_specs=[row(512), row(512), row(512), row(512), full((1, 512)), full((1, 512))],
        out_specs=row(512),
        out_shape=jax.ShapeDtypeStruct((m, 512), BF16),
        compiler_params=_cp("parallel"),
        name="ssd_post",
    )(y_f, y_b, xbc, proj, jnp.repeat(ssd_d, SSD_P).reshape(1, 512), ssd_norm.reshape(1, 512))


GLA_QK = GLA_HEADS * GLA_DK
GLA_SAFE_RANGE = 60.0


def _gla_prep_kernel(p_ref, g2_ref, gb_ref, lo_ref, up_ref, cum_out):
    logit = _dot_split(p_ref[...], g2_ref[...]) + gb_ref[...]
    log_a = -_softplus(-logit) * (1.0 / GLA_GATE_NORM)
    cum_out[:, :GLA_QK] = _dot_mask_data(lo_ref[...], log_a[:, :GLA_QK], 3)
    cum_out[:, GLA_QK:] = _dot_mask_data(up_ref[...], log_a[:, GLA_QK:], 3)


def _gla_prep(st, proj, g2p, gb):
    m = proj.shape[0]
    tr = _row_tile(st)
    full = lambda shape: pl.BlockSpec(shape, lambda i: (0, 0))
    lower, upper = _tri_blocks(tr)
    return pl.pallas_call(
        _gla_prep_kernel,
        grid=(m // tr,),
        in_specs=[pl.BlockSpec((tr, LANE), lambda i: (i, (GLA_W - LANE) // LANE)), full((LANE, 2 * GLA_QK)),
                  full((1, 2 * GLA_QK)),
                  full((tr, tr)), full((tr, tr))],
        out_specs=pl.BlockSpec((tr, 2 * GLA_QK), lambda i: (i, 0)),
        out_shape=jax.ShapeDtypeStruct((m, 2 * GLA_QK), F32),
        compiler_params=_cp("parallel"),
        name="gla_prep",
    )(proj, g2p, gb, lower, upper)


def _gla_intra_exact(p_ref, c_ref, rows, z, seg_ref):
    r0 = rows.start
    q = p_ref[rows, 0:GLA_QK] * (GLA_DK ** -0.5)
    cum = c_ref[rows, :]
    row = lax.broadcasted_iota(jnp.int32, (CHUNK, 1), 0)
    seg = seg_ref[...]

    def body(j, acc):
        kj = p_ref[pl.ds(r0 + j, 1), GLA_QK:2 * GLA_QK]
        vj = p_ref[pl.ds(r0 + j, 1), 2 * GLA_QK:2 * GLA_QK + GLA_HEADS * GLA_DV]
        seen = (row >= j) if z == 0 else (row <= j)
        w = jnp.where(seen, q * kj * jnp.exp(jnp.minimum(cum - c_ref[pl.ds(r0 + j, 1), :], 0.0)), 0.0)
        score = _dot_data_mask(w, seg, 3)
        return tuple(a + score[:, h:h + 1] * vj[:, h * GLA_DV:(h + 1) * GLA_DV] for h, a in enumerate(acc))

    zero = jnp.zeros((CHUNK, GLA_DV), F32)
    return list(lax.fori_loop(0, CHUNK, body, (zero,) * GLA_HEADS))


def _gla_scan_kernel(pf_ref, pb_ref, cf_ref, cb_ref, seg_ref, s0_ref, yf_ref, yb_ref, s_ref, *, ncb):
    @pl.when(pl.program_id(1) == 0)
    def _():
        s_ref[...] = s0_ref[...]

    refs = ((pf_ref, cf_ref, yf_ref), (pb_ref, cb_ref, yb_ref))
    heads = [(z, h) for z in range(2) for h in range(GLA_HEADS)]
    hsl = [slice(h * GLA_DK, (h + 1) * GLA_DK) for h in range(GLA_HEADS)]
    state = {(z, h): s_ref[z, h] for z, h in heads}
    chunks = [(step, z) for step in range(ncb) for z in range(2)]
    rows = {(step, z): _chunk_rows(step, z, ncb) for step, z in chunks}
    cum = {k: refs[k[1]][1][rows[k], :] for k in chunks}
    mid = {k: cum[k][CHUNK // 2:CHUNK // 2 + 1] for k in chunks}
    span = functools.reduce(jnp.maximum, [jnp.max(jnp.abs(cum[k] - mid[k])) for k in chunks])

    def v_of(k, h):
        return _bf(refs[k[1]][0][rows[k], 2 * GLA_QK + h * GLA_DV:2 * GLA_QK + (h + 1) * GLA_DV])

    def intra_factored():
        att = {}
        for k in chunks:
            p_ref = refs[k[1]][0]
            q_mid = _bf(p_ref[rows[k], 0:GLA_QK] * (GLA_DK ** -0.5) * jnp.exp(cum[k] - mid[k]))
            k_mid = _bf(p_ref[rows[k], GLA_QK:2 * GLA_QK] * jnp.exp(mid[k] - cum[k]))
            incl = _dir_masks(k[1])[0]
            for h in range(GLA_HEADS):
                att[k, h] = _bf(jnp.where(incl, _dot_nt(q_mid[:, hsl[h]], k_mid[:, hsl[h]]), 0.0))
        return [_dot(att[k, h], v_of(k, h)) for k in chunks for h in range(GLA_HEADS)]

    def intra_exact():
        out = []
        for k in chunks:
            out += _gla_intra_exact(refs[k[1]][0], refs[k[1]][1], rows[k], k[1], seg_ref)
        return out

    intra = lax.cond(span <= GLA_SAFE_RANGE, intra_factored, intra_exact)
    intra = {(k, h): intra[i * GLA_HEADS + h] for i, k in enumerate(chunks) for h in range(GLA_HEADS)}

    for step in range(ncb):
        q_in, k_out, e_last = {}, {}, {}
        for z in range(2):
            k = (step, z)
            p_ref = refs[z][0]
            glast = cum[k][CHUNK - 1:CHUNK] if z == 0 else cum[k][0:1]
            q_in[z] = _bf(p_ref[rows[k], 0:GLA_QK] * (GLA_DK ** -0.5) * jnp.exp(cum[k]))
            k_out[z] = _bf(p_ref[rows[k], GLA_QK:2 * GLA_QK] * jnp.exp(glast - cum[k]))
            e_last[z] = jnp.exp(glast)
        inter = {(z, h): _dot_nt(q_in[z][:, hsl[h]], _bf(state[z, h])) for z, h in heads}
        inc = {(z, h): _dot_tn(v_of((step, z), h), k_out[z][:, hsl[h]]) for z, h in heads}
        for z in range(2):
            refs[z][2][rows[step, z], :] = _bf(_lanes([intra[(step, z), h] + inter[z, h] for h in range(GLA_HEADS)]))
            for h in range(GLA_HEADS):
                state[z, h] = state[z, h] * e_last[z][:, hsl[h]] + inc[z, h]
    for z, h in heads:
        s_ref[z, h] = state[z, h]


def _gla_scan(st, proj, cum, s0_t):
    ncb, rb, nblk, fwd, bwd = _scan_blocks(st)
    seg = _block_diag(GLA_QK, GLA_DK, 1.0)[:, ::GLA_DK]
    seg = jnp.pad(seg, ((0, 0), (0, LANE - GLA_HEADS)))
    state_spec = pl.BlockSpec((None, 2, GLA_HEADS, GLA_DV, GLA_DK), lambda b, j: (b, 0, 0, 0, 0))
    qkv_w = 2 * GLA_QK + GLA_HEADS * GLA_DV
    return pl.pallas_call(
        functools.partial(_gla_scan_kernel, ncb=ncb),
        grid=(st.n, nblk),
        in_specs=[pl.BlockSpec((rb, qkv_w), fwd), pl.BlockSpec((rb, qkv_w), bwd),
                  pl.BlockSpec((rb, GLA_QK), fwd), pl.BlockSpec((rb, GLA_QK), lambda b, j: (bwd(b, j)[0], 1)),
                  pl.BlockSpec((GLA_QK, LANE), lambda b, j: (0, 0)), state_spec],
        out_specs=[pl.BlockSpec((rb, 512), fwd), pl.BlockSpec((rb, 512), bwd), state_spec],
        out_shape=[jax.ShapeDtypeStruct((st.rows, 512), BF16), jax.ShapeDtypeStruct((st.rows, 512), BF16),
                   jax.ShapeDtypeStruct((st.n, 2, GLA_HEADS, GLA_DV, GLA_DK), F32)],
        compiler_params=_cp("parallel", "arbitrary"),
        name="gla_scan",
    )(proj, proj, cum, cum, seg, s0_t)


def _gla_post_kernel(yf_ref, yb_ref, p_ref, g_ref, o_ref):
    o = yf_ref[...].astype(F32) + yb_ref[...].astype(F32)
    gate = _silu(p_ref[...])
    outs = []
    for h in range(GLA_HEADS):
        oh = o[:, h * GLA_DV:(h + 1) * GLA_DV]
        outs.append(oh * lax.rsqrt(jnp.mean(oh * oh, axis=-1, keepdims=True) + 1e-6) * g_ref[...])
    o_ref[...] = _bf(_lanes(outs) * gate)


def _gla_post(st, y_f, y_b, proj, gla_norm):
    m = proj.shape[0]
    tr = _row_tile(st)
    return pl.pallas_call(
        _gla_post_kernel,
        grid=(m // tr,),
        in_specs=[pl.BlockSpec((tr, 512), lambda i: (i, 0)), pl.BlockSpec((tr, 512), lambda i: (i, 0)),
                  pl.BlockSpec((tr, 512), lambda i: (i, 2)), pl.BlockSpec((1, GLA_DV), lambda i: (0, 0))],
        out_specs=pl.BlockSpec((tr, 512), lambda i: (i, 0)),
        out_shape=jax.ShapeDtypeStruct((m, 512), BF16),
        compiler_params=_cp("parallel"),
        name="gla_post",
    )(y_f, y_b, proj, gla_norm.reshape(1, GLA_DV))


def _rwkv_prep_kernel(cur_ref, prev_ref, next_ref, mu_ref, w2_ref, w0_ref, a2_ref, a0_ref, g2_ref,
                      kkw_ref, ka_ref, rk_ref, bd_ref, in_out, lw_out, post_out, *, tiles_per_seq):
    cur = cur_ref[...]
    x_prev, x_next = _neighbours(cur, prev_ref[...], next_ref[...], tiles_per_seq)
    blk = cur + (0.5 * (x_prev + x_next) - cur) * mu_ref[...]
    r, k, v = blk[:, 0:512], blk[:, 512:1024], blk[:, 1024:1536]
    w_logit = w0_ref[...] + _dot_split(jnp.tanh(blk[:, 1536:1600]), w2_ref[...])
    lw_out[...] = -RWKV_DECAY_SCALE * jax.nn.sigmoid(w_logit)
    a = jax.nn.sigmoid(a0_ref[...] + _dot_split(blk[:, 1600:1664], a2_ref[...]))
    g = _dot_split(jax.nn.sigmoid(blk[:, 1664:1792]), g2_ref[...])
    bd = bd_ref[...]
    kk = k * kkw_ref[...]
    kk = kk * lax.rsqrt(_dot_data_mask(kk * kk, bd, 2) + 1e-12)
    k2 = k * (1.0 + (a - 1.0) * ka_ref[...])
    in_out[:, 0:512] = _bf(r)
    in_out[:, 512:1024] = _bf(k2)
    in_out[:, 1024:1536] = _bf(v)
    in_out[:, 1536:2048] = _bf(-kk)
    in_out[:, 2048:2560] = _bf(kk * a)
    post_out[:, 0:512] = _bf(g)
    post_out[:, 512:1024] = _bf(_dot_data_mask(r * k2 * rk_ref[...], bd, 2) * v)


def _rwkv_prep(st, proj, mu, w2, w0, a2, a0, g2, kkw, ka, rk):
    m = proj.shape[0]
    tr = _row_tile(st)
    full = lambda shape: pl.BlockSpec(shape, lambda i: (0, 0))
    row = lambda w: pl.BlockSpec((tr, w), lambda i: (i, 0))
    vec = lambda a: a.reshape(1, -1)
    return pl.pallas_call(
        functools.partial(_rwkv_prep_kernel, tiles_per_seq=st.t // tr),
        grid=(m // tr,),
        in_specs=_halo_specs(st, RWKV_W) + [full((1, RWKV_W)), full((64, 1024)), full((1, 1024)),
                                            full((64, 512)), full((1, 512)), full((128, 512)),
                                            full((1, 512)), full((1, 512)), full((1, 512)), full((512, 512))],
        out_specs=[row(2560), row(1024), row(1024)],
        out_shape=[jax.ShapeDtypeStruct((m, 2560), BF16), jax.ShapeDtypeStruct((m, 1024), F32),
                   jax.ShapeDtypeStruct((m, 1024), BF16)],
        compiler_params=_cp("parallel"),
        name="rwkv_prep",
    )(proj, proj, proj, vec(mu), jnp.concatenate([w2[0], w2[1]], axis=1), vec(w0), a2, vec(a0), g2,
      vec(kkw), vec(ka), vec(rk), _block_diag(512, RWKV_HD, 1.0))


def _rwkv_chunk(x_ref, lw_ref, rows, z):
    incl, strict = _dir_masks(z)
    lw = lw_ref[rows, :]
    cum = _dot_mask_data(incl.astype(BF16), lw, 3)
    cum_x = cum - lw
    mid = cum[CHUNK // 2:CHUNK // 2 + 1]
    glast = cum[CHUNK - 1:CHUNK] if z == 0 else cum[0:1]
    x = x_ref[rows, :]
    r, k, v, al, be = x[:, 0:512], x[:, 512:1024], x[:, 1024:1536], x[:, 1536:2048], x[:, 2048:2560]
    e_mid, e_nmid = jnp.exp(cum - mid), jnp.exp(mid - cum)
    e_out = jnp.exp(glast - cum)
    return dict(incl=incl, strict=strict, v=v,
                r_mid=_bf(r * e_mid), a_mid=_bf(al * jnp.exp(cum_x - mid)),
                b_mid=_bf(be * e_nmid), k_mid=_bf(k * e_nmid),
                r_in=_bf(r * jnp.exp(cum)), a_in=al * jnp.exp(cum_x),
                b_out=_bf(be * e_out), k_out=_bf(k * e_out), e_last=jnp.exp(glast))


_HEAD_SLICES = [slice(h * RWKV_HD, (h + 1) * RWKV_HD) for h in range(RWKV_HEADS)]


def _rwkv_state_free(chunks, hooks):
    hooks = list(hooks)

    def run_hook():
        if hooks:
            hooks.pop(0)()

    chains = [(c, sl) for c in chunks for sl in _HEAD_SLICES]
    run_hook()
    pair = [_dot_nt(_rows([c['a_mid'][:, sl], c['r_mid'][:, sl]]), _rows([c['b_mid'][:, sl], c['k_mid'][:, sl]]))
            for c, sl in chains]
    half = CHUNK
    a_ab = [_bf(jnp.where(c['strict'], p[:half, :half], 0.0)) for (c, _), p in zip(chains, pair)]
    a_ak = [_bf(jnp.where(c['strict'], p[:half, half:], 0.0)) for (c, _), p in zip(chains, pair)]
    a_rb = [_bf(jnp.where(c['incl'], p[half:, :half], 0.0)) for (c, _), p in zip(chains, pair)]
    a_rk = [_bf(jnp.where(c['incl'], p[half:, half:], 0.0)) for (c, _), p in zip(chains, pair)]
    vb = [_bf(c['v'][:, sl]) for c, sl in chains]
    both = [_dot(_rows([ak, rk]), v) for ak, rk, v in zip(a_ak, a_rk, vb)]
    av = [x[:half] for x in both]
    o0 = [x[half:] for x in both]
    sol = [_lanes([c['a_in'][:, sl], x]) for (c, sl), x in zip(chains, av)]
    powr = a_ab
    width = 2 * RWKV_HD
    for it in range(6):
        if it < 5:
            both = [_dot(p, _lanes([_bf(s), p])) for p, s in zip(powr, sol)]
            sol = [s + x[:, :width] for s, x in zip(sol, both)]
            powr = [_bf(x[:, width:]) for x in both]
        else:
            sol = [s + _dot(p, _bf(s)) for p, s in zip(powr, sol)]
        if it in (1, 3):
            run_hook()
    while hooks:
        run_hook()
    return [dict(w=_bf(s[:, :RWKV_HD]), u0=s[:, RWKV_HD:], o0=o, a_rb=a, v=c['v'][:, sl], r_in=c['r_in'][:, sl],
                 b_out=c['b_out'][:, sl], k_out=c['k_out'][:, sl], e_last=c['e_last'][:, sl])
            for (c, sl), s, o, a in zip(chains, sol, o0, a_rb)]


def _rwkv_state_stages(res, state, write_out):
    box = {}

    def read_state():
        box['su'] = [_dot_nt(_rows([c['w'], c['r_in']]), _bf(s)) for c, s in zip(res, state)]

    def update_state():
        box['u'] = [c['u0'] + su[:CHUNK] for c, su in zip(res, box['su'])]
        state[:] = [s * c['e_last'] + _dot_tn(_rows([_bf(u), _bf(c['v'])]), _rows([c['b_out'], c['k_out']]))
                    for c, s, u in zip(res, state, box['u'])]

    def emit():
        write_out([su[CHUNK:] + _dot(c['a_rb'], _bf(u)) + c['o0'] for c, su, u in zip(res, box['su'], box['u'])])

    return [read_state, update_state, emit]


def _rwkv_scan_kernel(xf_ref, xb_ref, lwf_ref, lwb_ref, s0_ref, yf_ref, yb_ref, s_ref, *, ncb):
    @pl.when(pl.program_id(1) == 0)
    def _():
        s_ref[...] = s0_ref[...]

    state = [s_ref[z, h] for z in range(2) for h in range(RWKV_HEADS)]

    def chunk_rows(step, z):
        cc = step if z == 0 else ncb - 1 - step
        return slice(cc * CHUNK, (cc + 1) * CHUNK)

    def pair_inputs(step):
        return [_rwkv_chunk(xf_ref, lwf_ref, chunk_rows(step, 0), 0),
                _rwkv_chunk(xb_ref, lwb_ref, chunk_rows(step, 1), 1)]

    def writer(step):
        def write_out(outs):
            yf_ref[chunk_rows(step, 0), :] = _bf(_lanes(outs[:RWKV_HEADS]))
            yb_ref[chunk_rows(step, 1), :] = _bf(_lanes(outs[RWKV_HEADS:]))
        return write_out

    res = _rwkv_state_free(pair_inputs(0), [])
    for step in range(ncb):
        hooks = _rwkv_state_stages(res, state, writer(step))
        if step + 1 < ncb:
            res = _rwkv_state_free(pair_inputs(step + 1), hooks)
        else:
            for hook in hooks:
                hook()
    for z in range(2):
        for h in range(RWKV_HEADS):
            s_ref[z, h] = state[z * RWKV_HEADS + h]


def _rwkv_scan(st, xin, lw, s0):
    ncb, rb, nblk, fwd, bwd = _scan_blocks(st)
    state_spec = pl.BlockSpec((None, 2, RWKV_HEADS, RWKV_HD, RWKV_HD), lambda b, j: (b, 0, 0, 0, 0))
    return pl.pallas_call(
        functools.partial(_rwkv_scan_kernel, ncb=ncb),
        grid=(st.n, nblk),
        in_specs=[pl.BlockSpec((rb, 2560), fwd), pl.BlockSpec((rb, 2560), bwd),
                  pl.BlockSpec((rb, 512), fwd), pl.BlockSpec((rb, 512), lambda b, j: (bwd(b, j)[0], 1)),
                  state_spec],
        out_specs=[pl.BlockSpec((rb, 512), fwd), pl.BlockSpec((rb, 512), bwd), state_spec],
        out_shape=[jax.ShapeDtypeStruct((st.rows, 512), BF16), jax.ShapeDtypeStruct((st.rows, 512), BF16),
                   jax.ShapeDtypeStruct((st.n, 2, RWKV_HEADS, RWKV_HD, RWKV_HD), F32)],
        compiler_params=_cp("parallel", "arbitrary"),
        name="rwkv_scan",
    )(xin, xin, lw, lw, s0)


def _rwkv_post_kernel(yf_ref, yb_ref, post_ref, lng_ref, lnb_ref, bd_ref, o_ref):
    o = yf_ref[...].astype(F32) + yb_ref[...].astype(F32)
    bd = bd_ref[...]
    d = o - _dot_data_mask(o, bd, 2)
    o = d * lax.rsqrt(_dot_data_mask(d * d, bd, 2) + RWKV_LN_EPS) * lng_ref[...] + lnb_ref[...]
    o_ref[...] = _bf((o + post_ref[:, 512:1024].astype(F32)) * post_ref[:, 0:512].astype(F32))


def _rwkv_post(st, y_f, y_b, post, ln_g, ln_b):
    m = post.shape[0]
    tr = _row_tile(st)
    full = lambda shape: pl.BlockSpec(shape, lambda i: (0, 0))
    row = lambda w: pl.BlockSpec((tr, w), lambda i: (i, 0))
    return pl.pallas_call(
        _rwkv_post_kernel,
        grid=(m // tr,),
        in_specs=[row(512), row(512), row(1024), full((1, 512)), full((1, 512)), full((512, 512))],
        out_specs=row(512),
        out_shape=jax.ShapeDtypeStruct((m, 512), BF16),
        compiler_params=_cp("parallel"),
        name="rwkv_post",
    )(y_f, y_b, post, ln_g.reshape(1, 512), ln_b.reshape(1, 512), _block_diag(512, RWKV_HD, 1.0 / RWKV_HD))


def _merge_kernel(h_ref, o0_ref, o1_ref, o2_ref, o3_ref, wg_ref, wb_ref, out_ref):
    h = h_ref[...]
    acc = None
    for i, o_ref in enumerate((o0_ref, o1_ref, o2_ref, o3_ref)):
        term = jax.nn.sigmoid(_dot(h, wg_ref[i])) * _dot(o_ref[...], wb_ref[i])
        acc = term if acc is None else acc + term
    out_ref[...] = _bf(acc)


def _merge(st, h, branch_outs, wg, wb):
    m, d = h.shape
    tm = _mm_tile(st)
    tn = 512
    row = lambda w: pl.BlockSpec((tm, w), lambda j, i: (i, 0))
    return pl.pallas_call(
        _merge_kernel,
        grid=(d // tn, m // tm),
        in_specs=[row(d)] + [row(BRANCH_W)] * 4 + [pl.BlockSpec((4, d, tn), lambda j, i: (0, 0, j)),
                                                    pl.BlockSpec((4, BRANCH_W, tn), lambda j, i: (0, 0, j))],
        out_specs=pl.BlockSpec((tm, tn), lambda j, i: (i, j)),
        out_shape=jax.ShapeDtypeStruct((m, d), BF16),
        compiler_params=_cp("parallel", "parallel"),
        name="merge",
    )(h, *branch_outs, wg, wb)


def _wo_kernel(m_ref, w_ref, x_ref, mod_ref, g_ref, o_ref, h_out):
    x = x_ref[...] + mod_ref[0, 2:3, :] * _dot(m_ref[...], w_ref[...])
    o_ref[...] = x
    h_out[...] = _norm_mod(x, g_ref[...], mod_ref, 4, 3)


def _out_proj(st, merged, w_o, x, mod_l, gain2):
    m, d = x.shape
    tm = _mm_tile(st)
    row = pl.BlockSpec((tm, d), lambda i: (i, 0))
    return pl.pallas_call(
        _wo_kernel,
        grid=(m // tm,),
        in_specs=[row, pl.BlockSpec((d, d), lambda i: (0, 0)), row, pl.BlockSpec((1, 6, d), st.group_map(tm)),
                  pl.BlockSpec((1, d), lambda i: (0, 0))],
        out_specs=[row, row],
        out_shape=[jax.ShapeDtypeStruct((m, d), F32), jax.ShapeDtypeStruct((m, d), BF16)],
        compiler_params=_cp("parallel"),
        name="out_proj",
    )(merged, w_o, x, mod_l, gain2.reshape(1, d))


def _ffn_kernel(h_ref, w1_ref, w3_ref, w2_ref, x_ref, mod_ref, fg_ref, o_ref, acc_ref, *, final):
    f = pl.program_id(1)

    @pl.when(f == 0)
    def _():
        acc_ref[...] = jnp.zeros_like(acc_ref)

    h = h_ref[...]
    u = _silu(_dot(h, w1_ref[...])) * _dot(h, w3_ref[...])
    acc_ref[...] += _dot(_bf(u), w2_ref[...])

    @pl.when(f == pl.num_programs(1) - 1)
    def _():
        x = x_ref[...] + mod_ref[0, 5:6, :] * acc_ref[...]
        o_ref[...] = _rms(x, fg_ref[...]) if final else x


def _ffn(st, h, w1, w3, w2, x, mod_l, final_gain, final):
    m, d = x.shape
    dff = w1.shape[1]
    tm = _mm_tile(st)
    tf = 512
    return pl.pallas_call(
        functools.partial(_ffn_kernel, final=final),
        grid=(m // tm, dff // tf),
        in_specs=[pl.BlockSpec((tm, d), lambda i, f: (i, 0)),
                  pl.BlockSpec((d, tf), lambda i, f: (0, f)),
                  pl.BlockSpec((d, tf), lambda i, f: (0, f)),
                  pl.BlockSpec((tf, d), lambda i, f: (f, 0)),
                  pl.BlockSpec((tm, d), lambda i, f: (i, 0)),
                  pl.BlockSpec((1, 6, d), lambda i, f: st.group_map(tm)(i)),
                  pl.BlockSpec((1, d), lambda i, f: (0, 0))],
        out_specs=pl.BlockSpec((tm, d), lambda i, f: (i, 0)),
        out_shape=jax.ShapeDtypeStruct((m, d), F32),
        scratch_shapes=[pltpu.VMEM((tm, d), F32)],
        compiler_params=_cp("parallel", "arbitrary"),
        name="ffn",
    )(h, w1, w3, w2, x, mod_l, final_gain.reshape(1, d))


def _split_w_in(w_in):
    z = lambda w: jnp.zeros(w_in.shape[:2] + (w,), w_in.dtype)
    att = w_in[..., 0:768]
    ssd = jnp.concatenate([w_in[..., 768:2064], z(SSD_W - 1296)], axis=-1)
    rwkv = w_in[..., 2064:3856]
    gla = jnp.concatenate([w_in[..., 3856:4880], w_in[..., 4896:5408], w_in[..., 4880:4896], z(LANE - 16)], axis=-1)
    return tuple(_bf(w) for w in (att, ssd, rwkv, gla))


def _block(st, x, mod_l, p, rope_tables, ctx, final_gain, final):
    h, proj_att, proj_ssd = _in_proj_a(st, x, p['norm1'], mod_l, p['w_att'], p['w_ssd'])
    proj_rwkv, proj_gla = _in_proj_b(st, h, p['w_rwkv'], p['w_gla'])

    v = proj_att[:, 640:768]
    if ctx is None:
        q, k_att, v_att, k = _att_prep(st, proj_att, p['q_norm'], p['k_norm'], None)
        o_att = _attention(st, q, k_att, v_att)
        s_ssd = jnp.zeros((st.n, 2, SSD_HEADS, SSD_N, SSD_P), F32)
        s_rwkv = jnp.zeros((st.n, 2, RWKV_HEADS, RWKV_HD, RWKV_HD), F32)
        s_gla_t = jnp.zeros((st.n, 2, GLA_HEADS, GLA_DV, GLA_DK), F32)
    else:
        ctx_k, ctx_v, s_ssd, s_rwkv, s_gla = ctx
        q, k_att, v_att = _att_prep(st, proj_att, p['q_norm'], p['k_norm'], rope_tables)
        k = None
        o_att = _attention(st, q, _join_cache(st, k_att, ctx_k, 2), _join_cache(st, v_att, ctx_v, 1))
        s_gla_t = jnp.swapaxes(s_gla, -1, -2)

    xbc, dt, cum, dtt, cumt = _ssd_prep(st, proj_ssd, p['ssd_conv_w'], p['ssd_conv_b'], p['ssd_dt_bias'],
                                        p['ssd_a_log'])
    y_ssd_f, y_ssd_b, new_ssd = _ssd_scan(st, xbc, dt, cum, dtt, cumt, s_ssd)
    o_ssd = _ssd_post(st, y_ssd_f, y_ssd_b, xbc, proj_ssd, p['ssd_d'], p['ssd_norm'])

    rin, lw, rpost = _rwkv_prep(st, proj_rwkv, p['rwkv_mu'], p['rwkv_w2'], p['rwkv_w0'], p['rwkv_a2'],
                                p['rwkv_a0'], p['rwkv_g2'], p['rwkv_kk'], p['rwkv_ka'], p['rwkv_rk'])
    y_rwkv_f, y_rwkv_b, new_rwkv = _rwkv_scan(st, rin, lw, s_rwkv)
    o_rwkv = _rwkv_post(st, y_rwkv_f, y_rwkv_b, rpost, p['rwkv_ln_g'], p['rwkv_ln_b'])

    g2 = jnp.concatenate([p['gla_g2'][0], p['gla_g2'][1]], axis=1)
    g2p = jnp.pad(g2, ((0, LANE - g2.shape[0]), (0, 0)))
    gla_cum = _gla_prep(st, proj_gla, g2p, p['gla_gb'].reshape(1, 2 * GLA_QK))
    y_gla_f, y_gla_b, new_gla_t = _gla_scan(st, proj_gla, gla_cum, s_gla_t)
    o_gla = _gla_post(st, y_gla_f, y_gla_b, proj_gla, p['gla_norm'])

    merged = _merge(st, h, (o_att, o_ssd, o_rwkv, o_gla), p['w_gate'], p['w_branch'])
    x, h2 = _out_proj(st, merged, p['w_o'], x, mod_l, p['norm2'])
    x = _ffn(st, h2, p['ffn_w1'], p['ffn_w3'], p['ffn_w2'], x, mod_l, final_gain, final)
    return x, (k, v, new_ssd, new_rwkv, jnp.swapaxes(new_gla_t, -1, -2))


def kernel(x_prompt, x_sample, cache_attn_k, cache_attn_v, state_ssd, state_rwkv, state_gla, c, c_ctx, w_mod, b_mod, norm1, norm2, w_in, q_norm, k_norm, ssd_conv_w, ssd_conv_b, ssd_dt_bias, ssd_a_log, ssd_d, ssd_norm, rwkv_mu, rwkv_w0, rwkv_w2, rwkv_a0, rwkv_a2, rwkv_g2, rwkv_kk, rwkv_ka, rwkv_rk, rwkv_ln_g, rwkv_ln_b, gla_g2, gla_gb, gla_norm, w_gate, w_branch, w_o, ffn_w1, ffn_w3, ffn_w2, final_norm):
    nb, seq, d = x_prompt.shape
    db, dseq, _ = x_sample.shape
    depth = w_in.shape[0]
    assert d == D_MODEL and seq % CHUNK == 0 and dseq % CHUNK == 0 and 1 + db <= MOD_ROWS
    ctx_st = _Stream(nb, seq, 0, False)
    lat_st = _Stream(db, dseq, 1, True)

    cond = jnp.concatenate([c_ctx[None], c, jnp.zeros((MOD_ROWS - 1 - db, d), F32)], axis=0)
    mod = _modulation(cond, w_mod, b_mod)

    w_att, w_ssd, w_rwkv, w_gla = _split_w_in(w_in)
    w_gate_b, w_branch_b, w_o_b = _bf(w_gate), _bf(w_branch), _bf(w_o)
    w1_b, w3_b, w2_b = _bf(ffn_w1), _bf(ffn_w3), _bf(ffn_w2)

    def params_at(l):
        return dict(norm1=norm1[l], norm2=norm2[l], w_att=w_att[l], w_ssd=w_ssd[l], w_rwkv=w_rwkv[l],
                    w_gla=w_gla[l], q_norm=q_norm[l], k_norm=k_norm[l],
                    ssd_conv_w=ssd_conv_w[l], ssd_conv_b=ssd_conv_b[l], ssd_dt_bias=ssd_dt_bias[l],
                    ssd_a_log=ssd_a_log[l], ssd_d=ssd_d[l], ssd_norm=ssd_norm[l],
                    rwkv_mu=rwkv_mu[l], rwkv_w0=rwkv_w0[l], rwkv_w2=rwkv_w2[l], rwkv_a0=rwkv_a0[l],
                    rwkv_a2=rwkv_a2[l], rwkv_g2=rwkv_g2[l], rwkv_kk=rwkv_kk[l], rwkv_ka=rwkv_ka[l],
                    rwkv_rk=rwkv_rk[l], rwkv_ln_g=rwkv_ln_g[l], rwkv_ln_b=rwkv_ln_b[l],
                    gla_g2=gla_g2[l], gla_gb=gla_gb[l], gla_norm=gla_norm[l],
                    w_gate=w_gate_b[l], w_branch=w_branch_b[l], w_o=w_o_b[l],
                    ffn_w1=w1_b[l], ffn_w3=w3_b[l], ffn_w2=w2_b[l])

    xp = x_prompt.reshape(nb * seq, d)
    new_k, new_v, new_ssd, new_rwkv, new_gla = [], [], [], [], []
    for l in range(depth):
        xp, (k_l, v_l, ssd_l, rwkv_l, gla_l) = _block(ctx_st, xp, mod[l], params_at(l), None, None, final_norm,
                                                      l == depth - 1)
        new_k.append(k_l.reshape(nb, seq, ATT_KV, HEAD_DIM))
        new_v.append(v_l.reshape(nb, seq, ATT_KV, HEAD_DIM))
        new_ssd.append(ssd_l)
        new_rwkv.append(rwkv_l)
        new_gla.append(gla_l)

    rope_tables = _rope_tables(dseq)
    xs = x_sample.reshape(db * dseq, d)
    for l in range(depth):
        ctx = (cache_attn_k[:, l], cache_attn_v[:, l], state_ssd[:, l], state_rwkv[:, l], state_gla[:, l])
        xs, _ = _block(lat_st, xs, mod[l], params_at(l), rope_tables, ctx, final_norm, l == depth - 1)

    y_prompt = xp.reshape(nb, seq, d)
    y_sample = xs.reshape(db, dseq, d)
    return (y_prompt, y_sample, jnp.stack(new_k, axis=1), jnp.stack(new_v, axis=1),
            jnp.stack(new_ssd, axis=1), jnp.stack(new_rwkv, axis=1), jnp.stack(new_gla, axis=1))
```

```python
import functools

import jax
import jax.numpy as jnp
import numpy as np
from jax import lax
from jax.experimental import pallas as pl
from jax.experimental.pallas import tpu as pltpu

F32 = jnp.float32
BF16 = jnp.bfloat16

D_MODEL = 2048
GRID_W = 64
ATT_HEADS = 8
ATT_KV = 2
ATT_GROUP = ATT_HEADS // ATT_KV
HEAD_DIM = 64
ROPE_THETA = 10000.0
SSD_HEADS = 8
SSD_P = 64
SSD_N = 64
SSD_GROUPS = 2
RWKV_HEADS = 8
RWKV_HD = 64
RWKV_DECAY_SCALE = 0.6065306597126334
RWKV_LN_EPS = 64e-5
GLA_HEADS = 4
GLA_DK = 64
GLA_DV = 128
GLA_GATE_NORM = 16.0
CHUNK = 64
BRANCH_W = 512

ATT_W = 768
SSD_W = 1408
RWKV_W = 1792
GLA_W = 1664
LANE = 128
SUBLANE = 8
MOD_ROWS = 16
VMEM_LIMIT = 56 * 1024 * 1024
ROW_TILE = 256
MM_TILE = 512
SCAN_CHUNKS = 4
KEY_TILE = 256
ATT_Q_TILE = 256
KEY_UNROLL = 6
LOG2E = 1.4426950408889634
ATT_MIN_ROW_SUM = 2.0 ** -90


def _cp(*sem):
    return pltpu.CompilerParams(dimension_semantics=sem, vmem_limit_bytes=VMEM_LIMIT)


def _bf(x):
    return x.astype(BF16)


def _dot(a, b):
    return jnp.dot(a, b, preferred_element_type=F32)


def _dot_nt(a, b):
    return lax.dot_general(a, b, (((1,), (1,)), ((), ())), preferred_element_type=F32)


def _dot_tn(a, b):
    return lax.dot_general(a, b, (((0,), (0,)), ((), ())), preferred_element_type=F32)


def _pieces(a, n):
    out = []
    for _ in range(n):
        piece = _bf(a)
        out.append(piece)
        a = a - piece.astype(F32)
    return out


def _dot_data_mask(a, mask, n):
    return functools.reduce(jnp.add, [_dot(piece, mask) for piece in _pieces(a, n)])


def _dot_mask_data(mask, a, n):
    return functools.reduce(jnp.add, [_dot(mask, piece) for piece in _pieces(a, n)])


def _dot_nt_mask_data(mask, a, n):
    return functools.reduce(jnp.add, [_dot_nt(mask, piece) for piece in _pieces(a, n)])


def _dot_split(a, b):
    a_hi, a_lo = _pieces(a, 2)
    b_hi, b_lo = _pieces(b, 2)
    return _dot(a_hi, b_hi) + (_dot(a_hi, b_lo) + _dot(a_lo, b_hi))


def _silu(x):
    return x * jax.nn.sigmoid(x)


def _softplus(x):
    return jnp.maximum(x, 0.0) + jnp.log1p(jnp.exp(-jnp.abs(x)))


def _lanes(pieces):
    return jnp.concatenate(pieces, axis=1)


def _rows(pieces):
    return jnp.concatenate(pieces, axis=0)


def _mod_kernel(c_ref, w_ref, b_ref, o_ref):
    c = c_ref[...]
    o_ref[0] = _dot(_bf(_silu(c)), _bf(w_ref[0])) + b_ref[0]


def _modulation(cond, w_mod, b_mod):
    nl, d, n6 = w_mod.shape
    tn = 1024
    out = pl.pallas_call(
        _mod_kernel,
        grid=(nl, n6 // tn),
        in_specs=[pl.BlockSpec((MOD_ROWS, d), lambda l, j: (0, 0)),
                  pl.BlockSpec((1, d, tn), lambda l, j: (l, 0, j)),
                  pl.BlockSpec((1, 1, tn), lambda l, j: (l, 0, j))],
        out_specs=pl.BlockSpec((1, MOD_ROWS, tn), lambda l, j: (l, 0, j)),
        out_shape=jax.ShapeDtypeStruct((nl, MOD_ROWS, n6), F32),
        compiler_params=_cp("parallel", "parallel"),
        name="modulation",
    )(cond, w_mod, b_mod.reshape(nl, 1, n6))
    return out.reshape(nl, MOD_ROWS, 6, d)


class _Stream:
    def __init__(self, n, t, group0, per_seq):
        self.n, self.t, self.group0, self.per_seq = n, t, group0, per_seq
        self.rows = n * t

    def group_map(self, tile):
        g0, per_seq, t = self.group0, self.per_seq, self.t
        if per_seq:
            return lambda i: (g0 + (i * tile) // t, 0, 0)
        return lambda i: (g0, 0, 0)


def _mm_tile(st):
    limit = st.t if st.per_seq else st.rows
    return min(MM_TILE, limit)


def _row_tile(st):
    return min(ROW_TILE, st.t)


def _rms(x, gain):
    return x * lax.rsqrt(jnp.mean(x * x, axis=-1, keepdims=True) + 1e-6) * gain


def _norm_mod(x, gain, mod_ref, sc_idx, sh_idx):
    return _bf(_rms(x, gain) * (1.0 + mod_ref[0, sc_idx:sc_idx + 1, :]) + mod_ref[0, sh_idx:sh_idx + 1, :])


def _in_proj_a_kernel(*refs, rope):
    x_ref, g_ref, mod_ref, wa_ref, ws_ref, qn_ref, kn_ref, bdq_ref, bdk_ref = refs[:9]
    tables = [r[...] for r in refs[9:12]] if rope else None
    outs = refs[12:] if rope else refs[9:]
    h = _norm_mod(x_ref[...], g_ref[...], mod_ref, 1, 0)
    outs[0][...] = h
    outs[1][...] = _dot(h, ws_ref[...])
    q, k_att, v_att, k, v = _att_operands(_dot(h, wa_ref[...]), qn_ref[...], kn_ref[...], bdq_ref[...],
                                          bdk_ref[...], tables)
    outs[2][...] = q
    for kv in range(ATT_KV):
        outs[3][kv] = k_att[kv]
        outs[4][kv] = v_att[kv]
    if not rope:
        outs[5][...] = k
        outs[6][...] = v


def _in_proj_a(st, x, gain, mod_l, w_att, w_ssd, q_norm, k_norm, rope_tables):
    m, d = x.shape
    tm = _mm_tile(st)
    rope = rope_tables is not None
    row = lambda w: pl.BlockSpec((tm, w), lambda i: (i, 0))
    full = lambda a: pl.BlockSpec(a.shape, lambda i: (0, 0))
    args = [x, gain.reshape(1, d), mod_l, w_att, w_ssd, jnp.tile(q_norm, ATT_HEADS).reshape(1, 512),
            jnp.tile(k_norm, ATT_KV).reshape(1, 128), _block_diag(512, HEAD_DIM, 1.0 / HEAD_DIM),
            _block_diag(128, HEAD_DIM, 1.0 / HEAD_DIM)]
    in_specs = [row(d), full(args[1]), pl.BlockSpec((1, 6, d), st.group_map(tm))] + [full(a) for a in args[3:]]
    if rope:
        tps = st.t // tm
        in_specs += [pl.BlockSpec((tm, LANE), lambda i: (i % tps, 0))] * 3
        args += list(rope_tables)
    kv_spec = pl.BlockSpec((ATT_KV, tm, LANE), lambda i: (0, i, 0))
    kv_shape = jax.ShapeDtypeStruct((ATT_KV, m, LANE), BF16)
    out_specs = [row(d), row(SSD_W), row(ATT_HEADS * LANE), kv_spec, kv_spec]
    out_shape = [jax.ShapeDtypeStruct((m, d), BF16), jax.ShapeDtypeStruct((m, SSD_W), F32),
                 jax.ShapeDtypeStruct((m, ATT_HEADS * LANE), BF16), kv_shape, kv_shape]
    if not rope:
        out_specs += [row(128), row(128)]
        out_shape += [jax.ShapeDtypeStruct((m, 128), F32)] * 2
    return pl.pallas_call(
        functools.partial(_in_proj_a_kernel, rope=rope),
        grid=(m // tm,),
        in_specs=in_specs,
        out_specs=out_specs,
        out_shape=out_shape,
        compiler_params=_cp("parallel"),
        name="in_proj_a",
    )(*args)


def _in_proj_b_kernel(h_ref, wr_ref, wg_ref, pr_out, pg_out):
    h = h_ref[...]
    pr_out[...] = _dot(h, wr_ref[...])
    pg_out[...] = _dot(h, wg_ref[...])


def _in_proj_b(st, h, w_rwkv, w_gla):
    m, d = h.shape
    tm = _mm_tile(st)
    row = lambda w: pl.BlockSpec((tm, w), lambda i: (i, 0))
    full = lambda a: pl.BlockSpec(a.shape, lambda i: (0, 0))
    return pl.pallas_call(
        _in_proj_b_kernel,
        grid=(m // tm,),
        in_specs=[row(d), full(w_rwkv), full(w_gla)],
        out_specs=[row(RWKV_W), row(GLA_W)],
        out_shape=[jax.ShapeDtypeStruct((m, RWKV_W), F32), jax.ShapeDtypeStruct((m, GLA_W), F32)],
        compiler_params=_cp("parallel"),
        name="in_proj_b",
    )(h, w_rwkv, w_gla)


def _att_operands(p, q_gain, k_gain, bdq, bdk, tables):
    aq = p[:, :512]
    ak = p[:, 512:640]
    av = p[:, 640:768]
    q = aq * lax.rsqrt(_dot_data_mask(aq * aq, bdq, 2) + 1e-6) * q_gain
    k = ak * lax.rsqrt(_dot_data_mask(ak * ak, bdk, 2) + 1e-6) * k_gain
    k_plain = k
    if tables is not None:
        c, sa, sb = tables
        k = k * c + pltpu.roll(k, LANE - 16, 1) * sa + pltpu.roll(k, 16, 1) * sb
        c4, sa4, sb4 = _lanes([c] * 4), _lanes([sa] * 4), _lanes([sb] * 4)
        q = q * c4 + pltpu.roll(q, 512 - 16, 1) * sa4 + pltpu.roll(q, 16, 1) * sb4
    tr = p.shape[0]
    spare = lax.broadcasted_iota(jnp.int32, (tr, LANE - HEAD_DIM), 1)
    one_col = (spare == 0).astype(F32)
    two_cols = (spare < 2).astype(F32)
    q = q * (HEAD_DIM ** -0.5 * LOG2E)
    q_len = jnp.sqrt(_dot_data_mask(q * q, bdq, 2) * HEAD_DIM)
    q_att = _bf(_lanes([piece for h in range(ATT_HEADS)
                        for piece in (q[:, h * HEAD_DIM:(h + 1) * HEAD_DIM],
                                      one_col * q_len[:, h * HEAD_DIM:(h + 1) * HEAD_DIM])]))
    heads = [slice(kv * HEAD_DIM, (kv + 1) * HEAD_DIM) for kv in range(ATT_KV)]
    k_att = [_bf(_lanes([k[:, sl], two_cols])) for sl in heads]
    v_att = [_bf(_lanes([av[:, sl], one_col])) for sl in heads]
    return q_att, k_att, v_att, k_plain, av


def _block_diag(width, block, value):
    idx = np.arange(width) // block
    return _bf(jnp.asarray((idx[:, None] == idx[None, :]).astype(np.float32) * value))


def _join_cache(st, own, cache, ones):
    n, p, kv, hd = cache.shape
    c = jnp.transpose(cache, (2, 0, 1, 3))
    pad = jnp.zeros((kv, n, p, LANE - hd), c.dtype).at[..., :ones].set(1.0)
    c = _bf(jnp.concatenate([c, pad], axis=-1))
    joined = jnp.concatenate([own.reshape(kv, n, st.t, LANE), c], axis=2)
    return joined.reshape(kv, n * (st.t + p), LANE)


def _rope_tables(t):
    half = HEAD_DIM // 2
    nf = half // 2
    freqs = ROPE_THETA ** (-jnp.arange(nf, dtype=F32) / nf)
    tt = jnp.arange(t)
    ang_r = (tt // GRID_W).astype(F32)[:, None] * freqs[None, :]
    ang_c = (tt % GRID_W).astype(F32)[:, None] * freqs[None, :]
    zero = jnp.zeros_like(ang_r)
    cos = jnp.concatenate([jnp.cos(ang_r)] * 2 + [jnp.cos(ang_c)] * 2, axis=1)
    sa = jnp.concatenate([-jnp.sin(ang_r), zero, -jnp.sin(ang_c), zero], axis=1)
    sb = jnp.concatenate([zero, jnp.sin(ang_r), zero, jnp.sin(ang_c)], axis=1)
    return tuple(jnp.tile(a, (1, LANE // HEAD_DIM)) for a in (cos, sa, sb))


def _key_bound_kernel(k_ref, o_ref):
    ones = jnp.ones((LANE, LANE), BF16)
    for kv in range(ATT_KV):
        k = k_ref[kv].astype(F32)
        k = jnp.where(lax.broadcasted_iota(jnp.int32, k.shape, 1) < HEAD_DIM, k, 0.0)
        best = jnp.max(_dot_data_mask(k * k, ones, 2), axis=0, keepdims=True)
        o_ref[kv] = jnp.broadcast_to(best, (SUBLANE, LANE))


def _key_bound(n, keys_per_seq, k):
    return pl.pallas_call(
        _key_bound_kernel,
        grid=(n,),
        in_specs=[pl.BlockSpec((ATT_KV, keys_per_seq, LANE), lambda b: (0, b, 0))],
        out_specs=pl.BlockSpec((None, ATT_KV, SUBLANE, LANE), lambda b: (b, 0, 0, 0)),
        out_shape=jax.ShapeDtypeStruct((n, ATT_KV, SUBLANE, LANE), F32),
        compiler_params=_cp("parallel"),
        name="key_bound",
    )(k)


def _attn_kernel(q_ref, k_ref, v_ref, kb_ref, o_ref, m_scr, acc_scr, *, tq, n_tiles):
    rows = ATT_GROUP * tq
    lane = lax.broadcasted_iota(jnp.int32, (rows, LANE), 1)
    outs = []
    for kv in range(ATT_KV):
        qs = _rows([q_ref[:, (kv * ATT_GROUP + g) * LANE:(kv * ATT_GROUP + g + 1) * LANE]
                    for g in range(ATT_GROUP)]).astype(F32)

        def key_tile(ref, j):
            return ref[kv, pl.ds(pl.multiple_of(j * KEY_TILE, KEY_TILE), KEY_TILE), :]

        def weighted_sum(q_shift, unroll):
            acc_scr[...] = jnp.zeros((rows, LANE), F32)

            def sum_body(j, carry):
                p = jnp.exp2(_dot_nt(q_shift, key_tile(k_ref, j)))
                acc_scr[...] += _dot(_bf(p), key_tile(v_ref, j))
                return carry

            lax.fori_loop(0, n_tiles, sum_body, 0, unroll=unroll)
            return acc_scr[...]

        k_len = jnp.sqrt(kb_ref[kv][0:1, :])
        unroll = max(u for u in range(1, KEY_UNROLL + 1) if n_tiles % u == 0)
        acc = weighted_sum(_bf(qs * jnp.where(lane == HEAD_DIM, -k_len, 1.0)), unroll)

        def exact_shift():
            m_scr[...] = jnp.full((rows, LANE), -jnp.inf, F32)
            q0 = _bf(jnp.where(lane >= HEAD_DIM, 0.0, qs))

            def max_body(j, carry):
                sc = _dot_nt(q0, key_tile(k_ref, j))
                m_scr[...] = jnp.maximum(m_scr[...], jnp.maximum(sc[:, :LANE], sc[:, LANE:]))
                return carry

            lax.fori_loop(0, n_tiles, max_body, 0)
            row_max = jnp.max(m_scr[...], axis=-1, keepdims=True)
            hi = _bf(row_max).astype(F32)
            shifted = jnp.where(lane == HEAD_DIM, -hi, jnp.where(lane == HEAD_DIM + 1, hi - row_max, qs))
            return weighted_sum(_bf(shifted), 1)

        row_sum = acc[:, HEAD_DIM:HEAD_DIM + 1]
        row_sum_ok = jnp.logical_and(jnp.min(row_sum) >= ATT_MIN_ROW_SUM, jnp.max(row_sum) <= 1.0 / ATT_MIN_ROW_SUM)
        acc = lax.cond(row_sum_ok, lambda: acc, exact_shift)
        o = acc[:, :HEAD_DIM] / acc[:, HEAD_DIM:HEAD_DIM + 1]
        outs += [o[g * tq:(g + 1) * tq] for g in range(ATT_GROUP)]
    o_ref[...] = _bf(_lanes(outs))


def _attention(st, q, k, v):
    tq = min(ATT_Q_TILE, st.t)
    nq = st.t // tq
    keys = k.shape[1] // st.n
    assert keys % KEY_TILE == 0
    kv_spec = pl.BlockSpec((ATT_KV, keys, LANE), lambda b, i: (0, b, 0))
    rows = ATT_GROUP * tq
    return pl.pallas_call(
        functools.partial(_attn_kernel, tq=tq, n_tiles=keys // KEY_TILE),
        grid=(st.n, nq),
        in_specs=[pl.BlockSpec((tq, ATT_HEADS * LANE), lambda b, i: (b * nq + i, 0)), kv_spec, kv_spec,
                  pl.BlockSpec((None, ATT_KV, SUBLANE, LANE), lambda b, i: (b, 0, 0, 0))],
        out_specs=pl.BlockSpec((tq, 512), lambda b, i: (b * nq + i, 0)),
        out_shape=jax.ShapeDtypeStruct((st.rows, 512), BF16),
        scratch_shapes=[pltpu.VMEM((rows, LANE), F32), pltpu.VMEM((rows, LANE), F32)],
        compiler_params=_cp("parallel", "parallel"),
        name="attention",
    )(q, k, v, _key_bound(st.n, keys, k))


def _halo_specs(st, width):
    tr = _row_tile(st)
    per8 = tr // SUBLANE
    last8 = st.rows // SUBLANE - 1
    return [pl.BlockSpec((tr, width), lambda i: (i, 0)),
            pl.BlockSpec((SUBLANE, width), lambda i: (jnp.maximum(i * per8 - 1, 0), 0)),
            pl.BlockSpec((SUBLANE, width), lambda i: (jnp.minimum((i + 1) * per8, last8), 0))]


def _neighbours(cur, prev8, next8, tiles_per_seq):
    tr = cur.shape[0]
    j = pl.program_id(0) % tiles_per_seq
    pr = jnp.where(j != 0, prev8[SUBLANE - 1:SUBLANE], 0.0)
    nx = jnp.where(j != tiles_per_seq - 1, next8[0:1], 0.0)
    row = lax.broadcasted_iota(jnp.int32, cur.shape, 0)
    x_prev = jnp.where(row == 0, pr, pltpu.roll(cur, 1, 0))
    x_next = jnp.where(row == tr - 1, nx, pltpu.roll(cur, tr - 1, 0))
    return x_prev, x_next


def _dir_masks(z):
    ri = lax.broadcasted_iota(jnp.int32, (CHUNK, CHUNK), 0)
    ci = lax.broadcasted_iota(jnp.int32, (CHUNK, CHUNK), 1)
    return ((ci <= ri), (ci < ri)) if z == 0 else ((ci >= ri), (ci > ri))


def _tri_blocks(tr):
    i = np.arange(tr)
    same = (i[:, None] // CHUNK) == (i[None, :] // CHUNK)
    lower = same & (i[None, :] <= i[:, None])
    upper = same & (i[None, :] >= i[:, None])
    return _bf(jnp.asarray(lower.astype(np.float32))), _bf(jnp.asarray(upper.astype(np.float32)))


def _scan_blocks(st):
    ncb = min(SCAN_CHUNKS, st.t // CHUNK)
    rb = ncb * CHUNK
    nblk = st.t // rb
    fwd = lambda b, j: (b * nblk + j, 0)
    bwd = lambda b, j: (b * nblk + nblk - 1 - j, 0)
    return ncb, rb, nblk, fwd, bwd


def _chunk_rows(step, z, ncb):
    cc = step if z == 0 else ncb - 1 - step
    return slice(cc * CHUNK, (cc + 1) * CHUNK)


def _ssd_prep_kernel(cur_ref, prev_ref, next_ref, cw_ref, cb_ref, dtb_ref, alog_ref, lo_ref, up_ref,
                     xbc_out, dt_out, cum_out, dtt_out, cumt_out, *, tiles_per_seq):
    cur = cur_ref[...]
    xc = cur[:, 512:1280]
    x_prev, x_next = _neighbours(xc, prev_ref[:, 512:1280], next_ref[:, 512:1280], tiles_per_seq)
    cw = cw_ref[...]
    conv = cb_ref[...] + x_prev * cw[0:1] + xc * cw[1:2] + x_next * cw[2:3]
    xbc_out[...] = _silu(conv)
    dt = _softplus(cur[:, 1280:1408] + dtb_ref[...])
    dt_out[...] = dt
    ld = dt * -jnp.exp(alog_ref[...])
    lane = lax.broadcasted_iota(jnp.int32, ld.shape, 1)
    cum = jnp.where(lane < SSD_HEADS, _dot_mask_data(lo_ref[...], ld, 3), _dot_mask_data(up_ref[...], ld, 3))
    cum_out[...] = cum
    pick = (lax.broadcasted_iota(jnp.int32, (2 * SSD_HEADS, LANE), 0)
            == lax.broadcasted_iota(jnp.int32, (2 * SSD_HEADS, LANE), 1)).astype(BF16)
    dtt_out[...] = _dot_nt_mask_data(pick, dt, 3)
    cumt_out[...] = _dot_nt_mask_data(pick, cum, 3)


def _pad_lanes(v, width=LANE):
    v = v.reshape(1, -1)
    return jnp.pad(v, ((0, 0), (0, width - v.shape[1])))


def _ssd_prep(st, proj, conv_w, conv_b, dt_bias, a_log):
    m = proj.shape[0]
    tr = _row_tile(st)
    full = lambda shape: pl.BlockSpec(shape, lambda i: (0, 0))
    row = lambda w: pl.BlockSpec((tr, w), lambda i: (i, 0))
    col = pl.BlockSpec((2 * SSD_HEADS, tr), lambda i: (0, i))
    lower, upper = _tri_blocks(tr)
    return pl.pallas_call(
        functools.partial(_ssd_prep_kernel, tiles_per_seq=st.t // tr),
        grid=(m // tr,),
        in_specs=_halo_specs(st, SSD_W) + [full((3, 768)), full((1, 768)), full((1, LANE)), full((1, LANE)),
                                           full((tr, tr)), full((tr, tr))],
        out_specs=[row(768), row(LANE), row(LANE), col, col],
        out_shape=[jax.ShapeDtypeStruct((m, 768), F32), jax.ShapeDtypeStruct((m, LANE), F32),
                   jax.ShapeDtypeStruct((m, LANE), F32), jax.ShapeDtypeStruct((2 * SSD_HEADS, m), F32),
                   jax.ShapeDtypeStruct((2 * SSD_HEADS, m), F32)],
        compiler_params=_cp("parallel"),
        name="ssd_prep",
    )(proj, proj, proj, conv_w.T, conv_b.reshape(1, 768), _pad_lanes(dt_bias), _pad_lanes(a_log), lower, upper)


def _ssd_scan_kernel(xf_ref, xb_ref, dtf_ref, dtb_ref, cf_ref, cb_ref, dttf_ref, dttb_ref, ctf_ref, ctb_ref,
                     sel_ref, s0_ref, yf_ref, yb_ref, s_ref, *, ncb):
    @pl.when(pl.program_id(1) == 0)
    def _():
        s_ref[...] = s0_ref[...]

    hpg = SSD_HEADS // SSD_GROUPS
    state = {(z, g): _lanes([s_ref[z, g * hpg + hh] for hh in range(hpg)])
             for z in range(2) for g in range(SSD_GROUPS)}
    refs = ((xf_ref, dtf_ref, cf_ref, dttf_ref, ctf_ref, yf_ref), (xb_ref, dtb_ref, cb_ref, dttb_ref, ctb_ref, yb_ref))
    groups = [(z, g) for z in range(2) for g in range(SSD_GROUPS)]
    heads = [(z, h) for z in range(2) for h in range(SSD_HEADS)]
    for step in range(ncb):
        xbc, dt, cum, dtt, cumt, incl, rows = {}, {}, {}, {}, {}, {}, {}
        for z in range(2):
            x_ref, dt_ref, c_ref, dtt_ref, ct_ref, _ = refs[z]
            rows[z] = _chunk_rows(step, z, ncb)
            xbc[z] = x_ref[rows[z], :]
            dt[z] = _dot_data_mask(dt_ref[rows[z], :], sel_ref[z], 3)
            cum[z] = _dot_data_mask(c_ref[rows[z], :], sel_ref[z], 3)
            dtt[z], cumt[z] = dtt_ref[:, rows[z]], ct_ref[:, rows[z]]
            incl[z] = _dir_masks(z)[0]
        bmat = {(z, g): xbc[z][:, 512 + g * SSD_N:512 + (g + 1) * SSD_N] for z, g in groups}
        cmat = {(z, g): _bf(xbc[z][:, 640 + g * SSD_N:640 + (g + 1) * SSD_N]) for z, g in groups}
        cb = {k: _dot_nt(cmat[k], _bf(bmat[k])) for k in groups}
        cs = {k: _dot(cmat[k], _bf(state[k])) for k in groups}
        gw = hpg * SSD_P
        glast = {z: cum[z][CHUNK - 1:CHUNK] if z == 0 else cum[z][0:1] for z in range(2)}
        xs = {z: xbc[z][:, :SSD_HEADS * SSD_P] for z in range(2)}
        xw = {z: xs[z] * (dt[z] * jnp.exp(glast[z] - cum[z])) for z in range(2)}
        inc = {(z, g): _dot_tn(_bf(bmat[z, g]), _bf(xw[z][:, g * gw:(g + 1) * gw])) for z, g in groups}
        att = {}
        for z, h in heads:
            ln = z * SSD_HEADS + h
            gcol = cum[z][:, h * SSD_P:(h + 1) * SSD_P]
            dec = jnp.exp(jnp.where(incl[z], gcol - cumt[z][ln:ln + 1], -jnp.inf))
            att[z, h] = _bf(cb[z, h // hpg] * dec * dtt[z][ln:ln + 1])
        intra = {(z, h): _dot(att[z, h], _bf(xs[z][:, h * SSD_P:(h + 1) * SSD_P])) for z, h in heads}
        for z in range(2):
            e_in = jnp.exp(cum[z])
            e_last = jnp.exp(glast[z])
            outs = []
            for g in range(SSD_GROUPS):
                hs = range(g * hpg, (g + 1) * hpg)
                outs.append(_lanes([intra[z, h] for h in hs]) + cs[z, g] * e_in[:, g * gw:(g + 1) * gw])
                state[z, g] = state[z, g] * e_last[:, g * gw:(g + 1) * gw] + inc[z, g]
            refs[z][5][rows[z], :] = _bf(_lanes(outs))
    for z in range(2):
        for h in range(SSD_HEADS):
            s_ref[z, h] = state[z, h // hpg][:, (h % hpg) * SSD_P:(h % hpg + 1) * SSD_P]


def _ssd_scan(st, xbc, dt, cum, dtt, cumt, s0):
    ncb, rb, nblk, fwd, bwd = _scan_blocks(st)
    fwd_t = lambda b, j: (0, fwd(b, j)[0])
    bwd_t = lambda b, j: (0, bwd(b, j)[0])
    state_spec = pl.BlockSpec((None, 2, SSD_HEADS, SSD_N, SSD_P), lambda b, j: (b, 0, 0, 0, 0))
    rows = lambda w, m: pl.BlockSpec((rb, w), m)
    cols = lambda m: pl.BlockSpec((2 * SSD_HEADS, rb), m)
    lane = np.arange(LANE)[None, :, None]
    col = np.arange(SSD_HEADS * SSD_P)[None, None, :]
    sel = _bf(jnp.asarray((lane == np.arange(2)[:, None, None] * SSD_HEADS + col // SSD_P).astype(np.float32)))
    return pl.pallas_call(
        functools.partial(_ssd_scan_kernel, ncb=ncb),
        grid=(st.n, nblk),
        in_specs=[rows(768, fwd), rows(768, bwd), rows(LANE, fwd), rows(LANE, bwd), rows(LANE, fwd), rows(LANE, bwd),
                  cols(fwd_t), cols(bwd_t), cols(fwd_t), cols(bwd_t),
                  pl.BlockSpec(sel.shape, lambda b, j: (0, 0, 0)), state_spec],
        out_specs=[rows(512, fwd), rows(512, bwd), state_spec],
        out_shape=[jax.ShapeDtypeStruct((st.rows, 512), BF16), jax.ShapeDtypeStruct((st.rows, 512), BF16),
                   jax.ShapeDtypeStruct((st.n, 2, SSD_HEADS, SSD_N, SSD_P), F32)],
        compiler_params=_cp("parallel", "arbitrary"),
        name="ssd_scan",
    )(xbc, xbc, dt, dt, cum, cum, dtt, dtt, cumt, cumt, sel, s0)


def _ssd_post_kernel(yf_ref, yb_ref, xbc_ref, p_ref, d_ref, g_ref, o_ref):
    y = yf_ref[...].astype(F32) + yb_ref[...].astype(F32) + d_ref[...] * xbc_ref[:, :512]
    y = y * _silu(p_ref[:, :512])
    o_ref[...] = _bf(y * lax.rsqrt(jnp.mean(y * y, axis=-1, keepdims=True) + 1e-6) * g_ref[...])


def _ssd_post(st, y_f, y_b, xbc, proj, ssd_d, ssd_norm):
    m = proj.shape[0]
    tr = _row_tile(st)
    full = lambda shape: pl.BlockSpec(shape, lambda i: (0, 0))
    row = lambda w: pl.BlockSpec((tr, w), lambda i: (i, 0))
    return pl.pallas_call(
        _ssd_post_kernel,
        grid=(m // tr,),
        in_specs=[row(512), row(512), row(512), row(512), full((1, 512)), full((1, 512))],
        out_specs=row(512),
        out_shape=jax.ShapeDtypeStruct((m, 512), BF16),
        compiler_params=_cp("parallel"),
        name="ssd_post",
    )(y_f, y_b, xbc, proj, jnp.repeat(ssd_d, SSD_P).reshape(1, 512), ssd_norm.reshape(1, 512))


GLA_QK = GLA_HEADS * GLA_DK
GLA_SAFE_RANGE = 60.0


def _gla_prep_kernel(p_ref, g2_ref, gb_ref, lo_ref, up_ref, cum_out):
    logit = _dot_split(p_ref[...], g2_ref[...]) + gb_ref[...]
    log_a = -_softplus(-logit) * (1.0 / GLA_GATE_NORM)
    cum_out[:, :GLA_QK] = _dot_mask_data(lo_ref[...], log_a[:, :GLA_QK], 3)
    cum_out[:, GLA_QK:] = _dot_mask_data(up_ref[...], log_a[:, GLA_QK:], 3)


def _gla_prep(st, proj, g2p, gb):
    m = proj.shape[0]
    tr = _row_tile(st)
    full = lambda shape: pl.BlockSpec(shape, lambda i: (0, 0))
    lower, upper = _tri_blocks(tr)
    return pl.pallas_call(
        _gla_prep_kernel,
        grid=(m // tr,),
        in_specs=[pl.BlockSpec((tr, LANE), lambda i: (i, (GLA_W - LANE) // LANE)), full((LANE, 2 * GLA_QK)),
                  full((1, 2 * GLA_QK)),
                  full((tr, tr)), full((tr, tr))],
        out_specs=pl.BlockSpec((tr, 2 * GLA_QK), lambda i: (i, 0)),
        out_shape=jax.ShapeDtypeStruct((m, 2 * GLA_QK), F32),
        compiler_params=_cp("parallel"),
        name="gla_prep",
    )(proj, g2p, gb, lower, upper)


def _gla_intra_exact(p_ref, c_ref, rows, z, seg_ref):
    r0 = rows.start
    q = p_ref[rows, 0:GLA_QK] * (GLA_DK ** -0.5)
    cum = c_ref[rows, :]
    row = lax.broadcasted_iota(jnp.int32, (CHUNK, 1), 0)
    seg = seg_ref[...]

    def body(j, acc):
        kj = p_ref[pl.ds(r0 + j, 1), GLA_QK:2 * GLA_QK]
        vj = p_ref[pl.ds(r0 + j, 1), 2 * GLA_QK:2 * GLA_QK + GLA_HEADS * GLA_DV]
        seen = (row >= j) if z == 0 else (row <= j)
        w = jnp.where(seen, q * kj * jnp.exp(jnp.minimum(cum - c_ref[pl.ds(r0 + j, 1), :], 0.0)), 0.0)
        score = _dot_data_mask(w, seg, 3)
        return tuple(a + score[:, h:h + 1] * vj[:, h * GLA_DV:(h + 1) * GLA_DV] for h, a in enumerate(acc))

    zero = jnp.zeros((CHUNK, GLA_DV), F32)
    return list(lax.fori_loop(0, CHUNK, body, (zero,) * GLA_HEADS))


def _gla_scan_kernel(pf_ref, pb_ref, cf_ref, cb_ref, seg_ref, s0_ref, yf_ref, yb_ref, s_ref, *, ncb):
    @pl.when(pl.program_id(1) == 0)
    def _():
        s_ref[...] = s0_ref[...]

    refs = ((pf_ref, cf_ref, yf_ref), (pb_ref, cb_ref, yb_ref))
    heads = [(z, h) for z in range(2) for h in range(GLA_HEADS)]
    hsl = [slice(h * GLA_DK, (h + 1) * GLA_DK) for h in range(GLA_HEADS)]
    state = {(z, h): s_ref[z, h] for z, h in heads}
    chunks = [(step, z) for step in range(ncb) for z in range(2)]
    rows = {(step, z): _chunk_rows(step, z, ncb) for step, z in chunks}
    cum = {k: refs[k[1]][1][rows[k], :] for k in chunks}
    mid = {k: cum[k][CHUNK // 2:CHUNK // 2 + 1] for k in chunks}
    span = functools.reduce(jnp.maximum, [jnp.max(jnp.abs(cum[k] - mid[k])) for k in chunks])

    def v_of(k, h):
        return _bf(refs[k[1]][0][rows[k], 2 * GLA_QK + h * GLA_DV:2 * GLA_QK + (h + 1) * GLA_DV])

    def intra_factored():
        att = {}
        for k in chunks:
            p_ref = refs[k[1]][0]
            q_mid = _bf(p_ref[rows[k], 0:GLA_QK] * (GLA_DK ** -0.5) * jnp.exp(cum[k] - mid[k]))
            k_mid = _bf(p_ref[rows[k], GLA_QK:2 * GLA_QK] * jnp.exp(mid[k] - cum[k]))
            incl = _dir_masks(k[1])[0]
            for h in range(GLA_HEADS):
                att[k, h] = _bf(jnp.where(incl, _dot_nt(q_mid[:, hsl[h]], k_mid[:, hsl[h]]), 0.0))
        return [_dot(att[k, h], v_of(k, h)) for k in chunks for h in range(GLA_HEADS)]

    def intra_exact():
        out = []
        for k in chunks:
            out += _gla_intra_exact(refs[k[1]][0], refs[k[1]][1], rows[k], k[1], seg_ref)
        return out

    intra = lax.cond(span <= GLA_SAFE_RANGE, intra_factored, intra_exact)
    intra = {(k, h): intra[i * GLA_HEADS + h] for i, k in enumerate(chunks) for h in range(GLA_HEADS)}

    for step in range(ncb):
        q_in, k_out, e_last = {}, {}, {}
        for z in range(2):
            k = (step, z)
            p_ref = refs[z][0]
            glast = cum[k][CHUNK - 1:CHUNK] if z == 0 else cum[k][0:1]
            q_in[z] = _bf(p_ref[rows[k], 0:GLA_QK] * (GLA_DK ** -0.5) * jnp.exp(cum[k]))
            k_out[z] = _bf(p_ref[rows[k], GLA_QK:2 * GLA_QK] * jnp.exp(glast - cum[k]))
            e_last[z] = jnp.exp(glast)
        inter = {(z, h): _dot_nt(q_in[z][:, hsl[h]], _bf(state[z, h])) for z, h in heads}
        inc = {(z, h): _dot_tn(v_of((step, z), h), k_out[z][:, hsl[h]]) for z, h in heads}
        for z in range(2):
            refs[z][2][rows[step, z], :] = _bf(_lanes([intra[(step, z), h] + inter[z, h] for h in range(GLA_HEADS)]))
            for h in range(GLA_HEADS):
                state[z, h] = state[z, h] * e_last[z][:, hsl[h]] + inc[z, h]
    for z, h in heads:
        s_ref[z, h] = state[z, h]


def _gla_scan(st, proj, cum, s0_t):
    ncb, rb, nblk, fwd, bwd = _scan_blocks(st)
    seg = _block_diag(GLA_QK, GLA_DK, 1.0)[:, ::GLA_DK]
    seg = jnp.pad(seg, ((0, 0), (0, LANE - GLA_HEADS)))
    state_spec = pl.BlockSpec((None, 2, GLA_HEADS, GLA_DV, GLA_DK), lambda b, j: (b, 0, 0, 0, 0))
    qkv_w = 2 * GLA_QK + GLA_HEADS * GLA_DV
    return pl.pallas_call(
        functools.partial(_gla_scan_kernel, ncb=ncb),
        grid=(st.n, nblk),
        in_specs=[pl.BlockSpec((rb, qkv_w), fwd), pl.BlockSpec((rb, qkv_w), bwd),
                  pl.BlockSpec((rb, GLA_QK), fwd), pl.BlockSpec((rb, GLA_QK), lambda b, j: (bwd(b, j)[0], 1)),
                  pl.BlockSpec((GLA_QK, LANE), lambda b, j: (0, 0)), state_spec],
        out_specs=[pl.BlockSpec((rb, 512), fwd), pl.BlockSpec((rb, 512), bwd), state_spec],
        out_shape=[jax.ShapeDtypeStruct((st.rows, 512), BF16), jax.ShapeDtypeStruct((st.rows, 512), BF16),
                   jax.ShapeDtypeStruct((st.n, 2, GLA_HEADS, GLA_DV, GLA_DK), F32)],
        compiler_params=_cp("parallel", "arbitrary"),
        name="gla_scan",
    )(proj, proj, cum, cum, seg, s0_t)


def _gla_post_kernel(yf_ref, yb_ref, p_ref, g_ref, o_ref):
    o = yf_ref[...].astype(F32) + yb_ref[...].astype(F32)
    gate = _silu(p_ref[...])
    outs = []
    for h in range(GLA_HEADS):
        oh = o[:, h * GLA_DV:(h + 1) * GLA_DV]
        outs.append(oh * lax.rsqrt(jnp.mean(oh * oh, axis=-1, keepdims=True) + 1e-6) * g_ref[...])
    o_ref[...] = _bf(_lanes(outs) * gate)


def _gla_post(st, y_f, y_b, proj, gla_norm):
    m = proj.shape[0]
    tr = _row_tile(st)
    return pl.pallas_call(
        _gla_post_kernel,
        grid=(m // tr,),
        in_specs=[pl.BlockSpec((tr, 512), lambda i: (i, 0)), pl.BlockSpec((tr, 512), lambda i: (i, 0)),
                  pl.BlockSpec((tr, 512), lambda i: (i, 2)), pl.BlockSpec((1, GLA_DV), lambda i: (0, 0))],
        out_specs=pl.BlockSpec((tr, 512), lambda i: (i, 0)),
        out_shape=jax.ShapeDtypeStruct((m, 512), BF16),
        compiler_params=_cp("parallel"),
        name="gla_post",
    )(y_f, y_b, proj, gla_norm.reshape(1, GLA_DV))


def _rwkv_prep_kernel(cur_ref, prev_ref, next_ref, mu_ref, w2_ref, w0_ref, a2_ref, a0_ref, g2_ref,
                      kkw_ref, ka_ref, rk_ref, bd_ref, in_out, lw_out, post_out, *, tiles_per_seq):
    cur = cur_ref[...]
    x_prev, x_next = _neighbours(cur, prev_ref[...], next_ref[...], tiles_per_seq)
    blk = cur + (0.5 * (x_prev + x_next) - cur) * mu_ref[...]
    r, k, v = blk[:, 0:512], blk[:, 512:1024], blk[:, 1024:1536]
    w_logit = w0_ref[...] + _dot_split(jnp.tanh(blk[:, 1536:1600]), w2_ref[...])
    lw_out[...] = -RWKV_DECAY_SCALE * jax.nn.sigmoid(w_logit)
    a = jax.nn.sigmoid(a0_ref[...] + _dot_split(blk[:, 1600:1664], a2_ref[...]))
    g = _dot_split(jax.nn.sigmoid(blk[:, 1664:1792]), g2_ref[...])
    bd = bd_ref[...]
    kk = k * kkw_ref[...]
    kk = kk * lax.rsqrt(_dot_data_mask(kk * kk, bd, 2) + 1e-12)
    k2 = k * (1.0 + (a - 1.0) * ka_ref[...])
    in_out[:, 0:512] = _bf(r)
    in_out[:, 512:1024] = _bf(k2)
    in_out[:, 1024:1536] = _bf(v)
    in_out[:, 1536:2048] = _bf(-kk)
    in_out[:, 2048:2560] = _bf(kk * a)
    post_out[:, 0:512] = _bf(g)
    post_out[:, 512:1024] = _bf(_dot_data_mask(r * k2 * rk_ref[...], bd, 2) * v)


def _rwkv_prep(st, proj, mu, w2, w0, a2, a0, g2, kkw, ka, rk):
    m = proj.shape[0]
    tr = _row_tile(st)
    full = lambda shape: pl.BlockSpec(shape, lambda i: (0, 0))
    row = lambda w: pl.BlockSpec((tr, w), lambda i: (i, 0))
    vec = lambda a: a.reshape(1, -1)
    return pl.pallas_call(
        functools.partial(_rwkv_prep_kernel, tiles_per_seq=st.t // tr),
        grid=(m // tr,),
        in_specs=_halo_specs(st, RWKV_W) + [full((1, RWKV_W)), full((64, 1024)), full((1, 1024)),
                                            full((64, 512)), full((1, 512)), full((128, 512)),
                                            full((1, 512)), full((1, 512)), full((1, 512)), full((512, 512))],
        out_specs=[row(2560), row(1024), row(1024)],
        out_shape=[jax.ShapeDtypeStruct((m, 2560), BF16), jax.ShapeDtypeStruct((m, 1024), F32),
                   jax.ShapeDtypeStruct((m, 1024), BF16)],
        compiler_params=_cp("parallel"),
        name="rwkv_prep",
    )(proj, proj, proj, vec(mu), jnp.concatenate([w2[0], w2[1]], axis=1), vec(w0), a2, vec(a0), g2,
      vec(kkw), vec(ka), vec(rk), _block_diag(512, RWKV_HD, 1.0))


def _rwkv_chunk(x_ref, lw_ref, rows, z):
    incl, strict = _dir_masks(z)
    lw = lw_ref[rows, :]
    cum = _dot_mask_data(incl.astype(BF16), lw, 3)
    cum_x = cum - lw
    mid = cum[CHUNK // 2:CHUNK // 2 + 1]
    glast = cum[CHUNK - 1:CHUNK] if z == 0 else cum[0:1]
    x = x_ref[rows, :]
    r, k, v, al, be = x[:, 0:512], x[:, 512:1024], x[:, 1024:1536], x[:, 1536:2048], x[:, 2048:2560]
    e_mid, e_nmid = jnp.exp(cum - mid), jnp.exp(mid - cum)
    e_out = jnp.exp(glast - cum)
    return dict(incl=incl, strict=strict, v=v,
                r_mid=_bf(r * e_mid), a_mid=_bf(al * jnp.exp(cum_x - mid)),
                b_mid=_bf(be * e_nmid), k_mid=_bf(k * e_nmid),
                r_in=_bf(r * jnp.exp(cum)), a_in=al * jnp.exp(cum_x),
                b_out=_bf(be * e_out), k_out=_bf(k * e_out), e_last=jnp.exp(glast))


_HEAD_SLICES = [slice(h * RWKV_HD, (h + 1) * RWKV_HD) for h in range(RWKV_HEADS)]


def _rwkv_state_free(chunks, hooks):
    hooks = list(hooks)

    def run_hook():
        if hooks:
            hooks.pop(0)()

    chains = [(c, sl) for c in chunks for sl in _HEAD_SLICES]
    run_hook()
    pair = [_dot_nt(_rows([c['a_mid'][:, sl], c['r_mid'][:, sl]]), _rows([c['b_mid'][:, sl], c['k_mid'][:, sl]]))
            for c, sl in chains]
    half = CHUNK
    a_ab = [_bf(jnp.where(c['strict'], p[:half, :half], 0.0)) for (c, _), p in zip(chains, pair)]
    a_ak = [_bf(jnp.where(c['strict'], p[:half, half:], 0.0)) for (c, _), p in zip(chains, pair)]
    a_rb = [_bf(jnp.where(c['incl'], p[half:, :half], 0.0)) for (c, _), p in zip(chains, pair)]
    a_rk = [_bf(jnp.where(c['incl'], p[half:, half:], 0.0)) for (c, _), p in zip(chains, pair)]
    vb = [_bf(c['v'][:, sl]) for c, sl in chains]
    both = [_dot(_rows([ak, rk]), v) for ak, rk, v in zip(a_ak, a_rk, vb)]
    av = [x[:half] for x in both]
    o0 = [x[half:] for x in both]
    sol = [_lanes([c['a_in'][:, sl], x]) for (c, sl), x in zip(chains, av)]
    powr = a_ab
    width = 2 * RWKV_HD
    for it in range(6):
        if it < 5:
            both = [_dot(p, _lanes([_bf(s), p])) for p, s in zip(powr, sol)]
            sol = [s + x[:, :width] for s, x in zip(sol, both)]
            powr = [_bf(x[:, width:]) for x in both]
        else:
            sol = [s + _dot(p, _bf(s)) for p, s in zip(powr, sol)]
        if it in (1, 3):
            run_hook()
    while hooks:
        run_hook()
    return [dict(w=_bf(s[:, :RWKV_HD]), u0=s[:, RWKV_HD:], o0=o, a_rb=a, v=c['v'][:, sl], r_in=c['r_in'][:, sl],
                 b_out=c['b_out'][:, sl], k_out=c['k_out'][:, sl], e_last=c['e_last'][:, sl])
            for (c, sl), s, o, a in zip(chains, sol, o0, a_rb)]


def _rwkv_state_stages(res, state, write_out):
    box = {}

    def read_state():
        box['su'] = [_dot_nt(_rows([c['w'], c['r_in']]), _bf(s)) for c, s in zip(res, state)]

    def update_state():
        box['u'] = [c['u0'] + su[:CHUNK] for c, su in zip(res, box['su'])]
        state[:] = [s * c['e_last'] + _dot_tn(_rows([_bf(u), _bf(c['v'])]), _rows([c['b_out'], c['k_out']]))
                    for c, s, u in zip(res, state, box['u'])]

    def emit():
        write_out([su[CHUNK:] + _dot(c['a_rb'], _bf(u)) + c['o0'] for c, su, u in zip(res, box['su'], box['u'])])

    return [read_state, update_state, emit]


def _rwkv_scan_kernel(xf_ref, xb_ref, lwf_ref, lwb_ref, s0_ref, yf_ref, yb_ref, s_ref, *, ncb):
    @pl.when(pl.program_id(1) == 0)
    def _():
        s_ref[...] = s0_ref[...]

    state = [s_ref[z, h] for z in range(2) for h in range(RWKV_HEADS)]

    def chunk_rows(step, z):
        cc = step if z == 0 else ncb - 1 - step
        return slice(cc * CHUNK, (cc + 1) * CHUNK)

    def pair_inputs(step):
        return [_rwkv_chunk(xf_ref, lwf_ref, chunk_rows(step, 0), 0),
                _rwkv_chunk(xb_ref, lwb_ref, chunk_rows(step, 1), 1)]

    def writer(step):
        def write_out(outs):
            yf_ref[chunk_rows(step, 0), :] = _bf(_lanes(outs[:RWKV_HEADS]))
            yb_ref[chunk_rows(step, 1), :] = _bf(_lanes(outs[RWKV_HEADS:]))
        return write_out

    res = _rwkv_state_free(pair_inputs(0), [])
    for step in range(ncb):
        hooks = _rwkv_state_stages(res, state, writer(step))
        if step + 1 < ncb:
            res = _rwkv_state_free(pair_inputs(step + 1), hooks)
        else:
            for hook in hooks:
                hook()
    for z in range(2):
        for h in range(RWKV_HEADS):
            s_ref[z, h] = state[z * RWKV_HEADS + h]


def _rwkv_scan(st, xin, lw, s0):
    ncb, rb, nblk, fwd, bwd = _scan_blocks(st)
    state_spec = pl.BlockSpec((None, 2, RWKV_HEADS, RWKV_HD, RWKV_HD), lambda b, j: (b, 0, 0, 0, 0))
    return pl.pallas_call(
        functools.partial(_rwkv_scan_kernel, ncb=ncb),
        grid=(st.n, nblk),
        in_specs=[pl.BlockSpec((rb, 2560), fwd), pl.BlockSpec((rb, 2560), bwd),
                  pl.BlockSpec((rb, 512), fwd), pl.BlockSpec((rb, 512), lambda b, j: (bwd(b, j)[0], 1)),
                  state_spec],
        out_specs=[pl.BlockSpec((rb, 512), fwd), pl.BlockSpec((rb, 512), bwd), state_spec],
        out_shape=[jax.ShapeDtypeStruct((st.rows, 512), BF16), jax.ShapeDtypeStruct((st.rows, 512), BF16),
                   jax.ShapeDtypeStruct((st.n, 2, RWKV_HEADS, RWKV_HD, RWKV_HD), F32)],
        compiler_params=_cp("parallel", "arbitrary"),
        name="rwkv_scan",
    )(xin, xin, lw, lw, s0)


def _rwkv_post_kernel(yf_ref, yb_ref, post_ref, lng_ref, lnb_ref, bd_ref, o_ref):
    o = yf_ref[...].astype(F32) + yb_ref[...].astype(F32)
    bd = bd_ref[...]
    d = o - _dot_data_mask(o, bd, 2)
    o = d * lax.rsqrt(_dot_data_mask(d * d, bd, 2) + RWKV_LN_EPS) * lng_ref[...] + lnb_ref[...]
    o_ref[...] = _bf((o + post_ref[:, 512:1024].astype(F32)) * post_ref[:, 0:512].astype(F32))


def _rwkv_post(st, y_f, y_b, post, ln_g, ln_b):
    m = post.shape[0]
    tr = _row_tile(st)
    full = lambda shape: pl.BlockSpec(shape, lambda i: (0, 0))
    row = lambda w: pl.BlockSpec((tr, w), lambda i: (i, 0))
    return pl.pallas_call(
        _rwkv_post_kernel,
        grid=(m // tr,),
        in_specs=[row(512), row(512), row(1024), full((1, 512)), full((1, 512)), full((512, 512))],
        out_specs=row(512),
        out_shape=jax.ShapeDtypeStruct((m, 512), BF16),
        compiler_params=_cp("parallel"),
        name="rwkv_post",
    )(y_f, y_b, post, ln_g.reshape(1, 512), ln_b.reshape(1, 512), _block_diag(512, RWKV_HD, 1.0 / RWKV_HD))


def _merge_kernel(h_ref, o0_ref, o1_ref, o2_ref, o3_ref, wg_ref, wb_ref, out_ref):
    h = h_ref[...]
    acc = None
    for i, o_ref in enumerate((o0_ref, o1_ref, o2_ref, o3_ref)):
        term = jax.nn.sigmoid(_dot(h, wg_ref[i])) * _dot(o_ref[...], wb_ref[i])
        acc = term if acc is None else acc + term
    out_ref[...] = _bf(acc)


def _merge(st, h, branch_outs, wg, wb):
    m, d = h.shape
    tm = _mm_tile(st)
    tn = 512
    row = lambda w: pl.BlockSpec((tm, w), lambda j, i: (i, 0))
    return pl.pallas_call(
        _merge_kernel,
        grid=(d // tn, m // tm),
        in_specs=[row(d)] + [row(BRANCH_W)] * 4 + [pl.BlockSpec((4, d, tn), lambda j, i: (0, 0, j)),
                                                    pl.BlockSpec((4, BRANCH_W, tn), lambda j, i: (0, 0, j))],
        out_specs=pl.BlockSpec((tm, tn), lambda j, i: (i, j)),
        out_shape=jax.ShapeDtypeStruct((m, d), BF16),
        compiler_params=_cp("parallel", "parallel"),
        name="merge",
    )(h, *branch_outs, wg, wb)


def _wo_kernel(m_ref, w_ref, x_ref, mod_ref, g_ref, o_ref, h_out):
    x = x_ref[...] + mod_ref[0, 2:3, :] * _dot(m_ref[...], w_ref[...])
    o_ref[...] = x
    h_out[...] = _norm_mod(x, g_ref[...], mod_ref, 4, 3)


def _out_proj(st, merged, w_o, x, mod_l, gain2):
    m, d = x.shape
    tm = _mm_tile(st)
    row = pl.BlockSpec((tm, d), lambda i: (i, 0))
    return pl.pallas_call(
        _wo_kernel,
        grid=(m // tm,),
        in_specs=[row, pl.BlockSpec((d, d), lambda i: (0, 0)), row, pl.BlockSpec((1, 6, d), st.group_map(tm)),
                  pl.BlockSpec((1, d), lambda i: (0, 0))],
        out_specs=[row, row],
        out_shape=[jax.ShapeDtypeStruct((m, d), F32), jax.ShapeDtypeStruct((m, d), BF16)],
        compiler_params=_cp("parallel"),
        name="out_proj",
    )(merged, w_o, x, mod_l, gain2.reshape(1, d))


def _ffn_kernel(h_ref, w1_ref, w3_ref, w2_ref, x_ref, mod_ref, fg_ref, o_ref, acc_ref, *, final):
    f = pl.program_id(1)

    @pl.when(f == 0)
    def _():
        acc_ref[...] = jnp.zeros_like(acc_ref)

    h = h_ref[...]
    u = _silu(_dot(h, w1_ref[...])) * _dot(h, w3_ref[...])
    acc_ref[...] += _dot(_bf(u), w2_ref[...])

    @pl.when(f == pl.num_programs(1) - 1)
    def _():
        x = x_ref[...] + mod_ref[0, 5:6, :] * acc_ref[...]
        o_ref[...] = _rms(x, fg_ref[...]) if final else x


def _ffn(st, h, w1, w3, w2, x, mod_l, final_gain, final):
    m, d = x.shape
    dff = w1.shape[1]
    tm = _mm_tile(st)
    tf = 512
    return pl.pallas_call(
        functools.partial(_ffn_kernel, final=final),
        grid=(m // tm, dff // tf),
        in_specs=[pl.BlockSpec((tm, d), lambda i, f: (i, 0)),
                  pl.BlockSpec((d, tf), lambda i, f: (0, f)),
                  pl.BlockSpec((d, tf), lambda i, f: (0, f)),
                  pl.BlockSpec((tf, d), lambda i, f: (f, 0)),
                  pl.BlockSpec((tm, d), lambda i, f: (i, 0)),
                  pl.BlockSpec((1, 6, d), lambda i, f: st.group_map(tm)(i)),
                  pl.BlockSpec((1, d), lambda i, f: (0, 0))],
        out_specs=pl.BlockSpec((tm, d), lambda i, f: (i, 0)),
        out_shape=jax.ShapeDtypeStruct((m, d), F32),
        scratch_shapes=[pltpu.VMEM((tm, d), F32)],
        compiler_params=_cp("parallel", "arbitrary"),
        name="ffn",
    )(h, w1, w3, w2, x, mod_l, final_gain.reshape(1, d))


def _split_w_in(w_in):
    z = lambda w: jnp.zeros(w_in.shape[:2] + (w,), w_in.dtype)
    att = w_in[..., 0:768]
    ssd = jnp.concatenate([w_in[..., 768:2064], z(SSD_W - 1296)], axis=-1)
    rwkv = w_in[..., 2064:3856]
    gla = jnp.concatenate([w_in[..., 3856:4880], w_in[..., 4896:5408], w_in[..., 4880:4896], z(LANE - 16)], axis=-1)
    return tuple(_bf(w) for w in (att, ssd, rwkv, gla))


def _block(st, x, mod_l, p, rope_tables, ctx, final_gain, final):
    att_args = (st, x, p['norm1'], mod_l, p['w_att'], p['w_ssd'], p['q_norm'], p['k_norm'])

    if ctx is None:
        h, proj_ssd, q, k_att, v_att, k, v = _in_proj_a(*att_args, None)
        o_att = _attention(st, q, k_att, v_att)
        s_ssd = jnp.zeros((st.n, 2, SSD_HEADS, SSD_N, SSD_P), F32)
        s_rwkv = jnp.zeros((st.n, 2, RWKV_HEADS, RWKV_HD, RWKV_HD), F32)
        s_gla_t = jnp.zeros((st.n, 2, GLA_HEADS, GLA_DV, GLA_DK), F32)
    else:
        ctx_k, ctx_v, s_ssd, s_rwkv, s_gla = ctx
        h, proj_ssd, q, k_att, v_att = _in_proj_a(*att_args, rope_tables)
        k = v = None
        o_att = _attention(st, q, _join_cache(st, k_att, ctx_k, 2), _join_cache(st, v_att, ctx_v, 1))
        s_gla_t = jnp.swapaxes(s_gla, -1, -2)

    proj_rwkv, proj_gla = _in_proj_b(st, h, p['w_rwkv'], p['w_gla'])

    xbc, dt, cum, dtt, cumt = _ssd_prep(st, proj_ssd, p['ssd_conv_w'], p['ssd_conv_b'], p['ssd_dt_bias'],
                                        p['ssd_a_log'])
    y_ssd_f, y_ssd_b, new_ssd = _ssd_scan(st, xbc, dt, cum, dtt, cumt, s_ssd)
    o_ssd = _ssd_post(st, y_ssd_f, y_ssd_b, xbc, proj_ssd, p['ssd_d'], p['ssd_norm'])

    rin, lw, rpost = _rwkv_prep(st, proj_rwkv, p['rwkv_mu'], p['rwkv_w2'], p['rwkv_w0'], p['rwkv_a2'],
                                p['rwkv_a0'], p['rwkv_g2'], p['rwkv_kk'], p['rwkv_ka'], p['rwkv_rk'])
    y_rwkv_f, y_rwkv_b, new_rwkv = _rwkv_scan(st, rin, lw, s_rwkv)
    o_rwkv = _rwkv_post(st, y_rwkv_f, y_rwkv_b, rpost, p['rwkv_ln_g'], p['rwkv_ln_b'])

    g2 = jnp.concatenate([p['gla_g2'][0], p['gla_g2'][1]], axis=1)
    g2p = jnp.pad(g2, ((0, LANE - g2.shape[0]), (0, 0)))
    gla_cum = _gla_prep(st, proj_gla, g2p, p['gla_gb'].reshape(1, 2 * GLA_QK))
    y_gla_f, y_gla_b, new_gla_t = _gla_scan(st, proj_gla, gla_cum, s_gla_t)
    o_gla = _gla_post(st, y_gla_f, y_gla_b, proj_gla, p['gla_norm'])

    merged = _merge(st, h, (o_att, o_ssd, o_rwkv, o_gla), p['w_gate'], p['w_branch'])
    x, h2 = _out_proj(st, merged, p['w_o'], x, mod_l, p['norm2'])
    x = _ffn(st, h2, p['ffn_w1'], p['ffn_w3'], p['ffn_w2'], x, mod_l, final_gain, final)
    return x, (k, v, new_ssd, new_rwkv, jnp.swapaxes(new_gla_t, -1, -2))


def kernel(x_prompt, x_sample, cache_attn_k, cache_attn_v, state_ssd, state_rwkv, state_gla, c, c_ctx, w_mod, b_mod, norm1, norm2, w_in, q_norm, k_norm, ssd_conv_w, ssd_conv_b, ssd_dt_bias, ssd_a_log, ssd_d, ssd_norm, rwkv_mu, rwkv_w0, rwkv_w2, rwkv_a0, rwkv_a2, rwkv_g2, rwkv_kk, rwkv_ka, rwkv_rk, rwkv_ln_g, rwkv_ln_b, gla_g2, gla_gb, gla_norm, w_gate, w_branch, w_o, ffn_w1, ffn_w3, ffn_w2, final_norm):
    nb, seq, d = x_prompt.shape
    db, dseq, _ = x_sample.shape
    depth = w_in.shape[0]
    assert d == D_MODEL and seq % CHUNK == 0 and dseq % CHUNK == 0 and 1 + db <= MOD_ROWS
    ctx_st = _Stream(nb, seq, 0, False)
    lat_st = _Stream(db, dseq, 1, True)

    cond = jnp.concatenate([c_ctx[None], c, jnp.zeros((MOD_ROWS - 1 - db, d), F32)], axis=0)
    mod = _modulation(cond, w_mod, b_mod)

    w_att, w_ssd, w_rwkv, w_gla = _split_w_in(w_in)
    w_gate_b, w_branch_b, w_o_b = _bf(w_gate), _bf(w_branch), _bf(w_o)
    w1_b, w3_b, w2_b = _bf(ffn_w1), _bf(ffn_w3), _bf(ffn_w2)

    def params_at(l):
        return dict(norm1=norm1[l], norm2=norm2[l], w_att=w_att[l], w_ssd=w_ssd[l], w_rwkv=w_rwkv[l],
                    w_gla=w_gla[l], q_norm=q_norm[l], k_norm=k_norm[l],
                    ssd_conv_w=ssd_conv_w[l], ssd_conv_b=ssd_conv_b[l], ssd_dt_bias=ssd_dt_bias[l],
                    ssd_a_log=ssd_a_log[l], ssd_d=ssd_d[l], ssd_norm=ssd_norm[l],
                    rwkv_mu=rwkv_mu[l], rwkv_w0=rwkv_w0[l], rwkv_w2=rwkv_w2[l], rwkv_a0=rwkv_a0[l],
                    rwkv_a2=rwkv_a2[l], rwkv_g2=rwkv_g2[l], rwkv_kk=rwkv_kk[l], rwkv_ka=rwkv_ka[l],
                    rwkv_rk=rwkv_rk[l], rwkv_ln_g=rwkv_ln_g[l], rwkv_ln_b=rwkv_ln_b[l],
                    gla_g2=gla_g2[l], gla_gb=gla_gb[l], gla_norm=gla_norm[l],
                    w_gate=w_gate_b[l], w_branch=w_branch_b[l], w_o=w_o_b[l],
                    ffn_w1=w1_b[l], ffn_w3=w3_b[l], ffn_w2=w2_b[l])

    xp = x_prompt.reshape(nb * seq, d)
    new_k, new_v, new_ssd, new_rwkv, new_gla = [], [], [], [], []
    for l in range(depth):
        xp, (k_l, v_l, ssd_l, rwkv_l, gla_l) = _block(ctx_st, xp, mod[l], params_at(l), None, None, final_norm,
                                                      l == depth - 1)
        new_k.append(k_l.reshape(nb, seq, ATT_KV, HEAD_DIM))
        new_v.append(v_l.reshape(nb, seq, ATT_KV, HEAD_DIM))
        new_ssd.append(ssd_l)
        new_rwkv.append(rwkv_l)
        new_gla.append(gla_l)

    rope_tables = _rope_tables(dseq)
    xs = x_sample.reshape(db * dseq, d)
    for l in range(depth):
        ctx = (cache_attn_k[:, l], cache_attn_v[:, l], state_ssd[:, l], state_rwkv[:, l], state_gla[:, l])
        xs, _ = _block(lat_st, xs, mod[l], params_at(l), rope_tables, ctx, final_norm, l == depth - 1)

    y_prompt = xp.reshape(nb, seq, d)
    y_sample = xs.reshape(db, dseq, d)
    return (y_prompt, y_sample, jnp.stack(new_k, axis=1), jnp.stack(new_v, axis=1),
            jnp.stack(new_ssd, axis=1), jnp.stack(new_rwkv, axis=1), jnp.stack(new_gla, axis=1))
```

```python
import functools

import jax
import jax.numpy as jnp
import numpy as np
from jax import lax
from jax.experimental import pallas as pl
from jax.experimental.pallas import tpu as pltpu

F32 = jnp.float32
BF16 = jnp.bfloat16

D_MODEL = 2048
GRID_W = 64
ATT_HEADS = 8
ATT_KV = 2
ATT_GROUP = ATT_HEADS // ATT_KV
HEAD_DIM = 64
ROPE_THETA = 10000.0
SSD_HEADS = 8
SSD_P = 64
SSD_N = 64
SSD_GROUPS = 2
RWKV_HEADS = 8
RWKV_HD = 64
RWKV_DECAY_SCALE = 0.6065306597126334
RWKV_LN_EPS = 64e-5
GLA_HEADS = 4
GLA_DK = 64
GLA_DV = 128
GLA_GATE_NORM = 16.0
CHUNK = 64
BRANCH_W = 512

ATT_W = 768
SSD_W = 1408
RWKV_W = 1792
GLA_W = 1664
LANE = 128
SUBLANE = 8
MOD_ROWS = 16
VMEM_LIMIT = 56 * 1024 * 1024
ROW_TILE = 256
MM_TILE = 512
SCAN_CHUNKS = 4
KEY_TILE = 256
ATT_Q_TILE = 512
KEY_UNROLL = 6
LOG2E = 1.4426950408889634
ATT_MIN_ROW_SUM = 2.0 ** -90


def _cp(*sem):
    return pltpu.CompilerParams(dimension_semantics=sem, vmem_limit_bytes=VMEM_LIMIT)


def _bf(x):
    return x.astype(BF16)


def _dot(a, b):
    return jnp.dot(a, b, preferred_element_type=F32)


def _dot_nt(a, b):
    return lax.dot_general(a, b, (((1,), (1,)), ((), ())), preferred_element_type=F32)


def _dot_tn(a, b):
    return lax.dot_general(a, b, (((0,), (0,)), ((), ())), preferred_element_type=F32)


def _pieces(a, n):
    out = []
    for _ in range(n):
        piece = _bf(a)
        out.append(piece)
        a = a - piece.astype(F32)
    return out


def _dot_data_mask(a, mask, n):
    return functools.reduce(jnp.add, [_dot(piece, mask) for piece in _pieces(a, n)])


def _dot_mask_data(mask, a, n):
    return functools.reduce(jnp.add, [_dot(mask, piece) for piece in _pieces(a, n)])


def _dot_nt_mask_data(mask, a, n):
    return functools.reduce(jnp.add, [_dot_nt(mask, piece) for piece in _pieces(a, n)])


def _dot_split(a, b):
    a_hi, a_lo = _pieces(a, 2)
    b_hi, b_lo = _pieces(b, 2)
    return _dot(a_hi, b_hi) + (_dot(a_hi, b_lo) + _dot(a_lo, b_hi))


def _silu(x):
    return x * jax.nn.sigmoid(x)


def _softplus(x):
    return jnp.maximum(x, 0.0) + jnp.log1p(jnp.exp(-jnp.abs(x)))


def _lanes(pieces):
    return jnp.concatenate(pieces, axis=1)


def _rows(pieces):
    return jnp.concatenate(pieces, axis=0)


def _mod_kernel(c_ref, w_ref, b_ref, o_ref):
    c = c_ref[...]
    o_ref[0] = _dot(_bf(_silu(c)), _bf(w_ref[0])) + b_ref[0]


def _modulation(cond, w_mod, b_mod):
    nl, d, n6 = w_mod.shape
    tn = 1024
    out = pl.pallas_call(
        _mod_kernel,
        grid=(nl, n6 // tn),
        in_specs=[pl.BlockSpec((MOD_ROWS, d), lambda l, j: (0, 0)),
                  pl.BlockSpec((1, d, tn), lambda l, j: (l, 0, j)),
                  pl.BlockSpec((1, 1, tn), lambda l, j: (l, 0, j))],
        out_specs=pl.BlockSpec((1, MOD_ROWS, tn), lambda l, j: (l, 0, j)),
        out_shape=jax.ShapeDtypeStruct((nl, MOD_ROWS, n6), F32),
        compiler_params=_cp("parallel", "parallel"),
        name="modulation",
    )(cond, w_mod, b_mod.reshape(nl, 1, n6))
    return out.reshape(nl, MOD_ROWS, 6, d)


class _Stream:
    def __init__(self, n, t, group0, per_seq):
        self.n, self.t, self.group0, self.per_seq = n, t, group0, per_seq
        self.rows = n * t

    def group_map(self, tile):
        g0, per_seq, t = self.group0, self.per_seq, self.t
        if per_seq:
            return lambda i: (g0 + (i * tile) // t, 0, 0)
        return lambda i: (g0, 0, 0)


def _mm_tile(st):
    limit = st.t if st.per_seq else st.rows
    return min(MM_TILE, limit)


def _row_tile(st):
    return min(ROW_TILE, st.t)


def _rms(x, gain):
    return x * lax.rsqrt(jnp.mean(x * x, axis=-1, keepdims=True) + 1e-6) * gain


def _norm_mod(x, gain, mod_ref, sc_idx, sh_idx):
    return _bf(_rms(x, gain) * (1.0 + mod_ref[0, sc_idx:sc_idx + 1, :]) + mod_ref[0, sh_idx:sh_idx + 1, :])


def _in_proj_a_kernel(*refs, rope):
    x_ref, g_ref, mod_ref, wa_ref, ws_ref, qn_ref, kn_ref, bdq_ref, bdk_ref = refs[:9]
    tables = [r[...] for r in refs[9:12]] if rope else None
    outs = refs[12:] if rope else refs[9:]
    h = _norm_mod(x_ref[...], g_ref[...], mod_ref, 1, 0)
    outs[0][...] = h
    outs[1][...] = _dot(h, ws_ref[...])
    q, k_att, v_att, k, v = _att_operands(_dot(h, wa_ref[...]), qn_ref[...], kn_ref[...], bdq_ref[...],
                                          bdk_ref[...], tables)
    outs[2][...] = q
    for kv in range(ATT_KV):
        outs[3][kv] = k_att[kv]
        outs[4][kv] = v_att[kv]
    if not rope:
        outs[5][...] = k
        outs[6][...] = v


def _in_proj_a(st, x, gain, mod_l, w_att, w_ssd, q_norm, k_norm, rope_tables):
    m, d = x.shape
    tm = _mm_tile(st)
    rope = rope_tables is not None
    row = lambda w: pl.BlockSpec((tm, w), lambda i: (i, 0))
    full = lambda a: pl.BlockSpec(a.shape, lambda i: (0, 0))
    args = [x, gain.reshape(1, d), mod_l, w_att, w_ssd, jnp.tile(q_norm, ATT_HEADS).reshape(1, 512),
            jnp.tile(k_norm, ATT_KV).reshape(1, 128), _block_diag(512, HEAD_DIM, 1.0 / HEAD_DIM),
            _block_diag(128, HEAD_DIM, 1.0 / HEAD_DIM)]
    in_specs = [row(d), full(args[1]), pl.BlockSpec((1, 6, d), st.group_map(tm))] + [full(a) for a in args[3:]]
    if rope:
        tps = st.t // tm
        in_specs += [pl.BlockSpec((tm, LANE), lambda i: (i % tps, 0))] * 3
        args += list(rope_tables)
    kv_spec = pl.BlockSpec((ATT_KV, tm, LANE), lambda i: (0, i, 0))
    kv_shape = jax.ShapeDtypeStruct((ATT_KV, m, LANE), BF16)
    out_specs = [row(d), row(SSD_W), row(ATT_HEADS * LANE), kv_spec, kv_spec]
    out_shape = [jax.ShapeDtypeStruct((m, d), BF16), jax.ShapeDtypeStruct((m, SSD_W), F32),
                 jax.ShapeDtypeStruct((m, ATT_HEADS * LANE), BF16), kv_shape, kv_shape]
    if not rope:
        out_specs += [row(128), row(128)]
        out_shape += [jax.ShapeDtypeStruct((m, 128), F32)] * 2
    return pl.pallas_call(
        functools.partial(_in_proj_a_kernel, rope=rope),
        grid=(m // tm,),
        in_specs=in_specs,
        out_specs=out_specs,
        out_shape=out_shape,
        compiler_params=_cp("parallel"),
        name="in_proj_a",
    )(*args)


def _in_proj_b_kernel(h_ref, wr_ref, wg_ref, pr_out, pg_out):
    h = h_ref[...]
    pr_out[...] = _dot(h, wr_ref[...])
    pg_out[...] = _dot(h, wg_ref[...])


def _in_proj_b(st, h, w_rwkv, w_gla):
    m, d = h.shape
    tm = _mm_tile(st)
    row = lambda w: pl.BlockSpec((tm, w), lambda i: (i, 0))
    full = lambda a: pl.BlockSpec(a.shape, lambda i: (0, 0))
    return pl.pallas_call(
        _in_proj_b_kernel,
        grid=(m // tm,),
        in_specs=[row(d), full(w_rwkv), full(w_gla)],
        out_specs=[row(RWKV_W), row(GLA_W)],
        out_shape=[jax.ShapeDtypeStruct((m, RWKV_W), F32), jax.ShapeDtypeStruct((m, GLA_W), F32)],
        compiler_params=_cp("parallel"),
        name="in_proj_b",
    )(h, w_rwkv, w_gla)


def _att_operands(p, q_gain, k_gain, bdq, bdk, tables):
    aq = p[:, :512]
    ak = p[:, 512:640]
    av = p[:, 640:768]
    q = aq * lax.rsqrt(_dot_data_mask(aq * aq, bdq, 2) + 1e-6) * q_gain
    k = ak * lax.rsqrt(_dot_data_mask(ak * ak, bdk, 2) + 1e-6) * k_gain
    k_plain = k
    if tables is not None:
        c, sa, sb = tables
        k = k * c + pltpu.roll(k, LANE - 16, 1) * sa + pltpu.roll(k, 16, 1) * sb
        c4, sa4, sb4 = _lanes([c] * 4), _lanes([sa] * 4), _lanes([sb] * 4)
        q = q * c4 + pltpu.roll(q, 512 - 16, 1) * sa4 + pltpu.roll(q, 16, 1) * sb4
    tr = p.shape[0]
    spare = lax.broadcasted_iota(jnp.int32, (tr, LANE - HEAD_DIM), 1)
    one_col = (spare == 0).astype(F32)
    two_cols = (spare < 2).astype(F32)
    q = q * (HEAD_DIM ** -0.5 * LOG2E)
    q_len = jnp.sqrt(_dot_data_mask(q * q, bdq, 2) * HEAD_DIM)
    q_att = _bf(_lanes([piece for h in range(ATT_HEADS)
                        for piece in (q[:, h * HEAD_DIM:(h + 1) * HEAD_DIM],
                                      one_col * q_len[:, h * HEAD_DIM:(h + 1) * HEAD_DIM])]))
    heads = [slice(kv * HEAD_DIM, (kv + 1) * HEAD_DIM) for kv in range(ATT_KV)]
    k_att = [_bf(_lanes([k[:, sl], two_cols])) for sl in heads]
    v_att = [_bf(_lanes([av[:, sl], one_col])) for sl in heads]
    return q_att, k_att, v_att, k_plain, av


def _block_diag(width, block, value):
    idx = np.arange(width) // block
    return _bf(jnp.asarray((idx[:, None] == idx[None, :]).astype(np.float32) * value))


def _join_cache(st, own, cache, ones):
    n, p, kv, hd = cache.shape
    c = jnp.transpose(cache, (2, 0, 1, 3))
    pad = jnp.zeros((kv, n, p, LANE - hd), c.dtype).at[..., :ones].set(1.0)
    c = _bf(jnp.concatenate([c, pad], axis=-1))
    joined = jnp.concatenate([own.reshape(kv, n, st.t, LANE), c], axis=2)
    return joined.reshape(kv, n * (st.t + p), LANE)


def _rope_tables(t):
    half = HEAD_DIM // 2
    nf = half // 2
    freqs = ROPE_THETA ** (-jnp.arange(nf, dtype=F32) / nf)
    tt = jnp.arange(t)
    ang_r = (tt // GRID_W).astype(F32)[:, None] * freqs[None, :]
    ang_c = (tt % GRID_W).astype(F32)[:, None] * freqs[None, :]
    zero = jnp.zeros_like(ang_r)
    cos = jnp.concatenate([jnp.cos(ang_r)] * 2 + [jnp.cos(ang_c)] * 2, axis=1)
    sa = jnp.concatenate([-jnp.sin(ang_r), zero, -jnp.sin(ang_c), zero], axis=1)
    sb = jnp.concatenate([zero, jnp.sin(ang_r), zero, jnp.sin(ang_c)], axis=1)
    return tuple(jnp.tile(a, (1, LANE // HEAD_DIM)) for a in (cos, sa, sb))


def _key_bound_kernel(k_ref, o_ref):
    ones = jnp.ones((LANE, LANE), BF16)
    for kv in range(ATT_KV):
        k = k_ref[kv].astype(F32)
        k = jnp.where(lax.broadcasted_iota(jnp.int32, k.shape, 1) < HEAD_DIM, k, 0.0)
        best = jnp.max(_dot_data_mask(k * k, ones, 2), axis=0, keepdims=True)
        o_ref[kv] = jnp.broadcast_to(best, (SUBLANE, LANE))


def _key_bound(n, keys_per_seq, k):
    return pl.pallas_call(
        _key_bound_kernel,
        grid=(n,),
        in_specs=[pl.BlockSpec((ATT_KV, keys_per_seq, LANE), lambda b: (0, b, 0))],
        out_specs=pl.BlockSpec((None, ATT_KV, SUBLANE, LANE), lambda b: (b, 0, 0, 0)),
        out_shape=jax.ShapeDtypeStruct((n, ATT_KV, SUBLANE, LANE), F32),
        compiler_params=_cp("parallel"),
        name="key_bound",
    )(k)


def _attn_kernel(q_ref, k_ref, v_ref, kb_ref, o_ref, m_scr, acc_scr, *, tq, n_tiles):
    rows = ATT_GROUP * tq
    lane = lax.broadcasted_iota(jnp.int32, (rows, LANE), 1)
    outs = []
    for kv in range(ATT_KV):
        qs = _rows([q_ref[:, (kv * ATT_GROUP + g) * LANE:(kv * ATT_GROUP + g + 1) * LANE]
                    for g in range(ATT_GROUP)]).astype(F32)

        def key_tile(ref, j):
            return ref[kv, pl.ds(pl.multiple_of(j * KEY_TILE, KEY_TILE), KEY_TILE), :]

        def weighted_sum(q_shift, unroll):
            acc_scr[...] = jnp.zeros((rows, LANE), F32)

            def sum_body(j, carry):
                p = jnp.exp2(_dot_nt(q_shift, key_tile(k_ref, j)))
                acc_scr[...] += _dot(_bf(p), key_tile(v_ref, j))
                return carry

            lax.fori_loop(0, n_tiles, sum_body, 0, unroll=unroll)
            return acc_scr[...]

        k_len = jnp.sqrt(kb_ref[kv][0:1, :])
        unroll = max(u for u in range(1, KEY_UNROLL + 1) if n_tiles % u == 0)
        acc = weighted_sum(_bf(qs * jnp.where(lane == HEAD_DIM, -k_len, 1.0)), unroll)

        def exact_shift():
            m_scr[...] = jnp.full((rows, LANE), -jnp.inf, F32)
            q0 = _bf(jnp.where(lane >= HEAD_DIM, 0.0, qs))

            def max_body(j, carry):
                sc = _dot_nt(q0, key_tile(k_ref, j))
                m_scr[...] = jnp.maximum(m_scr[...], jnp.maximum(sc[:, :LANE], sc[:, LANE:]))
                return carry

            lax.fori_loop(0, n_tiles, max_body, 0)
            row_max = jnp.max(m_scr[...], axis=-1, keepdims=True)
            hi = _bf(row_max).astype(F32)
            shifted = jnp.where(lane == HEAD_DIM, -hi, jnp.where(lane == HEAD_DIM + 1, hi - row_max, qs))
            return weighted_sum(_bf(shifted), 1)

        row_sum = acc[:, HEAD_DIM:HEAD_DIM + 1]
        row_sum_ok = jnp.logical_and(jnp.min(row_sum) >= ATT_MIN_ROW_SUM, jnp.max(row_sum) <= 1.0 / ATT_MIN_ROW_SUM)
        acc = lax.cond(row_sum_ok, lambda: acc, exact_shift)
        o = acc[:, :HEAD_DIM] / acc[:, HEAD_DIM:HEAD_DIM + 1]
        outs += [o[g * tq:(g + 1) * tq] for g in range(ATT_GROUP)]
    o_ref[...] = _bf(_lanes(outs))


def _attention(st, q, k, v):
    tq = min(ATT_Q_TILE, st.t)
    nq = st.t // tq
    keys = k.shape[1] // st.n
    assert keys % KEY_TILE == 0
    kv_spec = pl.BlockSpec((ATT_KV, keys, LANE), lambda b, i: (0, b, 0))
    rows = ATT_GROUP * tq
    return pl.pallas_call(
        functools.partial(_attn_kernel, tq=tq, n_tiles=keys // KEY_TILE),
        grid=(st.n, nq),
        in_specs=[pl.BlockSpec((tq, ATT_HEADS * LANE), lambda b, i: (b * nq + i, 0)), kv_spec, kv_spec,
                  pl.BlockSpec((None, ATT_KV, SUBLANE, LANE), lambda b, i: (b, 0, 0, 0))],
        out_specs=pl.BlockSpec((tq, 512), lambda b, i: (b * nq + i, 0)),
        out_shape=jax.ShapeDtypeStruct((st.rows, 512), BF16),
        scratch_shapes=[pltpu.VMEM((rows, LANE), F32), pltpu.VMEM((rows, LANE), F32)],
        compiler_params=_cp("parallel", "parallel"),
        name="attention",
    )(q, k, v, _key_bound(st.n, keys, k))


def _halo_specs(st, width):
    tr = _row_tile(st)
    per8 = tr // SUBLANE
    last8 = st.rows // SUBLANE - 1
    return [pl.BlockSpec((tr, width), lambda i: (i, 0)),
            pl.BlockSpec((SUBLANE, width), lambda i: (jnp.maximum(i * per8 - 1, 0), 0)),
            pl.BlockSpec((SUBLANE, width), lambda i: (jnp.minimum((i + 1) * per8, last8), 0))]


def _neighbours(cur, prev8, next8, tiles_per_seq):
    tr = cur.shape[0]
    j = pl.program_id(0) % tiles_per_seq
    pr = jnp.where(j != 0, prev8[SUBLANE - 1:SUBLANE], 0.0)
    nx = jnp.where(j != tiles_per_seq - 1, next8[0:1], 0.0)
    row = lax.broadcasted_iota(jnp.int32, cur.shape, 0)
    x_prev = jnp.where(row == 0, pr, pltpu.roll(cur, 1, 0))
    x_next = jnp.where(row == tr - 1, nx, pltpu.roll(cur, tr - 1, 0))
    return x_prev, x_next


def _dir_masks(z):
    ri = lax.broadcasted_iota(jnp.int32, (CHUNK, CHUNK), 0)
    ci = lax.broadcasted_iota(jnp.int32, (CHUNK, CHUNK), 1)
    return ((ci <= ri), (ci < ri)) if z == 0 else ((ci >= ri), (ci > ri))


def _tri_blocks(tr):
    i = np.arange(tr)
    same = (i[:, None] // CHUNK) == (i[None, :] // CHUNK)
    lower = same & (i[None, :] <= i[:, None])
    upper = same & (i[None, :] >= i[:, None])
    return _bf(jnp.asarray(lower.astype(np.float32))), _bf(jnp.asarray(upper.astype(np.float32)))


def _scan_blocks(st):
    ncb = min(SCAN_CHUNKS, st.t // CHUNK)
    rb = ncb * CHUNK
    nblk = st.t // rb
    fwd = lambda b, j: (b * nblk + j, 0)
    bwd = lambda b, j: (b * nblk + nblk - 1 - j, 0)
    return ncb, rb, nblk, fwd, bwd


def _chunk_rows(step, z, ncb):
    cc = step if z == 0 else ncb - 1 - step
    return slice(cc * CHUNK, (cc + 1) * CHUNK)


def _ssd_prep_kernel(cur_ref, prev_ref, next_ref, cw_ref, cb_ref, dtb_ref, alog_ref, lo_ref, up_ref,
                     xbc_out, dt_out, cum_out, dtt_out, cumt_out, *, tiles_per_seq):
    cur = cur_ref[...]
    xc = cur[:, 512:1280]
    x_prev, x_next = _neighbours(xc, prev_ref[:, 512:1280], next_ref[:, 512:1280], tiles_per_seq)
    cw = cw_ref[...]
    conv = cb_ref[...] + x_prev * cw[0:1] + xc * cw[1:2] + x_next * cw[2:3]
    xbc_out[...] = _bf(_silu(conv))
    dt = _softplus(cur[:, 1280:1408] + dtb_ref[...])
    dt_out[...] = dt
    ld = dt * -jnp.exp(alog_ref[...])
    lane = lax.broadcasted_iota(jnp.int32, ld.shape, 1)
    cum = jnp.where(lane < SSD_HEADS, _dot_mask_data(lo_ref[...], ld, 3), _dot_mask_data(up_ref[...], ld, 3))
    cum_out[...] = cum
    pick = (lax.broadcasted_iota(jnp.int32, (2 * SSD_HEADS, LANE), 0)
            == lax.broadcasted_iota(jnp.int32, (2 * SSD_HEADS, LANE), 1)).astype(BF16)
    dtt_out[...] = _dot_nt_mask_data(pick, dt, 3)
    cumt_out[...] = _dot_nt_mask_data(pick, cum, 3)


def _pad_lanes(v, width=LANE):
    v = v.reshape(1, -1)
    return jnp.pad(v, ((0, 0), (0, width - v.shape[1])))


def _ssd_prep(st, proj, conv_w, conv_b, dt_bias, a_log):
    m = proj.shape[0]
    tr = _row_tile(st)
    full = lambda shape: pl.BlockSpec(shape, lambda i: (0, 0))
    row = lambda w: pl.BlockSpec((tr, w), lambda i: (i, 0))
    col = pl.BlockSpec((2 * SSD_HEADS, tr), lambda i: (0, i))
    lower, upper = _tri_blocks(tr)
    return pl.pallas_call(
        functools.partial(_ssd_prep_kernel, tiles_per_seq=st.t // tr),
        grid=(m // tr,),
        in_specs=_halo_specs(st, SSD_W) + [full((3, 768)), full((1, 768)), full((1, LANE)), full((1, LANE)),
                                           full((tr, tr)), full((tr, tr))],
        out_specs=[row(768), row(LANE), row(LANE), col, col],
        out_shape=[jax.ShapeDtypeStruct((m, 768), BF16), jax.ShapeDtypeStruct((m, LANE), F32),
                   jax.ShapeDtypeStruct((m, LANE), F32), jax.ShapeDtypeStruct((2 * SSD_HEADS, m), F32),
                   jax.ShapeDtypeStruct((2 * SSD_HEADS, m), F32)],
        compiler_params=_cp("parallel"),
        name="ssd_prep",
    )(proj, proj, proj, conv_w.T, conv_b.reshape(1, 768), _pad_lanes(dt_bias), _pad_lanes(a_log), lower, upper)


def _ssd_scan_kernel(xf_ref, xb_ref, dtf_ref, dtb_ref, cf_ref, cb_ref, dttf_ref, dttb_ref, ctf_ref, ctb_ref,
                     sel_ref, s0_ref, yf_ref, yb_ref, s_ref, *, ncb):
    @pl.when(pl.program_id(1) == 0)
    def _():
        s_ref[...] = s0_ref[...]

    hpg = SSD_HEADS // SSD_GROUPS
    state = {(z, g): _lanes([s_ref[z, g * hpg + hh] for hh in range(hpg)])
             for z in range(2) for g in range(SSD_GROUPS)}
    refs = ((xf_ref, dtf_ref, cf_ref, dttf_ref, ctf_ref, yf_ref), (xb_ref, dtb_ref, cb_ref, dttb_ref, ctb_ref, yb_ref))
    groups = [(z, g) for z in range(2) for g in range(SSD_GROUPS)]
    heads = [(z, h) for z in range(2) for h in range(SSD_HEADS)]
    for step in range(ncb):
        xbc, dt, cum, dtt, cumt, incl, rows = {}, {}, {}, {}, {}, {}, {}
        for z in range(2):
            x_ref, dt_ref, c_ref, dtt_ref, ct_ref, _ = refs[z]
            rows[z] = _chunk_rows(step, z, ncb)
            xbc[z] = x_ref[rows[z], :]
            dt[z] = _dot_data_mask(dt_ref[rows[z], :], sel_ref[z], 3)
            cum[z] = _dot_data_mask(c_ref[rows[z], :], sel_ref[z], 3)
            dtt[z], cumt[z] = dtt_ref[:, rows[z]], ct_ref[:, rows[z]]
            incl[z] = _dir_masks(z)[0]
        bmat = {(z, g): xbc[z][:, 512 + g * SSD_N:512 + (g + 1) * SSD_N] for z, g in groups}
        cmat = {(z, g): _bf(xbc[z][:, 640 + g * SSD_N:640 + (g + 1) * SSD_N]) for z, g in groups}
        cb = {k: _dot_nt(cmat[k], _bf(bmat[k])) for k in groups}
        cs = {k: _dot(cmat[k], _bf(state[k])) for k in groups}
        gw = hpg * SSD_P
        glast = {z: cum[z][CHUNK - 1:CHUNK] if z == 0 else cum[z][0:1] for z in range(2)}
        xs = {z: xbc[z][:, :SSD_HEADS * SSD_P] for z in range(2)}
        xw = {z: xs[z] * (dt[z] * jnp.exp(glast[z] - cum[z])) for z in range(2)}
        inc = {(z, g): _dot_tn(_bf(bmat[z, g]), _bf(xw[z][:, g * gw:(g + 1) * gw])) for z, g in groups}
        att = {}
        for z, h in heads:
            ln = z * SSD_HEADS + h
            gcol = cum[z][:, h * SSD_P:(h + 1) * SSD_P]
            dec = jnp.exp(jnp.where(incl[z], gcol - cumt[z][ln:ln + 1], -jnp.inf))
            att[z, h] = _bf(cb[z, h // hpg] * dec * dtt[z][ln:ln + 1])
        intra = {(z, h): _dot(att[z, h], _bf(xs[z][:, h * SSD_P:(h + 1) * SSD_P])) for z, h in heads}
        for z in range(2):
            e_in = jnp.exp(cum[z])
            e_last = jnp.exp(glast[z])
            outs = []
            for g in range(SSD_GROUPS):
                hs = range(g * hpg, (g + 1) * hpg)
                outs.append(_lanes([intra[z, h] for h in hs]) + cs[z, g] * e_in[:, g * gw:(g + 1) * gw])
                state[z, g] = state[z, g] * e_last[:, g * gw:(g + 1) * gw] + inc[z, g]
            refs[z][5][rows[z], :] = _bf(_lanes(outs))
    for z in range(2):
        for h in range(SSD_HEADS):
            s_ref[z, h] = state[z, h // hpg][:, (h % hpg) * SSD_P:(h % hpg + 1) * SSD_P]


def _ssd_scan(st, xbc, dt, cum, dtt, cumt, s0):
    ncb, rb, nblk, fwd, bwd = _scan_blocks(st)
    fwd_t = lambda b, j: (0, fwd(b, j)[0])
    bwd_t = lambda b, j: (0, bwd(b, j)[0])
    state_spec = pl.BlockSpec((None, 2, SSD_HEADS, SSD_N, SSD_P), lambda b, j: (b, 0, 0, 0, 0))
    rows = lambda w, m: pl.BlockSpec((rb, w), m)
    cols = lambda m: pl.BlockSpec((2 * SSD_HEADS, rb), m)
    lane = np.arange(LANE)[None, :, None]
    col = np.arange(SSD_HEADS * SSD_P)[None, None, :]
    sel = _bf(jnp.asarray((lane == np.arange(2)[:, None, None] * SSD_HEADS + col // SSD_P).astype(np.float32)))
    return pl.pallas_call(
        functools.partial(_ssd_scan_kernel, ncb=ncb),
        grid=(st.n, nblk),
        in_specs=[rows(768, fwd), rows(768, bwd), rows(LANE, fwd), rows(LANE, bwd), rows(LANE, fwd), rows(LANE, bwd),
                  cols(fwd_t), cols(bwd_t), cols(fwd_t), cols(bwd_t),
                  pl.BlockSpec(sel.shape, lambda b, j: (0, 0, 0)), state_spec],
        out_specs=[rows(512, fwd), rows(512, bwd), state_spec],
        out_shape=[jax.ShapeDtypeStruct((st.rows, 512), BF16), jax.ShapeDtypeStruct((st.rows, 512), BF16),
                   jax.ShapeDtypeStruct((st.n, 2, SSD_HEADS, SSD_N, SSD_P), F32)],
        compiler_params=_cp("parallel", "arbitrary"),
        name="ssd_scan",
    )(xbc, xbc, dt, dt, cum, cum, dtt, dtt, cumt, cumt, sel, s0)


def _ssd_post_kernel(yf_ref, yb_ref, xbc_ref, p_ref, d_ref, g_ref, o_ref):
    y = yf_ref[...].astype(F32) + yb_ref[...].astype(F32) + d_ref[...] * xbc_ref[:, :512]
    y = y * _silu(p_ref[:, :512])
    o_ref[...] = _bf(y * lax.rsqrt(jnp.mean(y * y, axis=-1, keepdims=True) + 1e-6) * g_ref[...])


def _ssd_post(st, y_f, y_b, xbc, proj, ssd_d, ssd_norm):
    m = proj.shape[0]
    tr = _row_tile(st)
    full = lambda shape: pl.BlockSpec(shape, lambda i: (0, 0))
    row = lambda w: pl.BlockSpec((tr, w), lambda i: (i, 0))
    return pl.pallas_call(
        _ssd_post_kernel,
        grid=(m // tr,),
        in_specs=[row(512), row(512), row(512), row(512), full((1, 512)), full((1, 512))],
        out_specs=row(512),
        out_shape=jax.ShapeDtypeStruct((m, 512), BF16),
        compiler_params=_cp("parallel"),
        name="ssd_post",
    )(y_f, y_b, xbc, proj, jnp.repeat(ssd_d, SSD_P).reshape(1, 512), ssd_norm.reshape(1, 512))


GLA_QK = GLA_HEADS * GLA_DK
GLA_SAFE_RANGE = 60.0


def _gla_prep_kernel(p_ref, g2_ref, gb_ref, lo_ref, up_ref, cum_out):
    logit = _dot_split(p_ref[...], g2_ref[...]) + gb_ref[...]
    log_a = -_softplus(-logit) * (1.0 / GLA_GATE_NORM)
    cum_out[:, :GLA_QK] = _dot_mask_data(lo_ref[...], log_a[:, :GLA_QK], 3)
    cum_out[:, GLA_QK:] = _dot_mask_data(up_ref[...], log_a[:, GLA_QK:], 3)


def _gla_prep(st, proj, g2p, gb):
    m = proj.shape[0]
    tr = _row_tile(st)
    full = lambda shape: pl.BlockSpec(shape, lambda i: (0, 0))
    lower, upper = _tri_blocks(tr)
    return pl.pallas_call(
        _gla_prep_kernel,
        grid=(m // tr,),
        in_specs=[pl.BlockSpec((tr, LANE), lambda i: (i, (GLA_W - LANE) // LANE)), full((LANE, 2 * GLA_QK)),
                  full((1, 2 * GLA_QK)),
                  full((tr, tr)), full((tr, tr))],
        out_specs=pl.BlockSpec((tr, 2 * GLA_QK), lambda i: (i, 0)),
        out_shape=jax.ShapeDtypeStruct((m, 2 * GLA_QK), F32),
        compiler_params=_cp("parallel"),
        name="gla_prep",
    )(proj, g2p, gb, lower, upper)


def _gla_intra_exact(p_ref, c_ref, rows, z, seg_ref):
    r0 = rows.start
    q = p_ref[rows, 0:GLA_QK] * (GLA_DK ** -0.5)
    cum = c_ref[rows, :]
    row = lax.broadcasted_iota(jnp.int32, (CHUNK, 1), 0)
    seg = seg_ref[...]

    def body(j, acc):
        kj = p_ref[pl.ds(r0 + j, 1), GLA_QK:2 * GLA_QK]
        vj = p_ref[pl.ds(r0 + j, 1), 2 * GLA_QK:2 * GLA_QK + GLA_HEADS * GLA_DV]
        seen = (row >= j) if z == 0 else (row <= j)
        w = jnp.where(seen, q * kj * jnp.exp(jnp.minimum(cum - c_ref[pl.ds(r0 + j, 1), :], 0.0)), 0.0)
        score = _dot_data_mask(w, seg, 3)
        return tuple(a + score[:, h:h + 1] * vj[:, h * GLA_DV:(h + 1) * GLA_DV] for h, a in enumerate(acc))

    zero = jnp.zeros((CHUNK, GLA_DV), F32)
    return list(lax.fori_loop(0, CHUNK, body, (zero,) * GLA_HEADS))


def _gla_scan_kernel(pf_ref, pb_ref, cf_ref, cb_ref, seg_ref, s0_ref, yf_ref, yb_ref, s_ref, *, ncb):
    @pl.when(pl.program_id(1) == 0)
    def _():
        s_ref[...] = s0_ref[...]

    refs = ((pf_ref, cf_ref, yf_ref), (pb_ref, cb_ref, yb_ref))
    heads = [(z, h) for z in range(2) for h in range(GLA_HEADS)]
    hsl = [slice(h * GLA_DK, (h + 1) * GLA_DK) for h in range(GLA_HEADS)]
    state = {(z, h): s_ref[z, h] for z, h in heads}
    chunks = [(step, z) for step in range(ncb) for z in range(2)]
    rows = {(step, z): _chunk_rows(step, z, ncb) for step, z in chunks}
    cum = {k: refs[k[1]][1][rows[k], :] for k in chunks}
    mid = {k: cum[k][CHUNK // 2:CHUNK // 2 + 1] for k in chunks}
    span = functools.reduce(jnp.maximum, [jnp.max(jnp.abs(cum[k] - mid[k])) for k in chunks])

    def v_of(k, h):
        return _bf(refs[k[1]][0][rows[k], 2 * GLA_QK + h * GLA_DV:2 * GLA_QK + (h + 1) * GLA_DV])

    def intra_factored():
        att = {}
        for k in chunks:
            p_ref = refs[k[1]][0]
            q_mid = _bf(p_ref[rows[k], 0:GLA_QK] * (GLA_DK ** -0.5) * jnp.exp(cum[k] - mid[k]))
            k_mid = _bf(p_ref[rows[k], GLA_QK:2 * GLA_QK] * jnp.exp(mid[k] - cum[k]))
            incl = _dir_masks(k[1])[0]
            for h in range(GLA_HEADS):
                att[k, h] = _bf(jnp.where(incl, _dot_nt(q_mid[:, hsl[h]], k_mid[:, hsl[h]]), 0.0))
        return [_dot(att[k, h], v_of(k, h)) for k in chunks for h in range(GLA_HEADS)]

    def intra_exact():
        out = []
        for k in chunks:
            out += _gla_intra_exact(refs[k[1]][0], refs[k[1]][1], rows[k], k[1], seg_ref)
        return out

    intra = lax.cond(span <= GLA_SAFE_RANGE, intra_factored, intra_exact)
    intra = {(k, h): intra[i * GLA_HEADS + h] for i, k in enumerate(chunks) for h in range(GLA_HEADS)}

    for step in range(ncb):
        q_in, k_out, e_last = {}, {}, {}
        for z in range(2):
            k = (step, z)
            p_ref = refs[z][0]
            glast = cum[k][CHUNK - 1:CHUNK] if z == 0 else cum[k][0:1]
            q_in[z] = _bf(p_ref[rows[k], 0:GLA_QK] * (GLA_DK ** -0.5) * jnp.exp(cum[k]))
            k_out[z] = _bf(p_ref[rows[k], GLA_QK:2 * GLA_QK] * jnp.exp(glast - cum[k]))
            e_last[z] = jnp.exp(glast)
        inter = {(z, h): _dot_nt(q_in[z][:, hsl[h]], _bf(state[z, h])) for z, h in heads}
        inc = {(z, h): _dot_tn(v_of((step, z), h), k_out[z][:, hsl[h]]) for z, h in heads}
        for z in range(2):
            refs[z][2][rows[step, z], :] = _bf(_lanes([intra[(step, z), h] + inter[z, h] for h in range(GLA_HEADS)]))
            for h in range(GLA_HEADS):
                state[z, h] = state[z, h] * e_last[z][:, hsl[h]] + inc[z, h]
    for z, h in heads:
        s_ref[z, h] = state[z, h]


def _gla_scan(st, proj, cum, s0_t):
    ncb, rb, nblk, fwd, bwd = _scan_blocks(st)
    seg = _block_diag(GLA_QK, GLA_DK, 1.0)[:, ::GLA_DK]
    seg = jnp.pad(seg, ((0, 0), (0, LANE - GLA_HEADS)))
    state_spec = pl.BlockSpec((None, 2, GLA_HEADS, GLA_DV, GLA_DK), lambda b, j: (b, 0, 0, 0, 0))
    qkv_w = 2 * GLA_QK + GLA_HEADS * GLA_DV
    return pl.pallas_call(
        functools.partial(_gla_scan_kernel, ncb=ncb),
        grid=(st.n, nblk),
        in_specs=[pl.BlockSpec((rb, qkv_w), fwd), pl.BlockSpec((rb, qkv_w), bwd),
                  pl.BlockSpec((rb, GLA_QK), fwd), pl.BlockSpec((rb, GLA_QK), lambda b, j: (bwd(b, j)[0], 1)),
                  pl.BlockSpec((GLA_QK, LANE), lambda b, j: (0, 0)), state_spec],
        out_specs=[pl.BlockSpec((rb, 512), fwd), pl.BlockSpec((rb, 512), bwd), state_spec],
        out_shape=[jax.ShapeDtypeStruct((st.rows, 512), BF16), jax.ShapeDtypeStruct((st.rows, 512), BF16),
                   jax.ShapeDtypeStruct((st.n, 2, GLA_HEADS, GLA_DV, GLA_DK), F32)],
        compiler_params=_cp("parallel", "arbitrary"),
        name="gla_scan",
    )(proj, proj, cum, cum, seg, s0_t)


def _gla_post_kernel(yf_ref, yb_ref, p_ref, g_ref, o_ref):
    o = yf_ref[...].astype(F32) + yb_ref[...].astype(F32)
    gate = _silu(p_ref[...])
    outs = []
    for h in range(GLA_HEADS):
        oh = o[:, h * GLA_DV:(h + 1) * GLA_DV]
        outs.append(oh * lax.rsqrt(jnp.mean(oh * oh, axis=-1, keepdims=True) + 1e-6) * g_ref[...])
    o_ref[...] = _bf(_lanes(outs) * gate)


def _gla_post(st, y_f, y_b, proj, gla_norm):
    m = proj.shape[0]
    tr = _row_tile(st)
    return pl.pallas_call(
        _gla_post_kernel,
        grid=(m // tr,),
        in_specs=[pl.BlockSpec((tr, 512), lambda i: (i, 0)), pl.BlockSpec((tr, 512), lambda i: (i, 0)),
                  pl.BlockSpec((tr, 512), lambda i: (i, 2)), pl.BlockSpec((1, GLA_DV), lambda i: (0, 0))],
        out_specs=pl.BlockSpec((tr, 512), lambda i: (i, 0)),
        out_shape=jax.ShapeDtypeStruct((m, 512), BF16),
        compiler_params=_cp("parallel"),
        name="gla_post",
    )(y_f, y_b, proj, gla_norm.reshape(1, GLA_DV))


def _rwkv_prep_kernel(cur_ref, prev_ref, next_ref, mu_ref, w2_ref, w0_ref, a2_ref, a0_ref, g2_ref,
                      kkw_ref, ka_ref, rk_ref, bd_ref, in_out, lw_out, post_out, *, tiles_per_seq):
    cur = cur_ref[...]
    x_prev, x_next = _neighbours(cur, prev_ref[...], next_ref[...], tiles_per_seq)
    blk = cur + (0.5 * (x_prev + x_next) - cur) * mu_ref[...]
    r, k, v = blk[:, 0:512], blk[:, 512:1024], blk[:, 1024:1536]
    w_logit = w0_ref[...] + _dot_split(jnp.tanh(blk[:, 1536:1600]), w2_ref[...])
    lw_out[...] = -RWKV_DECAY_SCALE * jax.nn.sigmoid(w_logit)
    a = jax.nn.sigmoid(a0_ref[...] + _dot_split(blk[:, 1600:1664], a2_ref[...]))
    g = _dot_split(jax.nn.sigmoid(blk[:, 1664:1792]), g2_ref[...])
    bd = bd_ref[...]
    kk = k * kkw_ref[...]
    kk = kk * lax.rsqrt(_dot_data_mask(kk * kk, bd, 2) + 1e-12)
    k2 = k * (1.0 + (a - 1.0) * ka_ref[...])
    in_out[:, 0:512] = _bf(r)
    in_out[:, 512:1024] = _bf(k2)
    in_out[:, 1024:1536] = _bf(v)
    in_out[:, 1536:2048] = _bf(-kk)
    in_out[:, 2048:2560] = _bf(kk * a)
    post_out[:, 0:512] = _bf(g)
    post_out[:, 512:1024] = _bf(_dot_data_mask(r * k2 * rk_ref[...], bd, 2) * v)


def _rwkv_prep(st, proj, mu, w2, w0, a2, a0, g2, kkw, ka, rk):
    m = proj.shape[0]
    tr = _row_tile(st)
    full = lambda shape: pl.BlockSpec(shape, lambda i: (0, 0))
    row = lambda w: pl.BlockSpec((tr, w), lambda i: (i, 0))
    vec = lambda a: a.reshape(1, -1)
    return pl.pallas_call(
        functools.partial(_rwkv_prep_kernel, tiles_per_seq=st.t // tr),
        grid=(m // tr,),
        in_specs=_halo_specs(st, RWKV_W) + [full((1, RWKV_W)), full((64, 1024)), full((1, 1024)),
                                            full((64, 512)), full((1, 512)), full((128, 512)),
                                            full((1, 512)), full((1, 512)), full((1, 512)), full((512, 512))],
        out_specs=[row(2560), row(1024), row(1024)],
        out_shape=[jax.ShapeDtypeStruct((m, 2560), BF16), jax.ShapeDtypeStruct((m, 1024), F32),
                   jax.ShapeDtypeStruct((m, 1024), BF16)],
        compiler_params=_cp("parallel"),
        name="rwkv_prep",
    )(proj, proj, proj, vec(mu), jnp.concatenate([w2[0], w2[1]], axis=1), vec(w0), a2, vec(a0), g2,
      vec(kkw), vec(ka), vec(rk), _block_diag(512, RWKV_HD, 1.0))


def _rwkv_chunk(x_ref, lw_ref, rows, z):
    incl, strict = _dir_masks(z)
    lw = lw_ref[rows, :]
    cum = _dot_mask_data(incl.astype(BF16), lw, 3)
    cum_x = cum - lw
    mid = cum[CHUNK // 2:CHUNK // 2 + 1]
    glast = cum[CHUNK - 1:CHUNK] if z == 0 else cum[0:1]
    x = x_ref[rows, :]
    r, k, v, al, be = x[:, 0:512], x[:, 512:1024], x[:, 1024:1536], x[:, 1536:2048], x[:, 2048:2560]
    e_mid, e_nmid = jnp.exp(cum - mid), jnp.exp(mid - cum)
    e_out = jnp.exp(glast - cum)
    return dict(incl=incl, strict=strict, v=v,
                r_mid=_bf(r * e_mid), a_mid=_bf(al * jnp.exp(cum_x - mid)),
                b_mid=_bf(be * e_nmid), k_mid=_bf(k * e_nmid),
                r_in=_bf(r * jnp.exp(cum)), a_in=al * jnp.exp(cum_x),
                b_out=_bf(be * e_out), k_out=_bf(k * e_out), e_last=jnp.exp(glast))


_HEAD_SLICES = [slice(h * RWKV_HD, (h + 1) * RWKV_HD) for h in range(RWKV_HEADS)]


def _rwkv_state_free(chunks, hooks):
    hooks = list(hooks)

    def run_hook():
        if hooks:
            hooks.pop(0)()

    chains = [(c, sl) for c in chunks for sl in _HEAD_SLICES]
    run_hook()
    pair = [_dot_nt(_rows([c['a_mid'][:, sl], c['r_mid'][:, sl]]), _rows([c['b_mid'][:, sl], c['k_mid'][:, sl]]))
            for c, sl in chains]
    half = CHUNK
    a_ab = [_bf(jnp.where(c['strict'], p[:half, :half], 0.0)) for (c, _), p in zip(chains, pair)]
    a_ak = [_bf(jnp.where(c['strict'], p[:half, half:], 0.0)) for (c, _), p in zip(chains, pair)]
    a_rb = [_bf(jnp.where(c['incl'], p[half:, :half], 0.0)) for (c, _), p in zip(chains, pair)]
    a_rk = [_bf(jnp.where(c['incl'], p[half:, half:], 0.0)) for (c, _), p in zip(chains, pair)]
    vb = [_bf(c['v'][:, sl]) for c, sl in chains]
    both = [_dot(_rows([ak, rk]), v) for ak, rk, v in zip(a_ak, a_rk, vb)]
    av = [x[:half] for x in both]
    o0 = [x[half:] for x in both]
    sol = [_lanes([c['a_in'][:, sl], x]) for (c, sl), x in zip(chains, av)]
    powr = a_ab
    width = 2 * RWKV_HD
    for it in range(6):
        if it < 5:
            both = [_dot(p, _lanes([_bf(s), p])) for p, s in zip(powr, sol)]
            sol = [s + x[:, :width] for s, x in zip(sol, both)]
            powr = [_bf(x[:, width:]) for x in both]
        else:
            sol = [s + _dot(p, _bf(s)) for p, s in zip(powr, sol)]
        if it in (1, 3):
            run_hook()
    while hooks:
        run_hook()
    return [dict(w=_bf(s[:, :RWKV_HD]), u0=s[:, RWKV_HD:], o0=o, a_rb=a, v=c['v'][:, sl], r_in=c['r_in'][:, sl],
                 b_out=c['b_out'][:, sl], k_out=c['k_out'][:, sl], e_last=c['e_last'][:, sl])
            for (c, sl), s, o, a in zip(chains, sol, o0, a_rb)]


def _rwkv_state_stages(res, state, write_out):
    box = {}

    def read_state():
        box['su'] = [_dot_nt(_rows([c['w'], c['r_in']]), _bf(s)) for c, s in zip(res, state)]

    def update_state():
        box['u'] = [c['u0'] + su[:CHUNK] for c, su in zip(res, box['su'])]
        state[:] = [s * c['e_last'] + _dot_tn(_rows([_bf(u), _bf(c['v'])]), _rows([c['b_out'], c['k_out']]))
                    for c, s, u in zip(res, state, box['u'])]

    def emit():
        write_out([su[CHUNK:] + _dot(c['a_rb'], _bf(u)) + c['o0'] for c, su, u in zip(res, box['su'], box['u'])])

    return [read_state, update_state, emit]


def _rwkv_scan_kernel(xf_ref, xb_ref, lwf_ref, lwb_ref, s0_ref, yf_ref, yb_ref, s_ref, *, ncb):
    @pl.when(pl.program_id(1) == 0)
    def _():
        s_ref[...] = s0_ref[...]

    state = [s_ref[z, h] for z in range(2) for h in range(RWKV_HEADS)]

    def chunk_rows(step, z):
        cc = step if z == 0 else ncb - 1 - step
        return slice(cc * CHUNK, (cc + 1) * CHUNK)

    def pair_inputs(step):
        return [_rwkv_chunk(xf_ref, lwf_ref, chunk_rows(step, 0), 0),
                _rwkv_chunk(xb_ref, lwb_ref, chunk_rows(step, 1), 1)]

    def writer(step):
        def write_out(outs):
            yf_ref[chunk_rows(step, 0), :] = _bf(_lanes(outs[:RWKV_HEADS]))
            yb_ref[chunk_rows(step, 1), :] = _bf(_lanes(outs[RWKV_HEADS:]))
        return write_out

    res = _rwkv_state_free(pair_inputs(0), [])
    for step in range(ncb):
        hooks = _rwkv_state_stages(res, state, writer(step))
        if step + 1 < ncb:
            res = _rwkv_state_free(pair_inputs(step + 1), hooks)
        else:
            for hook in hooks:
                hook()
    for z in range(2):
        for h in range(RWKV_HEADS):
            s_ref[z, h] = state[z * RWKV_HEADS + h]


def _rwkv_scan(st, xin, lw, s0):
    ncb, rb, nblk, fwd, bwd = _scan_blocks(st)
    state_spec = pl.BlockSpec((None, 2, RWKV_HEADS, RWKV_HD, RWKV_HD), lambda b, j: (b, 0, 0, 0, 0))
    return pl.pallas_call(
        functools.partial(_rwkv_scan_kernel, ncb=ncb),
        grid=(st.n, nblk),
        in_specs=[pl.BlockSpec((rb, 2560), fwd), pl.BlockSpec((rb, 2560), bwd),
                  pl.BlockSpec((rb, 512), fwd), pl.BlockSpec((rb, 512), lambda b, j: (bwd(b, j)[0], 1)),
                  state_spec],
        out_specs=[pl.BlockSpec((rb, 512), fwd), pl.BlockSpec((rb, 512), bwd), state_spec],
        out_shape=[jax.ShapeDtypeStruct((st.rows, 512), BF16), jax.ShapeDtypeStruct((st.rows, 512), BF16),
                   jax.ShapeDtypeStruct((st.n, 2, RWKV_HEADS, RWKV_HD, RWKV_HD), F32)],
        compiler_params=_cp("parallel", "arbitrary"),
        name="rwkv_scan",
    )(xin, xin, lw, lw, s0)


def _rwkv_post_kernel(yf_ref, yb_ref, post_ref, lng_ref, lnb_ref, bd_ref, o_ref):
    o = yf_ref[...].astype(F32) + yb_ref[...].astype(F32)
    bd = bd_ref[...]
    d = o - _dot_data_mask(o, bd, 2)
    o = d * lax.rsqrt(_dot_data_mask(d * d, bd, 2) + RWKV_LN_EPS) * lng_ref[...] + lnb_ref[...]
    o_ref[...] = _bf((o + post_ref[:, 512:1024].astype(F32)) * post_ref[:, 0:512].astype(F32))


def _rwkv_post(st, y_f, y_b, post, ln_g, ln_b):
    m = post.shape[0]
    tr = _row_tile(st)
    full = lambda shape: pl.BlockSpec(shape, lambda i: (0, 0))
    row = lambda w: pl.BlockSpec((tr, w), lambda i: (i, 0))
    return pl.pallas_call(
        _rwkv_post_kernel,
        grid=(m // tr,),
        in_specs=[row(512), row(512), row(1024), full((1, 512)), full((1, 512)), full((512, 512))],
        out_specs=row(512),
        out_shape=jax.ShapeDtypeStruct((m, 512), BF16),
        compiler_params=_cp("parallel"),
        name="rwkv_post",
    )(y_f, y_b, post, ln_g.reshape(1, 512), ln_b.reshape(1, 512), _block_diag(512, RWKV_HD, 1.0 / RWKV_HD))


def _merge_kernel(h_ref, o0_ref, o1_ref, o2_ref, o3_ref, wg_ref, wb_ref, out_ref):
    h = h_ref[...]
    acc = None
    for i, o_ref in enumerate((o0_ref, o1_ref, o2_ref, o3_ref)):
        term = jax.nn.sigmoid(_dot(h, wg_ref[i])) * _dot(o_ref[...], wb_ref[i])
        acc = term if acc is None else acc + term
    out_ref[...] = _bf(acc)


def _merge(st, h, branch_outs, wg, wb):
    m, d = h.shape
    tm = _mm_tile(st)
    tn = 512
    row = lambda w: pl.BlockSpec((tm, w), lambda j, i: (i, 0))
    return pl.pallas_call(
        _merge_kernel,
        grid=(d // tn, m // tm),
        in_specs=[row(d)] + [row(BRANCH_W)] * 4 + [pl.BlockSpec((4, d, tn), lambda j, i: (0, 0, j)),
                                                    pl.BlockSpec((4, BRANCH_W, tn), lambda j, i: (0, 0, j))],
        out_specs=pl.BlockSpec((tm, tn), lambda j, i: (i, j)),
        out_shape=jax.ShapeDtypeStruct((m, d), BF16),
        compiler_params=_cp("parallel", "parallel"),
        name="merge",
    )(h, *branch_outs, wg, wb)


def _wo_kernel(m_ref, w_ref, x_ref, mod_ref, g_ref, o_ref, h_out):
    x = x_ref[...] + mod_ref[0, 2:3, :] * _dot(m_ref[...], w_ref[...])
    o_ref[...] = x
    h_out[...] = _norm_mod(x, g_ref[...], mod_ref, 4, 3)


def _out_proj(st, merged, w_o, x, mod_l, gain2):
    m, d = x.shape
    tm = _mm_tile(st)
    row = pl.BlockSpec((tm, d), lambda i: (i, 0))
    return pl.pallas_call(
        _wo_kernel,
        grid=(m // tm,),
        in_specs=[row, pl.BlockSpec((d, d), lambda i: (0, 0)), row, pl.BlockSpec((1, 6, d), st.group_map(tm)),
                  pl.BlockSpec((1, d), lambda i: (0, 0))],
        out_specs=[row, row],
        out_shape=[jax.ShapeDtypeStruct((m, d), F32), jax.ShapeDtypeStruct((m, d), BF16)],
        compiler_params=_cp("parallel"),
        name="out_proj",
    )(merged, w_o, x, mod_l, gain2.reshape(1, d))


def _ffn_kernel(h_ref, w1_ref, w3_ref, w2_ref, x_ref, mod_ref, fg_ref, o_ref, acc_ref, *, final):
    f = pl.program_id(1)

    @pl.when(f == 0)
    def _():
        acc_ref[...] = jnp.zeros_like(acc_ref)

    h = h_ref[...]
    u = _silu(_dot(h, w1_ref[...])) * _dot(h, w3_ref[...])
    acc_ref[...] += _dot(_bf(u), w2_ref[...])

    @pl.when(f == pl.num_programs(1) - 1)
    def _():
        x = x_ref[...] + mod_ref[0, 5:6, :] * acc_ref[...]
        o_ref[...] = _rms(x, fg_ref[...]) if final else x


def _ffn(st, h, w1, w3, w2, x, mod_l, final_gain, final):
    m, d = x.shape
    dff = w1.shape[1]
    tm = _mm_tile(st)
    tf = 512
    return pl.pallas_call(
        functools.partial(_ffn_kernel, final=final),
        grid=(m // tm, dff // tf),
        in_specs=[pl.BlockSpec((tm, d), lambda i, f: (i, 0)),
                  pl.BlockSpec((d, tf), lambda i, f: (0, f)),
                  pl.BlockSpec((d, tf), lambda i, f: (0, f)),
                  pl.BlockSpec((tf, d), lambda i, f: (f, 0)),
                  pl.BlockSpec((tm, d), lambda i, f: (i, 0)),
                  pl.BlockSpec((1, 6, d), lambda i, f: st.group_map(tm)(i)),
                  pl.BlockSpec((1, d), lambda i, f: (0, 0))],
        out_specs=pl.BlockSpec((tm, d), lambda i, f: (i, 0)),
        out_shape=jax.ShapeDtypeStruct((m, d), F32),
        scratch_shapes=[pltpu.VMEM((tm, d), F32)],
        compiler_params=_cp("parallel", "arbitrary"),
        name="ffn",
    )(h, w1, w3, w2, x, mod_l, final_gain.reshape(1, d))


def _split_w_in(w_in):
    z = lambda w: jnp.zeros(w_in.shape[:2] + (w,), w_in.dtype)
    att = w_in[..., 0:768]
    ssd = jnp.concatenate([w_in[..., 768:2064], z(SSD_W - 1296)], axis=-1)
    rwkv = w_in[..., 2064:3856]
    gla = jnp.concatenate([w_in[..., 3856:4880], w_in[..., 4896:5408], w_in[..., 4880:4896], z(LANE - 16)], axis=-1)
    return tuple(_bf(w) for w in (att, ssd, rwkv, gla))


def _block(st, x, mod_l, p, rope_tables, ctx, final_gain, final):
    att_args = (st, x, p['norm1'], mod_l, p['w_att'], p['w_ssd'], p['q_norm'], p['k_norm'])

    if ctx is None:
        h, proj_ssd, q, k_att, v_att, k, v = _in_proj_a(*att_args, None)
        o_att = _attention(st, q, k_att, v_att)
        s_ssd = jnp.zeros((st.n, 2, SSD_HEADS, SSD_N, SSD_P), F32)
        s_rwkv = jnp.zeros((st.n, 2, RWKV_HEADS, RWKV_HD, RWKV_HD), F32)
        s_gla_t = jnp.zeros((st.n, 2, GLA_HEADS, GLA_DV, GLA_DK), F32)
    else:
        ctx_k, ctx_v, s_ssd, s_rwkv, s_gla = ctx
        h, proj_ssd, q, k_att, v_att = _in_proj_a(*att_args, rope_tables)
        k = v = None
        o_att = _attention(st, q, _join_cache(st, k_att, ctx_k, 2), _join_cache(st, v_att, ctx_v, 1))
        s_gla_t = jnp.swapaxes(s_gla, -1, -2)

    proj_rwkv, proj_gla = _in_proj_b(st, h, p['w_rwkv'], p['w_gla'])

    xbc, dt, cum, dtt, cumt = _ssd_prep(st, proj_ssd, p['ssd_conv_w'], p['ssd_conv_b'], p['ssd_dt_bias'],
                                        p['ssd_a_log'])
    y_ssd_f, y_ssd_b, new_ssd = _ssd_scan(st, xbc, dt, cum, dtt, cumt, s_ssd)
    o_ssd = _ssd_post(st, y_ssd_f, y_ssd_b, xbc, proj_ssd, p['ssd_d'], p['ssd_norm'])

    rin, lw, rpost = _rwkv_prep(st, proj_rwkv, p['rwkv_mu'], p['rwkv_w2'], p['rwkv_w0'], p['rwkv_a2'],
                                p['rwkv_a0'], p['rwkv_g2'], p['rwkv_kk'], p['rwkv_ka'], p['rwkv_rk'])
    y_rwkv_f, y_rwkv_b, new_rwkv = _rwkv_scan(st, rin, lw, s_rwkv)
    o_rwkv = _rwkv_post(st, y_rwkv_f, y_rwkv_b, rpost, p['rwkv_ln_g'], p['rwkv_ln_b'])

    g2 = jnp.concatenate([p['gla_g2'][0], p['gla_g2'][1]], axis=1)
    g2p = jnp.pad(g2, ((0, LANE - g2.shape[0]), (0, 0)))
    gla_cum = _gla_prep(st, proj_gla, g2p, p['gla_gb'].reshape(1, 2 * GLA_QK))
    y_gla_f, y_gla_b, new_gla_t = _gla_scan(st, proj_gla, gla_cum, s_gla_t)
    o_gla = _gla_post(st, y_gla_f, y_gla_b, proj_gla, p['gla_norm'])

    merged = _merge(st, h, (o_att, o_ssd, o_rwkv, o_gla), p['w_gate'], p['w_branch'])
    x, h2 = _out_proj(st, merged, p['w_o'], x, mod_l, p['norm2'])
    x = _ffn(st, h2, p['ffn_w1'], p['ffn_w3'], p['ffn_w2'], x, mod_l, final_gain, final)
    return x, (k, v, new_ssd, new_rwkv, jnp.swapaxes(new_gla_t, -1, -2))


def kernel(x_prompt, x_sample, cache_attn_k, cache_attn_v, state_ssd, state_rwkv, state_gla, c, c_ctx, w_mod, b_mod, norm1, norm2, w_in, q_norm, k_norm, ssd_conv_w, ssd_conv_b, ssd_dt_bias, ssd_a_log, ssd_d, ssd_norm, rwkv_mu, rwkv_w0, rwkv_w2, rwkv_a0, rwkv_a2, rwkv_g2, rwkv_kk, rwkv_ka, rwkv_rk, rwkv_ln_g, rwkv_ln_b, gla_g2, gla_gb, gla_norm, w_gate, w_branch, w_o, ffn_w1, ffn_w3, ffn_w2, final_norm):
    nb, seq, d = x_prompt.shape
    db, dseq, _ = x_sample.shape
    depth = w_in.shape[0]
    assert d == D_MODEL and seq % CHUNK == 0 and dseq % CHUNK == 0 and 1 + db <= MOD_ROWS
    ctx_st = _Stream(nb, seq, 0, False)
    lat_st = _Stream(db, dseq, 1, True)

    cond = jnp.concatenate([c_ctx[None], c, jnp.zeros((MOD_ROWS - 1 - db, d), F32)], axis=0)
    mod = _modulation(cond, w_mod, b_mod)

    w_att, w_ssd, w_rwkv, w_gla = _split_w_in(w_in)
    w_gate_b, w_branch_b, w_o_b = _bf(w_gate), _bf(w_branch), _bf(w_o)
    w1_b, w3_b, w2_b = _bf(ffn_w1), _bf(ffn_w3), _bf(ffn_w2)

    def params_at(l):
        return dict(norm1=norm1[l], norm2=norm2[l], w_att=w_att[l], w_ssd=w_ssd[l], w_rwkv=w_rwkv[l],
                    w_gla=w_gla[l], q_norm=q_norm[l], k_norm=k_norm[l],
                    ssd_conv_w=ssd_conv_w[l], ssd_conv_b=ssd_conv_b[l], ssd_dt_bias=ssd_dt_bias[l],
                    ssd_a_log=ssd_a_log[l], ssd_d=ssd_d[l], ssd_norm=ssd_norm[l],
                    rwkv_mu=rwkv_mu[l], rwkv_w0=rwkv_w0[l], rwkv_w2=rwkv_w2[l], rwkv_a0=rwkv_a0[l],
                    rwkv_a2=rwkv_a2[l], rwkv_g2=rwkv_g2[l], rwkv_kk=rwkv_kk[l], rwkv_ka=rwkv_ka[l],
                    rwkv_rk=rwkv_rk[l], rwkv_ln_g=rwkv_ln_g[l], rwkv_ln_b=rwkv_ln_b[l],
                    gla_g2=gla_g2[l], gla_gb=gla_gb[l], gla_norm=gla_norm[l],
                    w_gate=w_gate_b[l], w_branch=w_branch_b[l], w_o=w_o_b[l],
                    ffn_w1=w1_b[l], ffn_w3=w3_b[l], ffn_w2=w2_b[l])

    xp = x_prompt.reshape(nb * seq, d)
    new_k, new_v, new_ssd, new_rwkv, new_gla = [], [], [], [], []
    for l in range(depth):
        xp, (k_l, v_l, ssd_l, rwkv_l, gla_l) = _block(ctx_st, xp, mod[l], params_at(l), None, None, final_norm,
                                                      l == depth - 1)
        new_k.append(k_l.reshape(nb, seq, ATT_KV, HEAD_DIM))
        new_v.append(v_l.reshape(nb, seq, ATT_KV, HEAD_DIM))
        new_ssd.append(ssd_l)
        new_rwkv.append(rwkv_l)
        new_gla.append(gla_l)

    rope_tables = _rope_tables(dseq)
    xs = x_sample.reshape(db * dseq, d)
    for l in range(depth):
        ctx = (cache_attn_k[:, l], cache_attn_v[:, l], state_ssd[:, l], state_rwkv[:, l], state_gla[:, l])
        xs, _ = _block(lat_st, xs, mod[l], params_at(l), rope_tables, ctx, final_norm, l == depth - 1)

    y_prompt = xp.reshape(nb, seq, d)
    y_sample = xs.reshape(db, dseq, d)
    return (y_prompt, y_sample, jnp.stack(new_k, axis=1), jnp.stack(new_v, axis=1),
            jnp.stack(new_ssd, axis=1), jnp.stack(new_rwkv, axis=1), jnp.stack(new_gla, axis=1))
```

```python
import functools

import jax
import jax.numpy as jnp
import numpy as np
from jax import lax
from jax.experimental import pallas as pl
from jax.experimental.pallas import tpu as pltpu

F32 = jnp.float32
BF16 = jnp.bfloat16

D_MODEL = 2048
GRID_W = 64
ATT_HEADS = 8
ATT_KV = 2
ATT_GROUP = ATT_HEADS // ATT_KV
HEAD_DIM = 64
ROPE_THETA = 10000.0
SSD_HEADS = 8
SSD_P = 64
SSD_N = 64
SSD_GROUPS = 2
RWKV_HEADS = 8
RWKV_HD = 64
RWKV_DECAY_SCALE = 0.6065306597126334
RWKV_LN_EPS = 64e-5
GLA_HEADS = 4
GLA_DK = 64
GLA_DV = 128
GLA_GATE_NORM = 16.0
CHUNK = 64
BRANCH_W = 512

ATT_W = 768
SSD_W = 1408
RWKV_W = 1792
GLA_W = 1664
LANE = 128
SUBLANE = 8
MOD_ROWS = 16
VMEM_LIMIT = 56 * 1024 * 1024
ROW_TILE = 512
MM_TILE = 512
SCAN_CHUNKS = 4
KEY_TILE = 256
ATT_Q_TILE = 512
KEY_UNROLL = 6
LOG2E = 1.4426950408889634
ATT_MIN_ROW_SUM = 2.0 ** -90


def _cp(*sem):
    return pltpu.CompilerParams(dimension_semantics=sem, vmem_limit_bytes=VMEM_LIMIT)


def _bf(x):
    return x.astype(BF16)


def _dot(a, b):
    return jnp.dot(a, b, preferred_element_type=F32)


def _dot_nt(a, b):
    return lax.dot_general(a, b, (((1,), (1,)), ((), ())), preferred_element_type=F32)


def _dot_tn(a, b):
    return lax.dot_general(a, b, (((0,), (0,)), ((), ())), preferred_element_type=F32)


def _pieces(a, n):
    out = []
    for _ in range(n):
        piece = _bf(a)
        out.append(piece)
        a = a - piece.astype(F32)
    return out


def _dot_data_mask(a, mask, n):
    return functools.reduce(jnp.add, [_dot(piece, mask) for piece in _pieces(a, n)])


def _dot_mask_data(mask, a, n):
    return functools.reduce(jnp.add, [_dot(mask, piece) for piece in _pieces(a, n)])


def _dot_nt_mask_data(mask, a, n):
    return functools.reduce(jnp.add, [_dot_nt(mask, piece) for piece in _pieces(a, n)])


def _dot_split(a, b):
    a_hi, a_lo = _pieces(a, 2)
    b_hi, b_lo = _pieces(b, 2)
    return _dot(a_hi, b_hi) + (_dot(a_hi, b_lo) + _dot(a_lo, b_hi))


def _silu(x):
    return x * jax.nn.sigmoid(x)


def _softplus(x):
    return jnp.maximum(x, 0.0) + jnp.log1p(jnp.exp(-jnp.abs(x)))


def _lanes(pieces):
    return jnp.concatenate(pieces, axis=1)


def _rows(pieces):
    return jnp.concatenate(pieces, axis=0)


def _mod_kernel(c_ref, w_ref, b_ref, o_ref):
    c = c_ref[...]
    o_ref[0] = _dot(_bf(_silu(c)), _bf(w_ref[0])) + b_ref[0]


def _modulation(cond, w_mod, b_mod):
    nl, d, n6 = w_mod.shape
    tn = 1024
    out = pl.pallas_call(
        _mod_kernel,
        grid=(nl, n6 // tn),
        in_specs=[pl.BlockSpec((MOD_ROWS, d), lambda l, j: (0, 0)),
                  pl.BlockSpec((1, d, tn), lambda l, j: (l, 0, j)),
                  pl.BlockSpec((1, 1, tn), lambda l, j: (l, 0, j))],
        out_specs=pl.BlockSpec((1, MOD_ROWS, tn), lambda l, j: (l, 0, j)),
        out_shape=jax.ShapeDtypeStruct((nl, MOD_ROWS, n6), F32),
        compiler_params=_cp("parallel", "parallel"),
        name="modulation",
    )(cond, w_mod, b_mod.reshape(nl, 1, n6))
    return out.reshape(nl, MOD_ROWS, 6, d)


class _Stream:
    def __init__(self, n, t, group0, per_seq):
        self.n, self.t, self.group0, self.per_seq = n, t, group0, per_seq
        self.rows = n * t

    def group_map(self, tile):
        g0, per_seq, t = self.group0, self.per_seq, self.t
        if per_seq:
            return lambda i: (g0 + (i * tile) // t, 0, 0)
        return lambda i: (g0, 0, 0)


def _mm_tile(st):
    limit = st.t if st.per_seq else st.rows
    return min(MM_TILE, limit)


def _row_tile(st):
    return min(ROW_TILE, st.t)


def _rms(x, gain):
    return x * lax.rsqrt(jnp.mean(x * x, axis=-1, keepdims=True) + 1e-6) * gain


def _norm_mod(x, gain, mod_ref, sc_idx, sh_idx):
    return _bf(_rms(x, gain) * (1.0 + mod_ref[0, sc_idx:sc_idx + 1, :]) + mod_ref[0, sh_idx:sh_idx + 1, :])


def _in_proj_a_kernel(*refs, rope):
    x_ref, g_ref, mod_ref, wa_ref, ws_ref, qn_ref, kn_ref, bdq_ref, bdk_ref = refs[:9]
    tables = [r[...] for r in refs[9:12]] if rope else None
    outs = refs[12:] if rope else refs[9:]
    h = _norm_mod(x_ref[...], g_ref[...], mod_ref, 1, 0)
    outs[0][...] = h
    outs[1][...] = _dot(h, ws_ref[...])
    q, k_att, v_att, k, v = _att_operands(_dot(h, wa_ref[...]), qn_ref[...], kn_ref[...], bdq_ref[...],
                                          bdk_ref[...], tables)
    outs[2][...] = q
    for kv in range(ATT_KV):
        outs[3][kv] = k_att[kv]
        outs[4][kv] = v_att[kv]
    if not rope:
        outs[5][...] = k
        outs[6][...] = v


def _in_proj_a(st, x, gain, mod_l, w_att, w_ssd, q_norm, k_norm, rope_tables):
    m, d = x.shape
    tm = _mm_tile(st)
    rope = rope_tables is not None
    row = lambda w: pl.BlockSpec((tm, w), lambda i: (i, 0))
    full = lambda a: pl.BlockSpec(a.shape, lambda i: (0, 0))
    args = [x, gain.reshape(1, d), mod_l, w_att, w_ssd, jnp.tile(q_norm, ATT_HEADS).reshape(1, 512),
            jnp.tile(k_norm, ATT_KV).reshape(1, 128), _block_diag(512, HEAD_DIM, 1.0 / HEAD_DIM),
            _block_diag(128, HEAD_DIM, 1.0 / HEAD_DIM)]
    in_specs = [row(d), full(args[1]), pl.BlockSpec((1, 6, d), st.group_map(tm))] + [full(a) for a in args[3:]]
    if rope:
        tps = st.t // tm
        in_specs += [pl.BlockSpec((tm, LANE), lambda i: (i % tps, 0))] * 3
        args += list(rope_tables)
    kv_spec = pl.BlockSpec((ATT_KV, tm, LANE), lambda i: (0, i, 0))
    kv_shape = jax.ShapeDtypeStruct((ATT_KV, m, LANE), BF16)
    out_specs = [row(d), row(SSD_W), row(ATT_HEADS * LANE), kv_spec, kv_spec]
    out_shape = [jax.ShapeDtypeStruct((m, d), BF16), jax.ShapeDtypeStruct((m, SSD_W), F32),
                 jax.ShapeDtypeStruct((m, ATT_HEADS * LANE), BF16), kv_shape, kv_shape]
    if not rope:
        out_specs += [row(128), row(128)]
        out_shape += [jax.ShapeDtypeStruct((m, 128), F32)] * 2
    return pl.pallas_call(
        functools.partial(_in_proj_a_kernel, rope=rope),
        grid=(m // tm,),
        in_specs=in_specs,
        out_specs=out_specs,
        out_shape=out_shape,
        compiler_params=_cp("parallel"),
        name="in_proj_a",
    )(*args)


def _in_proj_b_kernel(h_ref, wr_ref, wg_ref, pr_out, pg_out):
    h = h_ref[...]
    pr_out[...] = _dot(h, wr_ref[...])
    pg_out[...] = _dot(h, wg_ref[...])


def _in_proj_b(st, h, w_rwkv, w_gla):
    m, d = h.shape
    tm = _mm_tile(st)
    row = lambda w: pl.BlockSpec((tm, w), lambda i: (i, 0))
    full = lambda a: pl.BlockSpec(a.shape, lambda i: (0, 0))
    return pl.pallas_call(
        _in_proj_b_kernel,
        grid=(m // tm,),
        in_specs=[row(d), full(w_rwkv), full(w_gla)],
        out_specs=[row(RWKV_W), row(GLA_W)],
        out_shape=[jax.ShapeDtypeStruct((m, RWKV_W), F32), jax.ShapeDtypeStruct((m, GLA_W), F32)],
        compiler_params=_cp("parallel"),
        name="in_proj_b",
    )(h, w_rwkv, w_gla)


def _att_operands(p, q_gain, k_gain, bdq, bdk, tables):
    aq = p[:, :512]
    ak = p[:, 512:640]
    av = p[:, 640:768]
    q = aq * lax.rsqrt(_dot_data_mask(aq * aq, bdq, 2) + 1e-6) * q_gain
    k = ak * lax.rsqrt(_dot_data_mask(ak * ak, bdk, 2) + 1e-6) * k_gain
    k_plain = k
    if tables is not None:
        c, sa, sb = tables
        k = k * c + pltpu.roll(k, LANE - 16, 1) * sa + pltpu.roll(k, 16, 1) * sb
        c4, sa4, sb4 = _lanes([c] * 4), _lanes([sa] * 4), _lanes([sb] * 4)
        q = q * c4 + pltpu.roll(q, 512 - 16, 1) * sa4 + pltpu.roll(q, 16, 1) * sb4
    tr = p.shape[0]
    spare = lax.broadcasted_iota(jnp.int32, (tr, LANE - HEAD_DIM), 1)
    one_col = (spare == 0).astype(F32)
    two_cols = (spare < 2).astype(F32)
    q = q * (HEAD_DIM ** -0.5 * LOG2E)
    q_len = jnp.sqrt(_dot_data_mask(q * q, bdq, 2) * HEAD_DIM)
    q_att = _bf(_lanes([piece for h in range(ATT_HEADS)
                        for piece in (q[:, h * HEAD_DIM:(h + 1) * HEAD_DIM],
                                      one_col * q_len[:, h * HEAD_DIM:(h + 1) * HEAD_DIM])]))
    heads = [slice(kv * HEAD_DIM, (kv + 1) * HEAD_DIM) for kv in range(ATT_KV)]
    k_att = [_bf(_lanes([k[:, sl], two_cols])) for sl in heads]
    v_att = [_bf(_lanes([av[:, sl], one_col])) for sl in heads]
    return q_att, k_att, v_att, k_plain, av


def _block_diag(width, block, value):
    idx = np.arange(width) // block
    return _bf(jnp.asarray((idx[:, None] == idx[None, :]).astype(np.float32) * value))


def _join_cache(st, own, cache, ones):
    n, p, kv, hd = cache.shape
    c = jnp.transpose(cache, (2, 0, 1, 3))
    pad = jnp.zeros((kv, n, p, LANE - hd), c.dtype).at[..., :ones].set(1.0)
    c = _bf(jnp.concatenate([c, pad], axis=-1))
    joined = jnp.concatenate([own.reshape(kv, n, st.t, LANE), c], axis=2)
    return joined.reshape(kv, n * (st.t + p), LANE)


def _rope_tables(t):
    half = HEAD_DIM // 2
    nf = half // 2
    freqs = ROPE_THETA ** (-jnp.arange(nf, dtype=F32) / nf)
    tt = jnp.arange(t)
    ang_r = (tt // GRID_W).astype(F32)[:, None] * freqs[None, :]
    ang_c = (tt % GRID_W).astype(F32)[:, None] * freqs[None, :]
    zero = jnp.zeros_like(ang_r)
    cos = jnp.concatenate([jnp.cos(ang_r)] * 2 + [jnp.cos(ang_c)] * 2, axis=1)
    sa = jnp.concatenate([-jnp.sin(ang_r), zero, -jnp.sin(ang_c), zero], axis=1)
    sb = jnp.concatenate([zero, jnp.sin(ang_r), zero, jnp.sin(ang_c)], axis=1)
    return tuple(jnp.tile(a, (1, LANE // HEAD_DIM)) for a in (cos, sa, sb))


def _key_bound_kernel(k_ref, o_ref):
    ones = jnp.ones((LANE, LANE), BF16)
    for kv in range(ATT_KV):
        k = k_ref[kv].astype(F32)
        k = jnp.where(lax.broadcasted_iota(jnp.int32, k.shape, 1) < HEAD_DIM, k, 0.0)
        best = jnp.max(_dot_data_mask(k * k, ones, 2), axis=0, keepdims=True)
        o_ref[kv] = jnp.broadcast_to(best, (SUBLANE, LANE))


def _key_bound(n, keys_per_seq, k):
    return pl.pallas_call(
        _key_bound_kernel,
        grid=(n,),
        in_specs=[pl.BlockSpec((ATT_KV, keys_per_seq, LANE), lambda b: (0, b, 0))],
        out_specs=pl.BlockSpec((None, ATT_KV, SUBLANE, LANE), lambda b: (b, 0, 0, 0)),
        out_shape=jax.ShapeDtypeStruct((n, ATT_KV, SUBLANE, LANE), F32),
        compiler_params=_cp("parallel"),
        name="key_bound",
    )(k)


def _attn_kernel(q_ref, k_ref, v_ref, kb_ref, o_ref, m_scr, acc_scr, *, tq, n_tiles):
    rows = ATT_GROUP * tq
    lane = lax.broadcasted_iota(jnp.int32, (rows, LANE), 1)
    outs = []
    for kv in range(ATT_KV):
        qs = _rows([q_ref[:, (kv * ATT_GROUP + g) * LANE:(kv * ATT_GROUP + g + 1) * LANE]
                    for g in range(ATT_GROUP)]).astype(F32)

        def key_tile(ref, j):
            return ref[kv, pl.ds(pl.multiple_of(j * KEY_TILE, KEY_TILE), KEY_TILE), :]

        def weighted_sum(q_shift, unroll):
            acc_scr[...] = jnp.zeros((rows, LANE), F32)

            def sum_body(j, carry):
                p = jnp.exp2(_dot_nt(q_shift, key_tile(k_ref, j)))
                acc_scr[...] += _dot(_bf(p), key_tile(v_ref, j))
                return carry

            lax.fori_loop(0, n_tiles, sum_body, 0, unroll=unroll)
            return acc_scr[...]

        k_len = jnp.sqrt(kb_ref[kv][0:1, :])
        unroll = max(u for u in range(1, KEY_UNROLL + 1) if n_tiles % u == 0)
        acc = weighted_sum(_bf(qs * jnp.where(lane == HEAD_DIM, -k_len, 1.0)), unroll)

        def exact_shift():
            m_scr[...] = jnp.full((rows, LANE), -jnp.inf, F32)
            q0 = _bf(jnp.where(lane >= HEAD_DIM, 0.0, qs))

            def max_body(j, carry):
                sc = _dot_nt(q0, key_tile(k_ref, j))
                m_scr[...] = jnp.maximum(m_scr[...], jnp.maximum(sc[:, :LANE], sc[:, LANE:]))
                return carry

            lax.fori_loop(0, n_tiles, max_body, 0)
            row_max = jnp.max(m_scr[...], axis=-1, keepdims=True)
            hi = _bf(row_max).astype(F32)
            shifted = jnp.where(lane == HEAD_DIM, -hi, jnp.where(lane == HEAD_DIM + 1, hi - row_max, qs))
            return weighted_sum(_bf(shifted), 1)

        row_sum = acc[:, HEAD_DIM:HEAD_DIM + 1]
        row_sum_ok = jnp.logical_and(jnp.min(row_sum) >= ATT_MIN_ROW_SUM, jnp.max(row_sum) <= 1.0 / ATT_MIN_ROW_SUM)
        acc = lax.cond(row_sum_ok, lambda: acc, exact_shift)
        o = acc[:, :HEAD_DIM] / acc[:, HEAD_DIM:HEAD_DIM + 1]
        outs += [o[g * tq:(g + 1) * tq] for g in range(ATT_GROUP)]
    o_ref[...] = _bf(_lanes(outs))


def _attention(st, q, k, v):
    tq = min(ATT_Q_TILE, st.t)
    nq = st.t // tq
    keys = k.shape[1] // st.n
    assert keys % KEY_TILE == 0
    kv_spec = pl.BlockSpec((ATT_KV, keys, LANE), lambda b, i: (0, b, 0))
    rows = ATT_GROUP * tq
    return pl.pallas_call(
        functools.partial(_attn_kernel, tq=tq, n_tiles=keys // KEY_TILE),
        grid=(st.n, nq),
        in_specs=[pl.BlockSpec((tq, ATT_HEADS * LANE), lambda b, i: (b * nq + i, 0)), kv_spec, kv_spec,
                  pl.BlockSpec((None, ATT_KV, SUBLANE, LANE), lambda b, i: (b, 0, 0, 0))],
        out_specs=pl.BlockSpec((tq, 512), lambda b, i: (b * nq + i, 0)),
        out_shape=jax.ShapeDtypeStruct((st.rows, 512), BF16),
        scratch_shapes=[pltpu.VMEM((rows, LANE), F32), pltpu.VMEM((rows, LANE), F32)],
        compiler_params=_cp("parallel", "parallel"),
        name="attention",
    )(q, k, v, _key_bound(st.n, keys, k))


def _halo_specs(st, width):
    tr = _row_tile(st)
    per8 = tr // SUBLANE
    last8 = st.rows // SUBLANE - 1
    return [pl.BlockSpec((tr, width), lambda i: (i, 0)),
            pl.BlockSpec((SUBLANE, width), lambda i: (jnp.maximum(i * per8 - 1, 0), 0)),
            pl.BlockSpec((SUBLANE, width), lambda i: (jnp.minimum((i + 1) * per8, last8), 0))]


def _neighbours(cur, prev8, next8, tiles_per_seq):
    tr = cur.shape[0]
    j = pl.program_id(0) % tiles_per_seq
    pr = jnp.where(j != 0, prev8[SUBLANE - 1:SUBLANE], 0.0)
    nx = jnp.where(j != tiles_per_seq - 1, next8[0:1], 0.0)
    row = lax.broadcasted_iota(jnp.int32, cur.shape, 0)
    x_prev = jnp.where(row == 0, pr, pltpu.roll(cur, 1, 0))
    x_next = jnp.where(row == tr - 1, nx, pltpu.roll(cur, tr - 1, 0))
    return x_prev, x_next


def _dir_masks(z):
    ri = lax.broadcasted_iota(jnp.int32, (CHUNK, CHUNK), 0)
    ci = lax.broadcasted_iota(jnp.int32, (CHUNK, CHUNK), 1)
    return ((ci <= ri), (ci < ri)) if z == 0 else ((ci >= ri), (ci > ri))


def _tri_blocks(tr):
    i = np.arange(tr)
    same = (i[:, None] // CHUNK) == (i[None, :] // CHUNK)
    lower = same & (i[None, :] <= i[:, None])
    upper = same & (i[None, :] >= i[:, None])
    return _bf(jnp.asarray(lower.astype(np.float32))), _bf(jnp.asarray(upper.astype(np.float32)))


def _scan_blocks(st):
    ncb = min(SCAN_CHUNKS, st.t // CHUNK)
    rb = ncb * CHUNK
    nblk = st.t // rb
    fwd = lambda b, j: (b * nblk + j, 0)
    bwd = lambda b, j: (b * nblk + nblk - 1 - j, 0)
    return ncb, rb, nblk, fwd, bwd


def _chunk_rows(step, z, ncb):
    cc = step if z == 0 else ncb - 1 - step
    return slice(cc * CHUNK, (cc + 1) * CHUNK)


def _ssd_prep_kernel(cur_ref, prev_ref, next_ref, cw_ref, cb_ref, dtb_ref, alog_ref, lo_ref, up_ref,
                     xbc_out, dt_out, cum_out, dtt_out, cumt_out, *, tiles_per_seq):
    cur = cur_ref[...]
    xc = cur[:, 512:1280]
    x_prev, x_next = _neighbours(xc, prev_ref[:, 512:1280], next_ref[:, 512:1280], tiles_per_seq)
    cw = cw_ref[...]
    conv = cb_ref[...] + x_prev * cw[0:1] + xc * cw[1:2] + x_next * cw[2:3]
    xbc_out[...] = _bf(_silu(conv))
    dt = _softplus(cur[:, 1280:1408] + dtb_ref[...])
    dt_out[...] = dt
    ld = dt * -jnp.exp(alog_ref[...])
    lane = lax.broadcasted_iota(jnp.int32, ld.shape, 1)
    cum = jnp.where(lane < SSD_HEADS, _dot_mask_data(lo_ref[...], ld, 3), _dot_mask_data(up_ref[...], ld, 3))
    cum_out[...] = cum
    pick = (lax.broadcasted_iota(jnp.int32, (2 * SSD_HEADS, LANE), 0)
            == lax.broadcasted_iota(jnp.int32, (2 * SSD_HEADS, LANE), 1)).astype(BF16)
    dtt_out[...] = _dot_nt_mask_data(pick, dt, 3)
    cumt_out[...] = _dot_nt_mask_data(pick, cum, 3)


def _pad_lanes(v, width=LANE):
    v = v.reshape(1, -1)
    return jnp.pad(v, ((0, 0), (0, width - v.shape[1])))


def _ssd_prep(st, proj, conv_w, conv_b, dt_bias, a_log):
    m = proj.shape[0]
    tr = _row_tile(st)
    full = lambda shape: pl.BlockSpec(shape, lambda i: (0, 0))
    row = lambda w: pl.BlockSpec((tr, w), lambda i: (i, 0))
    col = pl.BlockSpec((2 * SSD_HEADS, tr), lambda i: (0, i))
    lower, upper = _tri_blocks(tr)
    return pl.pallas_call(
        functools.partial(_ssd_prep_kernel, tiles_per_seq=st.t // tr),
        grid=(m // tr,),
        in_specs=_halo_specs(st, SSD_W) + [full((3, 768)), full((1, 768)), full((1, LANE)), full((1, LANE)),
                                           full((tr, tr)), full((tr, tr))],
        out_specs=[row(768), row(LANE), row(LANE), col, col],
        out_shape=[jax.ShapeDtypeStruct((m, 768), BF16), jax.ShapeDtypeStruct((m, LANE), F32),
                   jax.ShapeDtypeStruct((m, LANE), F32), jax.ShapeDtypeStruct((2 * SSD_HEADS, m), F32),
                   jax.ShapeDtypeStruct((2 * SSD_HEADS, m), F32)],
        compiler_params=_cp("parallel"),
        name="ssd_prep",
    )(proj, proj, proj, conv_w.T, conv_b.reshape(1, 768), _pad_lanes(dt_bias), _pad_lanes(a_log), lower, upper)


def _ssd_scan_kernel(xf_ref, xb_ref, dtf_ref, dtb_ref, cf_ref, cb_ref, dttf_ref, dttb_ref, ctf_ref, ctb_ref,
                     sel_ref, s0_ref, yf_ref, yb_ref, s_ref, *, ncb):
    @pl.when(pl.program_id(1) == 0)
    def _():
        s_ref[...] = s0_ref[...]

    hpg = SSD_HEADS // SSD_GROUPS
    state = {(z, g): _lanes([s_ref[z, g * hpg + hh] for hh in range(hpg)])
             for z in range(2) for g in range(SSD_GROUPS)}
    refs = ((xf_ref, dtf_ref, cf_ref, dttf_ref, ctf_ref, yf_ref), (xb_ref, dtb_ref, cb_ref, dttb_ref, ctb_ref, yb_ref))
    groups = [(z, g) for z in range(2) for g in range(SSD_GROUPS)]
    heads = [(z, h) for z in range(2) for h in range(SSD_HEADS)]
    for step in range(ncb):
        xbc, dt, cum, dtt, cumt, incl, rows = {}, {}, {}, {}, {}, {}, {}
        for z in range(2):
            x_ref, dt_ref, c_ref, dtt_ref, ct_ref, _ = refs[z]
            rows[z] = _chunk_rows(step, z, ncb)
            xbc[z] = x_ref[rows[z], :]
            dt[z] = _dot_data_mask(dt_ref[rows[z], :], sel_ref[z], 3)
            cum[z] = _dot_data_mask(c_ref[rows[z], :], sel_ref[z], 3)
            dtt[z], cumt[z] = dtt_ref[:, rows[z]], ct_ref[:, rows[z]]
            incl[z] = _dir_masks(z)[0]
        bmat = {(z, g): xbc[z][:, 512 + g * SSD_N:512 + (g + 1) * SSD_N] for z, g in groups}
        cmat = {(z, g): _bf(xbc[z][:, 640 + g * SSD_N:640 + (g + 1) * SSD_N]) for z, g in groups}
        cb = {k: _dot_nt(cmat[k], _bf(bmat[k])) for k in groups}
        cs = {k: _dot(cmat[k], _bf(state[k])) for k in groups}
        gw = hpg * SSD_P
        glast = {z: cum[z][CHUNK - 1:CHUNK] if z == 0 else cum[z][0:1] for z in range(2)}
        xs = {z: xbc[z][:, :SSD_HEADS * SSD_P] for z in range(2)}
        xw = {z: xs[z] * (dt[z] * jnp.exp(glast[z] - cum[z])) for z in range(2)}
        inc = {(z, g): _dot_tn(_bf(bmat[z, g]), _bf(xw[z][:, g * gw:(g + 1) * gw])) for z, g in groups}
        att = {}
        for z, h in heads:
            ln = z * SSD_HEADS + h
            gcol = cum[z][:, h * SSD_P:(h + 1) * SSD_P]
            dec = jnp.exp(jnp.where(incl[z], gcol - cumt[z][ln:ln + 1], -jnp.inf))
            att[z, h] = _bf(cb[z, h // hpg] * dec * dtt[z][ln:ln + 1])
        intra = {(z, h): _dot(att[z, h], _bf(xs[z][:, h * SSD_P:(h + 1) * SSD_P])) for z, h in heads}
        for z in range(2):
            e_in = jnp.exp(cum[z])
            e_last = jnp.exp(glast[z])
            outs = []
            for g in range(SSD_GROUPS):
                hs = range(g * hpg, (g + 1) * hpg)
                outs.append(_lanes([intra[z, h] for h in hs]) + cs[z, g] * e_in[:, g * gw:(g + 1) * gw])
                state[z, g] = state[z, g] * e_last[:, g * gw:(g + 1) * gw] + inc[z, g]
            refs[z][5][rows[z], :] = _bf(_lanes(outs))
    for z in range(2):
        for h in range(SSD_HEADS):
            s_ref[z, h] = state[z, h // hpg][:, (h % hpg) * SSD_P:(h % hpg + 1) * SSD_P]


def _ssd_scan(st, xbc, dt, cum, dtt, cumt, s0):
    ncb, rb, nblk, fwd, bwd = _scan_blocks(st)
    fwd_t = lambda b, j: (0, fwd(b, j)[0])
    bwd_t = lambda b, j: (0, bwd(b, j)[0])
    state_spec = pl.BlockSpec((None, 2, SSD_HEADS, SSD_N, SSD_P), lambda b, j: (b, 0, 0, 0, 0))
    rows = lambda w, m: pl.BlockSpec((rb, w), m)
    cols = lambda m: pl.BlockSpec((2 * SSD_HEADS, rb), m)
    lane = np.arange(LANE)[None, :, None]
    col = np.arange(SSD_HEADS * SSD_P)[None, None, :]
    sel = _bf(jnp.asarray((lane == np.arange(2)[:, None, None] * SSD_HEADS + col // SSD_P).astype(np.float32)))
    return pl.pallas_call(
        functools.partial(_ssd_scan_kernel, ncb=ncb),
        grid=(st.n, nblk),
        in_specs=[rows(768, fwd), rows(768, bwd), rows(LANE, fwd), rows(LANE, bwd), rows(LANE, fwd), rows(LANE, bwd),
                  cols(fwd_t), cols(bwd_t), cols(fwd_t), cols(bwd_t),
                  pl.BlockSpec(sel.shape, lambda b, j: (0, 0, 0)), state_spec],
        out_specs=[rows(512, fwd), rows(512, bwd), state_spec],
        out_shape=[jax.ShapeDtypeStruct((st.rows, 512), BF16), jax.ShapeDtypeStruct((st.rows, 512), BF16),
                   jax.ShapeDtypeStruct((st.n, 2, SSD_HEADS, SSD_N, SSD_P), F32)],
        compiler_params=_cp("parallel", "arbitrary"),
        name="ssd_scan",
    )(xbc, xbc, dt, dt, cum, cum, dtt, dtt, cumt, cumt, sel, s0)


def _ssd_post_kernel(yf_ref, yb_ref, xbc_ref, p_ref, d_ref, g_ref, o_ref):
    y = yf_ref[...].astype(F32) + yb_ref[...].astype(F32) + d_ref[...] * xbc_ref[:, :512]
    y = y * _silu(p_ref[:, :512])
    o_ref[...] = _bf(y * lax.rsqrt(jnp.mean(y * y, axis=-1, keepdims=True) + 1e-6) * g_ref[...])


def _ssd_post(st, y_f, y_b, xbc, proj, ssd_d, ssd_norm):
    m = proj.shape[0]
    tr = _row_tile(st)
    full = lambda shape: pl.BlockSpec(shape, lambda i: (0, 0))
    row = lambda w: pl.BlockSpec((tr, w), lambda i: (i, 0))
    return pl.pallas_call(
        _ssd_post_kernel,
        grid=(m // tr,),
        in_specs=[row(512), row(512), row(512), row(512), full((1, 512)), full((1, 512))],
        out_specs=row(512),
        out_shape=jax.ShapeDtypeStruct((m, 512), BF16),
        compiler_params=_cp("parallel"),
        name="ssd_post",
    )(y_f, y_b, xbc, proj, jnp.repeat(ssd_d, SSD_P).reshape(1, 512), ssd_norm.reshape(1, 512))


GLA_QK = GLA_HEADS * GLA_DK
GLA_SAFE_RANGE = 60.0


def _gla_prep_kernel(p_ref, g2_ref, gb_ref, lo_ref, up_ref, cum_out):
    logit = _dot_split(p_ref[...], g2_ref[...]) + gb_ref[...]
    log_a = -_softplus(-logit) * (1.0 / GLA_GATE_NORM)
    cum_out[:, :GLA_QK] = _dot_mask_data(lo_ref[...], log_a[:, :GLA_QK], 3)
    cum_out[:, GLA_QK:] = _dot_mask_data(up_ref[...], log_a[:, GLA_QK:], 3)


def _gla_prep(st, proj, g2p, gb):
    m = proj.shape[0]
    tr = _row_tile(st)
    full = lambda shape: pl.BlockSpec(shape, lambda i: (0, 0))
    lower, upper = _tri_blocks(tr)
    return pl.pallas_call(
        _gla_prep_kernel,
        grid=(m // tr,),
        in_specs=[pl.BlockSpec((tr, LANE), lambda i: (i, (GLA_W - LANE) // LANE)), full((LANE, 2 * GLA_QK)),
                  full((1, 2 * GLA_QK)),
                  full((tr, tr)), full((tr, tr))],
        out_specs=pl.BlockSpec((tr, 2 * GLA_QK), lambda i: (i, 0)),
        out_shape=jax.ShapeDtypeStruct((m, 2 * GLA_QK), F32),
        compiler_params=_cp("parallel"),
        name="gla_prep",
    )(proj, g2p, gb, lower, upper)


def _gla_intra_exact(p_ref, c_ref, rows, z, seg_ref):
    r0 = rows.start
    q = p_ref[rows, 0:GLA_QK] * (GLA_DK ** -0.5)
    cum = c_ref[rows, :]
    row = lax.broadcasted_iota(jnp.int32, (CHUNK, 1), 0)
    seg = seg_ref[...]

    def body(j, acc):
        kj = p_ref[pl.ds(r0 + j, 1), GLA_QK:2 * GLA_QK]
        vj = p_ref[pl.ds(r0 + j, 1), 2 * GLA_QK:2 * GLA_QK + GLA_HEADS * GLA_DV]
        seen = (row >= j) if z == 0 else (row <= j)
        w = jnp.where(seen, q * kj * jnp.exp(jnp.minimum(cum - c_ref[pl.ds(r0 + j, 1), :], 0.0)), 0.0)
        score = _dot_data_mask(w, seg, 3)
        return tuple(a + score[:, h:h + 1] * vj[:, h * GLA_DV:(h + 1) * GLA_DV] for h, a in enumerate(acc))

    zero = jnp.zeros((CHUNK, GLA_DV), F32)
    return list(lax.fori_loop(0, CHUNK, body, (zero,) * GLA_HEADS))


def _gla_scan_kernel(pf_ref, pb_ref, cf_ref, cb_ref, seg_ref, s0_ref, yf_ref, yb_ref, s_ref, *, ncb):
    @pl.when(pl.program_id(1) == 0)
    def _():
        s_ref[...] = s0_ref[...]

    refs = ((pf_ref, cf_ref, yf_ref), (pb_ref, cb_ref, yb_ref))
    heads = [(z, h) for z in range(2) for h in range(GLA_HEADS)]
    hsl = [slice(h * GLA_DK, (h + 1) * GLA_DK) for h in range(GLA_HEADS)]
    state = {(z, h): s_ref[z, h] for z, h in heads}
    chunks = [(step, z) for step in range(ncb) for z in range(2)]
    rows = {(step, z): _chunk_rows(step, z, ncb) for step, z in chunks}
    cum = {k: refs[k[1]][1][rows[k], :] for k in chunks}
    mid = {k: cum[k][CHUNK // 2:CHUNK // 2 + 1] for k in chunks}
    span = functools.reduce(jnp.maximum, [jnp.max(jnp.abs(cum[k] - mid[k])) for k in chunks])

    def v_of(k, h):
        return _bf(refs[k[1]][0][rows[k], 2 * GLA_QK + h * GLA_DV:2 * GLA_QK + (h + 1) * GLA_DV])

    def intra_factored():
        att = {}
        for k in chunks:
            p_ref = refs[k[1]][0]
            q_mid = _bf(p_ref[rows[k], 0:GLA_QK] * (GLA_DK ** -0.5) * jnp.exp(cum[k] - mid[k]))
            k_mid = _bf(p_ref[rows[k], GLA_QK:2 * GLA_QK] * jnp.exp(mid[k] - cum[k]))
            incl = _dir_masks(k[1])[0]
            for h in range(GLA_HEADS):
                att[k, h] = _bf(jnp.where(incl, _dot_nt(q_mid[:, hsl[h]], k_mid[:, hsl[h]]), 0.0))
        return [_dot(att[k, h], v_of(k, h)) for k in chunks for h in range(GLA_HEADS)]

    def intra_exact():
        out = []
        for k in chunks:
            out += _gla_intra_exact(refs[k[1]][0], refs[k[1]][1], rows[k], k[1], seg_ref)
        return out

    intra = lax.cond(span <= GLA_SAFE_RANGE, intra_factored, intra_exact)
    intra = {(k, h): intra[i * GLA_HEADS + h] for i, k in enumerate(chunks) for h in range(GLA_HEADS)}

    for step in range(ncb):
        q_in, k_out, e_last = {}, {}, {}
        for z in range(2):
            k = (step, z)
            p_ref = refs[z][0]
            glast = cum[k][CHUNK - 1:CHUNK] if z == 0 else cum[k][0:1]
            q_in[z] = _bf(p_ref[rows[k], 0:GLA_QK] * (GLA_DK ** -0.5) * jnp.exp(cum[k]))
            k_out[z] = _bf(p_ref[rows[k], GLA_QK:2 * GLA_QK] * jnp.exp(glast - cum[k]))
            e_last[z] = jnp.exp(glast)
        inter = {(z, h): _dot_nt(q_in[z][:, hsl[h]], _bf(state[z, h])) for z, h in heads}
        inc = {(z, h): _dot_tn(v_of((step, z), h), k_out[z][:, hsl[h]]) for z, h in heads}
        for z in range(2):
            refs[z][2][rows[step, z], :] = _bf(_lanes([intra[(step, z), h] + inter[z, h] for h in range(GLA_HEADS)]))
            for h in range(GLA_HEADS):
                state[z, h] = state[z, h] * e_last[z][:, hsl[h]] + inc[z, h]
    for z, h in heads:
        s_ref[z, h] = state[z, h]


def _gla_scan(st, proj, cum, s0_t):
    ncb, rb, nblk, fwd, bwd = _scan_blocks(st)
    seg = _block_diag(GLA_QK, GLA_DK, 1.0)[:, ::GLA_DK]
    seg = jnp.pad(seg, ((0, 0), (0, LANE - GLA_HEADS)))
    state_spec = pl.BlockSpec((None, 2, GLA_HEADS, GLA_DV, GLA_DK), lambda b, j: (b, 0, 0, 0, 0))
    qkv_w = 2 * GLA_QK + GLA_HEADS * GLA_DV
    return pl.pallas_call(
        functools.partial(_gla_scan_kernel, ncb=ncb),
        grid=(st.n, nblk),
        in_specs=[pl.BlockSpec((rb, qkv_w), fwd), pl.BlockSpec((rb, qkv_w), bwd),
                  pl.BlockSpec((rb, GLA_QK), fwd), pl.BlockSpec((rb, GLA_QK), lambda b, j: (bwd(b, j)[0], 1)),
                  pl.BlockSpec((GLA_QK, LANE), lambda b, j: (0, 0)), state_spec],
        out_specs=[pl.BlockSpec((rb, 512), fwd), pl.BlockSpec((rb, 512), bwd), state_spec],
        out_shape=[jax.ShapeDtypeStruct((st.rows, 512), BF16), jax.ShapeDtypeStruct((st.rows, 512), BF16),
                   jax.ShapeDtypeStruct((st.n, 2, GLA_HEADS, GLA_DV, GLA_DK), F32)],
        compiler_params=_cp("parallel", "arbitrary"),
        name="gla_scan",
    )(proj, proj, cum, cum, seg, s0_t)


def _gla_post_kernel(yf_ref, yb_ref, p_ref, g_ref, o_ref):
    o = yf_ref[...].astype(F32) + yb_ref[...].astype(F32)
    gate = _silu(p_ref[...])
    outs = []
    for h in range(GLA_HEADS):
        oh = o[:, h * GLA_DV:(h + 1) * GLA_DV]
        outs.append(oh * lax.rsqrt(jnp.mean(oh * oh, axis=-1, keepdims=True) + 1e-6) * g_ref[...])
    o_ref[...] = _bf(_lanes(outs) * gate)


def _gla_post(st, y_f, y_b, proj, gla_norm):
    m = proj.shape[0]
    tr = _row_tile(st)
    return pl.pallas_call(
        _gla_post_kernel,
        grid=(m // tr,),
        in_specs=[pl.BlockSpec((tr, 512), lambda i: (i, 0)), pl.BlockSpec((tr, 512), lambda i: (i, 0)),
                  pl.BlockSpec((tr, 512), lambda i: (i, 2)), pl.BlockSpec((1, GLA_DV), lambda i: (0, 0))],
        out_specs=pl.BlockSpec((tr, 512), lambda i: (i, 0)),
        out_shape=jax.ShapeDtypeStruct((m, 512), BF16),
        compiler_params=_cp("parallel"),
        name="gla_post",
    )(y_f, y_b, proj, gla_norm.reshape(1, GLA_DV))


def _rwkv_prep_kernel(cur_ref, prev_ref, next_ref, mu_ref, w2_ref, w0_ref, a2_ref, a0_ref, g2_ref,
                      kkw_ref, ka_ref, rk_ref, bd_ref, in_out, lw_out, post_out, *, tiles_per_seq):
    cur = cur_ref[...]
    x_prev, x_next = _neighbours(cur, prev_ref[...], next_ref[...], tiles_per_seq)
    blk = cur + (0.5 * (x_prev + x_next) - cur) * mu_ref[...]
    r, k, v = blk[:, 0:512], blk[:, 512:1024], blk[:, 1024:1536]
    w_logit = w0_ref[...] + _dot_split(jnp.tanh(blk[:, 1536:1600]), w2_ref[...])
    lw_out[...] = -RWKV_DECAY_SCALE * jax.nn.sigmoid(w_logit)
    a = jax.nn.sigmoid(a0_ref[...] + _dot_split(blk[:, 1600:1664], a2_ref[...]))
    g = _dot_split(jax.nn.sigmoid(blk[:, 1664:1792]), g2_ref[...])
    bd = bd_ref[...]
    kk = k * kkw_ref[...]
    kk = kk * lax.rsqrt(_dot_data_mask(kk * kk, bd, 2) + 1e-12)
    k2 = k * (1.0 + (a - 1.0) * ka_ref[...])
    in_out[:, 0:512] = _bf(r)
    in_out[:, 512:1024] = _bf(k2)
    in_out[:, 1024:1536] = _bf(v)
    in_out[:, 1536:2048] = _bf(-kk)
    in_out[:, 2048:2560] = _bf(kk * a)
    post_out[:, 0:512] = _bf(g)
    post_out[:, 512:1024] = _bf(_dot_data_mask(r * k2 * rk_ref[...], bd, 2) * v)


def _rwkv_prep(st, proj, mu, w2, w0, a2, a0, g2, kkw, ka, rk):
    m = proj.shape[0]
    tr = _row_tile(st)
    full = lambda shape: pl.BlockSpec(shape, lambda i: (0, 0))
    row = lambda w: pl.BlockSpec((tr, w), lambda i: (i, 0))
    vec = lambda a: a.reshape(1, -1)
    return pl.pallas_call(
        functools.partial(_rwkv_prep_kernel, tiles_per_seq=st.t // tr),
        grid=(m // tr,),
        in_specs=_halo_specs(st, RWKV_W) + [full((1, RWKV_W)), full((64, 1024)), full((1, 1024)),
                                            full((64, 512)), full((1, 512)), full((128, 512)),
                                            full((1, 512)), full((1, 512)), full((1, 512)), full((512, 512))],
        out_specs=[row(2560), row(1024), row(1024)],
        out_shape=[jax.ShapeDtypeStruct((m, 2560), BF16), jax.ShapeDtypeStruct((m, 1024), F32),
                   jax.ShapeDtypeStruct((m, 1024), BF16)],
        compiler_params=_cp("parallel"),
        name="rwkv_prep",
    )(proj, proj, proj, vec(mu), jnp.concatenate([w2[0], w2[1]], axis=1), vec(w0), a2, vec(a0), g2,
      vec(kkw), vec(ka), vec(rk), _block_diag(512, RWKV_HD, 1.0))


def _rwkv_chunk(x_ref, lw_ref, rows, z):
    incl, strict = _dir_masks(z)
    lw = lw_ref[rows, :]
    cum = _dot_mask_data(incl.astype(BF16), lw, 3)
    cum_x = cum - lw
    mid = cum[CHUNK // 2:CHUNK // 2 + 1]
    glast = cum[CHUNK - 1:CHUNK] if z == 0 else cum[0:1]
    x = x_ref[rows, :]
    r, k, v, al, be = x[:, 0:512], x[:, 512:1024], x[:, 1024:1536], x[:, 1536:2048], x[:, 2048:2560]
    e_mid, e_nmid = jnp.exp(cum - mid), jnp.exp(mid - cum)
    e_out = jnp.exp(glast - cum)
    return dict(incl=incl, strict=strict, v=v,
                r_mid=_bf(r * e_mid), a_mid=_bf(al * jnp.exp(cum_x - mid)),
                b_mid=_bf(be * e_nmid), k_mid=_bf(k * e_nmid),
                r_in=_bf(r * jnp.exp(cum)), a_in=al * jnp.exp(cum_x),
                b_out=_bf(be * e_out), k_out=_bf(k * e_out), e_last=jnp.exp(glast))


_HEAD_SLICES = [slice(h * RWKV_HD, (h + 1) * RWKV_HD) for h in range(RWKV_HEADS)]


def _rwkv_state_free(chunks, hooks):
    hooks = list(hooks)

    def run_hook():
        if hooks:
            hooks.pop(0)()

    chains = [(c, sl) for c in chunks for sl in _HEAD_SLICES]
    run_hook()
    pair = [_dot_nt(_rows([c['a_mid'][:, sl], c['r_mid'][:, sl]]), _rows([c['b_mid'][:, sl], c['k_mid'][:, sl]]))
            for c, sl in chains]
    half = CHUNK
    a_ab = [_bf(jnp.where(c['strict'], p[:half, :half], 0.0)) for (c, _), p in zip(chains, pair)]
    a_ak = [_bf(jnp.where(c['strict'], p[:half, half:], 0.0)) for (c, _), p in zip(chains, pair)]
    a_rb = [_bf(jnp.where(c['incl'], p[half:, :half], 0.0)) for (c, _), p in zip(chains, pair)]
    a_rk = [_bf(jnp.where(c['incl'], p[half:, half:], 0.0)) for (c, _), p in zip(chains, pair)]
    vb = [_bf(c['v'][:, sl]) for c, sl in chains]
    both = [_dot(_rows([ak, rk]), v) for ak, rk, v in zip(a_ak, a_rk, vb)]
    av = [x[:half] for x in both]
    o0 = [x[half:] for x in both]
    sol = [_lanes([c['a_in'][:, sl], x]) for (c, sl), x in zip(chains, av)]
    powr = a_ab
    width = 2 * RWKV_HD
    for it in range(6):
        if it < 5:
            both = [_dot(p, _lanes([_bf(s), p])) for p, s in zip(powr, sol)]
            sol = [s + x[:, :width] for s, x in zip(sol, both)]
            powr = [_bf(x[:, width:]) for x in both]
        else:
            sol = [s + _dot(p, _bf(s)) for p, s in zip(powr, sol)]
        if it in (1, 3):
            run_hook()
    while hooks:
        run_hook()
    return [dict(w=_bf(s[:, :RWKV_HD]), u0=s[:, RWKV_HD:], o0=o, a_rb=a, v=c['v'][:, sl], r_in=c['r_in'][:, sl],
                 b_out=c['b_out'][:, sl], k_out=c['k_out'][:, sl], e_last=c['e_last'][:, sl])
            for (c, sl), s, o, a in zip(chains, sol, o0, a_rb)]


def _rwkv_state_stages(res, state, write_out):
    box = {}

    def read_state():
        box['su'] = [_dot_nt(_rows([c['w'], c['r_in']]), _bf(s)) for c, s in zip(res, state)]

    def update_state():
        box['u'] = [c['u0'] + su[:CHUNK] for c, su in zip(res, box['su'])]
        state[:] = [s * c['e_last'] + _dot_tn(_rows([_bf(u), _bf(c['v'])]), _rows([c['b_out'], c['k_out']]))
                    for c, s, u in zip(res, state, box['u'])]

    def emit():
        write_out([su[CHUNK:] + _dot(c['a_rb'], _bf(u)) + c['o0'] for c, su, u in zip(res, box['su'], box['u'])])

    return [read_state, update_state, emit]


def _rwkv_scan_kernel(xf_ref, xb_ref, lwf_ref, lwb_ref, s0_ref, yf_ref, yb_ref, s_ref, *, ncb):
    @pl.when(pl.program_id(1) == 0)
    def _():
        s_ref[...] = s0_ref[...]

    state = [s_ref[z, h] for z in range(2) for h in range(RWKV_HEADS)]

    def chunk_rows(step, z):
        cc = step if z == 0 else ncb - 1 - step
        return slice(cc * CHUNK, (cc + 1) * CHUNK)

    def pair_inputs(step):
        return [_rwkv_chunk(xf_ref, lwf_ref, chunk_rows(step, 0), 0),
                _rwkv_chunk(xb_ref, lwb_ref, chunk_rows(step, 1), 1)]

    def writer(step):
        def write_out(outs):
            yf_ref[chunk_rows(step, 0), :] = _bf(_lanes(outs[:RWKV_HEADS]))
            yb_ref[chunk_rows(step, 1), :] = _bf(_lanes(outs[RWKV_HEADS:]))
        return write_out

    res = _rwkv_state_free(pair_inputs(0), [])
    for step in range(ncb):
        hooks = _rwkv_state_stages(res, state, writer(step))
        if step + 1 < ncb:
            res = _rwkv_state_free(pair_inputs(step + 1), hooks)
        else:
            for hook in hooks:
                hook()
    for z in range(2):
        for h in range(RWKV_HEADS):
            s_ref[z, h] = state[z * RWKV_HEADS + h]


def _rwkv_scan(st, xin, lw, s0):
    ncb, rb, nblk, fwd, bwd = _scan_blocks(st)
    state_spec = pl.BlockSpec((None, 2, RWKV_HEADS, RWKV_HD, RWKV_HD), lambda b, j: (b, 0, 0, 0, 0))
    return pl.pallas_call(
        functools.partial(_rwkv_scan_kernel, ncb=ncb),
        grid=(st.n, nblk),
        in_specs=[pl.BlockSpec((rb, 2560), fwd), pl.BlockSpec((rb, 2560), bwd),
                  pl.BlockSpec((rb, 512), fwd), pl.BlockSpec((rb, 512), lambda b, j: (bwd(b, j)[0], 1)),
                  state_spec],
        out_specs=[pl.BlockSpec((rb, 512), fwd), pl.BlockSpec((rb, 512), bwd), state_spec],
        out_shape=[jax.ShapeDtypeStruct((st.rows, 512), BF16), jax.ShapeDtypeStruct((st.rows, 512), BF16),
                   jax.ShapeDtypeStruct((st.n, 2, RWKV_HEADS, RWKV_HD, RWKV_HD), F32)],
        compiler_params=_cp("parallel", "arbitrary"),
        name="rwkv_scan",
    )(xin, xin, lw, lw, s0)


def _rwkv_post_kernel(yf_ref, yb_ref, post_ref, lng_ref, lnb_ref, bd_ref, o_ref):
    o = yf_ref[...].astype(F32) + yb_ref[...].astype(F32)
    bd = bd_ref[...]
    d = o - _dot_data_mask(o, bd, 2)
    o = d * lax.rsqrt(_dot_data_mask(d * d, bd, 2) + RWKV_LN_EPS) * lng_ref[...] + lnb_ref[...]
    o_ref[...] = _bf((o + post_ref[:, 512:1024].astype(F32)) * post_ref[:, 0:512].astype(F32))


def _rwkv_post(st, y_f, y_b, post, ln_g, ln_b):
    m = post.shape[0]
    tr = _row_tile(st)
    full = lambda shape: pl.BlockSpec(shape, lambda i: (0, 0))
    row = lambda w: pl.BlockSpec((tr, w), lambda i: (i, 0))
    return pl.pallas_call(
        _rwkv_post_kernel,
        grid=(m // tr,),
        in_specs=[row(512), row(512), row(1024), full((1, 512)), full((1, 512)), full((512, 512))],
        out_specs=row(512),
        out_shape=jax.ShapeDtypeStruct((m, 512), BF16),
        compiler_params=_cp("parallel"),
        name="rwkv_post",
    )(y_f, y_b, post, ln_g.reshape(1, 512), ln_b.reshape(1, 512), _block_diag(512, RWKV_HD, 1.0 / RWKV_HD))


def _merge_kernel(h_ref, o0_ref, o1_ref, o2_ref, o3_ref, wg_ref, wb_ref, out_ref):
    h = h_ref[...]
    acc = None
    for i, o_ref in enumerate((o0_ref, o1_ref, o2_ref, o3_ref)):
        term = jax.nn.sigmoid(_dot(h, wg_ref[i])) * _dot(o_ref[...], wb_ref[i])
        acc = term if acc is None else acc + term
    out_ref[...] = _bf(acc)


def _merge(st, h, branch_outs, wg, wb):
    m, d = h.shape
    tm = _mm_tile(st)
    tn = 512
    row = lambda w: pl.BlockSpec((tm, w), lambda j, i: (i, 0))
    return pl.pallas_call(
        _merge_kernel,
        grid=(d // tn, m // tm),
        in_specs=[row(d)] + [row(BRANCH_W)] * 4 + [pl.BlockSpec((4, d, tn), lambda j, i: (0, 0, j)),
                                                    pl.BlockSpec((4, BRANCH_W, tn), lambda j, i: (0, 0, j))],
        out_specs=pl.BlockSpec((tm, tn), lambda j, i: (i, j)),
        out_shape=jax.ShapeDtypeStruct((m, d), BF16),
        compiler_params=_cp("parallel", "parallel"),
        name="merge",
    )(h, *branch_outs, wg, wb)


def _wo_kernel(m_ref, w_ref, x_ref, mod_ref, g_ref, o_ref, h_out):
    x = x_ref[...] + mod_ref[0, 2:3, :] * _dot(m_ref[...], w_ref[...])
    o_ref[...] = x
    h_out[...] = _norm_mod(x, g_ref[...], mod_ref, 4, 3)


def _out_proj(st, merged, w_o, x, mod_l, gain2):
    m, d = x.shape
    tm = _mm_tile(st)
    row = pl.BlockSpec((tm, d), lambda i: (i, 0))
    return pl.pallas_call(
        _wo_kernel,
        grid=(m // tm,),
        in_specs=[row, pl.BlockSpec((d, d), lambda i: (0, 0)), row, pl.BlockSpec((1, 6, d), st.group_map(tm)),
                  pl.BlockSpec((1, d), lambda i: (0, 0))],
        out_specs=[row, row],
        out_shape=[jax.ShapeDtypeStruct((m, d), F32), jax.ShapeDtypeStruct((m, d), BF16)],
        compiler_params=_cp("parallel"),
        name="out_proj",
    )(merged, w_o, x, mod_l, gain2.reshape(1, d))


def _ffn_kernel(h_ref, w1_ref, w3_ref, w2_ref, x_ref, mod_ref, fg_ref, o_ref, acc_ref, *, final):
    f = pl.program_id(1)

    @pl.when(f == 0)
    def _():
        acc_ref[...] = jnp.zeros_like(acc_ref)

    h = h_ref[...]
    u = _silu(_dot(h, w1_ref[...])) * _dot(h, w3_ref[...])
    acc_ref[...] += _dot(_bf(u), w2_ref[...])

    @pl.when(f == pl.num_programs(1) - 1)
    def _():
        x = x_ref[...] + mod_ref[0, 5:6, :] * acc_ref[...]
        o_ref[...] = _rms(x, fg_ref[...]) if final else x


def _ffn(st, h, w1, w3, w2, x, mod_l, final_gain, final):
    m, d = x.shape
    dff = w1.shape[1]
    tm = _mm_tile(st)
    tf = 512
    return pl.pallas_call(
        functools.partial(_ffn_kernel, final=final),
        grid=(m // tm, dff // tf),
        in_specs=[pl.BlockSpec((tm, d), lambda i, f: (i, 0)),
                  pl.BlockSpec((d, tf), lambda i, f: (0, f)),
                  pl.BlockSpec((d, tf), lambda i, f: (0, f)),
                  pl.BlockSpec((tf, d), lambda i, f: (f, 0)),
                  pl.BlockSpec((tm, d), lambda i, f: (i, 0)),
                  pl.BlockSpec((1, 6, d), lambda i, f: st.group_map(tm)(i)),
                  pl.BlockSpec((1, d), lambda i, f: (0, 0))],
        out_specs=pl.BlockSpec((tm, d), lambda i, f: (i, 0)),
        out_shape=jax.ShapeDtypeStruct((m, d), F32),
        scratch_shapes=[pltpu.VMEM((tm, d), F32)],
        compiler_params=_cp("parallel", "arbitrary"),
        name="ffn",
    )(h, w1, w3, w2, x, mod_l, final_gain.reshape(1, d))


def _split_w_in(w_in):
    z = lambda w: jnp.zeros(w_in.shape[:2] + (w,), w_in.dtype)
    att = w_in[..., 0:768]
    ssd = jnp.concatenate([w_in[..., 768:2064], z(SSD_W - 1296)], axis=-1)
    rwkv = w_in[..., 2064:3856]
    gla = jnp.concatenate([w_in[..., 3856:4880], w_in[..., 4896:5408], w_in[..., 4880:4896], z(LANE - 16)], axis=-1)
    return tuple(_bf(w) for w in (att, ssd, rwkv, gla))


def _block(st, x, mod_l, p, rope_tables, ctx, final_gain, final):
    att_args = (st, x, p['norm1'], mod_l, p['w_att'], p['w_ssd'], p['q_norm'], p['k_norm'])

    if ctx is None:
        h, proj_ssd, q, k_att, v_att, k, v = _in_proj_a(*att_args, None)
        o_att = _attention(st, q, k_att, v_att)
        s_ssd = jnp.zeros((st.n, 2, SSD_HEADS, SSD_N, SSD_P), F32)
        s_rwkv = jnp.zeros((st.n, 2, RWKV_HEADS, RWKV_HD, RWKV_HD), F32)
        s_gla_t = jnp.zeros((st.n, 2, GLA_HEADS, GLA_DV, GLA_DK), F32)
    else:
        ctx_k, ctx_v, s_ssd, s_rwkv, s_gla = ctx
        h, proj_ssd, q, k_att, v_att = _in_proj_a(*att_args, rope_tables)
        k = v = None
        o_att = _attention(st, q, _join_cache(st, k_att, ctx_k, 2), _join_cache(st, v_att, ctx_v, 1))
        s_gla_t = jnp.swapaxes(s_gla, -1, -2)

    proj_rwkv, proj_gla = _in_proj_b(st, h, p['w_rwkv'], p['w_gla'])

    xbc, dt, cum, dtt, cumt = _ssd_prep(st, proj_ssd, p['ssd_conv_w'], p['ssd_conv_b'], p['ssd_dt_bias'],
                                        p['ssd_a_log'])
    y_ssd_f, y_ssd_b, new_ssd = _ssd_scan(st, xbc, dt, cum, dtt, cumt, s_ssd)
    o_ssd = _ssd_post(st, y_ssd_f, y_ssd_b, xbc, proj_ssd, p['ssd_d'], p['ssd_norm'])

    rin, lw, rpost = _rwkv_prep(st, proj_rwkv, p['rwkv_mu'], p['rwkv_w2'], p['rwkv_w0'], p['rwkv_a2'],
                                p['rwkv_a0'], p['rwkv_g2'], p['rwkv_kk'], p['rwkv_ka'], p['rwkv_rk'])
    y_rwkv_f, y_rwkv_b, new_rwkv = _rwkv_scan(st, rin, lw, s_rwkv)
    o_rwkv = _rwkv_post(st, y_rwkv_f, y_rwkv_b, rpost, p['rwkv_ln_g'], p['rwkv_ln_b'])

    g2 = jnp.concatenate([p['gla_g2'][0], p['gla_g2'][1]], axis=1)
    g2p = jnp.pad(g2, ((0, LANE - g2.shape[0]), (0, 0)))
    gla_cum = _gla_prep(st, proj_gla, g2p, p['gla_gb'].reshape(1, 2 * GLA_QK))
    y_gla_f, y_gla_b, new_gla_t = _gla_scan(st, proj_gla, gla_cum, s_gla_t)
    o_gla = _gla_post(st, y_gla_f, y_gla_b, proj_gla, p['gla_norm'])

    merged = _merge(st, h, (o_att, o_ssd, o_rwkv, o_gla), p['w_gate'], p['w_branch'])
    x, h2 = _out_proj(st, merged, p['w_o'], x, mod_l, p['norm2'])
    x = _ffn(st, h2, p['ffn_w1'], p['ffn_w3'], p['ffn_w2'], x, mod_l, final_gain, final)
    return x, (k, v, new_ssd, new_rwkv, jnp.swapaxes(new_gla_t, -1, -2))


def kernel(x_prompt, x_sample, cache_attn_k, cache_attn_v, state_ssd, state_rwkv, state_gla, c, c_ctx, w_mod, b_mod, norm1, norm2, w_in, q_norm, k_norm, ssd_conv_w, ssd_conv_b, ssd_dt_bias, ssd_a_log, ssd_d, ssd_norm, rwkv_mu, rwkv_w0, rwkv_w2, rwkv_a0, rwkv_a2, rwkv_g2, rwkv_kk, rwkv_ka, rwkv_rk, rwkv_ln_g, rwkv_ln_b, gla_g2, gla_gb, gla_norm, w_gate, w_branch, w_o, ffn_w1, ffn_w3, ffn_w2, final_norm):
    nb, seq, d = x_prompt.shape
    db, dseq, _ = x_sample.shape
    depth = w_in.shape[0]
    assert d == D_MODEL and seq % CHUNK == 0 and dseq % CHUNK == 0 and 1 + db <= MOD_ROWS
    ctx_st = _Stream(nb, seq, 0, False)
    lat_st = _Stream(db, dseq, 1, True)

    cond = jnp.concatenate([c_ctx[None], c, jnp.zeros((MOD_ROWS - 1 - db, d), F32)], axis=0)
    mod = _modulation(cond, w_mod, b_mod)

    w_att, w_ssd, w_rwkv, w_gla = _split_w_in(w_in)
    w_gate_b, w_branch_b, w_o_b = _bf(w_gate), _bf(w_branch), _bf(w_o)
    w1_b, w3_b, w2_b = _bf(ffn_w1), _bf(ffn_w3), _bf(ffn_w2)

    def params_at(l):
        return dict(norm1=norm1[l], norm2=norm2[l], w_att=w_att[l], w_ssd=w_ssd[l], w_rwkv=w_rwkv[l],
                    w_gla=w_gla[l], q_norm=q_norm[l], k_norm=k_norm[l],
                    ssd_conv_w=ssd_conv_w[l], ssd_conv_b=ssd_conv_b[l], ssd_dt_bias=ssd_dt_bias[l],
                    ssd_a_log=ssd_a_log[l], ssd_d=ssd_d[l], ssd_norm=ssd_norm[l],
                    rwkv_mu=rwkv_mu[l], rwkv_w0=rwkv_w0[l], rwkv_w2=rwkv_w2[l], rwkv_a0=rwkv_a0[l],
                    rwkv_a2=rwkv_a2[l], rwkv_g2=rwkv_g2[l], rwkv_kk=rwkv_kk[l], rwkv_ka=rwkv_ka[l],
                    rwkv_rk=rwkv_rk[l], rwkv_ln_g=rwkv_ln_g[l], rwkv_ln_b=rwkv_ln_b[l],
                    gla_g2=gla_g2[l], gla_gb=gla_gb[l], gla_norm=gla_norm[l],
                    w_gate=w_gate_b[l], w_branch=w_branch_b[l], w_o=w_o_b[l],
                    ffn_w1=w1_b[l], ffn_w3=w3_b[l], ffn_w2=w2_b[l])

    xp = x_prompt.reshape(nb * seq, d)
    new_k, new_v, new_ssd, new_rwkv, new_gla = [], [], [], [], []
    for l in range(depth):
        xp, (k_l, v_l, ssd_l, rwkv_l, gla_l) = _block(ctx_st, xp, mod[l], params_at(l), None, None, final_norm,
                                                      l == depth - 1)
        new_k.append(k_l.reshape(nb, seq, ATT_KV, HEAD_DIM))
        new_v.append(v_l.reshape(nb, seq, ATT_KV, HEAD_DIM))
        new_ssd.append(ssd_l)
        new_rwkv.append(rwkv_l)
        new_gla.append(gla_l)

    rope_tables = _rope_tables(dseq)
    xs = x_sample.reshape(db * dseq, d)
    for l in range(depth):
        ctx = (cache_attn_k[:, l], cache_attn_v[:, l], state_ssd[:, l], state_rwkv[:, l], state_gla[:, l])
        xs, _ = _block(lat_st, xs, mod[l], params_at(l), rope_tables, ctx, final_norm, l == depth - 1)

    y_prompt = xp.reshape(nb, seq, d)
    y_sample = xs.reshape(db, dseq, d)
    return (y_prompt, y_sample, jnp.stack(new_k, axis=1), jnp.stack(new_v, axis=1),
            jnp.stack(new_ssd, axis=1), jnp.stack(new_rwkv, axis=1), jnp.stack(new_gla, axis=1))
```
